```python
import math
import jax, jax.numpy as jnp
from jax import lax
import numpy as np

D_MODEL = 1024
BATCH = 8
SEQ = 2048
DEPTH = 2
DEC_BATCH = 128
DEC_SEQ = 8
PAST_LEN = 16384
PAGE_SIZE = 128

DK_A = 128
DV_A = 128
H_A = D_MODEL // 128
KEY_A = H_A * DK_A
VAL_A = H_A * DV_A
CONV_W = 4
CONV_DIM = 2 * KEY_A + VAL_A
CHUNK = 64
HD_B = 64
H_B = D_MODEL // HD_B
H_KV = H_B // 4
G_B = H_B // H_KV
Q_B = H_B * HD_B
KV_B = H_KV * HD_B
WINDOW = 128
N_BUCKETS = 32
MAX_DISTANCE = 128
D_FF = 4 * D_MODEL
N_BRANCH = 2
N_IN = CONV_DIM + VAL_A + 2 * H_A + Q_B + 2 * KV_B + N_BRANCH * D_MODEL
ALPHA = (2 * DEPTH) ** 0.25
BETA_INIT = (8 * DEPTH) ** -0.25
LN_EPS = 1e-5
RMS_EPS = 1e-6

kernel_name = 'hybrid_gdn_swa_sink_decoder_step'


def layer_norm(x, g, b):
    xf = x.astype(jnp.float32)
    mu = jnp.mean(xf, -1, keepdims=True)
    var = jnp.mean(jnp.square(xf - mu), -1, keepdims=True)
    return ((xf - mu) * lax.rsqrt(var + LN_EPS) * g.astype(jnp.float32) + b.astype(jnp.float32)).astype(x.dtype)


def l2norm(x):
    return x * lax.rsqrt(jnp.sum(x * x, -1, keepdims=True) + 1e-6)


def t5_bucket(dist):
    n = jnp.maximum(dist, 0)
    max_exact = N_BUCKETS // 2
    large = max_exact + (jnp.log(jnp.maximum(n, max_exact).astype(jnp.float32) / max_exact)
                         / math.log(MAX_DISTANCE / max_exact) * (N_BUCKETS - max_exact)).astype(jnp.int32)
    large = jnp.minimum(large, N_BUCKETS - 1)
    return jnp.where(n < max_exact, n, large)


def short_conv(u, buf, w):
    L = u.shape[1]
    up = jnp.concatenate([buf.astype(u.dtype), u], axis=1)
    y = sum(up[:, j:j + L] * w[j] for j in range(CONV_W))
    return jax.nn.silu(y), up[:, -(CONV_W - 1):]


def gated_delta_rule(q, k, v, g, beta, s0):
    B, L, H, DK = q.shape
    DV = v.shape[-1]
    C = min(CHUNK, L)
    pad = (-L) % C
    if pad:
        padw = lambda t: jnp.pad(t, [(0, 0), (0, pad)] + [(0, 0)] * (t.ndim - 2))
        q, k, v, g, beta = (padw(t) for t in (q, k, v, g, beta))
    N = (L + pad) // C

    def blocks(t):
        return jnp.moveaxis(t.reshape(B, N, C, H, *t.shape[3:]), 3, 1)

    q, k, v, g, beta = (blocks(t) for t in (q, k, v, g, beta))
    g = jnp.cumsum(g, axis=-1)
    causal = jnp.tril(jnp.ones((C, C), bool))
    strict = causal & ~jnp.eye(C, dtype=bool)
    decay = jnp.exp(jnp.where(causal, g[..., :, None] - g[..., None, :], -jnp.inf))
    kb = k * beta[..., None]
    m = jnp.where(strict, jnp.einsum('bhncd,bhnsd->bhncs', kb, k) * decay, 0.0)
    a = m + jnp.eye(C, dtype=m.dtype)
    u = lax.linalg.triangular_solve(a, v * beta[..., None], left_side=True, lower=True, unit_diagonal=True)
    w = lax.linalg.triangular_solve(a, kb * jnp.exp(g)[..., None], left_side=True, lower=True, unit_diagonal=True)
    qk = jnp.einsum('bhncd,bhnsd->bhncs', q, k) * decay
    qg = q * jnp.exp(g)[..., None]
    kd = k * jnp.exp(g[..., -1:] - g)[..., None]
    gl = jnp.exp(g[..., -1])
    xs = tuple(jnp.moveaxis(t, 2, 0) for t in (u, w, qk, qg, kd, gl))

    def step(s, xs_i):
        u_i, w_i, qk_i, qg_i, kd_i, gl_i = xs_i
        v_new = u_i - jnp.einsum('bhcd,bhde->bhce', w_i, s)
        o_i = jnp.einsum('bhcd,bhde->bhce', qg_i, s) + jnp.einsum('bhcs,bhse->bhce', qk_i, v_new)
        s = s * gl_i[..., None, None] + jnp.einsum('bhcd,bhce->bhde', kd_i, v_new)
        return s, o_i

    s, o = lax.scan(step, s0, xs)
    o = jnp.transpose(o, (1, 0, 3, 2, 4)).reshape(B, N * C, H, DV)[:, :L]
    return o, s


def window_attention(q, k, v, q_pos, k_pos, sink, rel_bias):
    dist = q_pos[:, :, None] - k_pos[:, None, :]
    valid = (dist >= 0) & (dist < WINDOW) & (k_pos[:, None, :] >= 0)
    bias = rel_bias.astype(jnp.float32)[t5_bucket(dist)]
    bias = bias.reshape(*dist.shape, H_KV, G_B).transpose(0, 3, 4, 1, 2)
    s = jnp.einsum('bnqhgd,bnkhd->bnhgqk', q, k, preferred_element_type=jnp.float32) * (HD_B ** -0.5) + bias
    s = jnp.where(valid[:, None, None], s, -jnp.inf)
    sk = sink.astype(jnp.float32).reshape(H_KV, G_B, 1, 1)
    mx = jnp.maximum(jnp.max(s, -1, keepdims=True), sk)
    p = jnp.exp(s - mx)
    denom = jnp.sum(p, -1, keepdims=True) + jnp.exp(sk - mx)
    return jnp.einsum('bnhgqk,bnkhd->bnqhgd', (p / denom).astype(v.dtype), v)


def swa_prompt(q, k, v, sink, rel_bias):
    B, L = q.shape[:2]
    nb = L // WINDOW
    qb = q.reshape(B, nb, WINDOW, H_KV, G_B, HD_B)

    def band(t):
        tp = jnp.concatenate([jnp.zeros_like(t[:, :WINDOW]), t], 1).reshape(B, nb + 1, WINDOW, H_KV, HD_B)
        return jnp.concatenate([tp[:, :-1], tp[:, 1:]], axis=2)

    pos = jnp.arange(-WINDOW, L, dtype=jnp.int32).reshape(nb + 1, WINDOW)
    k_pos = jnp.concatenate([pos[:-1], pos[1:]], axis=1)
    o = window_attention(qb, band(k), band(v), pos[1:], k_pos, sink, rel_bias)
    return o.reshape(B, L, Q_B), k[:, -WINDOW:], v[:, -WINDOW:]


def swa_sample(q, k, v, k_buf, v_buf, sink, rel_bias):
    B, L = q.shape[:2]
    nbuf = k_buf.shape[1]
    kf = jnp.concatenate([k_buf.astype(k.dtype), k], axis=1)
    vf = jnp.concatenate([v_buf.astype(v.dtype), v], axis=1)
    q_pos = (PAST_LEN + jnp.arange(L, dtype=jnp.int32))[None]
    k_pos = (PAST_LEN - nbuf + jnp.arange(nbuf + L, dtype=jnp.int32))[None]
    o = window_attention(q.reshape(B, 1, L, H_KV, G_B, HD_B), kf[:, None], vf[:, None], q_pos, k_pos, sink, rel_bias)
    return o.reshape(B, L, Q_B), kf[:, -WINDOW:], vf[:, -WINDOW:]


def trunk_layer(x, c, p, rel_bias, conv_buf, s0, k_buf, v_buf):
    B, L, _ = x.shape
    f32 = jnp.float32
    mod = (jax.nn.silu(c) @ p['w_ada'] + p['b_ada'])[:, None, :]
    sh1, sc1, gt1, sh2, sc2, gt2 = jnp.split(mod, 6, axis=-1)
    h = x * (1 + sc1) + sh1
    proj = h @ p['w_in']
    offs = [int(o) for o in np.cumsum([CONV_DIM, VAL_A, H_A, H_A, Q_B, KV_B, KV_B])]
    conv_in, z, b_a, a_a, q_b, k_b, v_b, gate_logits = jnp.split(proj, offs, axis=-1)

    conv_out, new_conv = short_conv(conv_in, conv_buf, p['w_conv'])
    qa, ka, va = jnp.split(conv_out.astype(f32), [KEY_A, 2 * KEY_A], axis=-1)
    qa = l2norm(qa.reshape(B, L, H_A, DK_A)) * (DK_A ** -0.5)
    ka = l2norm(ka.reshape(B, L, H_A, DK_A))
    va = va.reshape(B, L, H_A, DV_A)
    beta = jax.nn.sigmoid(b_a.astype(f32))
    g = -jnp.exp(p['a_log'].astype(f32)) * jax.nn.softplus(a_a.astype(f32) + p['dt_bias'].astype(f32))
    o_a, s_new = gated_delta_rule(qa, ka, va, g, beta, s0.astype(f32))
    o_a = o_a * lax.rsqrt(jnp.mean(jnp.square(o_a), -1, keepdims=True) + RMS_EPS) * p['w_onorm'].astype(f32)
    o_a = (o_a * jax.nn.silu(z.astype(f32).reshape(B, L, H_A, DV_A))).reshape(B, L, VAL_A).astype(x.dtype)

    qh = q_b.reshape(B, L, H_B, HD_B)
    kh = k_b.reshape(B, L, H_KV, HD_B)
    vh = v_b.reshape(B, L, H_KV, HD_B)
    if k_buf is None:
        o_b, new_k, new_v = swa_prompt(qh, kh, vh, p['sinks'], rel_bias)
    else:
        o_b, new_k, new_v = swa_sample(qh, kh, vh, k_buf, v_buf, p['sinks'], rel_bias)

    gates = jax.nn.sigmoid(gate_logits.astype(f32)).astype(x.dtype)
    g_a, g_b = jnp.split(gates, N_BRANCH, axis=-1)
    mixed = g_a * (o_a @ p['w_pa']) + g_b * (o_b @ p['w_pb'])
    x = layer_norm(ALPHA * x + gt1 * (mixed @ p['w_out']), p['ln1_g'], p['ln1_b'])

    h2 = x * (1 + sc2) + sh2
    ff = jnp.square(jax.nn.relu(h2 @ p['w_up'])) @ p['w_down']
    x = layer_norm(ALPHA * x + gt2 * ff, p['ln2_g'], p['ln2_b'])
    return x, s_new.astype(s0.dtype), new_conv, new_k, new_v


def setup_inputs(seed: int = 0) -> dict:
    key = jax.random.key(seed)
    ks = iter(jax.random.split(key, 32))
    nrm = lambda shape, s: s * jax.random.normal(next(ks), shape, jnp.float32)
    d = D_MODEL
    col_scale = jnp.ones((N_IN,), jnp.float32)
    col_scale = col_scale.at[2 * KEY_A:CONV_DIM].set(BETA_INIT)
    vb0 = CONV_DIM + VAL_A + 2 * H_A + Q_B + KV_B
    col_scale = col_scale.at[vb0:vb0 + KV_B].set(BETA_INIT)
    dt = jnp.exp(jax.random.uniform(next(ks), (DEPTH, H_A), jnp.float32, math.log(1e-3), math.log(1e-1)))
    return {
        'x_prompt': nrm((BATCH, SEQ, d), 1.0),
        'x_sample': nrm((DEC_BATCH, DEC_SEQ, d), 1.0),
        'state_delta': nrm((DEPTH, DEC_BATCH, H_A, DK_A, DV_A), DK_A ** -0.5),
        'state_conv': nrm((DEPTH, DEC_BATCH, CONV_W - 1, CONV_DIM), 1.0),
        'cache_k': nrm((DEPTH, DEC_BATCH, WINDOW, H_KV, HD_B), 1.0),
        'cache_v': nrm((DEPTH, DEC_BATCH, WINDOW, H_KV, HD_B), 1.0),
        'c_prompt': nrm((BATCH, d), 1.0),
        'c_sample': nrm((DEC_BATCH, d), 1.0),
        'rel_bias': nrm((N_BUCKETS, H_B), 0.5),
        'w_ada': nrm((DEPTH, d, 6 * d), d ** -0.5),
        'b_ada': nrm((DEPTH, 6 * d), 0.01),
        'w_in': nrm((DEPTH, d, N_IN), d ** -0.5) * col_scale,
        'w_conv': nrm((DEPTH, CONV_W, CONV_DIM), 0.5),
        'a_log': jnp.log(jax.random.uniform(next(ks), (DEPTH, H_A), jnp.float32, 1.0, 16.0)),
        'dt_bias': jnp.log(jnp.expm1(dt)),
        'w_onorm': 1.0 + nrm((DEPTH, DV_A), 0.02),
        'sinks': nrm((DEPTH, H_B), 0.5),
        'w_pa': nrm((DEPTH, VAL_A, d), VAL_A ** -0.5 * BETA_INIT),
        'w_pb': nrm((DEPTH, Q_B, d), Q_B ** -0.5 * BETA_INIT),
        'w_out': nrm((DEPTH, d, d), d ** -0.5 * BETA_INIT),
        'ln1_g': 1.0 + nrm((DEPTH, d), 0.02),
        'ln1_b': nrm((DEPTH, d), 0.01),
        'w_up': nrm((DEPTH, d, D_FF), d ** -0.5 * BETA_INIT),
        'w_down': nrm((DEPTH, D_FF, d), D_FF ** -0.5 * BETA_INIT),
        'ln2_g': 1.0 + nrm((DEPTH, d), 0.02),
        'ln2_b': nrm((DEPTH, d), 0.01),
    }


def reference(x_prompt, x_sample, state_delta, state_conv, cache_k, cache_v, c_prompt, c_sample,
              rel_bias, w_ada, b_ada, w_in, w_conv, a_log, dt_bias, w_onorm, sinks,
              w_pa, w_pb, w_out, ln1_g, ln1_b, w_up, w_down, ln2_g, ln2_b):
    yp, ys = x_prompt, x_sample
    bp = x_prompt.shape[0]
    pd, pc, pk, pv, sd, sc, sk, sv = [], [], [], [], [], [], [], []
    for l in range(DEPTH):
        p = dict(w_ada=w_ada[l], b_ada=b_ada[l], w_in=w_in[l], w_conv=w_conv[l], a_log=a_log[l],
                 dt_bias=dt_bias[l], w_onorm=w_onorm[l], sinks=sinks[l], w_pa=w_pa[l], w_pb=w_pb[l],
                 w_out=w_out[l], ln1_g=ln1_g[l], ln1_b=ln1_b[l], w_up=w_up[l], w_down=w_down[l],
                 ln2_g=ln2_g[l], ln2_b=ln2_b[l])
        yp, s_p, conv_p, k_p, v_p = trunk_layer(
            yp, c_prompt, p, rel_bias,
            jnp.zeros((bp, CONV_W - 1, CONV_DIM), x_prompt.dtype),
            jnp.zeros((bp, H_A, DK_A, DV_A), state_delta.dtype), None, None)
        ys, s_s, conv_s, k_s, v_s = trunk_layer(
            ys, c_sample, p, rel_bias, state_conv[l], state_delta[l], cache_k[l], cache_v[l])
        pd.append(s_p); pc.append(conv_p); pk.append(k_p); pv.append(v_p)
        sd.append(s_s); sc.append(conv_s); sk.append(k_s); sv.append(v_s)
    return (yp, ys, jnp.stack(pd), jnp.stack(pc), jnp.stack(pk), jnp.stack(pv),
            jnp.stack(sd), jnp.stack(sc), jnp.stack(sk), jnp.stack(sv))
```

```python
import functools
import math

import numpy as np
import jax
import jax.numpy as jnp
from jax import lax
from jax.experimental import pallas as pl
from jax.experimental.pallas import tpu as pltpu

F32 = jnp.float32
BF16 = jnp.bfloat16

D_MODEL = 1024
N_HEADS_A = 8
DK_A = 128
CONV_W = 4
CONV_DIM = 3 * D_MODEL
CHUNK = 64
HD_B = 64
N_HEADS_B = 16
N_KV_B = 4
GROUP_B = N_HEADS_B // N_KV_B
KV_B = N_KV_B * HD_B
WINDOW = 128
N_BUCKETS = 32
MAX_DISTANCE = 128
D_FF = 4 * D_MODEL
LN_EPS = 1e-5
RMS_EPS = 1e-6

OFF_Z = CONV_DIM
OFF_QB = OFF_Z + D_MODEL
OFF_KVB = OFF_QB + D_MODEL
OFF_GATES = OFF_KVB + 2 * KV_B
N_MAIN = OFF_GATES + 2 * D_MODEL

V7X_VMEM_BYTES = 64 * 1024 * 1024
VMEM_LIMIT = V7X_VMEM_BYTES - 8 * 1024 * 1024
HI = lax.Precision.HIGHEST


def _params(n_grid):
    return pltpu.CompilerParams(dimension_semantics=("arbitrary",) * n_grid, vmem_limit_bytes=VMEM_LIMIT)


def _dot(a, b):
    return jnp.dot(a, b, preferred_element_type=F32)


def _dot_nt(a, b):
    return lax.dot_general(a, b, (((1,), (1,)), ((), ())), preferred_element_type=F32)


def _dot_tn(a, b):
    return lax.dot_general(a, b, (((0,), (0,)), ((), ())), preferred_element_type=F32)


def _sigmoid(x):
    return 1.0 / (1.0 + jnp.exp(-x))


def _silu(x):
    return x * _sigmoid(x)


def _const_spec(shape):
    nd = len(shape)
    return pl.BlockSpec(shape, lambda *_: (0,) * nd, pipeline_mode=pl.Buffered(1))


def _ada_body(c_ref, w_ref, b_ref, o_ref):
    s = _silu(c_ref[...]).astype(BF16)
    o_ref[...] = _dot(s, w_ref[...].astype(BF16)) + b_ref[...]


def _ada(c_all, w_ada, b_ada):
    depth = w_ada.shape[0]
    n_rows = c_all.shape[0]
    n_col = w_ada.shape[2] // D_MODEL
    return pl.pallas_call(
        _ada_body,
        grid=(depth, n_col),
        in_specs=[pl.BlockSpec((n_rows, D_MODEL), lambda l, n: (0, 0)),
                  pl.BlockSpec((None, D_MODEL, D_MODEL), lambda l, n: (l, 0, n)),
                  pl.BlockSpec((None, 1, D_MODEL), lambda l, n: (l, 0, n))],
        out_specs=pl.BlockSpec((None, n_rows, D_MODEL), lambda l, n: (l, 0, n)),
        out_shape=jax.ShapeDtypeStruct((depth, n_rows, w_ada.shape[2]), F32),
        compiler_params=_params(2),
        name="ada",
    )(c_all, w_ada, b_ada.reshape(depth, 1, -1))


def _bucket_table():
    r = np.arange(WINDOW)[:, None]
    c = np.arange(2 * WINDOW)[None, :]
    dist = WINDOW + r - c
    n = np.maximum(dist, 0)
    max_exact = N_BUCKETS // 2
    ratio = np.maximum(n, max_exact).astype(np.float32) / np.float32(max_exact)
    large = max_exact + (np.log(ratio) / np.float32(math.log(MAX_DISTANCE / max_exact))
                         * np.float32(N_BUCKETS - max_exact)).astype(np.int32)
    large = np.minimum(large, N_BUCKETS - 1)
    bucket = np.where(n < max_exact, n, large).astype(np.int32)
    valid = ((dist >= 0) & (dist < WINDOW)).astype(np.int32)
    return bucket, valid


def _bias_body(rb_ref, bucket_ref, valid_ref, o_ref):
    h = pl.program_id(0)
    bucket = bucket_ref[...]
    acc = jnp.zeros(bucket.shape, F32)
    for j in range(N_BUCKETS):
        acc = jnp.where(bucket == j, rb_ref[j, h], acc)
    o_ref[...] = jnp.where(valid_ref[...] > 0, acc, -jnp.inf)


def _bias_table(rel_bias):
    bucket, valid = _bucket_table()
    return pl.pallas_call(
        _bias_body,
        grid=(N_HEADS_B,),
        in_specs=[pl.BlockSpec(memory_space=pltpu.SMEM),
                  pl.BlockSpec((WINDOW, 2 * WINDOW), lambda h: (0, 0)),
                  pl.BlockSpec((WINDOW, 2 * WINDOW), lambda h: (0, 0))],
        out_specs=pl.BlockSpec((None, WINDOW, 2 * WINDOW), lambda h: (h, 0, 0)),
        out_shape=jax.ShapeDtypeStruct((N_HEADS_B, WINDOW, 2 * WINDOW), F32),
        compiler_params=_params(1),
        name="bias_table",
    )(rel_bias.astype(F32), jnp.asarray(bucket), jnp.asarray(valid))


def _front_body(x_ref, mod_ref, st_ref, wm_ref, wba_ref, wconv_ref, gpar_ref,
                cs_ref, z_ref, gb_ref, qb_ref, k_ref, v_ref, gates_ref, cst_ref, cbuf, *, bt, tt):
    m = bt * tt
    sh1 = mod_ref[:, :, 0:D_MODEL]
    sc1 = mod_ref[:, :, D_MODEL:2 * D_MODEL]
    h = (x_ref[...] * (1.0 + sc1) + sh1).reshape(m, D_MODEL).astype(BF16)

    @pl.when(pl.program_id(1) == 0)
    def _():
        cbuf[:, 5:8, :] = st_ref[...]

    for s in range(3):
        cols = slice(s * D_MODEL, (s + 1) * D_MODEL)
        cbuf[:, 8:8 + tt, cols] = _dot(h, wm_ref[:, cols]).reshape(bt, tt, D_MODEL)
        y = cbuf[:, 5:5 + tt, cols] * wconv_ref[0:1, cols]
        for j in range(1, CONV_W):
            y = y + cbuf[:, 5 + j:5 + j + tt, cols] * wconv_ref[j:j + 1, cols]
        cs_ref[:, :, cols] = _silu(y)
    tail = cbuf[:, tt + 5:tt + 8, :]
    cst_ref[...] = tail
    cbuf[:, 5:8, :] = tail

    z_ref[...] = _dot(h, wm_ref[:, OFF_Z:OFF_QB]).reshape(bt, tt, D_MODEL)
    qb_ref[...] = (_dot(h, wm_ref[:, OFF_QB:OFF_KVB]) * (HD_B ** -0.5)).astype(BF16).reshape(bt, tt, D_MODEL)
    kv = _dot(h, wm_ref[:, OFF_KVB:OFF_GATES])
    k_ref[...] = kv[:, :KV_B].reshape(bt, tt, KV_B)
    v_ref[...] = kv[:, KV_B:].reshape(bt, tt, KV_B)
    gates_ref[...] = _sigmoid(_dot(h, wm_ref[:, OFF_GATES:N_MAIN])).reshape(bt, tt, 2 * D_MODEL)

    ba = _dot(h, wba_ref[...])
    xg = ba + gpar_ref[1:2, :]
    softplus = jnp.maximum(xg, 0.0) + jnp.log(1.0 + jnp.exp(-jnp.abs(xg)))
    lane = lax.broadcasted_iota(jnp.int32, ba.shape, 1)
    gb = jnp.where(lane < N_HEADS_A, _sigmoid(ba), gpar_ref[0:1, :] * softplus)
    gb_ref[...] = gb[:, :2 * N_HEADS_A].reshape(bt, tt, 2 * N_HEADS_A)


def _front(x, mod, conv_state, wm, wba, wconv, gpar, *, bt, tt):
    b, l, _ = x.shape
    grid = (b // bt, l // tt)
    tok = lambda n, dt=F32: jax.ShapeDtypeStruct((b, l, n), dt)
    tok_spec = lambda n: pl.BlockSpec((bt, tt, n), lambda i, t: (i, t, 0))
    return pl.pallas_call(
        functools.partial(_front_body, bt=bt, tt=tt),
        grid=grid,
        in_specs=[tok_spec(D_MODEL),
                  pl.BlockSpec((bt, 1, 6 * D_MODEL), lambda i, t: (i, 0, 0)),
                  pl.BlockSpec((bt, CONV_W - 1, CONV_DIM), lambda i, t: (i, 0, 0)),
                  _const_spec((D_MODEL, N_MAIN)),
                  _const_spec((D_MODEL, 128)),
                  _const_spec((CONV_W, CONV_DIM)),
                  _const_spec((2, 128))],
        out_specs=[tok_spec(CONV_DIM), tok_spec(D_MODEL), tok_spec(2 * N_HEADS_A), tok_spec(D_MODEL),
                   tok_spec(KV_B), tok_spec(KV_B), tok_spec(2 * D_MODEL),
                   pl.BlockSpec((bt, CONV_W - 1, CONV_DIM), lambda i, t: (i, 0, 0))],
        out_shape=[tok(CONV_DIM), tok(D_MODEL), tok(2 * N_HEADS_A), tok(D_MODEL, BF16),
                   tok(KV_B), tok(KV_B), tok(2 * D_MODEL),
                   jax.ShapeDtypeStruct((b, CONV_W - 1, CONV_DIM), F32)],
        scratch_shapes=[pltpu.VMEM((bt, tt + 8, CONV_DIM), F32)],
        compiler_params=_params(2),
        name="front",
    )(x, mod, conv_state, wm, wba, wconv, gpar)


def _delta_masks(r, block):
    i = lax.broadcasted_iota(jnp.int32, (r, r), 0)
    j = lax.broadcasted_iota(jnp.int32, (r, r), 1)
    shift = int(math.log2(block))
    same = (i >> shift) == (j >> shift)
    tri = same & (i >= j)
    strict = same & (i > j)
    levels = [((i >> (s + 1)) == (j >> (s + 1))) & ((i >> s) != (j >> s)) for s in range(shift)]
    return same.astype(F32), tri, strict, (i == j).astype(F32), levels


def _delta_pre(qr, kr, v, beta_b, g_b, masks):
    same, tri, strict, eye, levels = masks
    r = qr.shape[0]
    q = qr * lax.rsqrt(jnp.sum(qr * qr, -1, keepdims=True) + 1e-6) * (DK_A ** -0.5)
    k = kr * lax.rsqrt(jnp.sum(kr * kr, -1, keepdims=True) + 1e-6)
    gc = jnp.dot(tri.astype(F32), g_b, precision=HI, preferred_element_type=F32)
    gl = jnp.dot(same, g_b, precision=HI, preferred_element_type=F32)
    gc_row = gc.T[:r, :]
    decay = jnp.exp(jnp.where(tri, gc[:, :r] - gc_row, -jnp.inf))
    eg = jnp.exp(gc)
    kb = k * beta_b
    kk = _dot_nt(jnp.concatenate([kb, q], axis=0), k)
    mm = jnp.where(strict, kk[:r] * decay, 0.0)
    qk = kk[r:] * decay
    t = eye - jnp.where(levels[0], mm, 0.0)
    for lvl in levels[1:]:
        t = t - _dot(_dot(t, jnp.where(lvl, mm, 0.0)), t)
    uw = _dot(t, jnp.concatenate([v * beta_b, kb * eg], axis=1))
    u = uw[:, :DK_A]
    w = uw[:, DK_A:]
    return u, w, q * eg, k * jnp.exp(gl - gc), qk, jnp.exp(gl)


def _gated_norm(o, z, w_onorm):
    o = o * lax.rsqrt(jnp.mean(o * o, -1, keepdims=True) + RMS_EPS) * w_onorm
    return o * _silu(z)


def _delta_prompt_body(cs_ref, z_ref, gb_ref, wn_ref, o_ref, s_out_ref,
                       s_ref, u_s, w_s, qg_s, kd_s, qk_s, egl_s, *, tt):
    n_c = tt // CHUNK
    masks = _delta_masks(CHUNK, CHUNK)

    @pl.when(pl.program_id(1) == 0)
    def _():
        s_ref[...] = jnp.zeros(s_ref.shape, F32)

    def pre(idx, carry):
        hd = idx // n_c
        c = idx % n_c
        rows = pl.ds(pl.multiple_of(c * CHUNK, CHUNK), CHUNK)
        col = lambda base: pl.ds(pl.multiple_of(base + hd * DK_A, DK_A), DK_A)
        gbv = gb_ref[rows, :]
        lane = lax.broadcasted_iota(jnp.int32, gbv.shape, 1)
        beta = jnp.sum(jnp.where(lane == hd, gbv, 0.0), -1, keepdims=True)
        g = jnp.sum(jnp.where(lane == hd + N_HEADS_A, gbv, 0.0), -1, keepdims=True)
        beta_b = jnp.broadcast_to(beta, (CHUNK, DK_A))
        g_b = jnp.broadcast_to(g, (CHUNK, DK_A))
        u, w, qg, kd, qk, egl = _delta_pre(cs_ref[rows, col(0)], cs_ref[rows, col(D_MODEL)],
                                           cs_ref[rows, col(2 * D_MODEL)], beta_b, g_b, masks)
        u_s[rows, col(0)] = u
        w_s[rows, col(0)] = w
        qg_s[rows, col(0)] = qg
        kd_s[rows, col(0)] = kd
        qk_s[idx] = qk
        egl_s[idx] = egl[0:8, :]
        return carry

    lax.fori_loop(0, N_HEADS_A * n_c, pre, 0)

    def scan(c, carry):
        rows = pl.ds(pl.multiple_of(c * CHUNK, CHUNK), CHUNK)
        for hd in range(N_HEADS_A):
            cols = slice(hd * DK_A, (hd + 1) * DK_A)
            s = s_ref[hd]
            ws = _dot(jnp.concatenate([w_s[rows, cols], qg_s[rows, cols]], axis=0), s)
            v_new = u_s[rows, cols] - ws[:CHUNK]
            o = ws[CHUNK:] + _dot(qk_s[hd * n_c + c], v_new)
            s_ref[hd] = s * egl_s[hd * n_c + c][0:1, :] + _dot_tn(kd_s[rows, cols], v_new)
            o_ref[rows, cols] = _gated_norm(o, z_ref[rows, cols], wn_ref[...]).astype(BF16)
        return carry

    lax.fori_loop(0, n_c, scan, 0)

    @pl.when(pl.program_id(1) == pl.num_programs(1) - 1)
    def _():
        s_out_ref[...] = s_ref[...]


def _delta_prompt(cs, z, gb, w_onorm, *, tt):
    b, l, _ = cs.shape
    n_c = tt // CHUNK
    tok_spec = lambda n: pl.BlockSpec((None, tt, n), lambda i, t: (i, t, 0))
    return pl.pallas_call(
        functools.partial(_delta_prompt_body, tt=tt),
        grid=(b, l // tt),
        in_specs=[tok_spec(CONV_DIM), tok_spec(D_MODEL), tok_spec(2 * N_HEADS_A), _const_spec((1, DK_A))],
        out_specs=[tok_spec(D_MODEL),
                   pl.BlockSpec((None, N_HEADS_A, DK_A, DK_A), lambda i, t: (i, 0, 0, 0))],
        out_shape=[jax.ShapeDtypeStruct((b, l, D_MODEL), BF16),
                   jax.ShapeDtypeStruct((b, N_HEADS_A, DK_A, DK_A), F32)],
        scratch_shapes=[pltpu.VMEM((N_HEADS_A, DK_A, DK_A), F32)]
        + [pltpu.VMEM((tt, D_MODEL), F32)] * 4
        + [pltpu.VMEM((N_HEADS_A * n_c, CHUNK, CHUNK), F32), pltpu.VMEM((N_HEADS_A * n_c, 8, DK_A), F32)],
        compiler_params=_params(2),
        name="delta_prompt",
    )(cs, z, gb, w_onorm)


def _delta_sample_body(cs_ref, z_ref, gb_ref, wn_ref, s0_ref, o_ref, s_out_ref, *, bt, ls):
    r = N_HEADS_A * ls
    masks = _delta_masks(r, ls)

    def per_batch(bi, carry):
        cs = cs_ref[bi]
        gbv = gb_ref[bi]
        stack = lambda base: jnp.concatenate(
            [cs[:, base + hd * DK_A: base + (hd + 1) * DK_A] for hd in range(N_HEADS_A)], axis=0)
        beta_b = jnp.concatenate(
            [jnp.broadcast_to(gbv[:, hd:hd + 1], (ls, DK_A)) for hd in range(N_HEADS_A)], axis=0)
        g_b = jnp.concatenate(
            [jnp.broadcast_to(gbv[:, N_HEADS_A + hd:N_HEADS_A + hd + 1], (ls, DK_A)) for hd in range(N_HEADS_A)],
            axis=0)
        u, w, qg, kd, qk, egl = _delta_pre(stack(0), stack(D_MODEL), stack(2 * D_MODEL), beta_b, g_b, masks)
        ws, qs = [], []
        for hd in range(N_HEADS_A):
            rows = slice(hd * ls, (hd + 1) * ls)
            both = _dot(jnp.concatenate([w[rows], qg[rows]], axis=0), s0_ref[bi, hd])
            ws.append(both[:ls])
            qs.append(both[ls:])
        v_new = u - jnp.concatenate(ws, axis=0)
        o = jnp.concatenate(qs, axis=0) + _dot(qk, v_new)
        zb = z_ref[bi]
        for hd in range(N_HEADS_A):
            rows = slice(hd * ls, (hd + 1) * ls)
            cols = slice(hd * DK_A, (hd + 1) * DK_A)
            s_out_ref[bi, hd] = s0_ref[bi, hd] * egl[hd * ls:hd * ls + 1, :] + _dot_tn(kd[rows], v_new[rows])
            o_ref[bi, :, cols] = _gated_norm(o[rows], zb[:, cols], wn_ref[...]).astype(BF16)
        return carry

    lax.fori_loop(0, bt, per_batch, 0)


def _delta_sample(cs, z, gb, w_onorm, s0, *, bt):
    b, ls, _ = cs.shape
    tok_spec = lambda n: pl.BlockSpec((bt, ls, n), lambda i: (i, 0, 0))
    st_spec = pl.BlockSpec((bt, N_HEADS_A, DK_A, DK_A), lambda i: (i, 0, 0, 0))
    return pl.pallas_call(
        functools.partial(_delta_sample_body, bt=bt, ls=ls),
        grid=(b // bt,),
        in_specs=[tok_spec(CONV_DIM), tok_spec(D_MODEL), tok_spec(2 * N_HEADS_A), _const_spec((1, DK_A)), st_spec],
        out_specs=[tok_spec(D_MODEL), st_spec],
        out_shape=[jax.ShapeDtypeStruct((b, ls, D_MODEL), BF16),
                   jax.ShapeDtypeStruct((b, N_HEADS_A, DK_A, DK_A), F32)],
        compiler_params=_params(1),
        name="delta_sample",
    )(cs, z, gb, w_onorm, s0)


def _softmax_pv(s, sink, v):
    mx = jnp.maximum(jnp.max(s, -1, keepdims=True), sink)
    p = jnp.exp(s - mx)
    denom = jnp.sum(p, -1, keepdims=True) + jnp.exp(sink - mx)
    return _dot((p / denom).astype(BF16), v)


def _swa_prompt_body(q_ref, kp_ref, kc_ref, vp_ref, vc_ref, bias_ref, sink_ref, o_ref):
    first = pl.program_id(1) == 0
    col = lax.broadcasted_iota(jnp.int32, (WINDOW, 2 * WINDOW), 1)
    no_prev = first & (col < WINDOW)
    k2 = jnp.concatenate([kp_ref[...], kc_ref[...]], axis=0).astype(BF16)
    v2 = jnp.concatenate([vp_ref[...], vc_ref[...]], axis=0).astype(BF16)
    for hh in range(N_HEADS_B):
        kvh = hh // GROUP_B
        kv_cols = slice(kvh * HD_B, (kvh + 1) * HD_B)
        q = q_ref[:, hh * HD_B:(hh + 1) * HD_B]
        s = _dot_nt(q, k2[:, kv_cols]) + bias_ref[hh]
        s = jnp.where(no_prev, -jnp.inf, s)
        o = _softmax_pv(s, sink_ref[hh:hh + 1, 0:1], v2[:, kv_cols])
        o_ref[:, hh * HD_B:(hh + 1) * HD_B] = o.astype(BF16)


def _swa_prompt(qb, k, v, bias_tab, sink_tab):
    b, l, _ = qb.shape
    cur = lambda n: pl.BlockSpec((None, WINDOW, n), lambda i, t: (i, t, 0))
    prev = lambda n: pl.BlockSpec((None, WINDOW, n), lambda i, t: (i, jnp.maximum(t - 1, 0), 0))
    return pl.pallas_call(
        _swa_prompt_body,
        grid=(b, l // WINDOW),
        in_specs=[cur(D_MODEL), prev(KV_B), cur(KV_B), prev(KV_B), cur(KV_B),
                  _const_spec((N_HEADS_B, WINDOW, 2 * WINDOW)), _const_spec((N_HEADS_B, 128))],
        out_specs=cur(D_MODEL),
        out_shape=jax.ShapeDtypeStruct((b, l, D_MODEL), BF16),
        compiler_params=_params(2),
        name="swa_prompt",
    )(qb, k, k, v, v, bias_tab, sink_tab)


def _swa_sample_body(q_ref, kn_ref, vn_ref, kc_ref, vc_ref, bias_ref, sink_ref, o_ref, ko_ref, vo_ref, *, bt, ls):
    def per_batch(bi, carry):
        q = q_ref[bi]
        kf = jnp.concatenate([kc_ref[bi], kn_ref[bi]], axis=0)
        vf = jnp.concatenate([vc_ref[bi], vn_ref[bi]], axis=0)
        ko_ref[bi] = kf[ls:, :]
        vo_ref[bi] = vf[ls:, :]
        k2 = kf.astype(BF16)
        v2 = vf.astype(BF16)
        for kvh in range(N_KV_B):
            kv_cols = slice(kvh * HD_B, (kvh + 1) * HD_B)
            qg = jnp.concatenate([q[:, hh * HD_B:(hh + 1) * HD_B]
                                  for hh in range(kvh * GROUP_B, (kvh + 1) * GROUP_B)], axis=0)
            rows = slice(kvh * GROUP_B * ls, (kvh + 1) * GROUP_B * ls)
            s = _dot_nt(qg, k2[:, kv_cols]) + bias_ref[rows, :]
            o = _softmax_pv(s, sink_ref[rows, 0:1], v2[:, kv_cols])
            for g in range(GROUP_B):
                hh = kvh * GROUP_B + g
                o_ref[bi, :, hh * HD_B:(hh + 1) * HD_B] = o[g * ls:(g + 1) * ls].astype(BF16)
        return carry

    lax.fori_loop(0, bt, per_batch, 0)


def _swa_sample(qb, k_new, v_new, k_cache, v_cache, bias_s, sink_s, *, bt):
    b, ls, _ = qb.shape
    n_keys = WINDOW + ls
    new = lambda n: pl.BlockSpec((bt, ls, n), lambda i: (i, 0, 0))
    cache = pl.BlockSpec((bt, WINDOW, KV_B), lambda i: (i, 0, 0))
    return pl.pallas_call(
        functools.partial(_swa_sample_body, bt=bt, ls=ls),
        grid=(b // bt,),
        in_specs=[new(D_MODEL), new(KV_B), new(KV_B), cache, cache,
                  _const_spec((N_HEADS_B * ls, n_keys)), _const_spec((N_HEADS_B * ls, 128))],
        out_specs=[new(D_MODEL), cache, cache],
        out_shape=[jax.ShapeDtypeStruct((b, ls, D_MODEL), BF16),
                   jax.ShapeDtypeStruct((b, WINDOW, KV_B), F32),
                   jax.ShapeDtypeStruct((b, WINDOW, KV_B), F32)],
        compiler_params=_params(1),
        name="swa_sample",
    )(qb, k_new, v_new, k_cache, v_cache, bias_s, sink_s)


def _layer_norm(y, g, b):
    mu = jnp.mean(y, -1, keepdims=True)
    d = y - mu
    var = jnp.mean(d * d, -1, keepdims=True)
    return d * lax.rsqrt(var + LN_EPS) * g + b


def _merge_body(x_ref, oa_ref, ob_ref, gates_ref, mod_ref, wpa_ref, wpb_ref, wout_ref, lng_ref, lnb_ref,
                o_ref, *, bt, tt, alpha):
    m = bt * tt
    ga = gates_ref[:, :, 0:D_MODEL].reshape(m, D_MODEL)
    gb = gates_ref[:, :, D_MODEL:2 * D_MODEL].reshape(m, D_MODEL)
    mixed = ga * _dot(oa_ref[...].reshape(m, D_MODEL), wpa_ref[...]) \
        + gb * _dot(ob_ref[...].reshape(m, D_MODEL), wpb_ref[...])
    attn = _dot(mixed.astype(BF16), wout_ref[...]).reshape(bt, tt, D_MODEL)
    gt1 = mod_ref[:, :, 2 * D_MODEL:3 * D_MODEL]
    o_ref[...] = _layer_norm(alpha * x_ref[...] + gt1 * attn, lng_ref[...], lnb_ref[...])


def _merge(x, oa, ob, gates, mod, wpa, wpb, wout, lng, lnb, *, bt, tt, alpha):
    b, l, _ = x.shape
    tok_spec = lambda n: pl.BlockSpec((bt, tt, n), lambda i, t: (i, t, 0))
    return pl.pallas_call(
        functools.partial(_merge_body, bt=bt, tt=tt, alpha=alpha),
        grid=(b // bt, l // tt),
        in_specs=[tok_spec(D_MODEL), tok_spec(D_MODEL), tok_spec(D_MODEL), tok_spec(2 * D_MODEL),
                  pl.BlockSpec((bt, 1, 6 * D_MODEL), lambda i, t: (i, 0, 0)),
                  _const_spec((D_MODEL, D_MODEL)), _const_spec((D_MODEL, D_MODEL)), _const_spec((D_MODEL, D_MODEL)),
                  _const_spec((1, D_MODEL)), _const_spec((1, D_MODEL))],
        out_specs=tok_spec(D_MODEL),
        out_shape=jax.ShapeDtypeStruct((b, l, D_MODEL), F32),
        compiler_params=_params(2),
        name="merge",
    )(x, oa, ob, gates, mod, wpa, wpb, wout, lng, lnb)


def _mlp_body(x_ref, mod_ref, wup_ref, wdn_ref, lng_ref, lnb_ref, o_ref, *, bt, tt, alpha):
    m = bt * tt
    x = x_ref[...]
    sh2 = mod_ref[:, :, 3 * D_MODEL:4 * D_MODEL]
    sc2 = mod_ref[:, :, 4 * D_MODEL:5 * D_MODEL]
    gt2 = mod_ref[:, :, 5 * D_MODEL:6 * D_MODEL]
    h2 = (x * (1.0 + sc2) + sh2).reshape(m, D_MODEL).astype(BF16)
    ff = jnp.zeros((m, D_MODEL), F32)
    for c in range(D_FF // D_MODEL):
        cols = slice(c * D_MODEL, (c + 1) * D_MODEL)
        a = jnp.maximum(_dot(h2, wup_ref[:, cols]), 0.0)
        ff = ff + _dot((a * a).astype(BF16), wdn_ref[cols, :])
    o_ref[...] = _layer_norm(alpha * x + gt2 * ff.reshape(bt, tt, D_MODEL), lng_ref[...], lnb_ref[...])


def _mlp(x, mod, wup, wdn, lng, lnb, *, bt, tt, alpha):
    b, l, _ = x.shape
    tok_spec = pl.BlockSpec((bt, tt, D_MODEL), lambda i, t: (i, t, 0))
    return pl.pallas_call(
        functools.partial(_mlp_body, bt=bt, tt=tt, alpha=alpha),
        grid=(b // bt, l // tt),
        in_specs=[tok_spec, pl.BlockSpec((bt, 1, 6 * D_MODEL), lambda i, t: (i, 0, 0)),
                  _const_spec((D_MODEL, D_FF)), _const_spec((D_FF, D_MODEL)),
                  _const_spec((1, D_MODEL)), _const_spec((1, D_MODEL))],
        out_specs=tok_spec,
        out_shape=jax.ShapeDtypeStruct((b, l, D_MODEL), F32),
        compiler_params=_params(2),
        name="mlp",
    )(x, mod, wup, wdn, lng, lnb)


def _tiles(b, l, rows):
    tt = min(l, rows)
    bt = max(1, min(b, rows // tt))
    return bt, tt


def _layer_weights(l, w_in, w_conv, a_log, dt_bias, w_onorm, sinks, w_pa, w_pb, w_out,
                   ln1_g, ln1_b, w_up, w_down, ln2_g, ln2_b):
    win = w_in[l]
    o_ba = CONV_DIM + D_MODEL
    o_qb = o_ba + 2 * N_HEADS_A
    wm = jnp.concatenate([win[:, :o_ba], win[:, o_qb:]], axis=1).astype(BF16)
    wba = jnp.pad(win[:, o_ba:o_qb], ((0, 0), (0, 128 - 2 * N_HEADS_A))).astype(BF16)
    gpar = jnp.zeros((2, 128), F32)
    gpar = gpar.at[0, N_HEADS_A:2 * N_HEADS_A].set(-jnp.exp(a_log[l].astype(F32)))
    gpar = gpar.at[1, N_HEADS_A:2 * N_HEADS_A].set(dt_bias[l].astype(F32))
    row = lambda a: a[l].reshape(1, -1).astype(F32)
    return dict(wm=wm, wba=wba, wconv=w_conv[l].astype(F32), gpar=gpar, w_onorm=row(w_onorm), sinks=sinks[l].astype(F32),
                wpa=w_pa[l].astype(BF16), wpb=w_pb[l].astype(BF16), wout=w_out[l].astype(BF16),
                ln1_g=row(ln1_g), ln1_b=row(ln1_b), wup=w_up[l].astype(BF16), wdn=w_down[l].astype(BF16),
                ln2_g=row(ln2_g), ln2_b=row(ln2_b))


def _trunk_layer(x, mod, p, bias_tab, alpha, conv_state, s0, k_cache, v_cache):
    b, l, _ = x.shape
    prompt = k_cache is None
    bt, tt = _tiles(b, l, 256)
    cs, z, gb, qb, k, v, gates, conv_out = _front(x, mod, conv_state, p["wm"], p["wba"], p["wconv"], p["gpar"],
                                                  bt=bt, tt=tt)
    if prompt:
        oa, s_new = _delta_prompt(cs, z, gb, p["w_onorm"], tt=min(l, 256))
        sink_tab = jnp.broadcast_to(p["sinks"][:, None], (N_HEADS_B, 128))
        ob = _swa_prompt(qb, k, v, bias_tab, sink_tab)
        k_new, v_new = k[:, -WINDOW:], v[:, -WINDOW:]
    else:
        oa, s_new = _delta_sample(cs, z, gb, p["w_onorm"], s0, bt=min(b, 8))
        bias_s = bias_tab[:, :l, :WINDOW + l].reshape(N_HEADS_B * l, WINDOW + l)
        sink_s = jnp.broadcast_to(jnp.repeat(p["sinks"], l)[:, None], (N_HEADS_B * l, 128))
        ob, k_new, v_new = _swa_sample(qb, k, v, k_cache.reshape(b, WINDOW, KV_B), v_cache.reshape(b, WINDOW, KV_B),
                                       bias_s, sink_s, bt=min(b, 8))
    bt, tt = _tiles(b, l, 512)
    x1 = _merge(x, oa, ob, gates, mod, p["wpa"], p["wpb"], p["wout"], p["ln1_g"], p["ln1_b"], bt=bt, tt=tt, alpha=alpha)
    x2 = _mlp(x1, mod, p["wup"], p["wdn"], p["ln2_g"], p["ln2_b"], bt=bt, tt=tt, alpha=alpha)
    return (x2, s_new, conv_out, k_new.reshape(b, WINDOW, N_KV_B, HD_B), v_new.reshape(b, WINDOW, N_KV_B, HD_B))


def kernel(x_prompt, x_sample, state_delta, state_conv, cache_k, cache_v, c_prompt, c_sample, rel_bias, w_ada, b_ada, w_in, w_conv, a_log, dt_bias, w_onorm, sinks, w_pa, w_pb, w_out, ln1_g, ln1_b, w_up, w_down, ln2_g, ln2_b):
    depth = w_in.shape[0]
    alpha = (2 * depth) ** 0.25
    bp = x_prompt.shape[0]
    mod_all = _ada(jnp.concatenate([c_prompt, c_sample], axis=0), w_ada, b_ada)
    bias_tab = _bias_table(rel_bias)
    yp, ys = x_prompt, x_sample
    outs = [[] for _ in range(8)]
    for l in range(depth):
        p = _layer_weights(l, w_in, w_conv, a_log, dt_bias, w_onorm, sinks, w_pa, w_pb, w_out,
                           ln1_g, ln1_b, w_up, w_down, ln2_g, ln2_b)
        mod_p = mod_all[l, :bp][:, None, :]
        mod_s = mod_all[l, bp:][:, None, :]
        zero_conv = jnp.zeros((bp, CONV_W - 1, CONV_DIM), x_prompt.dtype)
        yp, *rest_p = _trunk_layer(yp, mod_p, p, bias_tab, alpha, zero_conv, None, None, None)
        ys, *rest_s = _trunk_layer(ys, mod_s, p, bias_tab, alpha, state_conv[l], state_delta[l], cache_k[l], cache_v[l])
        for acc, val in zip(outs, rest_p + rest_s):
            acc.append(val)
    return (yp, ys) + tuple(jnp.stack(a) for a in outs)
```

```python
import functools
import math

import numpy as np
import jax
import jax.numpy as jnp
from jax import lax
from jax.experimental import pallas as pl
from jax.experimental.pallas import tpu as pltpu

F32 = jnp.float32
BF16 = jnp.bfloat16

D_MODEL = 1024
N_HEADS_A = 8
DK_A = 128
CONV_W = 4
CONV_DIM = 3 * D_MODEL
CHUNK = 64
HD_B = 64
N_HEADS_B = 16
N_KV_B = 4
GROUP_B = N_HEADS_B // N_KV_B
KV_B = N_KV_B * HD_B
WINDOW = 128
N_BUCKETS = 32
MAX_DISTANCE = 128
D_FF = 4 * D_MODEL
LN_EPS = 1e-5
RMS_EPS = 1e-6

OFF_Z = CONV_DIM
OFF_QB = OFF_Z + D_MODEL
OFF_KVB = OFF_QB + D_MODEL
OFF_GATES = OFF_KVB + 2 * KV_B
N_MAIN = OFF_GATES + 2 * D_MODEL
GB_LANES = 128

V7X_VMEM_BYTES = 64 * 1024 * 1024
VMEM_LIMIT = V7X_VMEM_BYTES - 8 * 1024 * 1024


def _params(n_grid):
    return pltpu.CompilerParams(dimension_semantics=("arbitrary",) * n_grid, vmem_limit_bytes=VMEM_LIMIT)


def _dot(a, b):
    return jnp.dot(a, b, preferred_element_type=F32)


def _dot_nt(a, b):
    return lax.dot_general(a, b, (((1,), (1,)), ((), ())), preferred_element_type=F32)


def _dot_tn(a, b):
    return lax.dot_general(a, b, (((0,), (0,)), ((), ())), preferred_element_type=F32)


def _sigmoid(x):
    return 1.0 / (1.0 + jnp.exp(-x))


def _silu(x):
    return x * _sigmoid(x)


def _const_spec(shape):
    nd = len(shape)
    return pl.BlockSpec(shape, lambda *_: (0,) * nd, pipeline_mode=pl.Buffered(1))


def _ada_body(c_ref, w_ref, b_ref, o_ref):
    s = _silu(c_ref[...]).astype(BF16)
    o_ref[...] = _dot(s, w_ref[...].astype(BF16)) + b_ref[...]


def _ada(c_all, w_ada, b_ada):
    depth = w_ada.shape[0]
    n_rows = c_all.shape[0]
    n_col = w_ada.shape[2] // D_MODEL
    return pl.pallas_call(
        _ada_body,
        grid=(depth, n_col),
        in_specs=[pl.BlockSpec((n_rows, D_MODEL), lambda l, n: (0, 0)),
                  pl.BlockSpec((None, D_MODEL, D_MODEL), lambda l, n: (l, 0, n)),
                  pl.BlockSpec((None, 1, D_MODEL), lambda l, n: (l, 0, n))],
        out_specs=pl.BlockSpec((None, n_rows, D_MODEL), lambda l, n: (l, 0, n)),
        out_shape=jax.ShapeDtypeStruct((depth, n_rows, w_ada.shape[2]), F32),
        compiler_params=_params(2),
        name="ada",
    )(c_all, w_ada, b_ada.reshape(depth, 1, -1))


def _bucket_table():
    r = np.arange(WINDOW)[:, None]
    c = np.arange(2 * WINDOW)[None, :]
    dist = WINDOW + r - c
    n = np.maximum(dist, 0)
    max_exact = N_BUCKETS // 2
    ratio = np.maximum(n, max_exact).astype(np.float32) / np.float32(max_exact)
    large = max_exact + (np.log(ratio) / np.float32(math.log(MAX_DISTANCE / max_exact))
                         * np.float32(N_BUCKETS - max_exact)).astype(np.int32)
    large = np.minimum(large, N_BUCKETS - 1)
    bucket = np.where(n < max_exact, n, large).astype(np.int32)
    valid = ((dist >= 0) & (dist < WINDOW)).astype(np.int32)
    return bucket, valid


def _bias_body(rb_ref, bucket_ref, valid_ref, o_ref):
    h = pl.program_id(0)
    bucket = bucket_ref[...]
    acc = jnp.zeros(bucket.shape, F32)
    for j in range(N_BUCKETS):
        acc = jnp.where(bucket == j, rb_ref[j, h], acc)
    o_ref[...] = jnp.where(valid_ref[...] > 0, acc, -jnp.inf)


def _bias_table(rel_bias):
    bucket, valid = _bucket_table()
    return pl.pallas_call(
        _bias_body,
        grid=(N_HEADS_B,),
        in_specs=[pl.BlockSpec(memory_space=pltpu.SMEM),
                  pl.BlockSpec((WINDOW, 2 * WINDOW), lambda h: (0, 0)),
                  pl.BlockSpec((WINDOW, 2 * WINDOW), lambda h: (0, 0))],
        out_specs=pl.BlockSpec((None, WINDOW, 2 * WINDOW), lambda h: (h, 0, 0)),
        out_shape=jax.ShapeDtypeStruct((N_HEADS_B, WINDOW, 2 * WINDOW), F32),
        compiler_params=_params(1),
        name="bias_table",
    )(rel_bias.astype(F32), jnp.asarray(bucket), jnp.asarray(valid))


def _bias_t_body(rb_ref, sink_ref, bucket_ref, valid_ref, o_ref):
    l = pl.program_id(0)
    h = pl.program_id(1)
    bucket = bucket_ref[...]
    acc = jnp.zeros(bucket.shape, F32)
    for j in range(N_BUCKETS):
        acc = jnp.where(bucket == j, rb_ref[j, h], acc)
    acc = jnp.where(valid_ref[...] > 0, acc, -jnp.inf)
    key = lax.broadcasted_iota(jnp.int32, bucket.shape, 0)
    o_ref[...] = jnp.where(key == 0, sink_ref[l, h], acc)


def _bias_table_t(rel_bias, sinks):
    depth = sinks.shape[0]
    bucket, valid = _bucket_table()
    return pl.pallas_call(
        _bias_t_body,
        grid=(depth, N_HEADS_B),
        in_specs=[pl.BlockSpec(memory_space=pltpu.SMEM), pl.BlockSpec(memory_space=pltpu.SMEM),
                  pl.BlockSpec((2 * WINDOW, WINDOW), lambda l, h: (0, 0)),
                  pl.BlockSpec((2 * WINDOW, WINDOW), lambda l, h: (0, 0))],
        out_specs=pl.BlockSpec((None, None, 2 * WINDOW, WINDOW), lambda l, h: (l, h // GROUP_B, 0, h % GROUP_B)),
        out_shape=jax.ShapeDtypeStruct((depth, N_KV_B, 2 * WINDOW, GROUP_B * WINDOW), F32),
        compiler_params=_params(2),
        name="bias_table_t",
    )(rel_bias.astype(F32), sinks.astype(F32), jnp.asarray(bucket.T.copy()), jnp.asarray(valid.T.copy()))


def _front_body(x_ref, mod_ref, st_ref, wm_ref, wba_ref, wconv_ref, gpar_ref,
                cs_ref, z_ref, gb_ref, qb_ref, k_ref, v_ref, gates_ref, cst_ref, cbuf, *, bt, tt):
    m = bt * tt
    sh1 = mod_ref[:, :, 0:D_MODEL]
    sc1 = mod_ref[:, :, D_MODEL:2 * D_MODEL]
    h = (x_ref[...] * (1.0 + sc1) + sh1).reshape(m, D_MODEL).astype(BF16)

    @pl.when(pl.program_id(1) == 0)
    def _():
        cbuf[:, 5:8, :] = st_ref[...]

    for s in range(3):
        cols = slice(s * D_MODEL, (s + 1) * D_MODEL)
        cbuf[:, 8:8 + tt, cols] = _dot(h, wm_ref[:, cols]).reshape(bt, tt, D_MODEL)
        y = cbuf[:, 5:5 + tt, cols] * wconv_ref[0:1, cols]
        for j in range(1, CONV_W):
            y = y + cbuf[:, 5 + j:5 + j + tt, cols] * wconv_ref[j:j + 1, cols]
        cs_ref[:, :, cols] = _silu(y)
    tail = cbuf[:, tt + 5:tt + 8, :]
    cst_ref[...] = tail
    cbuf[:, 5:8, :] = tail

    z_ref[...] = _dot(h, wm_ref[:, OFF_Z:OFF_QB]).reshape(bt, tt, D_MODEL)
    qb_ref[...] = (_dot(h, wm_ref[:, OFF_QB:OFF_KVB]) * (HD_B ** -0.5)).astype(BF16).reshape(bt, tt, D_MODEL)
    kv = _dot(h, wm_ref[:, OFF_KVB:OFF_GATES])
    k_ref[...] = kv[:, :KV_B].reshape(bt, tt, KV_B)
    v_ref[...] = kv[:, KV_B:].reshape(bt, tt, KV_B)
    gates_ref[...] = _sigmoid(_dot(h, wm_ref[:, OFF_GATES:N_MAIN])).reshape(bt, tt, 2 * D_MODEL)

    ba = _dot(h, wba_ref[...])
    xg = ba + gpar_ref[1:2, :]
    softplus = jnp.maximum(xg, 0.0) + jnp.log(1.0 + jnp.exp(-jnp.abs(xg)))
    lane = lax.broadcasted_iota(jnp.int32, ba.shape, 1)
    gb = jnp.where(lane < N_HEADS_A, _sigmoid(ba), gpar_ref[0:1, :] * softplus)
    gb_ref[...] = gb.reshape(bt, tt, GB_LANES)


def _front(x, mod, conv_state, wm, wba, wconv, gpar, *, bt, tt):
    b, l, _ = x.shape
    grid = (b // bt, l // tt)
    tok = lambda n, dt=F32: jax.ShapeDtypeStruct((b, l, n), dt)
    tok_spec = lambda n: pl.BlockSpec((bt, tt, n), lambda i, t: (i, t, 0))
    return pl.pallas_call(
        functools.partial(_front_body, bt=bt, tt=tt),
        grid=grid,
        in_specs=[tok_spec(D_MODEL),
                  pl.BlockSpec((bt, 1, 6 * D_MODEL), lambda i, t: (i, 0, 0)),
                  pl.BlockSpec((bt, CONV_W - 1, CONV_DIM), lambda i, t: (i, 0, 0)),
                  _const_spec((D_MODEL, N_MAIN)),
                  _const_spec((D_MODEL, GB_LANES)),
                  _const_spec((CONV_W, CONV_DIM)),
                  _const_spec((2, GB_LANES))],
        out_specs=[tok_spec(CONV_DIM), tok_spec(D_MODEL), tok_spec(GB_LANES), tok_spec(D_MODEL),
                   tok_spec(KV_B), tok_spec(KV_B), tok_spec(2 * D_MODEL),
                   pl.BlockSpec((bt, CONV_W - 1, CONV_DIM), lambda i, t: (i, 0, 0))],
        out_shape=[tok(CONV_DIM), tok(D_MODEL), tok(GB_LANES), tok(D_MODEL, BF16),
                   tok(KV_B), tok(KV_B), tok(2 * D_MODEL),
                   jax.ShapeDtypeStruct((b, CONV_W - 1, CONV_DIM), F32)],
        scratch_shapes=[pltpu.VMEM((bt, tt + 8, CONV_DIM), F32)],
        compiler_params=_params(2),
        name="front",
    )(x, mod, conv_state, wm, wba, wconv, gpar)


def _delta_masks(r, block):
    i = lax.broadcasted_iota(jnp.int32, (r, r), 0)
    j = lax.broadcasted_iota(jnp.int32, (r, r), 1)
    shift = int(math.log2(block))
    same = (i >> shift) == (j >> shift)
    levels = [((i >> (s + 1)) == (j >> (s + 1))) & ((i >> s) != (j >> s)) for s in range(shift)]
    return dict(tri=same & (i >= j), strict=same & (i > j), upper=same & (i < j), eye=(i == j).astype(F32),
                levels=levels)


def _split_dot(a01, x):
    a01 = a01.astype(BF16)
    hi = x.astype(BF16)
    rest = x - hi.astype(F32)
    mid = rest.astype(BF16)
    lo = (rest - mid.astype(F32)).astype(BF16)
    return _dot(a01, hi) + _dot(a01, mid) + _dot(a01, lo)


def _decay_sums(g, masks):
    return _split_dot(masks["tri"], g), _split_dot(masks["upper"], g)


def _delta_pre(qr, kr, v, beta_b, gc, rev, gc_row, masks):
    tri, strict, eye, levels = masks["tri"], masks["strict"], masks["eye"], masks["levels"]
    r = qr.shape[0]
    q = qr * lax.rsqrt(jnp.sum(qr * qr, -1, keepdims=True) + 1e-6) * (DK_A ** -0.5)
    k = kr * lax.rsqrt(jnp.sum(kr * kr, -1, keepdims=True) + 1e-6)
    gc_col = gc[:, :r] if r <= DK_A else jnp.concatenate([gc] * (r // DK_A), axis=1)
    decay = jnp.exp(jnp.where(tri, gc_col - gc_row, -jnp.inf))
    eg = jnp.exp(gc)
    kb = k * beta_b
    kk = _dot_nt(jnp.concatenate([kb, q], axis=0), k)
    mm = jnp.where(strict, kk[:r] * decay, 0.0)
    qk = kk[r:] * decay
    t = eye - jnp.where(levels[0], mm, 0.0)
    for lvl in levels[1:]:
        t = t - _dot(_dot(t, jnp.where(lvl, mm, 0.0)), t)
    uw = _dot(t, jnp.concatenate([v * beta_b, kb * eg], axis=1))
    u = uw[:, :DK_A]
    w = uw[:, DK_A:]
    return u, w, q * eg, k * jnp.exp(rev), qk


def _gated_norm(o, z, w_onorm):
    o = o * lax.rsqrt(jnp.mean(o * o, -1, keepdims=True) + RMS_EPS) * w_onorm
    return o * _silu(z)


def _delta_prompt_body(cs_ref, z_ref, gb_ref, wn_ref, o_ref, s_out_ref,
                       s_ref, u_s, w_s, qg_s, kd_s, qk_s, egl_s, *, tt):
    n_c = tt // CHUNK
    masks = _delta_masks(tt, CHUNK)

    @pl.when(pl.program_id(1) == 0)
    def _():
        s_ref[...] = jnp.zeros(s_ref.shape, F32)

    gbv = gb_ref[...]
    gc_all, rev_all = _decay_sums(gbv, masks)
    gc_t = gc_all.T
    for hd in range(N_HEADS_A):
        cols = slice(hd * DK_A, (hd + 1) * DK_A)
        lg = N_HEADS_A + hd
        beta_b = jnp.broadcast_to(gbv[:, hd:hd + 1], (tt, DK_A))
        gc = jnp.broadcast_to(gc_all[:, lg:lg + 1], (tt, DK_A))
        rev = jnp.broadcast_to(rev_all[:, lg:lg + 1], (tt, DK_A))
        gc_row = jnp.broadcast_to(gc_t[lg:lg + 1, :], (tt, tt))
        u, w, qg, kd, qk = _delta_pre(cs_ref[:, cols], cs_ref[:, D_MODEL + hd * DK_A:D_MODEL + (hd + 1) * DK_A],
                                      cs_ref[:, 2 * D_MODEL + hd * DK_A:2 * D_MODEL + (hd + 1) * DK_A],
                                      beta_b, gc, rev, gc_row, masks)
        u_s[:, cols] = u
        w_s[:, cols] = w
        qg_s[:, cols] = qg
        kd_s[:, cols] = kd
        for c in range(n_c):
            blk = slice(c * CHUNK, (c + 1) * CHUNK)
            qk_s[hd * n_c + c] = qk[blk, blk]
            last = (c + 1) * CHUNK - 1
            egl_s[hd * n_c + c] = jnp.broadcast_to(jnp.exp(gc[last:last + 1, :]), (8, DK_A))

    def scan(c, carry):
        rows = pl.ds(pl.multiple_of(c * CHUNK, CHUNK), CHUNK)
        for hd in range(N_HEADS_A):
            cols = slice(hd * DK_A, (hd + 1) * DK_A)
            s = s_ref[hd]
            ws = _dot(jnp.concatenate([w_s[rows, cols], qg_s[rows, cols]], axis=0), s)
            v_new = u_s[rows, cols] - ws[:CHUNK]
            o = ws[CHUNK:] + _dot(qk_s[hd * n_c + c], v_new)
            s_ref[hd] = s * egl_s[hd * n_c + c][0:1, :] + _dot_tn(kd_s[rows, cols], v_new)
            o_ref[rows, cols] = _gated_norm(o, z_ref[rows, cols], wn_ref[...]).astype(BF16)
        return carry

    lax.fori_loop(0, n_c, scan, 0)

    @pl.when(pl.program_id(1) == pl.num_programs(1) - 1)
    def _():
        s_out_ref[...] = s_ref[...]


def _delta_prompt(cs, z, gb, w_onorm, *, tt):
    b, l, _ = cs.shape
    n_c = tt // CHUNK
    tok_spec = lambda n: pl.BlockSpec((None, tt, n), lambda i, t: (i, t, 0))
    return pl.pallas_call(
        functools.partial(_delta_prompt_body, tt=tt),
        grid=(b, l // tt),
        in_specs=[tok_spec(CONV_DIM), tok_spec(D_MODEL), tok_spec(GB_LANES), _const_spec((1, DK_A))],
        out_specs=[tok_spec(D_MODEL),
                   pl.BlockSpec((None, N_HEADS_A, DK_A, DK_A), lambda i, t: (i, 0, 0, 0))],
        out_shape=[jax.ShapeDtypeStruct((b, l, D_MODEL), BF16),
                   jax.ShapeDtypeStruct((b, N_HEADS_A, DK_A, DK_A), F32)],
        scratch_shapes=[pltpu.VMEM((N_HEADS_A, DK_A, DK_A), F32)]
        + [pltpu.VMEM((tt, D_MODEL), F32)] * 4
        + [pltpu.VMEM((N_HEADS_A * n_c, CHUNK, CHUNK), F32), pltpu.VMEM((N_HEADS_A * n_c, 8, DK_A), F32)],
        compiler_params=_params(2),
        name="delta_prompt",
    )(cs, z, gb, w_onorm)


def _delta_sample_body(cs_ref, z_ref, gb_ref, wn_ref, s0_ref, o_ref, s_out_ref, *, bt, ls):
    r = N_HEADS_A * ls
    masks = _delta_masks(r, ls)

    def per_batch(bi, carry):
        cs = cs_ref[bi]
        gbv = gb_ref[bi]
        stack = lambda base: jnp.concatenate(
            [cs[:, base + hd * DK_A: base + (hd + 1) * DK_A] for hd in range(N_HEADS_A)], axis=0)
        beta_b = jnp.concatenate(
            [jnp.broadcast_to(gbv[:, hd:hd + 1], (ls, DK_A)) for hd in range(N_HEADS_A)], axis=0)
        g_b = jnp.concatenate(
            [jnp.broadcast_to(gbv[:, N_HEADS_A + hd:N_HEADS_A + hd + 1], (ls, DK_A)) for hd in range(N_HEADS_A)],
            axis=0)
        gc, rev = _decay_sums(g_b, masks)
        egl = jnp.exp(gc)
        u, w, qg, kd, qk = _delta_pre(stack(0), stack(D_MODEL), stack(2 * D_MODEL), beta_b, gc, rev, gc.T[:r, :], masks)
        ws, qs = [], []
        for hd in range(N_HEADS_A):
            rows = slice(hd * ls, (hd + 1) * ls)
            both = _dot(jnp.concatenate([w[rows], qg[rows]], axis=0), s0_ref[bi, hd])
            ws.append(both[:ls])
            qs.append(both[ls:])
        v_new = u - jnp.concatenate(ws, axis=0)
        o = jnp.concatenate(qs, axis=0) + _dot(qk, v_new)
        zb = z_ref[bi]
        for hd in range(N_HEADS_A):
            rows = slice(hd * ls, (hd + 1) * ls)
            cols = slice(hd * DK_A, (hd + 1) * DK_A)
            last = (hd + 1) * ls - 1
            s_out_ref[bi, hd] = s0_ref[bi, hd] * egl[last:last + 1, :] + _dot_tn(kd[rows], v_new[rows])
            o_ref[bi, :, cols] = _gated_norm(o[rows], zb[:, cols], wn_ref[...]).astype(BF16)
        return carry

    lax.fori_loop(0, bt, per_batch, 0)


def _delta_sample(cs, z, gb, w_onorm, s0, *, bt):
    b, ls, _ = cs.shape
    tok_spec = lambda n: pl.BlockSpec((bt, ls, n), lambda i: (i, 0, 0))
    st_spec = pl.BlockSpec((bt, N_HEADS_A, DK_A, DK_A), lambda i: (i, 0, 0, 0))
    return pl.pallas_call(
        functools.partial(_delta_sample_body, bt=bt, ls=ls),
        grid=(b // bt,),
        in_specs=[tok_spec(CONV_DIM), tok_spec(D_MODEL), tok_spec(GB_LANES), _const_spec((1, DK_A)), st_spec],
        out_specs=[tok_spec(D_MODEL), st_spec],
        out_shape=[jax.ShapeDtypeStruct((b, ls, D_MODEL), BF16),
                   jax.ShapeDtypeStruct((b, N_HEADS_A, DK_A, DK_A), F32)],
        compiler_params=_params(1),
        name="delta_sample",
    )(cs, z, gb, w_onorm, s0)


def _softmax_pv(s, sink, v):
    mx = jnp.maximum(jnp.max(s, -1, keepdims=True), sink)
    p = jnp.exp(s - mx)
    denom = jnp.sum(p, -1, keepdims=True) + jnp.exp(sink - mx)
    return _dot((p / denom).astype(BF16), v)


def _swa_prompt_body(q_ref, kp_ref, kc_ref, vp_ref, vc_ref, bias_ref, o_ref):
    n_q = GROUP_B * WINDOW
    key = lax.broadcasted_iota(jnp.int32, (2 * WINDOW, n_q), 0)
    no_prev = (pl.program_id(1) == 0) & (key >= 1) & (key < WINDOW)
    k2 = jnp.concatenate([kp_ref[...], kc_ref[...]], axis=0)
    v2 = jnp.concatenate([vp_ref[...], vc_ref[...]], axis=0)
    krow = lax.broadcasted_iota(jnp.int32, v2.shape, 0)
    k2 = jnp.where(krow == 0, 0.0, k2).astype(BF16)
    v2t = jnp.where(krow == 0, 0.0, v2).T.astype(BF16)
    for kvh in range(N_KV_B):
        heads = range(kvh * GROUP_B, (kvh + 1) * GROUP_B)
        kv_cols = slice(kvh * HD_B, (kvh + 1) * HD_B)
        q = jnp.concatenate([q_ref[:, hh * HD_B:(hh + 1) * HD_B] for hh in heads], axis=0)
        s = _dot_nt(k2[:, kv_cols], q) + bias_ref[kvh]
        s = jnp.where(no_prev, -jnp.inf, s)
        p = jnp.exp(s - jnp.max(s, axis=0, keepdims=True))
        pn = (p / jnp.sum(p, axis=0, keepdims=True)).astype(BF16)
        ot = _dot(v2t[kv_cols, :], pn)
        for g, hh in enumerate(heads):
            o_ref[:, hh * HD_B:(hh + 1) * HD_B] = ot[:, g * WINDOW:(g + 1) * WINDOW].T.astype(BF16)


def _swa_prompt(qb, k, v, bias_t):
    b, l, _ = qb.shape
    cur = lambda n: pl.BlockSpec((None, WINDOW, n), lambda i, t: (i, t, 0))
    prev = lambda n: pl.BlockSpec((None, WINDOW, n), lambda i, t: (i, jnp.maximum(t - 1, 0), 0))
    return pl.pallas_call(
        _swa_prompt_body,
        grid=(b, l // WINDOW),
        in_specs=[cur(D_MODEL), prev(KV_B), cur(KV_B), prev(KV_B), cur(KV_B),
                  _const_spec((N_KV_B, 2 * WINDOW, GROUP_B * WINDOW))],
        out_specs=cur(D_MODEL),
        out_shape=jax.ShapeDtypeStruct((b, l, D_MODEL), BF16),
        compiler_params=_params(2),
        name="swa_prompt",
    )(qb, k, k, v, v, bias_t)


def _swa_sample_body(q_ref, kn_ref, vn_ref, kc_ref, vc_ref, bias_ref, sink_ref, o_ref, ko_ref, vo_ref, *, bt, ls):
    def per_batch(bi, carry):
        q = q_ref[bi]
        kf = jnp.concatenate([kc_ref[bi], kn_ref[bi]], axis=0)
        vf = jnp.concatenate([vc_ref[bi], vn_ref[bi]], axis=0)
        ko_ref[bi] = kf[ls:, :]
        vo_ref[bi] = vf[ls:, :]
        k2 = kf.astype(BF16)
        v2 = vf.astype(BF16)
        for kvh in range(N_KV_B):
            kv_cols = slice(kvh * HD_B, (kvh + 1) * HD_B)
            qg = jnp.concatenate([q[:, hh * HD_B:(hh + 1) * HD_B]
                                  for hh in range(kvh * GROUP_B, (kvh + 1) * GROUP_B)], axis=0)
            rows = slice(kvh * GROUP_B * ls, (kvh + 1) * GROUP_B * ls)
            s = _dot_nt(qg, k2[:, kv_cols]) + bias_ref[rows, :]
            o = _softmax_pv(s, sink_ref[rows, 0:1], v2[:, kv_cols])
            for g in range(GROUP_B):
                hh = kvh * GROUP_B + g
                o_ref[bi, :, hh * HD_B:(hh + 1) * HD_B] = o[g * ls:(g + 1) * ls].astype(BF16)
        return carry

    lax.fori_loop(0, bt, per_batch, 0)


def _swa_sample(qb, k_new, v_new, k_cache, v_cache, bias_s, sink_s, *, bt):
    b, ls, _ = qb.shape
    n_keys = WINDOW + ls
    new = lambda n: pl.BlockSpec((bt, ls, n), lambda i: (i, 0, 0))
    cache = pl.BlockSpec((bt, WINDOW, KV_B), lambda i: (i, 0, 0))
    return pl.pallas_call(
        functools.partial(_swa_sample_body, bt=bt, ls=ls),
        grid=(b // bt,),
        in_specs=[new(D_MODEL), new(KV_B), new(KV_B), cache, cache,
                  _const_spec((N_HEADS_B * ls, n_keys)), _const_spec((N_HEADS_B * ls, 128))],
        out_specs=[new(D_MODEL), cache, cache],
        out_shape=[jax.ShapeDtypeStruct((b, ls, D_MODEL), BF16),
                   jax.ShapeDtypeStruct((b, WINDOW, KV_B), F32),
                   jax.ShapeDtypeStruct((b, WINDOW, KV_B), F32)],
        compiler_params=_params(1),
        name="swa_sample",
    )(qb, k_new, v_new, k_cache, v_cache, bias_s, sink_s)


def _layer_norm(y, g, b):
    mu = jnp.mean(y, -1, keepdims=True)
    d = y - mu
    var = jnp.mean(d * d, -1, keepdims=True)
    return d * lax.rsqrt(var + LN_EPS) * g + b


def _merge_body(x_ref, oa_ref, ob_ref, gates_ref, mod_ref, wpa_ref, wpb_ref, wout_ref, lng_ref, lnb_ref,
                o_ref, *, bt, tt, alpha):
    m = bt * tt
    ga = gates_ref[:, :, 0:D_MODEL].reshape(m, D_MODEL)
    gb = gates_ref[:, :, D_MODEL:2 * D_MODEL].reshape(m, D_MODEL)
    mixed = ga * _dot(oa_ref[...].reshape(m, D_MODEL), wpa_ref[...]) \
        + gb * _dot(ob_ref[...].reshape(m, D_MODEL), wpb_ref[...])
    attn = _dot(mixed.astype(BF16), wout_ref[...]).reshape(bt, tt, D_MODEL)
    gt1 = mod_ref[:, :, 2 * D_MODEL:3 * D_MODEL]
    o_ref[...] = _layer_norm(alpha * x_ref[...] + gt1 * attn, lng_ref[...], lnb_ref[...])


def _merge(x, oa, ob, gates, mod, wpa, wpb, wout, lng, lnb, *, bt, tt, alpha):
    b, l, _ = x.shape
    tok_spec = lambda n: pl.BlockSpec((bt, tt, n), lambda i, t: (i, t, 0))
    return pl.pallas_call(
        functools.partial(_merge_body, bt=bt, tt=tt, alpha=alpha),
        grid=(b // bt, l // tt),
        in_specs=[tok_spec(D_MODEL), tok_spec(D_MODEL), tok_spec(D_MODEL), tok_spec(2 * D_MODEL),
                  pl.BlockSpec((bt, 1, 6 * D_MODEL), lambda i, t: (i, 0, 0)),
                  _const_spec((D_MODEL, D_MODEL)), _const_spec((D_MODEL, D_MODEL)), _const_spec((D_MODEL, D_MODEL)),
                  _const_spec((1, D_MODEL)), _const_spec((1, D_MODEL))],
        out_specs=tok_spec(D_MODEL),
        out_shape=jax.ShapeDtypeStruct((b, l, D_MODEL), F32),
        compiler_params=_params(2),
        name="merge",
    )(x, oa, ob, gates, mod, wpa, wpb, wout, lng, lnb)


def _mlp_body(x_ref, mod_ref, wup_ref, wdn_ref, lng_ref, lnb_ref, o_ref, *, bt, tt, alpha):
    m = bt * tt
    x = x_ref[...]
    sh2 = mod_ref[:, :, 3 * D_MODEL:4 * D_MODEL]
    sc2 = mod_ref[:, :, 4 * D_MODEL:5 * D_MODEL]
    gt2 = mod_ref[:, :, 5 * D_MODEL:6 * D_MODEL]
    h2 = (x * (1.0 + sc2) + sh2).reshape(m, D_MODEL).astype(BF16)
    ff = jnp.zeros((m, D_MODEL), F32)
    for c in range(D_FF // D_MODEL):
        cols = slice(c * D_MODEL, (c + 1) * D_MODEL)
        a = jnp.maximum(_dot(h2, wup_ref[:, cols]), 0.0)
        ff = ff + _dot((a * a).astype(BF16), wdn_ref[cols, :])
    o_ref[...] = _layer_norm(alpha * x + gt2 * ff.reshape(bt, tt, D_MODEL), lng_ref[...], lnb_ref[...])


def _mlp(x, mod, wup, wdn, lng, lnb, *, bt, tt, alpha):
    b, l, _ = x.shape
    tok_spec = pl.BlockSpec((bt, tt, D_MODEL), lambda i, t: (i, t, 0))
    return pl.pallas_call(
        functools.partial(_mlp_body, bt=bt, tt=tt, alpha=alpha),
        grid=(b // bt, l // tt),
        in_specs=[tok_spec, pl.BlockSpec((bt, 1, 6 * D_MODEL), lambda i, t: (i, 0, 0)),
                  _const_spec((D_MODEL, D_FF)), _const_spec((D_FF, D_MODEL)),
                  _const_spec((1, D_MODEL)), _const_spec((1, D_MODEL))],
        out_specs=tok_spec,
        out_shape=jax.ShapeDtypeStruct((b, l, D_MODEL), F32),
        compiler_params=_params(2),
        name="mlp",
    )(x, mod, wup, wdn, lng, lnb)


def _tiles(b, l, rows):
    tt = min(l, rows)
    bt = max(1, min(b, rows // tt))
    return bt, tt


def _layer_weights(l, w_in, w_conv, a_log, dt_bias, w_onorm, sinks, w_pa, w_pb, w_out,
                   ln1_g, ln1_b, w_up, w_down, ln2_g, ln2_b):
    win = w_in[l]
    o_ba = CONV_DIM + D_MODEL
    o_qb = o_ba + 2 * N_HEADS_A
    wm = jnp.concatenate([win[:, :o_ba], win[:, o_qb:]], axis=1).astype(BF16)
    wba = jnp.pad(win[:, o_ba:o_qb], ((0, 0), (0, GB_LANES - 2 * N_HEADS_A))).astype(BF16)
    gpar = jnp.zeros((2, GB_LANES), F32)
    gpar = gpar.at[0, N_HEADS_A:2 * N_HEADS_A].set(-jnp.exp(a_log[l].astype(F32)))
    gpar = gpar.at[1, N_HEADS_A:2 * N_HEADS_A].set(dt_bias[l].astype(F32))
    row = lambda a: a[l].reshape(1, -1).astype(F32)
    return dict(wm=wm, wba=wba, wconv=w_conv[l].astype(F32), gpar=gpar, w_onorm=row(w_onorm), sinks=sinks[l].astype(F32),
                wpa=w_pa[l].astype(BF16), wpb=w_pb[l].astype(BF16), wout=w_out[l].astype(BF16),
                ln1_g=row(ln1_g), ln1_b=row(ln1_b), wup=w_up[l].astype(BF16), wdn=w_down[l].astype(BF16),
                ln2_g=row(ln2_g), ln2_b=row(ln2_b))


def _trunk_layer(x, mod, p, bias_tab, alpha, conv_state, s0, k_cache, v_cache):
    b, l, _ = x.shape
    prompt = k_cache is None
    bt, tt = _tiles(b, l, 256)
    cs, z, gb, qb, k, v, gates, conv_out = _front(x, mod, conv_state, p["wm"], p["wba"], p["wconv"], p["gpar"],
                                                  bt=bt, tt=tt)
    if prompt:
        oa, s_new = _delta_prompt(cs, z, gb, p["w_onorm"], tt=min(l, 256))
        ob = _swa_prompt(qb, k, v, p["bias_t"])
        k_new, v_new = k[:, -WINDOW:], v[:, -WINDOW:]
    else:
        oa, s_new = _delta_sample(cs, z, gb, p["w_onorm"], s0, bt=min(b, 8))
        bias_s = bias_tab[:, :l, :WINDOW + l].reshape(N_HEADS_B * l, WINDOW + l)
        sink_s = jnp.broadcast_to(jnp.repeat(p["sinks"], l)[:, None], (N_HEADS_B * l, 128))
        ob, k_new, v_new = _swa_sample(qb, k, v, k_cache.reshape(b, WINDOW, KV_B), v_cache.reshape(b, WINDOW, KV_B),
                                       bias_s, sink_s, bt=min(b, 8))
    bt, tt = _tiles(b, l, 512)
    x1 = _merge(x, oa, ob, gates, mod, p["wpa"], p["wpb"], p["wout"], p["ln1_g"], p["ln1_b"], bt=bt, tt=tt, alpha=alpha)
    x2 = _mlp(x1, mod, p["wup"], p["wdn"], p["ln2_g"], p["ln2_b"], bt=bt, tt=tt, alpha=alpha)
    return (x2, s_new, conv_out, k_new.reshape(b, WINDOW, N_KV_B, HD_B), v_new.reshape(b, WINDOW, N_KV_B, HD_B))


def kernel(x_prompt, x_sample, state_delta, state_conv, cache_k, cache_v, c_prompt, c_sample, rel_bias, w_ada, b_ada, w_in, w_conv, a_log, dt_bias, w_onorm, sinks, w_pa, w_pb, w_out, ln1_g, ln1_b, w_up, w_down, ln2_g, ln2_b):
    depth = w_in.shape[0]
    alpha = (2 * depth) ** 0.25
    bp = x_prompt.shape[0]
    mod_all = _ada(jnp.concatenate([c_prompt, c_sample], axis=0), w_ada, b_ada)
    bias_tab = _bias_table(rel_bias)
    bias_t = _bias_table_t(rel_bias, sinks)
    yp, ys = x_prompt, x_sample
    outs = [[] for _ in range(8)]
    for l in range(depth):
        p = _layer_weights(l, w_in, w_conv, a_log, dt_bias, w_onorm, sinks, w_pa, w_pb, w_out,
                           ln1_g, ln1_b, w_up, w_down, ln2_g, ln2_b)
        p["bias_t"] = bias_t[l]
        mod_p = mod_all[l, :bp][:, None, :]
        mod_s = mod_all[l, bp:][:, None, :]
        zero_conv = jnp.zeros((bp, CONV_W - 1, CONV_DIM), x_prompt.dtype)
        yp, *rest_p = _trunk_layer(yp, mod_p, p, bias_tab, alpha, zero_conv, None, None, None)
        ys, *rest_s = _trunk_layer(ys, mod_s, p, bias_tab, alpha, state_conv[l], state_delta[l], cache_k[l], cache_v[l])
        for acc, val in zip(outs, rest_p + rest_s):
            acc.append(val)
    return (yp, ys) + tuple(jnp.stack(a) for a in outs)
```

```python
import functools
import math

import numpy as np
import jax
import jax.numpy as jnp
from jax import lax
from jax.experimental import pallas as pl
from jax.experimental.pallas import tpu as pltpu

F32 = jnp.float32
BF16 = jnp.bfloat16

D_MODEL = 1024
N_HEADS_A = 8
DK_A = 128
CONV_W = 4
CONV_DIM = 3 * D_MODEL
CHUNK = 64
HD_B = 64
N_HEADS_B = 16
N_KV_B = 4
GROUP_B = N_HEADS_B // N_KV_B
KV_B = N_KV_B * HD_B
WINDOW = 128
N_BUCKETS = 32
MAX_DISTANCE = 128
D_FF = 4 * D_MODEL
LN_EPS = 1e-5
RMS_EPS = 1e-6

OFF_Z = CONV_DIM
OFF_QB = OFF_Z + D_MODEL
OFF_KVB = OFF_QB + D_MODEL
OFF_GATES = OFF_KVB + 2 * KV_B
N_MAIN = OFF_GATES + 2 * D_MODEL
GB_LANES = 128

V7X_VMEM_BYTES = 64 * 1024 * 1024
VMEM_LIMIT = V7X_VMEM_BYTES - 8 * 1024 * 1024


def _params(n_grid):
    return pltpu.CompilerParams(dimension_semantics=("arbitrary",) * n_grid, vmem_limit_bytes=VMEM_LIMIT)


def _dot(a, b):
    return jnp.dot(a, b, preferred_element_type=F32)


def _dot_nt(a, b):
    return lax.dot_general(a, b, (((1,), (1,)), ((), ())), preferred_element_type=F32)


def _dot_tn(a, b):
    return lax.dot_general(a, b, (((0,), (0,)), ((), ())), preferred_element_type=F32)


def _sigmoid(x):
    return 1.0 / (1.0 + jnp.exp(-x))


def _silu(x):
    return x * _sigmoid(x)


def _const_spec(shape):
    nd = len(shape)
    return pl.BlockSpec(shape, lambda *_: (0,) * nd, pipeline_mode=pl.Buffered(1))


def _ada_body(c_ref, w_ref, b_ref, o_ref):
    s = _silu(c_ref[...]).astype(BF16)
    o_ref[...] = _dot(s, w_ref[...].astype(BF16)) + b_ref[...]


def _ada(c_all, w_ada, b_ada):
    depth = w_ada.shape[0]
    n_rows = c_all.shape[0]
    n_col = w_ada.shape[2] // D_MODEL
    return pl.pallas_call(
        _ada_body,
        grid=(depth, n_col),
        in_specs=[pl.BlockSpec((n_rows, D_MODEL), lambda l, n: (0, 0)),
                  pl.BlockSpec((None, D_MODEL, D_MODEL), lambda l, n: (l, 0, n)),
                  pl.BlockSpec((None, 1, D_MODEL), lambda l, n: (l, 0, n))],
        out_specs=pl.BlockSpec((None, n_rows, D_MODEL), lambda l, n: (l, 0, n)),
        out_shape=jax.ShapeDtypeStruct((depth, n_rows, w_ada.shape[2]), F32),
        compiler_params=_params(2),
        name="ada",
    )(c_all, w_ada, b_ada.reshape(depth, 1, -1))


def _bucket_table():
    r = np.arange(WINDOW)[:, None]
    c = np.arange(2 * WINDOW)[None, :]
    dist = WINDOW + r - c
    n = np.maximum(dist, 0)
    max_exact = N_BUCKETS // 2
    ratio = np.maximum(n, max_exact).astype(np.float32) / np.float32(max_exact)
    large = max_exact + (np.log(ratio) / np.float32(math.log(MAX_DISTANCE / max_exact))
                         * np.float32(N_BUCKETS - max_exact)).astype(np.int32)
    large = np.minimum(large, N_BUCKETS - 1)
    bucket = np.where(n < max_exact, n, large).astype(np.int32)
    valid = ((dist >= 0) & (dist < WINDOW)).astype(np.int32)
    return bucket, valid


def _bias_t_body(rb_ref, sink_ref, bucket_ref, valid_ref, o_ref):
    l = pl.program_id(0)
    h = pl.program_id(1)
    bucket = bucket_ref[...]
    acc = jnp.zeros(bucket.shape, F32)
    for j in range(N_BUCKETS):
        acc = jnp.where(bucket == j, rb_ref[j, h], acc)
    acc = jnp.where(valid_ref[...] > 0, acc, -jnp.inf)
    key = lax.broadcasted_iota(jnp.int32, bucket.shape, 0)
    o_ref[...] = jnp.where(key == 0, sink_ref[l, h], acc)


def _bias_table_t(rel_bias, sinks):
    depth = sinks.shape[0]
    bucket, valid = _bucket_table()
    return pl.pallas_call(
        _bias_t_body,
        grid=(depth, N_HEADS_B),
        in_specs=[pl.BlockSpec(memory_space=pltpu.SMEM), pl.BlockSpec(memory_space=pltpu.SMEM),
                  pl.BlockSpec((2 * WINDOW, WINDOW), lambda l, h: (0, 0)),
                  pl.BlockSpec((2 * WINDOW, WINDOW), lambda l, h: (0, 0))],
        out_specs=pl.BlockSpec((None, None, 2 * WINDOW, WINDOW), lambda l, h: (l, h // GROUP_B, 0, h % GROUP_B)),
        out_shape=jax.ShapeDtypeStruct((depth, N_KV_B, 2 * WINDOW, GROUP_B * WINDOW), F32),
        compiler_params=_params(2),
        name="bias_table_t",
    )(rel_bias.astype(F32), sinks.astype(F32), jnp.asarray(bucket.T.copy()), jnp.asarray(valid.T.copy()))


def _front_body(x_ref, mod_ref, st_ref, wm_ref, wba_ref, wconv_ref, gpar_ref,
                cs_ref, z_ref, gb_ref, qb_ref, k_ref, v_ref, gates_ref, cst_ref, cbuf, *, bt, tt):
    m = bt * tt
    sh1 = mod_ref[:, :, 0:D_MODEL]
    sc1 = mod_ref[:, :, D_MODEL:2 * D_MODEL]
    h = (x_ref[...] * (1.0 + sc1) + sh1).reshape(m, D_MODEL).astype(BF16)

    @pl.when(pl.program_id(1) == 0)
    def _():
        cbuf[:, 5:8, :] = st_ref[...]

    for s in range(3):
        cols = slice(s * D_MODEL, (s + 1) * D_MODEL)
        cbuf[:, 8:8 + tt, cols] = _dot(h, wm_ref[:, cols]).reshape(bt, tt, D_MODEL)
        y = cbuf[:, 5:5 + tt, cols] * wconv_ref[0:1, cols]
        for j in range(1, CONV_W):
            y = y + cbuf[:, 5 + j:5 + j + tt, cols] * wconv_ref[j:j + 1, cols]
        cs_ref[:, :, cols] = _silu(y)
    tail = cbuf[:, tt + 5:tt + 8, :]
    cst_ref[...] = tail
    cbuf[:, 5:8, :] = tail

    z_ref[...] = _dot(h, wm_ref[:, OFF_Z:OFF_QB]).reshape(bt, tt, D_MODEL)
    qb_ref[...] = (_dot(h, wm_ref[:, OFF_QB:OFF_KVB]) * (HD_B ** -0.5)).astype(BF16).reshape(bt, tt, D_MODEL)
    kv = _dot(h, wm_ref[:, OFF_KVB:OFF_GATES])
    k_ref[...] = kv[:, :KV_B].reshape(bt, tt, KV_B)
    v_ref[...] = kv[:, KV_B:].reshape(bt, tt, KV_B)
    gates_ref[...] = _sigmoid(_dot(h, wm_ref[:, OFF_GATES:N_MAIN])).reshape(bt, tt, 2 * D_MODEL)

    ba = _dot(h, wba_ref[...])
    xg = ba + gpar_ref[1:2, :]
    softplus = jnp.maximum(xg, 0.0) + jnp.log(1.0 + jnp.exp(-jnp.abs(xg)))
    lane = lax.broadcasted_iota(jnp.int32, ba.shape, 1)
    gb = jnp.where(lane < N_HEADS_A, _sigmoid(ba), gpar_ref[0:1, :] * softplus)
    gb_ref[...] = gb.reshape(bt, tt, GB_LANES)


def _front(x, mod, conv_state, wm, wba, wconv, gpar, *, bt, tt):
    b, l, _ = x.shape
    grid = (b // bt, l // tt)
    tok = lambda n, dt=F32: jax.ShapeDtypeStruct((b, l, n), dt)
    tok_spec = lambda n: pl.BlockSpec((bt, tt, n), lambda i, t: (i, t, 0))
    return pl.pallas_call(
        functools.partial(_front_body, bt=bt, tt=tt),
        grid=grid,
        in_specs=[tok_spec(D_MODEL),
                  pl.BlockSpec((bt, 1, 6 * D_MODEL), lambda i, t: (i, 0, 0)),
                  pl.BlockSpec((bt, CONV_W - 1, CONV_DIM), lambda i, t: (i, 0, 0)),
                  _const_spec((D_MODEL, N_MAIN)),
                  _const_spec((D_MODEL, GB_LANES)),
                  _const_spec((CONV_W, CONV_DIM)),
                  _const_spec((2, GB_LANES))],
        out_specs=[tok_spec(CONV_DIM), tok_spec(D_MODEL), tok_spec(GB_LANES), tok_spec(D_MODEL),
                   tok_spec(KV_B), tok_spec(KV_B), tok_spec(2 * D_MODEL),
                   pl.BlockSpec((bt, CONV_W - 1, CONV_DIM), lambda i, t: (i, 0, 0))],
        out_shape=[tok(CONV_DIM), tok(D_MODEL), tok(GB_LANES), tok(D_MODEL, BF16),
                   tok(KV_B), tok(KV_B), tok(2 * D_MODEL),
                   jax.ShapeDtypeStruct((b, CONV_W - 1, CONV_DIM), F32)],
        scratch_shapes=[pltpu.VMEM((bt, tt + 8, CONV_DIM), F32)],
        compiler_params=_params(2),
        name="front",
    )(x, mod, conv_state, wm, wba, wconv, gpar)


def _delta_masks(r, block):
    i = lax.broadcasted_iota(jnp.int32, (r, r), 0)
    j = lax.broadcasted_iota(jnp.int32, (r, r), 1)
    shift = int(math.log2(block))
    same = (i >> shift) == (j >> shift)
    levels = [((i >> (s + 1)) == (j >> (s + 1))) & ((i >> s) != (j >> s)) for s in range(shift)]
    return dict(tri=same & (i >= j), strict=same & (i > j), upper=same & (i < j), eye=(i == j).astype(F32),
                levels=levels)


def _split_dot(a01, x):
    a01 = a01.astype(BF16)
    hi = x.astype(BF16)
    rest = x - hi.astype(F32)
    mid = rest.astype(BF16)
    lo = (rest - mid.astype(F32)).astype(BF16)
    return _dot(a01, hi) + _dot(a01, mid) + _dot(a01, lo)


def _decay_sums(g, masks):
    return _split_dot(masks["tri"], g), _split_dot(masks["upper"], g)


def _delta_pre(problems, masks):
    tri, strict, eye, levels = masks["tri"], masks["strict"], masks["eye"], masks["levels"]
    n = len(problems)
    r = problems[0][0].shape[0]
    lhs, ks, rhs, decays, qgs, kds = [], [], [], [], [], []
    for qr, kr, v, beta_b, gc, rev, gc_row in problems:
        q = qr * lax.rsqrt(jnp.sum(qr * qr, -1, keepdims=True) + 1e-6) * (DK_A ** -0.5)
        k = kr * lax.rsqrt(jnp.sum(kr * kr, -1, keepdims=True) + 1e-6)
        gc_col = gc[:, :r] if r <= DK_A else jnp.concatenate([gc] * (r // DK_A), axis=1)
        decays.append(jnp.exp(jnp.where(tri, gc_col - gc_row, -jnp.inf)))
        eg = jnp.exp(gc)
        kb = k * beta_b
        lhs.append(jnp.concatenate([kb, q], axis=0).astype(BF16))
        ks.append(k.astype(BF16))
        rhs.append(jnp.concatenate([v * beta_b, kb * eg], axis=1).astype(BF16))
        qgs.append((q * eg).astype(BF16))
        kds.append((k * jnp.exp(rev)).astype(BF16))
    kk = [_dot_nt(lhs[i], ks[i]) for i in range(n)]
    mm = [jnp.where(strict, kk[i][:r] * decays[i], 0.0) for i in range(n)]
    qk = [(kk[i][r:] * decays[i]).astype(BF16) for i in range(n)]
    t = [eye - jnp.where(levels[0], mm[i], 0.0) for i in range(n)]
    for lvl in levels[1:]:
        tb = [t[i].astype(BF16) for i in range(n)]
        x = [_dot(tb[i], jnp.where(lvl, mm[i], 0.0).astype(BF16)).astype(BF16) for i in range(n)]
        y = [_dot(x[i], tb[i]) for i in range(n)]
        t = [t[i] - y[i] for i in range(n)]
    uw = [_dot(t[i].astype(BF16), rhs[i]) for i in range(n)]
    return [(uw[i][:, :DK_A], uw[i][:, DK_A:].astype(BF16), qgs[i], kds[i], qk[i]) for i in range(n)]


def _gated_norm(o, z, w_onorm):
    o = o * lax.rsqrt(jnp.mean(o * o, -1, keepdims=True) + RMS_EPS) * w_onorm
    return o * _silu(z)


def _delta_prompt_body(cs_ref, z_ref, gb_ref, wn_ref, o_ref, s_out_ref,
                       s_ref, u_s, o_s, wq_s, qkd_s, egl_s, *, tt):
    n_c = tt // CHUNK
    masks = _delta_masks(tt, CHUNK)

    @pl.when(pl.program_id(1) == 0)
    def _():
        s_ref[...] = jnp.zeros(s_ref.shape, F32)

    gbv = gb_ref[...]
    gc_all, rev_all = _decay_sums(gbv, masks)
    gc_t = gc_all.T
    heads = range(N_HEADS_A)
    hcols = [slice(hd * DK_A, (hd + 1) * DK_A) for hd in heads]
    problems = []
    for hd in heads:
        lg = N_HEADS_A + hd
        problems.append((cs_ref[:, hcols[hd]], cs_ref[:, D_MODEL + hd * DK_A:D_MODEL + (hd + 1) * DK_A],
                         cs_ref[:, 2 * D_MODEL + hd * DK_A:2 * D_MODEL + (hd + 1) * DK_A],
                         jnp.broadcast_to(gbv[:, hd:hd + 1], (tt, DK_A)),
                         jnp.broadcast_to(gc_all[:, lg:lg + 1], (tt, DK_A)),
                         jnp.broadcast_to(rev_all[:, lg:lg + 1], (tt, DK_A)),
                         jnp.broadcast_to(gc_t[lg:lg + 1, :], (tt, tt))))
    for hd, (u, w, qg, kd, qk) in enumerate(_delta_pre(problems, masks)):
        u_s[:, hcols[hd]] = u
        kdt = kd.T
        for c in range(n_c):
            blk = slice(c * CHUNK, (c + 1) * CHUNK)
            idx = hd * n_c + c
            wq_s[idx, 0:CHUNK, :] = w[blk]
            wq_s[idx, CHUNK:2 * CHUNK, :] = qg[blk]
            qkd_s[idx, 0:CHUNK, :] = qk[blk, blk]
            qkd_s[idx, CHUNK:CHUNK + DK_A, :] = kdt[:, blk]
            last = (c + 1) * CHUNK - 1
            egl_s[idx] = jnp.broadcast_to(jnp.exp(problems[hd][4][last:last + 1, :]), (8, DK_A))

    states = [s_ref[hd] for hd in heads]
    for c in range(n_c):
        blk = slice(c * CHUNK, (c + 1) * CHUNK)
        idx = [hd * n_c + c for hd in heads]
        ws = [_dot(wq_s[idx[hd]], states[hd].astype(BF16)) for hd in heads]
        v_new = [(u_s[blk, hcols[hd]] - ws[hd][:CHUNK]).astype(BF16) for hd in heads]
        upd = [_dot(qkd_s[idx[hd]], v_new[hd]) for hd in heads]
        for hd in heads:
            o_s[blk, hcols[hd]] = ws[hd][CHUNK:] + upd[hd][:CHUNK]
            states[hd] = states[hd] * egl_s[idx[hd]][0:1, :] + upd[hd][CHUNK:]
    for hd in heads:
        s_ref[hd] = states[hd]
        o_ref[:, hcols[hd]] = _gated_norm(o_s[:, hcols[hd]], z_ref[:, hcols[hd]], wn_ref[...]).astype(BF16)

    @pl.when(pl.program_id(1) == pl.num_programs(1) - 1)
    def _():
        s_out_ref[...] = s_ref[...]


def _delta_prompt(cs, z, gb, w_onorm, *, tt):
    b, l, _ = cs.shape
    n_c = tt // CHUNK
    tok_spec = lambda n: pl.BlockSpec((None, tt, n), lambda i, t: (i, t, 0))
    return pl.pallas_call(
        functools.partial(_delta_prompt_body, tt=tt),
        grid=(b, l // tt),
        in_specs=[tok_spec(CONV_DIM), tok_spec(D_MODEL), tok_spec(GB_LANES), _const_spec((1, DK_A))],
        out_specs=[tok_spec(D_MODEL),
                   pl.BlockSpec((None, N_HEADS_A, DK_A, DK_A), lambda i, t: (i, 0, 0, 0))],
        out_shape=[jax.ShapeDtypeStruct((b, l, D_MODEL), BF16),
                   jax.ShapeDtypeStruct((b, N_HEADS_A, DK_A, DK_A), F32)],
        scratch_shapes=[pltpu.VMEM((N_HEADS_A, DK_A, DK_A), F32),
                        pltpu.VMEM((tt, D_MODEL), F32), pltpu.VMEM((tt, D_MODEL), F32),
                        pltpu.VMEM((N_HEADS_A * n_c, 2 * CHUNK, DK_A), BF16),
                        pltpu.VMEM((N_HEADS_A * n_c, CHUNK + DK_A, CHUNK), BF16),
                        pltpu.VMEM((N_HEADS_A * n_c, 8, DK_A), F32)],
        compiler_params=_params(2),
        name="delta_prompt",
    )(cs, z, gb, w_onorm)


def _delta_sample_body(cs_ref, z_ref, gb_ref, wn_ref, s0_ref, o_ref, s_out_ref, *, bt, nb, ls):
    r = N_HEADS_A * ls
    masks = _delta_masks(r, ls)

    heads = range(N_HEADS_A)
    hrows = [slice(hd * ls, (hd + 1) * ls) for hd in heads]
    hcols = [slice(hd * DK_A, (hd + 1) * DK_A) for hd in heads]

    def per_group(gi, carry):
        bis = [gi * nb + d for d in range(nb)]
        problems, egl = [], []
        for bi in bis:
            cs = cs_ref[bi]
            gbv = gb_ref[bi]
            stack = lambda base: jnp.concatenate([cs[:, base + hd * DK_A: base + (hd + 1) * DK_A] for hd in heads],
                                                 axis=0)
            beta_b = jnp.concatenate([jnp.broadcast_to(gbv[:, hd:hd + 1], (ls, DK_A)) for hd in heads], axis=0)
            g_b = jnp.concatenate(
                [jnp.broadcast_to(gbv[:, N_HEADS_A + hd:N_HEADS_A + hd + 1], (ls, DK_A)) for hd in heads], axis=0)
            gc, rev = _decay_sums(g_b, masks)
            egl.append(jnp.exp(gc))
            problems.append((stack(0), stack(D_MODEL), stack(2 * D_MODEL), beta_b, gc, rev, gc.T[:r, :]))
        pre = _delta_pre(problems, masks)
        both = [[_dot(jnp.concatenate([pre[d][1][hrows[hd]], pre[d][2][hrows[hd]]], axis=0),
                      s0_ref[bis[d], hd].astype(BF16)) for hd in heads] for d in range(nb)]
        v_new = [(pre[d][0] - jnp.concatenate([both[d][hd][:ls] for hd in heads], axis=0)).astype(BF16)
                 for d in range(nb)]
        qkv = [_dot(pre[d][4], v_new[d]) for d in range(nb)]
        upd = [[_dot_tn(pre[d][3][hrows[hd]], v_new[d][hrows[hd]]) for hd in heads] for d in range(nb)]
        for d, bi in enumerate(bis):
            zb = z_ref[bi]
            for hd in heads:
                last = (hd + 1) * ls - 1
                s_out_ref[bi, hd] = s0_ref[bi, hd] * egl[d][last:last + 1, :] + upd[d][hd]
                o = both[d][hd][ls:] + qkv[d][hrows[hd]]
                o_ref[bi, :, hcols[hd]] = _gated_norm(o, zb[:, hcols[hd]], wn_ref[...]).astype(BF16)
        return carry

    lax.fori_loop(0, bt // nb, per_group, 0)


def _delta_sample(cs, z, gb, w_onorm, s0, *, bt):
    b, ls, _ = cs.shape
    tok_spec = lambda n: pl.BlockSpec((bt, ls, n), lambda i: (i, 0, 0))
    st_spec = pl.BlockSpec((bt, N_HEADS_A, DK_A, DK_A), lambda i: (i, 0, 0, 0))
    return pl.pallas_call(
        functools.partial(_delta_sample_body, bt=bt, nb=min(bt, 4), ls=ls),
        grid=(b // bt,),
        in_specs=[tok_spec(CONV_DIM), tok_spec(D_MODEL), tok_spec(GB_LANES), _const_spec((1, DK_A)), st_spec],
        out_specs=[tok_spec(D_MODEL), st_spec],
        out_shape=[jax.ShapeDtypeStruct((b, ls, D_MODEL), BF16),
                   jax.ShapeDtypeStruct((b, N_HEADS_A, DK_A, DK_A), F32)],
        compiler_params=_params(1),
        name="delta_sample",
    )(cs, z, gb, w_onorm, s0)


def _softmax_keys(s):
    p = jnp.exp(s - jnp.max(s, axis=0, keepdims=True))
    return (p / jnp.sum(p, axis=0, keepdims=True)).astype(BF16)


def _zero_key0(x):
    return jnp.where(lax.broadcasted_iota(jnp.int32, x.shape, 0) == 0, 0.0, x)


def _kv_cols(kvh):
    return slice(kvh * HD_B, (kvh + 1) * HD_B)


def _group_queries(q, kvh):
    return jnp.concatenate([q[:, hh * HD_B:(hh + 1) * HD_B] for hh in range(kvh * GROUP_B, (kvh + 1) * GROUP_B)],
                           axis=0)


def _swa_prompt_body(q_ref, kp_ref, kc_ref, vp_ref, vc_ref, bias_ref, o_ref):
    n_q = GROUP_B * WINDOW
    key = lax.broadcasted_iota(jnp.int32, (2 * WINDOW, n_q), 0)
    no_prev = (pl.program_id(1) == 0) & (key >= 1) & (key < WINDOW)
    k2 = _zero_key0(jnp.concatenate([kp_ref[...], kc_ref[...]], axis=0)).astype(BF16)
    v2t = _zero_key0(jnp.concatenate([vp_ref[...], vc_ref[...]], axis=0)).T.astype(BF16)
    q = q_ref[...]
    kvs = range(N_KV_B)
    s = [_dot_nt(k2[:, _kv_cols(kvh)], _group_queries(q, kvh)) for kvh in kvs]
    pn = [_softmax_keys(jnp.where(no_prev, -jnp.inf, s[kvh] + bias_ref[kvh])) for kvh in kvs]
    ot = [_dot(v2t[_kv_cols(kvh), :], pn[kvh]) for kvh in kvs]
    for kvh in kvs:
        for g in range(GROUP_B):
            hh = kvh * GROUP_B + g
            o_ref[:, hh * HD_B:(hh + 1) * HD_B] = ot[kvh][:, g * WINDOW:(g + 1) * WINDOW].T.astype(BF16)


def _swa_prompt(qb, k, v, bias_t):
    b, l, _ = qb.shape
    cur = lambda n: pl.BlockSpec((None, WINDOW, n), lambda i, t: (i, t, 0))
    prev = lambda n: pl.BlockSpec((None, WINDOW, n), lambda i, t: (i, jnp.maximum(t - 1, 0), 0))
    return pl.pallas_call(
        _swa_prompt_body,
        grid=(b, l // WINDOW),
        in_specs=[cur(D_MODEL), prev(KV_B), cur(KV_B), prev(KV_B), cur(KV_B),
                  _const_spec((N_KV_B, 2 * WINDOW, GROUP_B * WINDOW))],
        out_specs=cur(D_MODEL),
        out_shape=jax.ShapeDtypeStruct((b, l, D_MODEL), BF16),
        compiler_params=_params(2),
        name="swa_prompt",
    )(qb, k, k, v, v, bias_t)


def _swa_sample_body(q_ref, kn_ref, vn_ref, kc_ref, vc_ref, bias_ref, o_ref, ko_ref, vo_ref, *, bt, nb, ls):
    kvs = range(N_KV_B)

    def per_group(gi, carry):
        bis = [gi * nb + d for d in range(nb)]
        ks, vs, s = [], [], []
        for bi in bis:
            kf = jnp.concatenate([kc_ref[bi], kn_ref[bi]], axis=0)
            vf = jnp.concatenate([vc_ref[bi], vn_ref[bi]], axis=0)
            ko_ref[bi] = kf[ls:, :]
            vo_ref[bi] = vf[ls:, :]
            ks.append(_zero_key0(kf).astype(BF16))
            vs.append(_zero_key0(vf).astype(BF16))
        for d, bi in enumerate(bis):
            q = q_ref[bi]
            s.append([_dot_nt(ks[d][:, _kv_cols(kvh)], _group_queries(q, kvh)) for kvh in kvs])
        pn = [[_softmax_keys(s[d][kvh] + bias_ref[kvh]) for kvh in kvs] for d in range(nb)]
        o = [[_dot_tn(pn[d][kvh], vs[d][:, _kv_cols(kvh)]) for kvh in kvs] for d in range(nb)]
        for d, bi in enumerate(bis):
            for kvh in kvs:
                for g in range(GROUP_B):
                    hh = kvh * GROUP_B + g
                    o_ref[bi, :, hh * HD_B:(hh + 1) * HD_B] = o[d][kvh][g * ls:(g + 1) * ls].astype(BF16)
        return carry

    lax.fori_loop(0, bt // nb, per_group, 0)


def _swa_sample(qb, k_new, v_new, k_cache, v_cache, bias_s, *, bt):
    b, ls, _ = qb.shape
    n_keys = WINDOW + ls
    new = lambda n: pl.BlockSpec((bt, ls, n), lambda i: (i, 0, 0))
    cache = pl.BlockSpec((bt, WINDOW, KV_B), lambda i: (i, 0, 0))
    return pl.pallas_call(
        functools.partial(_swa_sample_body, bt=bt, nb=min(bt, 4), ls=ls),
        grid=(b // bt,),
        in_specs=[new(D_MODEL), new(KV_B), new(KV_B), cache, cache,
                  _const_spec((N_KV_B, n_keys, GROUP_B * ls))],
        out_specs=[new(D_MODEL), cache, cache],
        out_shape=[jax.ShapeDtypeStruct((b, ls, D_MODEL), BF16),
                   jax.ShapeDtypeStruct((b, WINDOW, KV_B), F32),
                   jax.ShapeDtypeStruct((b, WINDOW, KV_B), F32)],
        compiler_params=_params(1),
        name="swa_sample",
    )(qb, k_new, v_new, k_cache, v_cache, bias_s)


def _layer_norm(y, g, b):
    mu = jnp.mean(y, -1, keepdims=True)
    d = y - mu
    var = jnp.mean(d * d, -1, keepdims=True)
    return d * lax.rsqrt(var + LN_EPS) * g + b


def _merge_body(x_ref, oa_ref, ob_ref, gates_ref, mod_ref, wpa_ref, wpb_ref, wout_ref, lng_ref, lnb_ref,
                o_ref, *, bt, tt, alpha):
    m = bt * tt
    ga = gates_ref[:, :, 0:D_MODEL].reshape(m, D_MODEL)
    gb = gates_ref[:, :, D_MODEL:2 * D_MODEL].reshape(m, D_MODEL)
    mixed = ga * _dot(oa_ref[...].reshape(m, D_MODEL), wpa_ref[...]) \
        + gb * _dot(ob_ref[...].reshape(m, D_MODEL), wpb_ref[...])
    attn = _dot(mixed.astype(BF16), wout_ref[...]).reshape(bt, tt, D_MODEL)
    gt1 = mod_ref[:, :, 2 * D_MODEL:3 * D_MODEL]
    o_ref[...] = _layer_norm(alpha * x_ref[...] + gt1 * attn, lng_ref[...], lnb_ref[...])


def _merge(x, oa, ob, gates, mod, wpa, wpb, wout, lng, lnb, *, bt, tt, alpha):
    b, l, _ = x.shape
    tok_spec = lambda n: pl.BlockSpec((bt, tt, n), lambda i, t: (i, t, 0))
    return pl.pallas_call(
        functools.partial(_merge_body, bt=bt, tt=tt, alpha=alpha),
        grid=(b // bt, l // tt),
        in_specs=[tok_spec(D_MODEL), tok_spec(D_MODEL), tok_spec(D_MODEL), tok_spec(2 * D_MODEL),
                  pl.BlockSpec((bt, 1, 6 * D_MODEL), lambda i, t: (i, 0, 0)),
                  _const_spec((D_MODEL, D_MODEL)), _const_spec((D_MODEL, D_MODEL)), _const_spec((D_MODEL, D_MODEL)),
                  _const_spec((1, D_MODEL)), _const_spec((1, D_MODEL))],
        out_specs=tok_spec(D_MODEL),
        out_shape=jax.ShapeDtypeStruct((b, l, D_MODEL), F32),
        compiler_params=_params(2),
        name="merge",
    )(x, oa, ob, gates, mod, wpa, wpb, wout, lng, lnb)


def _mlp_body(x_ref, mod_ref, wup_ref, wdn_ref, lng_ref, lnb_ref, o_ref, *, bt, tt, alpha):
    m = bt * tt
    x = x_ref[...]
    sh2 = mod_ref[:, :, 3 * D_MODEL:4 * D_MODEL]
    sc2 = mod_ref[:, :, 4 * D_MODEL:5 * D_MODEL]
    gt2 = mod_ref[:, :, 5 * D_MODEL:6 * D_MODEL]
    h2 = (x * (1.0 + sc2) + sh2).reshape(m, D_MODEL).astype(BF16)
    ff = jnp.zeros((m, D_MODEL), F32)
    for c in range(D_FF // D_MODEL):
        cols = slice(c * D_MODEL, (c + 1) * D_MODEL)
        a = jnp.maximum(_dot(h2, wup_ref[:, cols]), 0.0)
        ff = ff + _dot((a * a).astype(BF16), wdn_ref[cols, :])
    o_ref[...] = _layer_norm(alpha * x + gt2 * ff.reshape(bt, tt, D_MODEL), lng_ref[...], lnb_ref[...])


def _mlp(x, mod, wup, wdn, lng, lnb, *, bt, tt, alpha):
    b, l, _ = x.shape
    tok_spec = pl.BlockSpec((bt, tt, D_MODEL), lambda i, t: (i, t, 0))
    return pl.pallas_call(
        functools.partial(_mlp_body, bt=bt, tt=tt, alpha=alpha),
        grid=(b // bt, l // tt),
        in_specs=[tok_spec, pl.BlockSpec((bt, 1, 6 * D_MODEL), lambda i, t: (i, 0, 0)),
                  _const_spec((D_MODEL, D_FF)), _const_spec((D_FF, D_MODEL)),
                  _const_spec((1, D_MODEL)), _const_spec((1, D_MODEL))],
        out_specs=tok_spec,
        out_shape=jax.ShapeDtypeStruct((b, l, D_MODEL), F32),
        compiler_params=_params(2),
        name="mlp",
    )(x, mod, wup, wdn, lng, lnb)


def _tiles(b, l, rows):
    tt = min(l, rows)
    bt = max(1, min(b, rows // tt))
    return bt, tt


def _layer_weights(l, w_in, w_conv, a_log, dt_bias, w_onorm, sinks, w_pa, w_pb, w_out,
                   ln1_g, ln1_b, w_up, w_down, ln2_g, ln2_b):
    win = w_in[l]
    o_ba = CONV_DIM + D_MODEL
    o_qb = o_ba + 2 * N_HEADS_A
    wm = jnp.concatenate([win[:, :o_ba], win[:, o_qb:]], axis=1).astype(BF16)
    wba = jnp.pad(win[:, o_ba:o_qb], ((0, 0), (0, GB_LANES - 2 * N_HEADS_A))).astype(BF16)
    gpar = jnp.zeros((2, GB_LANES), F32)
    gpar = gpar.at[0, N_HEADS_A:2 * N_HEADS_A].set(-jnp.exp(a_log[l].astype(F32)))
    gpar = gpar.at[1, N_HEADS_A:2 * N_HEADS_A].set(dt_bias[l].astype(F32))
    row = lambda a: a[l].reshape(1, -1).astype(F32)
    return dict(wm=wm, wba=wba, wconv=w_conv[l].astype(F32), gpar=gpar, w_onorm=row(w_onorm), sinks=sinks[l].astype(F32),
                wpa=w_pa[l].astype(BF16), wpb=w_pb[l].astype(BF16), wout=w_out[l].astype(BF16),
                ln1_g=row(ln1_g), ln1_b=row(ln1_b), wup=w_up[l].astype(BF16), wdn=w_down[l].astype(BF16),
                ln2_g=row(ln2_g), ln2_b=row(ln2_b))


def _trunk_layer(x, mod, p, alpha, conv_state, s0, k_cache, v_cache):
    b, l, _ = x.shape
    prompt = k_cache is None
    bt, tt = _tiles(b, l, 256)
    cs, z, gb, qb, k, v, gates, conv_out = _front(x, mod, conv_state, p["wm"], p["wba"], p["wconv"], p["gpar"],
                                                  bt=bt, tt=tt)
    if prompt:
        oa, s_new = _delta_prompt(cs, z, gb, p["w_onorm"], tt=min(l, 256))
        ob = _swa_prompt(qb, k, v, p["bias_t"])
        k_new, v_new = k[:, -WINDOW:], v[:, -WINDOW:]
    else:
        oa, s_new = _delta_sample(cs, z, gb, p["w_onorm"], s0, bt=min(b, 8))
        bias_s = p["bias_t"][:, :WINDOW + l, :].reshape(N_KV_B, WINDOW + l, GROUP_B, WINDOW)[..., :l]
        bias_s = bias_s.reshape(N_KV_B, WINDOW + l, GROUP_B * l)
        ob, k_new, v_new = _swa_sample(qb, k, v, k_cache.reshape(b, WINDOW, KV_B), v_cache.reshape(b, WINDOW, KV_B),
                                       bias_s, bt=min(b, 8))
    bt, tt = _tiles(b, l, 512)
    x1 = _merge(x, oa, ob, gates, mod, p["wpa"], p["wpb"], p["wout"], p["ln1_g"], p["ln1_b"], bt=bt, tt=tt, alpha=alpha)
    x2 = _mlp(x1, mod, p["wup"], p["wdn"], p["ln2_g"], p["ln2_b"], bt=bt, tt=tt, alpha=alpha)
    return (x2, s_new, conv_out, k_new.reshape(b, WINDOW, N_KV_B, HD_B), v_new.reshape(b, WINDOW, N_KV_B, HD_B))


def kernel(x_prompt, x_sample, state_delta, state_conv, cache_k, cache_v, c_prompt, c_sample, rel_bias, w_ada, b_ada, w_in, w_conv, a_log, dt_bias, w_onorm, sinks, w_pa, w_pb, w_out, ln1_g, ln1_b, w_up, w_down, ln2_g, ln2_b):
    depth = w_in.shape[0]
    alpha = (2 * depth) ** 0.25
    bp = x_prompt.shape[0]
    mod_all = _ada(jnp.concatenate([c_prompt, c_sample], axis=0), w_ada, b_ada)
    bias_t = _bias_table_t(rel_bias, sinks)
    yp, ys = x_prompt, x_sample
    outs = [[] for _ in range(8)]
    for l in range(depth):
        p = _layer_weights(l, w_in, w_conv, a_log, dt_bias, w_onorm, sinks, w_pa, w_pb, w_out,
                           ln1_g, ln1_b, w_up, w_down, ln2_g, ln2_b)
        p["bias_t"] = bias_t[l]
        mod_p = mod_all[l, :bp][:, None, :]
        mod_s = mod_all[l, bp:][:, None, :]
        zero_conv = jnp.zeros((bp, CONV_W - 1, CONV_DIM), x_prompt.dtype)
        yp, *rest_p = _trunk_layer(yp, mod_p, p, alpha, zero_conv, None, None, None)
        ys, *rest_s = _trunk_layer(ys, mod_s, p, alpha, state_conv[l], state_delta[l], cache_k[l], cache_v[l])
        for acc, val in zip(outs, rest_p + rest_s):
            acc.append(val)
    return (yp, ys) + tuple(jnp.stack(a) for a in outs)
```

```python
import functools
import math

import numpy as np
import jax
import jax.numpy as jnp
from jax import lax
from jax.experimental import pallas as pl
from jax.experimental.pallas import tpu as pltpu

F32 = jnp.float32
BF16 = jnp.bfloat16

D_MODEL = 1024
N_HEADS_A = 8
DK_A = 128
CONV_W = 4
CONV_DIM = 3 * D_MODEL
CHUNK = 64
HD_B = 64
N_HEADS_B = 16
N_KV_B = 4
GROUP_B = N_HEADS_B // N_KV_B
KV_B = N_KV_B * HD_B
WINDOW = 128
N_BUCKETS = 32
MAX_DISTANCE = 128
D_FF = 4 * D_MODEL
LN_EPS = 1e-5
RMS_EPS = 1e-6

OFF_Z = CONV_DIM
OFF_QB = OFF_Z + D_MODEL
OFF_KVB = OFF_QB + D_MODEL
OFF_GATES = OFF_KVB + 2 * KV_B
N_MAIN = OFF_GATES + 2 * D_MODEL
GB_LANES = 128

V7X_VMEM_BYTES = 64 * 1024 * 1024
VMEM_LIMIT = V7X_VMEM_BYTES - 8 * 1024 * 1024


def _params(n_grid):
    return pltpu.CompilerParams(dimension_semantics=("arbitrary",) * n_grid, vmem_limit_bytes=VMEM_LIMIT)


def _dot(a, b):
    return jnp.dot(a, b, preferred_element_type=F32)


def _dot_nt(a, b):
    return lax.dot_general(a, b, (((1,), (1,)), ((), ())), preferred_element_type=F32)


def _dot_tn(a, b):
    return lax.dot_general(a, b, (((0,), (0,)), ((), ())), preferred_element_type=F32)


def _sigmoid(x):
    return 1.0 / (1.0 + jnp.exp(-x))


def _silu(x):
    return x * _sigmoid(x)


def _const_spec(shape):
    nd = len(shape)
    return pl.BlockSpec(shape, lambda *_: (0,) * nd, pipeline_mode=pl.Buffered(1))


def _layer_spec(shape, layer):
    nd = len(shape)
    return pl.BlockSpec((None,) + tuple(shape), lambda *_: (layer,) + (0,) * nd, pipeline_mode=pl.Buffered(1))


def _cast_body(w_ref, o_ref):
    o_ref[...] = w_ref[...].astype(BF16)


def _cast_bf16(w):
    depth, k, n = w.shape
    bk = max(8, min(k, (1024 * 1024) // n))
    spec = pl.BlockSpec((None, bk, n), lambda l, i: (l, i, 0))
    return pl.pallas_call(
        _cast_body, grid=(depth, k // bk), in_specs=[spec], out_specs=spec,
        out_shape=jax.ShapeDtypeStruct(w.shape, BF16), compiler_params=_params(2), name="cast_bf16",
    )(w)


def _prep_w_in_body(w_ref, wm_ref, wba_ref):
    o_ba = CONV_DIM + D_MODEL
    o_qb = o_ba + 2 * N_HEADS_A
    w = w_ref[...]
    wm_ref[:, :o_ba] = w[:, :o_ba].astype(BF16)
    wm_ref[:, o_ba:] = w[:, o_qb:].astype(BF16)
    pad = jnp.zeros((w.shape[0], GB_LANES - 2 * N_HEADS_A), F32)
    wba_ref[...] = jnp.concatenate([w[:, o_ba:o_qb], pad], axis=1).astype(BF16)


def _prep_w_in(w_in):
    depth, k, n = w_in.shape
    bk = 128
    return pl.pallas_call(
        _prep_w_in_body,
        grid=(depth, k // bk),
        in_specs=[pl.BlockSpec((None, bk, n), lambda l, i: (l, i, 0))],
        out_specs=[pl.BlockSpec((None, bk, N_MAIN), lambda l, i: (l, i, 0)),
                   pl.BlockSpec((None, bk, GB_LANES), lambda l, i: (l, i, 0))],
        out_shape=[jax.ShapeDtypeStruct((depth, k, N_MAIN), BF16), jax.ShapeDtypeStruct((depth, k, GB_LANES), BF16)],
        compiler_params=_params(2),
        name="prep_w_in",
    )(w_in)


def _ada_body(c_ref, w_ref, b_ref, o_ref):
    s = _silu(c_ref[...]).astype(BF16)
    o_ref[...] = _dot(s, w_ref[...].astype(BF16)) + b_ref[...]


def _ada(c_all, w_ada, b_ada):
    depth = w_ada.shape[0]
    n_rows = c_all.shape[0]
    n_col = w_ada.shape[2] // D_MODEL
    return pl.pallas_call(
        _ada_body,
        grid=(depth, n_col),
        in_specs=[pl.BlockSpec((n_rows, D_MODEL), lambda l, n: (0, 0)),
                  pl.BlockSpec((None, D_MODEL, D_MODEL), lambda l, n: (l, 0, n)),
                  pl.BlockSpec((None, 1, D_MODEL), lambda l, n: (l, 0, n))],
        out_specs=pl.BlockSpec((None, n_rows, D_MODEL), lambda l, n: (l, 0, n)),
        out_shape=jax.ShapeDtypeStruct((depth, n_rows, w_ada.shape[2]), F32),
        compiler_params=_params(2),
        name="ada",
    )(c_all, w_ada, b_ada.reshape(depth, 1, -1))


def _bucket_table():
    r = np.arange(WINDOW)[:, None]
    c = np.arange(2 * WINDOW)[None, :]
    dist = WINDOW + r - c
    n = np.maximum(dist, 0)
    max_exact = N_BUCKETS // 2
    ratio = np.maximum(n, max_exact).astype(np.float32) / np.float32(max_exact)
    large = max_exact + (np.log(ratio) / np.float32(math.log(MAX_DISTANCE / max_exact))
                         * np.float32(N_BUCKETS - max_exact)).astype(np.int32)
    large = np.minimum(large, N_BUCKETS - 1)
    bucket = np.where(n < max_exact, n, large).astype(np.int32)
    valid = ((dist >= 0) & (dist < WINDOW)).astype(np.int32)
    return bucket, valid


def _bias_t_body(rb_ref, sink_ref, bucket_ref, valid_ref, o_ref):
    l = pl.program_id(0)
    h = pl.program_id(1)
    bucket = bucket_ref[...]
    acc = jnp.zeros(bucket.shape, F32)
    for j in range(N_BUCKETS):
        acc = jnp.where(bucket == j, rb_ref[j, h], acc)
    acc = jnp.where(valid_ref[...] > 0, acc, -jnp.inf)
    key = lax.broadcasted_iota(jnp.int32, bucket.shape, 0)
    o_ref[...] = jnp.where(key == 0, sink_ref[l, h], acc)


def _bias_table_t(rel_bias, sinks):
    depth = sinks.shape[0]
    bucket, valid = _bucket_table()
    return pl.pallas_call(
        _bias_t_body,
        grid=(depth, N_HEADS_B),
        in_specs=[pl.BlockSpec(memory_space=pltpu.SMEM), pl.BlockSpec(memory_space=pltpu.SMEM),
                  pl.BlockSpec((2 * WINDOW, WINDOW), lambda l, h: (0, 0)),
                  pl.BlockSpec((2 * WINDOW, WINDOW), lambda l, h: (0, 0))],
        out_specs=pl.BlockSpec((None, None, 2 * WINDOW, WINDOW), lambda l, h: (l, h // GROUP_B, 0, h % GROUP_B)),
        out_shape=jax.ShapeDtypeStruct((depth, N_KV_B, 2 * WINDOW, GROUP_B * WINDOW), F32),
        compiler_params=_params(2),
        name="bias_table_t",
    )(rel_bias.astype(F32), sinks.astype(F32), jnp.asarray(bucket.T.copy()), jnp.asarray(valid.T.copy()))


def _front_body(x_ref, mod_ref, st_ref, wm_ref, wba_ref, wconv_ref, gpar_ref,
                cs_ref, z_ref, gb_ref, qb_ref, k_ref, v_ref, gates_ref, cst_ref, cbuf, *, bt, tt):
    m = bt * tt
    sh1 = mod_ref[:, :, 0:D_MODEL]
    sc1 = mod_ref[:, :, D_MODEL:2 * D_MODEL]
    h = (x_ref[...] * (1.0 + sc1) + sh1).reshape(m, D_MODEL).astype(BF16)

    @pl.when(pl.program_id(1) == 0)
    def _():
        cbuf[:, 5:8, :] = st_ref[...]

    for s in range(3):
        cols = slice(s * D_MODEL, (s + 1) * D_MODEL)
        cbuf[:, 8:8 + tt, cols] = _dot(h, wm_ref[:, cols]).reshape(bt, tt, D_MODEL)
        y = cbuf[:, 5:5 + tt, cols] * wconv_ref[0:1, cols]
        for j in range(1, CONV_W):
            y = y + cbuf[:, 5 + j:5 + j + tt, cols] * wconv_ref[j:j + 1, cols]
        cs_ref[:, :, cols] = _silu(y)
    tail = cbuf[:, tt + 5:tt + 8, :]
    cst_ref[...] = tail
    cbuf[:, 5:8, :] = tail

    z_ref[...] = _dot(h, wm_ref[:, OFF_Z:OFF_QB]).reshape(bt, tt, D_MODEL)
    qb_ref[...] = (_dot(h, wm_ref[:, OFF_QB:OFF_KVB]) * (HD_B ** -0.5)).astype(BF16).reshape(bt, tt, D_MODEL)
    kv = _dot(h, wm_ref[:, OFF_KVB:OFF_GATES])
    k_ref[...] = kv[:, :KV_B].reshape(bt, tt, KV_B)
    v_ref[...] = kv[:, KV_B:].reshape(bt, tt, KV_B)
    gates_ref[...] = _sigmoid(_dot(h, wm_ref[:, OFF_GATES:N_MAIN])).reshape(bt, tt, 2 * D_MODEL)

    ba = _dot(h, wba_ref[...])
    xg = ba + gpar_ref[1:2, :]
    softplus = jnp.maximum(xg, 0.0) + jnp.log(1.0 + jnp.exp(-jnp.abs(xg)))
    lane = lax.broadcasted_iota(jnp.int32, ba.shape, 1)
    gb = jnp.where(lane < N_HEADS_A, _sigmoid(ba), gpar_ref[0:1, :] * softplus)
    gb_ref[...] = gb.reshape(bt, tt, GB_LANES)


def _front(x, mod, conv_state, wm, wba, wconv, gpar, *, layer, bt, tt):
    b, l, _ = x.shape
    grid = (b // bt, l // tt)
    tok = lambda n, dt=F32: jax.ShapeDtypeStruct((b, l, n), dt)
    tok_spec = lambda n: pl.BlockSpec((bt, tt, n), lambda i, t: (i, t, 0))
    return pl.pallas_call(
        functools.partial(_front_body, bt=bt, tt=tt),
        grid=grid,
        in_specs=[tok_spec(D_MODEL),
                  pl.BlockSpec((bt, 1, 6 * D_MODEL), lambda i, t: (i, 0, 0)),
                  pl.BlockSpec((bt, CONV_W - 1, CONV_DIM), lambda i, t: (i, 0, 0)),
                  _layer_spec((D_MODEL, N_MAIN), layer),
                  _layer_spec((D_MODEL, GB_LANES), layer),
                  _const_spec((CONV_W, CONV_DIM)),
                  _const_spec((2, GB_LANES))],
        out_specs=[tok_spec(CONV_DIM), tok_spec(D_MODEL), tok_spec(GB_LANES), tok_spec(D_MODEL),
                   tok_spec(KV_B), tok_spec(KV_B), tok_spec(2 * D_MODEL),
                   pl.BlockSpec((bt, CONV_W - 1, CONV_DIM), lambda i, t: (i, 0, 0))],
        out_shape=[tok(CONV_DIM), tok(D_MODEL), tok(GB_LANES), tok(D_MODEL, BF16),
                   tok(KV_B), tok(KV_B), tok(2 * D_MODEL),
                   jax.ShapeDtypeStruct((b, CONV_W - 1, CONV_DIM), F32)],
        scratch_shapes=[pltpu.VMEM((bt, tt + 8, CONV_DIM), F32)],
        compiler_params=_params(2),
        name="front",
    )(x, mod, conv_state, wm, wba, wconv, gpar)


def _delta_masks(r, block):
    i = lax.broadcasted_iota(jnp.int32, (r, r), 0)
    j = lax.broadcasted_iota(jnp.int32, (r, r), 1)
    shift = int(math.log2(block))
    same = (i >> shift) == (j >> shift)
    levels = [((i >> (s + 1)) == (j >> (s + 1))) & ((i >> s) != (j >> s)) for s in range(shift)]
    return dict(tri=same & (i >= j), strict=same & (i > j), upper=same & (i < j), eye=(i == j).astype(F32),
                levels=levels)


def _split_dot(a01, x):
    a01 = a01.astype(BF16)
    hi = x.astype(BF16)
    rest = x - hi.astype(F32)
    mid = rest.astype(BF16)
    lo = (rest - mid.astype(F32)).astype(BF16)
    return _dot(a01, hi) + _dot(a01, mid) + _dot(a01, lo)


def _decay_sums(g, masks):
    return _split_dot(masks["tri"], g), _split_dot(masks["upper"], g)


def _delta_pre(problems, masks):
    tri, strict, eye, levels = masks["tri"], masks["strict"], masks["eye"], masks["levels"]
    n = len(problems)
    r = problems[0][0].shape[0]
    lhs, ks, rhs, decays, qgs, kds = [], [], [], [], [], []
    for qr, kr, v, beta_b, gc, rev, gc_row in problems:
        q = qr * lax.rsqrt(jnp.sum(qr * qr, -1, keepdims=True) + 1e-6) * (DK_A ** -0.5)
        k = kr * lax.rsqrt(jnp.sum(kr * kr, -1, keepdims=True) + 1e-6)
        gc_col = gc[:, :r] if r <= DK_A else jnp.concatenate([gc] * (r // DK_A), axis=1)
        decays.append(jnp.exp(jnp.where(tri, gc_col - gc_row, -jnp.inf)))
        eg = jnp.exp(gc)
        kb = k * beta_b
        lhs.append(jnp.concatenate([kb, q], axis=0).astype(BF16))
        ks.append(k.astype(BF16))
        rhs.append(jnp.concatenate([v * beta_b, kb * eg], axis=1).astype(BF16))
        qgs.append((q * eg).astype(BF16))
        kds.append((k * jnp.exp(rev)).astype(BF16))
    kk = [_dot_nt(lhs[i], ks[i]) for i in range(n)]
    mm = [jnp.where(strict, kk[i][:r] * decays[i], 0.0) for i in range(n)]
    qk = [(kk[i][r:] * decays[i]).astype(BF16) for i in range(n)]
    t = [eye - jnp.where(levels[0], mm[i], 0.0) for i in range(n)]
    for lvl in levels[1:]:
        tb = [t[i].astype(BF16) for i in range(n)]
        x = [_dot(tb[i], jnp.where(lvl, mm[i], 0.0).astype(BF16)).astype(BF16) for i in range(n)]
        y = [_dot(x[i], tb[i]) for i in range(n)]
        t = [t[i] - y[i] for i in range(n)]
    uw = [_dot(t[i].astype(BF16), rhs[i]) for i in range(n)]
    return [(uw[i][:, :DK_A], uw[i][:, DK_A:].astype(BF16), qgs[i], kds[i], qk[i]) for i in range(n)]


def _gated_norm(o, z, w_onorm):
    o = o * lax.rsqrt(jnp.mean(o * o, -1, keepdims=True) + RMS_EPS) * w_onorm
    return o * _silu(z)


def _delta_prompt_body(cs_ref, z_ref, gb_ref, wn_ref, o_ref, s_out_ref,
                       s_ref, u_s, o_s, wq_s, qkd_s, egl_s, *, tt):
    n_c = tt // CHUNK
    masks = _delta_masks(tt, CHUNK)

    @pl.when(pl.program_id(1) == 0)
    def _():
        s_ref[...] = jnp.zeros(s_ref.shape, F32)

    gbv = gb_ref[...]
    gc_all, rev_all = _decay_sums(gbv, masks)
    gc_t = gc_all.T
    heads = range(N_HEADS_A)
    hcols = [slice(hd * DK_A, (hd + 1) * DK_A) for hd in heads]
    problems = []
    for hd in heads:
        lg = N_HEADS_A + hd
        problems.append((cs_ref[:, hcols[hd]], cs_ref[:, D_MODEL + hd * DK_A:D_MODEL + (hd + 1) * DK_A],
                         cs_ref[:, 2 * D_MODEL + hd * DK_A:2 * D_MODEL + (hd + 1) * DK_A],
                         jnp.broadcast_to(gbv[:, hd:hd + 1], (tt, DK_A)),
                         jnp.broadcast_to(gc_all[:, lg:lg + 1], (tt, DK_A)),
                         jnp.broadcast_to(rev_all[:, lg:lg + 1], (tt, DK_A)),
                         jnp.broadcast_to(gc_t[lg:lg + 1, :], (tt, tt))))
    for hd, (u, w, qg, kd, qk) in enumerate(_delta_pre(problems, masks)):
        u_s[:, hcols[hd]] = u
        kdt = kd.T
        for c in range(n_c):
            blk = slice(c * CHUNK, (c + 1) * CHUNK)
            idx = hd * n_c + c
            wq_s[idx, 0:CHUNK, :] = w[blk]
            wq_s[idx, CHUNK:2 * CHUNK, :] = qg[blk]
            qkd_s[idx, 0:CHUNK, :] = qk[blk, blk]
            qkd_s[idx, CHUNK:CHUNK + DK_A, :] = kdt[:, blk]
            last = (c + 1) * CHUNK - 1
            egl_s[idx] = jnp.broadcast_to(jnp.exp(problems[hd][4][last:last + 1, :]), (8, DK_A))

    states = [s_ref[hd] for hd in heads]
    for c in range(n_c):
        blk = slice(c * CHUNK, (c + 1) * CHUNK)
        idx = [hd * n_c + c for hd in heads]
        ws = [_dot(wq_s[idx[hd]], states[hd].astype(BF16)) for hd in heads]
        v_new = [(u_s[blk, hcols[hd]] - ws[hd][:CHUNK]).astype(BF16) for hd in heads]
        upd = [_dot(qkd_s[idx[hd]], v_new[hd]) for hd in heads]
        for hd in heads:
            o_s[blk, hcols[hd]] = ws[hd][CHUNK:] + upd[hd][:CHUNK]
            states[hd] = states[hd] * egl_s[idx[hd]][0:1, :] + upd[hd][CHUNK:]
    for hd in heads:
        s_ref[hd] = states[hd]
        o_ref[:, hcols[hd]] = _gated_norm(o_s[:, hcols[hd]], z_ref[:, hcols[hd]], wn_ref[...]).astype(BF16)

    @pl.when(pl.program_id(1) == pl.num_programs(1) - 1)
    def _():
        s_out_ref[...] = s_ref[...]


def _delta_prompt(cs, z, gb, w_onorm, *, tt):
    b, l, _ = cs.shape
    n_c = tt // CHUNK
    tok_spec = lambda n: pl.BlockSpec((None, tt, n), lambda i, t: (i, t, 0))
    return pl.pallas_call(
        functools.partial(_delta_prompt_body, tt=tt),
        grid=(b, l // tt),
        in_specs=[tok_spec(CONV_DIM), tok_spec(D_MODEL), tok_spec(GB_LANES), _const_spec((1, DK_A))],
        out_specs=[tok_spec(D_MODEL),
                   pl.BlockSpec((None, N_HEADS_A, DK_A, DK_A), lambda i, t: (i, 0, 0, 0))],
        out_shape=[jax.ShapeDtypeStruct((b, l, D_MODEL), BF16),
                   jax.ShapeDtypeStruct((b, N_HEADS_A, DK_A, DK_A), F32)],
        scratch_shapes=[pltpu.VMEM((N_HEADS_A, DK_A, DK_A), F32),
                        pltpu.VMEM((tt, D_MODEL), F32), pltpu.VMEM((tt, D_MODEL), F32),
                        pltpu.VMEM((N_HEADS_A * n_c, 2 * CHUNK, DK_A), BF16),
                        pltpu.VMEM((N_HEADS_A * n_c, CHUNK + DK_A, CHUNK), BF16),
                        pltpu.VMEM((N_HEADS_A * n_c, 8, DK_A), F32)],
        compiler_params=_params(2),
        name="delta_prompt",
    )(cs, z, gb, w_onorm)


def _delta_sample_body(cs_ref, z_ref, gb_ref, wn_ref, s0_ref, o_ref, s_out_ref, *, bt, nb, ls):
    r = N_HEADS_A * ls
    masks = _delta_masks(r, ls)

    heads = range(N_HEADS_A)
    hrows = [slice(hd * ls, (hd + 1) * ls) for hd in heads]
    hcols = [slice(hd * DK_A, (hd + 1) * DK_A) for hd in heads]

    def per_group(gi, carry):
        bis = [gi * nb + d for d in range(nb)]
        problems, egl = [], []
        for bi in bis:
            cs = cs_ref[bi]
            gbv = gb_ref[bi]
            stack = lambda base: jnp.concatenate([cs[:, base + hd * DK_A: base + (hd + 1) * DK_A] for hd in heads],
                                                 axis=0)
            beta_b = jnp.concatenate([jnp.broadcast_to(gbv[:, hd:hd + 1], (ls, DK_A)) for hd in heads], axis=0)
            g_b = jnp.concatenate(
                [jnp.broadcast_to(gbv[:, N_HEADS_A + hd:N_HEADS_A + hd + 1], (ls, DK_A)) for hd in heads], axis=0)
            gc, rev = _decay_sums(g_b, masks)
            egl.append(jnp.exp(gc))
            problems.append((stack(0), stack(D_MODEL), stack(2 * D_MODEL), beta_b, gc, rev, gc.T[:r, :]))
        pre = _delta_pre(problems, masks)
        both = [[_dot(jnp.concatenate([pre[d][1][hrows[hd]], pre[d][2][hrows[hd]]], axis=0),
                      s0_ref[bis[d], hd].astype(BF16)) for hd in heads] for d in range(nb)]
        v_new = [(pre[d][0] - jnp.concatenate([both[d][hd][:ls] for hd in heads], axis=0)).astype(BF16)
                 for d in range(nb)]
        qkv = [_dot(pre[d][4], v_new[d]) for d in range(nb)]
        upd = [[_dot_tn(pre[d][3][hrows[hd]], v_new[d][hrows[hd]]) for hd in heads] for d in range(nb)]
        for d, bi in enumerate(bis):
            zb = z_ref[bi]
            for hd in heads:
                last = (hd + 1) * ls - 1
                s_out_ref[bi, hd] = s0_ref[bi, hd] * egl[d][last:last + 1, :] + upd[d][hd]
                o = both[d][hd][ls:] + qkv[d][hrows[hd]]
                o_ref[bi, :, hcols[hd]] = _gated_norm(o, zb[:, hcols[hd]], wn_ref[...]).astype(BF16)
        return carry

    lax.fori_loop(0, bt // nb, per_group, 0)


def _delta_sample(cs, z, gb, w_onorm, s0, *, bt):
    b, ls, _ = cs.shape
    tok_spec = lambda n: pl.BlockSpec((bt, ls, n), lambda i: (i, 0, 0))
    st_spec = pl.BlockSpec((bt, N_HEADS_A, DK_A, DK_A), lambda i: (i, 0, 0, 0))
    return pl.pallas_call(
        functools.partial(_delta_sample_body, bt=bt, nb=min(bt, 4), ls=ls),
        grid=(b // bt,),
        in_specs=[tok_spec(CONV_DIM), tok_spec(D_MODEL), tok_spec(GB_LANES), _const_spec((1, DK_A)), st_spec],
        out_specs=[tok_spec(D_MODEL), st_spec],
        out_shape=[jax.ShapeDtypeStruct((b, ls, D_MODEL), BF16),
                   jax.ShapeDtypeStruct((b, N_HEADS_A, DK_A, DK_A), F32)],
        compiler_params=_params(1),
        name="delta_sample",
    )(cs, z, gb, w_onorm, s0)


def _softmax_keys(s):
    p = jnp.exp(s - jnp.max(s, axis=0, keepdims=True))
    return (p / jnp.sum(p, axis=0, keepdims=True)).astype(BF16)


def _zero_key0(x):
    return jnp.where(lax.broadcasted_iota(jnp.int32, x.shape, 0) == 0, 0.0, x)


def _kv_cols(kvh):
    return slice(kvh * HD_B, (kvh + 1) * HD_B)


def _group_queries(q, kvh):
    return jnp.concatenate([q[:, hh * HD_B:(hh + 1) * HD_B] for hh in range(kvh * GROUP_B, (kvh + 1) * GROUP_B)],
                           axis=0)


def _swa_prompt_body(q_ref, kp_ref, kc_ref, vp_ref, vc_ref, bias_ref, o_ref, ko_ref, vo_ref):
    @pl.when(pl.program_id(1) == pl.num_programs(1) - 1)
    def _():
        ko_ref[...] = kc_ref[...]
        vo_ref[...] = vc_ref[...]

    n_q = GROUP_B * WINDOW
    key = lax.broadcasted_iota(jnp.int32, (2 * WINDOW, n_q), 0)
    no_prev = (pl.program_id(1) == 0) & (key >= 1) & (key < WINDOW)
    k2 = _zero_key0(jnp.concatenate([kp_ref[...], kc_ref[...]], axis=0)).astype(BF16)
    v2t = _zero_key0(jnp.concatenate([vp_ref[...], vc_ref[...]], axis=0)).T.astype(BF16)
    q = q_ref[...]
    kvs = range(N_KV_B)
    s = [_dot_nt(k2[:, _kv_cols(kvh)], _group_queries(q, kvh)) for kvh in kvs]
    pn = [_softmax_keys(jnp.where(no_prev, -jnp.inf, s[kvh] + bias_ref[kvh])) for kvh in kvs]
    ot = [_dot(v2t[_kv_cols(kvh), :], pn[kvh]) for kvh in kvs]
    for kvh in kvs:
        for g in range(GROUP_B):
            hh = kvh * GROUP_B + g
            o_ref[:, hh * HD_B:(hh + 1) * HD_B] = ot[kvh][:, g * WINDOW:(g + 1) * WINDOW].T.astype(BF16)


def _swa_prompt(qb, k, v, bias_t):
    b, l, _ = qb.shape
    cur = lambda n: pl.BlockSpec((None, WINDOW, n), lambda i, t: (i, t, 0))
    prev = lambda n: pl.BlockSpec((None, WINDOW, n), lambda i, t: (i, jnp.maximum(t - 1, 0), 0))
    last = lambda n: pl.BlockSpec((None, WINDOW, n), lambda i, t: (i, 0, 0))
    return pl.pallas_call(
        _swa_prompt_body,
        grid=(b, l // WINDOW),
        in_specs=[cur(D_MODEL), prev(KV_B), cur(KV_B), prev(KV_B), cur(KV_B),
                  _const_spec((N_KV_B, 2 * WINDOW, GROUP_B * WINDOW))],
        out_specs=[cur(D_MODEL), last(KV_B), last(KV_B)],
        out_shape=[jax.ShapeDtypeStruct((b, l, D_MODEL), BF16),
                   jax.ShapeDtypeStruct((b, WINDOW, KV_B), F32), jax.ShapeDtypeStruct((b, WINDOW, KV_B), F32)],
        compiler_params=_params(2),
        name="swa_prompt",
    )(qb, k, k, v, v, bias_t)


def _swa_sample_body(q_ref, kn_ref, vn_ref, kc_ref, vc_ref, bias_ref, o_ref, ko_ref, vo_ref, *, bt, nb, ls):
    kvs = range(N_KV_B)

    def per_group(gi, carry):
        bis = [gi * nb + d for d in range(nb)]
        ks, vs, s = [], [], []
        for bi in bis:
            kf = jnp.concatenate([kc_ref[bi], kn_ref[bi]], axis=0)
            vf = jnp.concatenate([vc_ref[bi], vn_ref[bi]], axis=0)
            ko_ref[bi] = kf[ls:, :]
            vo_ref[bi] = vf[ls:, :]
            ks.append(_zero_key0(kf).astype(BF16))
            vs.append(_zero_key0(vf).astype(BF16))
        for d, bi in enumerate(bis):
            q = q_ref[bi]
            s.append([_dot_nt(ks[d][:, _kv_cols(kvh)], _group_queries(q, kvh)) for kvh in kvs])
        pn = [[_softmax_keys(s[d][kvh] + bias_ref[kvh]) for kvh in kvs] for d in range(nb)]
        o = [[_dot_tn(pn[d][kvh], vs[d][:, _kv_cols(kvh)]) for kvh in kvs] for d in range(nb)]
        for d, bi in enumerate(bis):
            for kvh in kvs:
                for g in range(GROUP_B):
                    hh = kvh * GROUP_B + g
                    o_ref[bi, :, hh * HD_B:(hh + 1) * HD_B] = o[d][kvh][g * ls:(g + 1) * ls].astype(BF16)
        return carry

    lax.fori_loop(0, bt // nb, per_group, 0)


def _swa_sample(qb, k_new, v_new, k_cache, v_cache, bias_s, *, bt):
    b, ls, _ = qb.shape
    n_keys = WINDOW + ls
    new = lambda n: pl.BlockSpec((bt, ls, n), lambda i: (i, 0, 0))
    cache = pl.BlockSpec((bt, WINDOW, KV_B), lambda i: (i, 0, 0))
    return pl.pallas_call(
        functools.partial(_swa_sample_body, bt=bt, nb=min(bt, 4), ls=ls),
        grid=(b // bt,),
        in_specs=[new(D_MODEL), new(KV_B), new(KV_B), cache, cache,
                  _const_spec((N_KV_B, n_keys, GROUP_B * ls))],
        out_specs=[new(D_MODEL), cache, cache],
        out_shape=[jax.ShapeDtypeStruct((b, ls, D_MODEL), BF16),
                   jax.ShapeDtypeStruct((b, WINDOW, KV_B), F32),
                   jax.ShapeDtypeStruct((b, WINDOW, KV_B), F32)],
        compiler_params=_params(1),
        name="swa_sample",
    )(qb, k_new, v_new, k_cache, v_cache, bias_s)


def _layer_norm(y, g, b):
    mu = jnp.mean(y, -1, keepdims=True)
    d = y - mu
    var = jnp.mean(d * d, -1, keepdims=True)
    return d * lax.rsqrt(var + LN_EPS) * g + b


def _merge_body(x_ref, oa_ref, ob_ref, gates_ref, mod_ref, wpa_ref, wpb_ref, wout_ref, lng_ref, lnb_ref,
                o_ref, *, bt, tt, alpha):
    m = bt * tt
    ga = gates_ref[:, :, 0:D_MODEL].reshape(m, D_MODEL)
    gb = gates_ref[:, :, D_MODEL:2 * D_MODEL].reshape(m, D_MODEL)
    mixed = ga * _dot(oa_ref[...].reshape(m, D_MODEL), wpa_ref[...]) \
        + gb * _dot(ob_ref[...].reshape(m, D_MODEL), wpb_ref[...])
    attn = _dot(mixed.astype(BF16), wout_ref[...]).reshape(bt, tt, D_MODEL)
    gt1 = mod_ref[:, :, 2 * D_MODEL:3 * D_MODEL]
    o_ref[...] = _layer_norm(alpha * x_ref[...] + gt1 * attn, lng_ref[...], lnb_ref[...])


def _merge(x, oa, ob, gates, mod, wpa, wpb, wout, lng, lnb, *, layer, bt, tt, alpha):
    b, l, _ = x.shape
    tok_spec = lambda n: pl.BlockSpec((bt, tt, n), lambda i, t: (i, t, 0))
    return pl.pallas_call(
        functools.partial(_merge_body, bt=bt, tt=tt, alpha=alpha),
        grid=(b // bt, l // tt),
        in_specs=[tok_spec(D_MODEL), tok_spec(D_MODEL), tok_spec(D_MODEL), tok_spec(2 * D_MODEL),
                  pl.BlockSpec((bt, 1, 6 * D_MODEL), lambda i, t: (i, 0, 0)),
                  _layer_spec((D_MODEL, D_MODEL), layer), _layer_spec((D_MODEL, D_MODEL), layer),
                  _layer_spec((D_MODEL, D_MODEL), layer),
                  _const_spec((1, D_MODEL)), _const_spec((1, D_MODEL))],
        out_specs=tok_spec(D_MODEL),
        out_shape=jax.ShapeDtypeStruct((b, l, D_MODEL), F32),
        compiler_params=_params(2),
        name="merge",
    )(x, oa, ob, gates, mod, wpa, wpb, wout, lng, lnb)


def _mlp_body(x_ref, mod_ref, wup_ref, wdn_ref, lng_ref, lnb_ref, o_ref, *, bt, tt, alpha):
    m = bt * tt
    x = x_ref[...]
    sh2 = mod_ref[:, :, 3 * D_MODEL:4 * D_MODEL]
    sc2 = mod_ref[:, :, 4 * D_MODEL:5 * D_MODEL]
    gt2 = mod_ref[:, :, 5 * D_MODEL:6 * D_MODEL]
    h2 = (x * (1.0 + sc2) + sh2).reshape(m, D_MODEL).astype(BF16)
    ff = jnp.zeros((m, D_MODEL), F32)
    for c in range(D_FF // D_MODEL):
        cols = slice(c * D_MODEL, (c + 1) * D_MODEL)
        a = jnp.maximum(_dot(h2, wup_ref[:, cols]), 0.0)
        ff = ff + _dot((a * a).astype(BF16), wdn_ref[cols, :])
    o_ref[...] = _layer_norm(alpha * x + gt2 * ff.reshape(bt, tt, D_MODEL), lng_ref[...], lnb_ref[...])


def _mlp(x, mod, wup, wdn, lng, lnb, *, layer, bt, tt, alpha):
    b, l, _ = x.shape
    tok_spec = pl.BlockSpec((bt, tt, D_MODEL), lambda i, t: (i, t, 0))
    return pl.pallas_call(
        functools.partial(_mlp_body, bt=bt, tt=tt, alpha=alpha),
        grid=(b // bt, l // tt),
        in_specs=[tok_spec, pl.BlockSpec((bt, 1, 6 * D_MODEL), lambda i, t: (i, 0, 0)),
                  _layer_spec((D_MODEL, D_FF), layer), _layer_spec((D_FF, D_MODEL), layer),
                  _const_spec((1, D_MODEL)), _const_spec((1, D_MODEL))],
        out_specs=tok_spec,
        out_shape=jax.ShapeDtypeStruct((b, l, D_MODEL), F32),
        compiler_params=_params(2),
        name="mlp",
    )(x, mod, wup, wdn, lng, lnb)


def _tiles(b, l, rows):
    tt = min(l, rows)
    bt = max(1, min(b, rows // tt))
    return bt, tt


def _layer_params(l, w_conv, a_log, dt_bias, w_onorm, ln1_g, ln1_b, ln2_g, ln2_b):
    gpar = jnp.zeros((2, GB_LANES), F32)
    gpar = gpar.at[0, N_HEADS_A:2 * N_HEADS_A].set(-jnp.exp(a_log[l].astype(F32)))
    gpar = gpar.at[1, N_HEADS_A:2 * N_HEADS_A].set(dt_bias[l].astype(F32))
    row = lambda a: a[l].reshape(1, -1).astype(F32)
    return dict(wconv=w_conv[l].astype(F32), gpar=gpar, w_onorm=row(w_onorm), ln1_g=row(ln1_g), ln1_b=row(ln1_b),
                ln2_g=row(ln2_g), ln2_b=row(ln2_b))


def _trunk_layer(x, mod, layer, w, p, alpha, conv_state, s0, k_cache, v_cache):
    b, l, _ = x.shape
    prompt = k_cache is None
    bt, tt = _tiles(b, l, 256)
    cs, z, gb, qb, k, v, gates, conv_out = _front(x, mod, conv_state, w["wm"], w["wba"], p["wconv"], p["gpar"],
                                                  layer=layer, bt=bt, tt=tt)
    if prompt:
        oa, s_new = _delta_prompt(cs, z, gb, p["w_onorm"], tt=min(l, 256))
        ob, k_new, v_new = _swa_prompt(qb, k, v, p["bias_t"])
    else:
        oa, s_new = _delta_sample(cs, z, gb, p["w_onorm"], s0, bt=min(b, 8))
        bias_s = p["bias_t"][:, :WINDOW + l, :].reshape(N_KV_B, WINDOW + l, GROUP_B, WINDOW)[..., :l]
        bias_s = bias_s.reshape(N_KV_B, WINDOW + l, GROUP_B * l)
        ob, k_new, v_new = _swa_sample(qb, k, v, k_cache.reshape(b, WINDOW, KV_B), v_cache.reshape(b, WINDOW, KV_B),
                                       bias_s, bt=min(b, 8))
    bt, tt = _tiles(b, l, 512)
    x1 = _merge(x, oa, ob, gates, mod, w["wpa"], w["wpb"], w["wout"], p["ln1_g"], p["ln1_b"],
                layer=layer, bt=bt, tt=tt, alpha=alpha)
    x2 = _mlp(x1, mod, w["wup"], w["wdn"], p["ln2_g"], p["ln2_b"], layer=layer, bt=bt, tt=tt, alpha=alpha)
    return (x2, s_new, conv_out, k_new.reshape(b, WINDOW, N_KV_B, HD_B), v_new.reshape(b, WINDOW, N_KV_B, HD_B))


def kernel(x_prompt, x_sample, state_delta, state_conv, cache_k, cache_v, c_prompt, c_sample, rel_bias, w_ada, b_ada, w_in, w_conv, a_log, dt_bias, w_onorm, sinks, w_pa, w_pb, w_out, ln1_g, ln1_b, w_up, w_down, ln2_g, ln2_b):
    depth = w_in.shape[0]
    alpha = (2 * depth) ** 0.25
    bp = x_prompt.shape[0]
    mod_all = _ada(jnp.concatenate([c_prompt, c_sample], axis=0), w_ada, b_ada)
    bias_t = _bias_table_t(rel_bias, sinks)
    wm, wba = _prep_w_in(w_in)
    w = dict(wm=wm, wba=wba, wpa=_cast_bf16(w_pa), wpb=_cast_bf16(w_pb), wout=_cast_bf16(w_out),
             wup=_cast_bf16(w_up), wdn=_cast_bf16(w_down))
    yp, ys = x_prompt, x_sample
    outs = [[] for _ in range(8)]
    for l in range(depth):
        p = _layer_params(l, w_conv, a_log, dt_bias, w_onorm, ln1_g, ln1_b, ln2_g, ln2_b)
        p["bias_t"] = bias_t[l]
        mod_p = mod_all[l, :bp][:, None, :]
        mod_s = mod_all[l, bp:][:, None, :]
        zero_conv = jnp.zeros((bp, CONV_W - 1, CONV_DIM), x_prompt.dtype)
        yp, *rest_p = _trunk_layer(yp, mod_p, l, w, p, alpha, zero_conv, None, None, None)
        ys, *rest_s = _trunk_layer(ys, mod_s, l, w, p, alpha, state_conv[l], state_delta[l], cache_k[l], cache_v[l])
        for acc, val in zip(outs, rest_p + rest_s):
            acc.append(val)
    return (yp, ys) + tuple(jnp.stack(a) for a in outs)
```

```python
import functools
import math

import numpy as np
import jax
import jax.numpy as jnp
from jax import lax
from jax.experimental import pallas as pl
from jax.experimental.pallas import tpu as pltpu

F32 = jnp.float32
BF16 = jnp.bfloat16

D_MODEL = 1024
N_HEADS_A = 8
DK_A = 128
CONV_W = 4
CONV_DIM = 3 * D_MODEL
CHUNK = 64
HD_B = 64
N_HEADS_B = 16
N_KV_B = 4
GROUP_B = N_HEADS_B // N_KV_B
KV_B = N_KV_B * HD_B
WINDOW = 128
N_BUCKETS = 32
MAX_DISTANCE = 128
D_FF = 4 * D_MODEL
LN_EPS = 1e-5
RMS_EPS = 1e-6

OFF_Z = CONV_DIM
OFF_QB = OFF_Z + D_MODEL
OFF_KVB = OFF_QB + D_MODEL
OFF_GATES = OFF_KVB + 2 * KV_B
N_MAIN = OFF_GATES + 2 * D_MODEL
GB_LANES = 128

V7X_VMEM_BYTES = 64 * 1024 * 1024
VMEM_LIMIT = V7X_VMEM_BYTES - 8 * 1024 * 1024


def _params(n_grid):
    return pltpu.CompilerParams(dimension_semantics=("arbitrary",) * n_grid, vmem_limit_bytes=VMEM_LIMIT)


def _dot(a, b):
    return jnp.dot(a, b, preferred_element_type=F32)


def _dot_nt(a, b):
    return lax.dot_general(a, b, (((1,), (1,)), ((), ())), preferred_element_type=F32)


def _dot_tn(a, b):
    return lax.dot_general(a, b, (((0,), (0,)), ((), ())), preferred_element_type=F32)


def _sigmoid(x):
    return 1.0 / (1.0 + jnp.exp(-x))


def _silu(x):
    return x * _sigmoid(x)


def _const_spec(shape):
    nd = len(shape)
    return pl.BlockSpec(shape, lambda *_: (0,) * nd, pipeline_mode=pl.Buffered(1))


def _layer_spec(shape, layer):
    nd = len(shape)
    return pl.BlockSpec((None,) + tuple(shape), lambda *_: (layer,) + (0,) * nd, pipeline_mode=pl.Buffered(1))


def _cast_body(w_ref, o_ref):
    o_ref[...] = w_ref[...].astype(BF16)


def _cast_bf16(w):
    depth, k, n = w.shape
    bk = max(8, min(k, (1024 * 1024) // n))
    spec = pl.BlockSpec((None, bk, n), lambda l, i: (l, i, 0))
    return pl.pallas_call(
        _cast_body, grid=(depth, k // bk), in_specs=[spec], out_specs=spec,
        out_shape=jax.ShapeDtypeStruct(w.shape, BF16), compiler_params=_params(2), name="cast_bf16",
    )(w)


def _prep_w_in_body(w_ref, wm_ref, wba_ref):
    o_ba = CONV_DIM + D_MODEL
    o_qb = o_ba + 2 * N_HEADS_A
    w = w_ref[...]
    wm_ref[:, :o_ba] = w[:, :o_ba].astype(BF16)
    wm_ref[:, o_ba:] = w[:, o_qb:].astype(BF16)
    pad = jnp.zeros((w.shape[0], GB_LANES - 2 * N_HEADS_A), F32)
    wba_ref[...] = jnp.concatenate([w[:, o_ba:o_qb], pad], axis=1).astype(BF16)


def _prep_w_in(w_in):
    depth, k, n = w_in.shape
    bk = 128
    return pl.pallas_call(
        _prep_w_in_body,
        grid=(depth, k // bk),
        in_specs=[pl.BlockSpec((None, bk, n), lambda l, i: (l, i, 0))],
        out_specs=[pl.BlockSpec((None, bk, N_MAIN), lambda l, i: (l, i, 0)),
                   pl.BlockSpec((None, bk, GB_LANES), lambda l, i: (l, i, 0))],
        out_shape=[jax.ShapeDtypeStruct((depth, k, N_MAIN), BF16), jax.ShapeDtypeStruct((depth, k, GB_LANES), BF16)],
        compiler_params=_params(2),
        name="prep_w_in",
    )(w_in)


def _ada_body(c_ref, w_ref, b_ref, o_ref):
    s = _silu(c_ref[...]).astype(BF16)
    o_ref[...] = _dot(s, w_ref[...].astype(BF16)) + b_ref[...]


def _ada(c_all, w_ada, b_ada):
    depth = w_ada.shape[0]
    n_rows = c_all.shape[0]
    n_col = w_ada.shape[2] // D_MODEL
    return pl.pallas_call(
        _ada_body,
        grid=(depth, n_col),
        in_specs=[pl.BlockSpec((n_rows, D_MODEL), lambda l, n: (0, 0)),
                  pl.BlockSpec((None, D_MODEL, D_MODEL), lambda l, n: (l, 0, n)),
                  pl.BlockSpec((None, 1, D_MODEL), lambda l, n: (l, 0, n))],
        out_specs=pl.BlockSpec((None, n_rows, D_MODEL), lambda l, n: (l, 0, n)),
        out_shape=jax.ShapeDtypeStruct((depth, n_rows, w_ada.shape[2]), F32),
        compiler_params=_params(2),
        name="ada",
    )(c_all, w_ada, b_ada.reshape(depth, 1, -1))


def _bucket_table():
    r = np.arange(WINDOW)[:, None]
    c = np.arange(2 * WINDOW)[None, :]
    dist = WINDOW + r - c
    n = np.maximum(dist, 0)
    max_exact = N_BUCKETS // 2
    ratio = np.maximum(n, max_exact).astype(np.float32) / np.float32(max_exact)
    large = max_exact + (np.log(ratio) / np.float32(math.log(MAX_DISTANCE / max_exact))
                         * np.float32(N_BUCKETS - max_exact)).astype(np.int32)
    large = np.minimum(large, N_BUCKETS - 1)
    bucket = np.where(n < max_exact, n, large).astype(np.int32)
    valid = ((dist >= 0) & (dist < WINDOW)).astype(np.int32)
    return bucket, valid


def _bias_t_body(rb_ref, sink_ref, bucket_ref, valid_ref, o_ref):
    l = pl.program_id(0)
    h = pl.program_id(1)
    bucket = bucket_ref[...]
    acc = jnp.zeros(bucket.shape, F32)
    for j in range(N_BUCKETS):
        acc = jnp.where(bucket == j, rb_ref[j, h], acc)
    acc = jnp.where(valid_ref[...] > 0, acc, -jnp.inf)
    key = lax.broadcasted_iota(jnp.int32, bucket.shape, 0)
    o_ref[...] = jnp.where(key == 0, sink_ref[l, h], acc)


def _bias_table_t(rel_bias, sinks):
    depth = sinks.shape[0]
    bucket, valid = _bucket_table()
    return pl.pallas_call(
        _bias_t_body,
        grid=(depth, N_HEADS_B),
        in_specs=[pl.BlockSpec(memory_space=pltpu.SMEM), pl.BlockSpec(memory_space=pltpu.SMEM),
                  pl.BlockSpec((2 * WINDOW, WINDOW), lambda l, h: (0, 0)),
                  pl.BlockSpec((2 * WINDOW, WINDOW), lambda l, h: (0, 0))],
        out_specs=pl.BlockSpec((None, None, 2 * WINDOW, WINDOW), lambda l, h: (l, h // GROUP_B, 0, h % GROUP_B)),
        out_shape=jax.ShapeDtypeStruct((depth, N_KV_B, 2 * WINDOW, GROUP_B * WINDOW), F32),
        compiler_params=_params(2),
        name="bias_table_t",
    )(rel_bias.astype(F32), sinks.astype(F32), jnp.asarray(bucket.T.copy()), jnp.asarray(valid.T.copy()))


def _front_body(x_ref, mod_ref, st_ref, wm_ref, wba_ref, wconv_ref, gpar_ref,
                cs_ref, z_ref, gb_ref, qb_ref, k_ref, v_ref, gates_ref, cst_ref, cbuf, *, bt, tt):
    m = bt * tt
    sh1 = mod_ref[:, :, 0:D_MODEL]
    sc1 = mod_ref[:, :, D_MODEL:2 * D_MODEL]
    h = (x_ref[...] * (1.0 + sc1) + sh1).reshape(m, D_MODEL).astype(BF16)

    @pl.when(pl.program_id(1) == 0)
    def _():
        cbuf[:, 5:8, :] = st_ref[...]

    for s in range(3):
        cols = slice(s * D_MODEL, (s + 1) * D_MODEL)
        cbuf[:, 8:8 + tt, cols] = _dot(h, wm_ref[:, cols]).reshape(bt, tt, D_MODEL)
        y = cbuf[:, 5:5 + tt, cols] * wconv_ref[0:1, cols]
        for j in range(1, CONV_W):
            y = y + cbuf[:, 5 + j:5 + j + tt, cols] * wconv_ref[j:j + 1, cols]
        cs_ref[:, :, cols] = _silu(y).astype(cs_ref.dtype)
    tail = cbuf[:, tt + 5:tt + 8, :]
    cst_ref[...] = tail
    cbuf[:, 5:8, :] = tail

    z_ref[...] = _dot(h, wm_ref[:, OFF_Z:OFF_QB]).astype(z_ref.dtype).reshape(bt, tt, D_MODEL)
    qb_ref[...] = (_dot(h, wm_ref[:, OFF_QB:OFF_KVB]) * (HD_B ** -0.5)).astype(BF16).reshape(bt, tt, D_MODEL)
    kv = _dot(h, wm_ref[:, OFF_KVB:OFF_GATES])
    k_ref[...] = kv[:, :KV_B].reshape(bt, tt, KV_B)
    v_ref[...] = kv[:, KV_B:].reshape(bt, tt, KV_B)
    gates = _sigmoid(_dot(h, wm_ref[:, OFF_GATES:N_MAIN]))
    gates_ref[...] = gates.astype(gates_ref.dtype).reshape(bt, tt, 2 * D_MODEL)

    ba = _dot(h, wba_ref[...])
    xg = ba + gpar_ref[1:2, :]
    softplus = jnp.maximum(xg, 0.0) + jnp.log(1.0 + jnp.exp(-jnp.abs(xg)))
    lane = lax.broadcasted_iota(jnp.int32, ba.shape, 1)
    gb = jnp.where(lane < N_HEADS_A, _sigmoid(ba), gpar_ref[0:1, :] * softplus)
    gb_ref[...] = gb.reshape(bt, tt, GB_LANES)


def _front(x, mod, conv_state, wm, wba, wconv, gpar, *, layer, bt, tt):
    b, l, _ = x.shape
    grid = (b // bt, l // tt)
    tok = lambda n, dt=F32: jax.ShapeDtypeStruct((b, l, n), dt)
    act = BF16 if tt % 16 == 0 else F32
    tok_spec = lambda n: pl.BlockSpec((bt, tt, n), lambda i, t: (i, t, 0))
    return pl.pallas_call(
        functools.partial(_front_body, bt=bt, tt=tt),
        grid=grid,
        in_specs=[tok_spec(D_MODEL),
                  pl.BlockSpec((bt, 1, 6 * D_MODEL), lambda i, t: (i, 0, 0)),
                  pl.BlockSpec((bt, CONV_W - 1, CONV_DIM), lambda i, t: (i, 0, 0)),
                  _layer_spec((D_MODEL, N_MAIN), layer),
                  _layer_spec((D_MODEL, GB_LANES), layer),
                  _const_spec((CONV_W, CONV_DIM)),
                  _const_spec((2, GB_LANES))],
        out_specs=[tok_spec(CONV_DIM), tok_spec(D_MODEL), tok_spec(GB_LANES), tok_spec(D_MODEL),
                   tok_spec(KV_B), tok_spec(KV_B), tok_spec(2 * D_MODEL),
                   pl.BlockSpec((bt, CONV_W - 1, CONV_DIM), lambda i, t: (i, 0, 0))],
        out_shape=[tok(CONV_DIM, act), tok(D_MODEL, act), tok(GB_LANES), tok(D_MODEL, BF16),
                   tok(KV_B), tok(KV_B), tok(2 * D_MODEL, act),
                   jax.ShapeDtypeStruct((b, CONV_W - 1, CONV_DIM), F32)],
        scratch_shapes=[pltpu.VMEM((bt, tt + 8, CONV_DIM), F32)],
        compiler_params=_params(2),
        name="front",
    )(x, mod, conv_state, wm, wba, wconv, gpar)


def _delta_masks(r, block):
    i = lax.broadcasted_iota(jnp.int32, (r, r), 0)
    j = lax.broadcasted_iota(jnp.int32, (r, r), 1)
    shift = int(math.log2(block))
    same = (i >> shift) == (j >> shift)
    levels = [((i >> (s + 1)) == (j >> (s + 1))) & ((i >> s) != (j >> s)) for s in range(shift)]
    return dict(tri=same & (i >= j), strict=same & (i > j), upper=same & (i < j), eye=(i == j).astype(F32),
                levels=levels)


def _split_dot(a01, x):
    a01 = a01.astype(BF16)
    hi = x.astype(BF16)
    rest = x - hi.astype(F32)
    mid = rest.astype(BF16)
    lo = (rest - mid.astype(F32)).astype(BF16)
    return _dot(a01, hi) + _dot(a01, mid) + _dot(a01, lo)


def _decay_sums(g, masks):
    return _split_dot(masks["tri"], g), _split_dot(masks["upper"], g)


def _delta_pre(problems, masks):
    tri, strict, eye, levels = masks["tri"], masks["strict"], masks["eye"], masks["levels"]
    n = len(problems)
    r = problems[0][0].shape[0]
    lhs, ks, rhs, decays, qgs, kds = [], [], [], [], [], []
    for qr, kr, v, beta_b, gc, rev, gc_row in problems:
        q = qr * lax.rsqrt(jnp.sum(qr * qr, -1, keepdims=True) + 1e-6) * (DK_A ** -0.5)
        k = kr * lax.rsqrt(jnp.sum(kr * kr, -1, keepdims=True) + 1e-6)
        gc_col = gc[:, :r] if r <= DK_A else jnp.concatenate([gc] * (r // DK_A), axis=1)
        decays.append(jnp.exp(jnp.where(tri, gc_col - gc_row, -jnp.inf)))
        eg = jnp.exp(gc)
        kb = k * beta_b
        lhs.append(jnp.concatenate([kb, q], axis=0).astype(BF16))
        ks.append(k.astype(BF16))
        rhs.append(jnp.concatenate([v * beta_b, kb * eg], axis=1).astype(BF16))
        qgs.append((q * eg).astype(BF16))
        kds.append((k * jnp.exp(rev)).astype(BF16))
    kk = [_dot_nt(lhs[i], ks[i]) for i in range(n)]
    mm = [jnp.where(strict, kk[i][:r] * decays[i], 0.0) for i in range(n)]
    qk = [(kk[i][r:] * decays[i]).astype(BF16) for i in range(n)]
    t = [eye - jnp.where(levels[0], mm[i], 0.0) for i in range(n)]
    for lvl in levels[1:]:
        tb = [t[i].astype(BF16) for i in range(n)]
        x = [_dot(tb[i], jnp.where(lvl, mm[i], 0.0).astype(BF16)).astype(BF16) for i in range(n)]
        y = [_dot(x[i], tb[i]) for i in range(n)]
        t = [t[i] - y[i] for i in range(n)]
    uw = [_dot(t[i].astype(BF16), rhs[i]) for i in range(n)]
    return [(uw[i][:, :DK_A], uw[i][:, DK_A:].astype(BF16), qgs[i], kds[i], qk[i]) for i in range(n)]


def _gated_norm(o, z, w_onorm):
    o = o * lax.rsqrt(jnp.mean(o * o, -1, keepdims=True) + RMS_EPS) * w_onorm
    return o * _silu(z)


def _delta_prompt_body(cs_ref, z_ref, gb_ref, wn_ref, o_ref, s_out_ref,
                       s_ref, u_s, o_s, wq_s, qkd_s, egl_s, *, tt):
    n_c = tt // CHUNK
    masks = _delta_masks(tt, CHUNK)

    @pl.when(pl.program_id(1) == 0)
    def _():
        s_ref[...] = jnp.zeros(s_ref.shape, F32)

    gbv = gb_ref[...]
    gc_all, rev_all = _decay_sums(gbv, masks)
    gc_t = gc_all.T
    heads = range(N_HEADS_A)
    hcols = [slice(hd * DK_A, (hd + 1) * DK_A) for hd in heads]
    problems = []
    for hd in heads:
        lg = N_HEADS_A + hd
        problems.append((cs_ref[:, hcols[hd]].astype(F32),
                         cs_ref[:, D_MODEL + hd * DK_A:D_MODEL + (hd + 1) * DK_A].astype(F32),
                         cs_ref[:, 2 * D_MODEL + hd * DK_A:2 * D_MODEL + (hd + 1) * DK_A].astype(F32),
                         jnp.broadcast_to(gbv[:, hd:hd + 1], (tt, DK_A)),
                         jnp.broadcast_to(gc_all[:, lg:lg + 1], (tt, DK_A)),
                         jnp.broadcast_to(rev_all[:, lg:lg + 1], (tt, DK_A)),
                         jnp.broadcast_to(gc_t[lg:lg + 1, :], (tt, tt))))
    for hd, (u, w, qg, kd, qk) in enumerate(_delta_pre(problems, masks)):
        u_s[:, hcols[hd]] = u
        kdt = kd.T
        for c in range(n_c):
            blk = slice(c * CHUNK, (c + 1) * CHUNK)
            idx = hd * n_c + c
            wq_s[idx, 0:CHUNK, :] = w[blk]
            wq_s[idx, CHUNK:2 * CHUNK, :] = qg[blk]
            qkd_s[idx, 0:CHUNK, :] = qk[blk, blk]
            qkd_s[idx, CHUNK:CHUNK + DK_A, :] = kdt[:, blk]
            last = (c + 1) * CHUNK - 1
            egl_s[idx] = jnp.broadcast_to(jnp.exp(problems[hd][4][last:last + 1, :]), (8, DK_A))

    states = [s_ref[hd] for hd in heads]
    for c in range(n_c):
        blk = slice(c * CHUNK, (c + 1) * CHUNK)
        idx = [hd * n_c + c for hd in heads]
        ws = [_dot(wq_s[idx[hd]], states[hd].astype(BF16)) for hd in heads]
        v_new = [(u_s[blk, hcols[hd]] - ws[hd][:CHUNK]).astype(BF16) for hd in heads]
        upd = [_dot(qkd_s[idx[hd]], v_new[hd]) for hd in heads]
        for hd in heads:
            o_s[blk, hcols[hd]] = ws[hd][CHUNK:] + upd[hd][:CHUNK]
            states[hd] = states[hd] * egl_s[idx[hd]][0:1, :] + upd[hd][CHUNK:]
    for hd in heads:
        s_ref[hd] = states[hd]
        o_ref[:, hcols[hd]] = _gated_norm(o_s[:, hcols[hd]], z_ref[:, hcols[hd]].astype(F32), wn_ref[...]).astype(BF16)

    @pl.when(pl.program_id(1) == pl.num_programs(1) - 1)
    def _():
        s_out_ref[...] = s_ref[...]


def _delta_prompt(cs, z, gb, w_onorm, *, tt):
    b, l, _ = cs.shape
    n_c = tt // CHUNK
    tok_spec = lambda n: pl.BlockSpec((None, tt, n), lambda i, t: (i, t, 0))
    return pl.pallas_call(
        functools.partial(_delta_prompt_body, tt=tt),
        grid=(b, l // tt),
        in_specs=[tok_spec(CONV_DIM), tok_spec(D_MODEL), tok_spec(GB_LANES), _const_spec((1, DK_A))],
        out_specs=[tok_spec(D_MODEL),
                   pl.BlockSpec((None, N_HEADS_A, DK_A, DK_A), lambda i, t: (i, 0, 0, 0))],
        out_shape=[jax.ShapeDtypeStruct((b, l, D_MODEL), BF16),
                   jax.ShapeDtypeStruct((b, N_HEADS_A, DK_A, DK_A), F32)],
        scratch_shapes=[pltpu.VMEM((N_HEADS_A, DK_A, DK_A), F32),
                        pltpu.VMEM((tt, D_MODEL), F32), pltpu.VMEM((tt, D_MODEL), F32),
                        pltpu.VMEM((N_HEADS_A * n_c, 2 * CHUNK, DK_A), BF16),
                        pltpu.VMEM((N_HEADS_A * n_c, CHUNK + DK_A, CHUNK), BF16),
                        pltpu.VMEM((N_HEADS_A * n_c, 8, DK_A), F32)],
        compiler_params=_params(2),
        name="delta_prompt",
    )(cs, z, gb, w_onorm)


def _delta_sample_body(cs_ref, z_ref, gb_ref, wn_ref, s0_ref, *rest, bt, nb, ls):
    o_ref, s_out_ref = rest[-2:]
    r = N_HEADS_A * ls
    masks = _delta_masks(r, ls)

    heads = range(N_HEADS_A)
    hrows = [slice(hd * ls, (hd + 1) * ls) for hd in heads]
    hcols = [slice(hd * DK_A, (hd + 1) * DK_A) for hd in heads]

    def per_group(gi, carry):
        bis = [gi * nb + d for d in range(nb)]
        problems, egl = [], []
        for bi in bis:
            cs = cs_ref[bi].astype(F32)
            gbv = gb_ref[bi]
            stack = lambda base: jnp.concatenate([cs[:, base + hd * DK_A: base + (hd + 1) * DK_A] for hd in heads],
                                                 axis=0)
            beta_b = jnp.concatenate([jnp.broadcast_to(gbv[:, hd:hd + 1], (ls, DK_A)) for hd in heads], axis=0)
            g_b = jnp.concatenate(
                [jnp.broadcast_to(gbv[:, N_HEADS_A + hd:N_HEADS_A + hd + 1], (ls, DK_A)) for hd in heads], axis=0)
            gc, rev = _decay_sums(g_b, masks)
            egl.append(jnp.exp(gc))
            problems.append((stack(0), stack(D_MODEL), stack(2 * D_MODEL), beta_b, gc, rev, gc.T[:r, :]))
        pre = _delta_pre(problems, masks)
        both = [[_dot(jnp.concatenate([pre[d][1][hrows[hd]], pre[d][2][hrows[hd]]], axis=0),
                      s0_ref[bis[d], hd].astype(BF16)) for hd in heads] for d in range(nb)]
        v_new = [(pre[d][0] - jnp.concatenate([both[d][hd][:ls] for hd in heads], axis=0)).astype(BF16)
                 for d in range(nb)]
        qkv = [_dot(pre[d][4], v_new[d]) for d in range(nb)]
        upd = [[_dot_tn(pre[d][3][hrows[hd]], v_new[d][hrows[hd]]) for hd in heads] for d in range(nb)]
        for d, bi in enumerate(bis):
            zb = z_ref[bi].astype(F32)
            for hd in heads:
                last = (hd + 1) * ls - 1
                s_out_ref[bi, hd] = s0_ref[bi, hd] * egl[d][last:last + 1, :] + upd[d][hd]
                o = both[d][hd][ls:] + qkv[d][hrows[hd]]
                o_ref[bi, :, hcols[hd]] = _gated_norm(o, zb[:, hcols[hd]], wn_ref[...]).astype(BF16)
        return carry

    lax.fori_loop(0, bt // nb, per_group, 0)


def _fill_layer(prev):
    if prev is None:
        return [], []
    return [prev], [pl.BlockSpec(memory_space=pl.ANY)]


def _delta_sample(cs, z, gb, w_onorm, s0_all, s_prev, *, layer, bt):
    b, ls, _ = cs.shape
    tok_spec = lambda n: pl.BlockSpec((bt, ls, n), lambda i: (i, 0, 0))
    st_spec = pl.BlockSpec((None, bt, N_HEADS_A, DK_A, DK_A), lambda i: (layer, i, 0, 0, 0))
    extra, extra_specs = _fill_layer(s_prev)
    return pl.pallas_call(
        functools.partial(_delta_sample_body, bt=bt, nb=min(bt, 4), ls=ls),
        grid=(b // bt,),
        in_specs=[tok_spec(CONV_DIM), tok_spec(D_MODEL), tok_spec(GB_LANES), _const_spec((1, DK_A)), st_spec]
        + extra_specs,
        out_specs=[tok_spec(D_MODEL), st_spec],
        out_shape=[jax.ShapeDtypeStruct((b, ls, D_MODEL), BF16), jax.ShapeDtypeStruct(s0_all.shape, F32)],
        input_output_aliases={5: 1} if extra else {},
        compiler_params=_params(1),
        name="delta_sample",
    )(cs, z, gb, w_onorm, s0_all, *extra)


def _softmax_keys(s):
    p = jnp.exp(s - jnp.max(s, axis=0, keepdims=True))
    return (p / jnp.sum(p, axis=0, keepdims=True)).astype(BF16)


def _zero_key0(x):
    return jnp.where(lax.broadcasted_iota(jnp.int32, x.shape, 0) == 0, 0.0, x)


def _kv_cols(kvh):
    return slice(kvh * HD_B, (kvh + 1) * HD_B)


def _group_queries(q, kvh):
    return jnp.concatenate([q[:, hh * HD_B:(hh + 1) * HD_B] for hh in range(kvh * GROUP_B, (kvh + 1) * GROUP_B)],
                           axis=0)


def _swa_prompt_body(q_ref, kp_ref, kc_ref, vp_ref, vc_ref, bias_ref, o_ref, ko_ref, vo_ref):
    @pl.when(pl.program_id(1) == pl.num_programs(1) - 1)
    def _():
        ko_ref[...] = kc_ref[...]
        vo_ref[...] = vc_ref[...]

    n_q = GROUP_B * WINDOW
    key = lax.broadcasted_iota(jnp.int32, (2 * WINDOW, n_q), 0)
    no_prev = (pl.program_id(1) == 0) & (key >= 1) & (key < WINDOW)
    k2 = _zero_key0(jnp.concatenate([kp_ref[...], kc_ref[...]], axis=0)).astype(BF16)
    v2t = _zero_key0(jnp.concatenate([vp_ref[...], vc_ref[...]], axis=0)).T.astype(BF16)
    q = q_ref[...]
    kvs = range(N_KV_B)
    s = [_dot_nt(k2[:, _kv_cols(kvh)], _group_queries(q, kvh)) for kvh in kvs]
    pn = [_softmax_keys(jnp.where(no_prev, -jnp.inf, s[kvh] + bias_ref[kvh])) for kvh in kvs]
    ot = [_dot(v2t[_kv_cols(kvh), :], pn[kvh]) for kvh in kvs]
    for kvh in kvs:
        for g in range(GROUP_B):
            hh = kvh * GROUP_B + g
            o_ref[:, hh * HD_B:(hh + 1) * HD_B] = ot[kvh][:, g * WINDOW:(g + 1) * WINDOW].T.astype(BF16)


def _swa_prompt(qb, k, v, bias_t):
    b, l, _ = qb.shape
    cur = lambda n: pl.BlockSpec((None, WINDOW, n), lambda i, t: (i, t, 0))
    prev = lambda n: pl.BlockSpec((None, WINDOW, n), lambda i, t: (i, jnp.maximum(t - 1, 0), 0))
    last = lambda n: pl.BlockSpec((None, WINDOW, n), lambda i, t: (i, 0, 0))
    return pl.pallas_call(
        _swa_prompt_body,
        grid=(b, l // WINDOW),
        in_specs=[cur(D_MODEL), prev(KV_B), cur(KV_B), prev(KV_B), cur(KV_B),
                  _const_spec((N_KV_B, 2 * WINDOW, GROUP_B * WINDOW))],
        out_specs=[cur(D_MODEL), last(KV_B), last(KV_B)],
        out_shape=[jax.ShapeDtypeStruct((b, l, D_MODEL), BF16),
                   jax.ShapeDtypeStruct((b, WINDOW, KV_B), F32), jax.ShapeDtypeStruct((b, WINDOW, KV_B), F32)],
        compiler_params=_params(2),
        name="swa_prompt",
    )(qb, k, k, v, v, bias_t)


def _swa_sample_body(q_ref, kn_ref, vn_ref, kc_ref, vc_ref, bias_ref, *rest, bt, nb, ls):
    o_ref, ko_ref, vo_ref = rest[-3:]
    kvs = range(N_KV_B)

    def per_group(gi, carry):
        bis = [gi * nb + d for d in range(nb)]
        ks, vs, s = [], [], []
        for bi in bis:
            kf = jnp.concatenate([kc_ref[bi], kn_ref[bi]], axis=0)
            vf = jnp.concatenate([vc_ref[bi], vn_ref[bi]], axis=0)
            ko_ref[bi] = kf[ls:, :]
            vo_ref[bi] = vf[ls:, :]
            ks.append(_zero_key0(kf).astype(BF16))
            vs.append(_zero_key0(vf).astype(BF16))
        for d, bi in enumerate(bis):
            q = q_ref[bi]
            s.append([_dot_nt(ks[d][:, _kv_cols(kvh)], _group_queries(q, kvh)) for kvh in kvs])
        pn = [[_softmax_keys(s[d][kvh] + bias_ref[kvh]) for kvh in kvs] for d in range(nb)]
        o = [[_dot_tn(pn[d][kvh], vs[d][:, _kv_cols(kvh)]) for kvh in kvs] for d in range(nb)]
        for d, bi in enumerate(bis):
            for kvh in kvs:
                for g in range(GROUP_B):
                    hh = kvh * GROUP_B + g
                    o_ref[bi, :, hh * HD_B:(hh + 1) * HD_B] = o[d][kvh][g * ls:(g + 1) * ls].astype(BF16)
        return carry

    lax.fori_loop(0, bt // nb, per_group, 0)


def _swa_sample(qb, k_new, v_new, k_cache_all, v_cache_all, bias_s, k_prev, v_prev, *, layer, bt):
    b, ls, _ = qb.shape
    n_keys = WINDOW + ls
    new = lambda n: pl.BlockSpec((bt, ls, n), lambda i: (i, 0, 0))
    cache = pl.BlockSpec((None, bt, WINDOW, KV_B), lambda i: (layer, i, 0, 0))
    extra_k, specs_k = _fill_layer(k_prev)
    extra_v, specs_v = _fill_layer(v_prev)
    return pl.pallas_call(
        functools.partial(_swa_sample_body, bt=bt, nb=min(bt, 4), ls=ls),
        grid=(b // bt,),
        in_specs=[new(D_MODEL), new(KV_B), new(KV_B), cache, cache,
                  _const_spec((N_KV_B, n_keys, GROUP_B * ls))] + specs_k + specs_v,
        out_specs=[new(D_MODEL), cache, cache],
        out_shape=[jax.ShapeDtypeStruct((b, ls, D_MODEL), BF16),
                   jax.ShapeDtypeStruct(k_cache_all.shape, F32), jax.ShapeDtypeStruct(v_cache_all.shape, F32)],
        input_output_aliases={6: 1, 7: 2} if extra_k else {},
        compiler_params=_params(1),
        name="swa_sample",
    )(qb, k_new, v_new, k_cache_all, v_cache_all, bias_s, *extra_k, *extra_v)


def _layer_norm(y, g, b):
    mu = jnp.mean(y, -1, keepdims=True)
    d = y - mu
    var = jnp.mean(d * d, -1, keepdims=True)
    return d * lax.rsqrt(var + LN_EPS) * g + b


def _merge_body(x_ref, oa_ref, ob_ref, gates_ref, mod_ref, wpa_ref, wpb_ref, wout_ref, lng_ref, lnb_ref,
                o_ref, *, bt, tt, alpha):
    m = bt * tt
    ga = gates_ref[:, :, 0:D_MODEL].astype(F32).reshape(m, D_MODEL)
    gb = gates_ref[:, :, D_MODEL:2 * D_MODEL].astype(F32).reshape(m, D_MODEL)
    mixed = ga * _dot(oa_ref[...].reshape(m, D_MODEL), wpa_ref[...]) \
        + gb * _dot(ob_ref[...].reshape(m, D_MODEL), wpb_ref[...])
    attn = _dot(mixed.astype(BF16), wout_ref[...]).reshape(bt, tt, D_MODEL)
    gt1 = mod_ref[:, :, 2 * D_MODEL:3 * D_MODEL]
    o_ref[...] = _layer_norm(alpha * x_ref[...] + gt1 * attn, lng_ref[...], lnb_ref[...])


def _merge(x, oa, ob, gates, mod, wpa, wpb, wout, lng, lnb, *, layer, bt, tt, alpha):
    b, l, _ = x.shape
    tok_spec = lambda n: pl.BlockSpec((bt, tt, n), lambda i, t: (i, t, 0))
    return pl.pallas_call(
        functools.partial(_merge_body, bt=bt, tt=tt, alpha=alpha),
        grid=(b // bt, l // tt),
        in_specs=[tok_spec(D_MODEL), tok_spec(D_MODEL), tok_spec(D_MODEL), tok_spec(2 * D_MODEL),
                  pl.BlockSpec((bt, 1, 6 * D_MODEL), lambda i, t: (i, 0, 0)),
                  _layer_spec((D_MODEL, D_MODEL), layer), _layer_spec((D_MODEL, D_MODEL), layer),
                  _layer_spec((D_MODEL, D_MODEL), layer),
                  _const_spec((1, D_MODEL)), _const_spec((1, D_MODEL))],
        out_specs=tok_spec(D_MODEL),
        out_shape=jax.ShapeDtypeStruct((b, l, D_MODEL), F32),
        compiler_params=_params(2),
        name="merge",
    )(x, oa, ob, gates, mod, wpa, wpb, wout, lng, lnb)


def _mlp_body(x_ref, mod_ref, wup_ref, wdn_ref, lng_ref, lnb_ref, o_ref, *, bt, tt, alpha):
    m = bt * tt
    x = x_ref[...]
    sh2 = mod_ref[:, :, 3 * D_MODEL:4 * D_MODEL]
    sc2 = mod_ref[:, :, 4 * D_MODEL:5 * D_MODEL]
    gt2 = mod_ref[:, :, 5 * D_MODEL:6 * D_MODEL]
    h2 = (x * (1.0 + sc2) + sh2).reshape(m, D_MODEL).astype(BF16)
    ff = jnp.zeros((m, D_MODEL), F32)
    for c in range(D_FF // D_MODEL):
        cols = slice(c * D_MODEL, (c + 1) * D_MODEL)
        a = jnp.maximum(_dot(h2, wup_ref[:, cols]), 0.0)
        ff = ff + _dot((a * a).astype(BF16), wdn_ref[cols, :])
    o_ref[...] = _layer_norm(alpha * x + gt2 * ff.reshape(bt, tt, D_MODEL), lng_ref[...], lnb_ref[...])


def _mlp(x, mod, wup, wdn, lng, lnb, *, layer, bt, tt, alpha):
    b, l, _ = x.shape
    tok_spec = pl.BlockSpec((bt, tt, D_MODEL), lambda i, t: (i, t, 0))
    return pl.pallas_call(
        functools.partial(_mlp_body, bt=bt, tt=tt, alpha=alpha),
        grid=(b // bt, l // tt),
        in_specs=[tok_spec, pl.BlockSpec((bt, 1, 6 * D_MODEL), lambda i, t: (i, 0, 0)),
                  _layer_spec((D_MODEL, D_FF), layer), _layer_spec((D_FF, D_MODEL), layer),
                  _const_spec((1, D_MODEL)), _const_spec((1, D_MODEL))],
        out_specs=tok_spec,
        out_shape=jax.ShapeDtypeStruct((b, l, D_MODEL), F32),
        compiler_params=_params(2),
        name="mlp",
    )(x, mod, wup, wdn, lng, lnb)


def _tiles(b, l, rows):
    tt = min(l, rows)
    bt = max(1, min(b, rows // tt))
    return bt, tt


def _layer_params(l, w_conv, a_log, dt_bias, w_onorm, ln1_g, ln1_b, ln2_g, ln2_b):
    gpar = jnp.zeros((2, GB_LANES), F32)
    gpar = gpar.at[0, N_HEADS_A:2 * N_HEADS_A].set(-jnp.exp(a_log[l].astype(F32)))
    gpar = gpar.at[1, N_HEADS_A:2 * N_HEADS_A].set(dt_bias[l].astype(F32))
    row = lambda a: a[l].reshape(1, -1).astype(F32)
    return dict(wconv=w_conv[l].astype(F32), gpar=gpar, w_onorm=row(w_onorm), ln1_g=row(ln1_g), ln1_b=row(ln1_b),
                ln2_g=row(ln2_g), ln2_b=row(ln2_b))


def _trunk_layer(x, mod, layer, w, p, alpha, conv_state, sample_state):
    b, l, _ = x.shape
    bt, tt = _tiles(b, l, 512 if l % 512 == 0 else 256)
    cs, z, gb, qb, k, v, gates, conv_out = _front(x, mod, conv_state, w["wm"], w["wba"], p["wconv"], p["gpar"],
                                                  layer=layer, bt=bt, tt=tt)
    if sample_state is None:
        oa, s_new = _delta_prompt(cs, z, gb, p["w_onorm"], tt=min(l, 256))
        ob, k_new, v_new = _swa_prompt(qb, k, v, p["bias_t"])
    else:
        s0_all, kc_all, vc_all, s_prev, k_prev, v_prev = sample_state
        oa, s_new = _delta_sample(cs, z, gb, p["w_onorm"], s0_all, s_prev, layer=layer, bt=min(b, 8))
        bias_s = p["bias_t"][:, :WINDOW + l, :].reshape(N_KV_B, WINDOW + l, GROUP_B, WINDOW)[..., :l]
        bias_s = bias_s.reshape(N_KV_B, WINDOW + l, GROUP_B * l)
        ob, k_new, v_new = _swa_sample(qb, k, v, kc_all, vc_all, bias_s, k_prev, v_prev, layer=layer, bt=min(b, 8))
    bt, tt = _tiles(b, l, 512)
    x1 = _merge(x, oa, ob, gates, mod, w["wpa"], w["wpb"], w["wout"], p["ln1_g"], p["ln1_b"],
                layer=layer, bt=bt, tt=tt, alpha=alpha)
    x2 = _mlp(x1, mod, w["wup"], w["wdn"], p["ln2_g"], p["ln2_b"], layer=layer, bt=bt, tt=tt, alpha=alpha)
    return x2, (s_new, conv_out, k_new, v_new)


def kernel(x_prompt, x_sample, state_delta, state_conv, cache_k, cache_v, c_prompt, c_sample, rel_bias, w_ada, b_ada, w_in, w_conv, a_log, dt_bias, w_onorm, sinks, w_pa, w_pb, w_out, ln1_g, ln1_b, w_up, w_down, ln2_g, ln2_b):
    depth = w_in.shape[0]
    alpha = (2 * depth) ** 0.25
    bp = x_prompt.shape[0]
    mod_all = _ada(jnp.concatenate([c_prompt, c_sample], axis=0), w_ada, b_ada)
    bias_t = _bias_table_t(rel_bias, sinks)
    wm, wba = _prep_w_in(w_in)
    w = dict(wm=wm, wba=wba, wpa=_cast_bf16(w_pa), wpb=_cast_bf16(w_pb), wout=_cast_bf16(w_out),
             wup=_cast_bf16(w_up), wdn=_cast_bf16(w_down))
    bs = x_sample.shape[0]
    kc_all = cache_k.reshape(depth, bs, WINDOW, KV_B)
    vc_all = cache_v.reshape(depth, bs, WINDOW, KV_B)
    yp, ys = x_prompt, x_sample
    prompt_outs = [[] for _ in range(4)]
    sample_conv = []
    s_all = k_all = v_all = None
    for l in range(depth):
        p = _layer_params(l, w_conv, a_log, dt_bias, w_onorm, ln1_g, ln1_b, ln2_g, ln2_b)
        p["bias_t"] = bias_t[l]
        mod_p = mod_all[l, :bp][:, None, :]
        mod_s = mod_all[l, bp:][:, None, :]
        zero_conv = jnp.zeros((bp, CONV_W - 1, CONV_DIM), x_prompt.dtype)
        yp, rest_p = _trunk_layer(yp, mod_p, l, w, p, alpha, zero_conv, None)
        ys, (s_all, conv_s, k_all, v_all) = _trunk_layer(ys, mod_s, l, w, p, alpha, state_conv[l],
                                                         (state_delta, kc_all, vc_all, s_all, k_all, v_all))
        for acc, val in zip(prompt_outs, rest_p):
            acc.append(val)
        sample_conv.append(conv_s)
    heads = lambda a: a.reshape(a.shape[:-1] + (N_KV_B, HD_B))
    pd, pc, pk, pv = (jnp.stack(a) for a in prompt_outs)
    return (yp, ys, pd, pc, heads(pk), heads(pv), s_all, jnp.stack(sample_conv), heads(k_all), heads(v_all))
```

```python
import functools
import math

import numpy as np
import jax
import jax.numpy as jnp
from jax import lax
from jax.experimental import pallas as pl
from jax.experimental.pallas import tpu as pltpu

F32 = jnp.float32
BF16 = jnp.bfloat16

D_MODEL = 1024
N_HEADS_A = 8
DK_A = 128
CONV_W = 4
CONV_DIM = 3 * D_MODEL
CHUNK = 64
HD_B = 64
N_HEADS_B = 16
N_KV_B = 4
GROUP_B = N_HEADS_B // N_KV_B
KV_B = N_KV_B * HD_B
WINDOW = 128
N_BUCKETS = 32
MAX_DISTANCE = 128
D_FF = 4 * D_MODEL
LN_EPS = 1e-5
RMS_EPS = 1e-6

OFF_Z = CONV_DIM
OFF_QB = OFF_Z + D_MODEL
OFF_KVB = OFF_QB + D_MODEL
OFF_GATES = OFF_KVB + 2 * KV_B
N_MAIN = OFF_GATES + 2 * D_MODEL
GB_LANES = 128

V7X_VMEM_BYTES = 64 * 1024 * 1024
VMEM_LIMIT = V7X_VMEM_BYTES - 8 * 1024 * 1024


def _params(n_grid):
    return pltpu.CompilerParams(dimension_semantics=("arbitrary",) * n_grid, vmem_limit_bytes=VMEM_LIMIT)


def _dot(a, b):
    return jnp.dot(a, b, preferred_element_type=F32)


def _dot_nt(a, b):
    return lax.dot_general(a, b, (((1,), (1,)), ((), ())), preferred_element_type=F32)


def _dot_tn(a, b):
    return lax.dot_general(a, b, (((0,), (0,)), ((), ())), preferred_element_type=F32)


def _sigmoid(x):
    return 1.0 / (1.0 + jnp.exp(-x))


def _silu(x):
    return x * _sigmoid(x)


def _const_spec(shape):
    nd = len(shape)
    return pl.BlockSpec(shape, lambda *_: (0,) * nd, pipeline_mode=pl.Buffered(1))


def _layer_spec(shape, layer):
    nd = len(shape)
    return pl.BlockSpec((None,) + tuple(shape), lambda *_: (layer,) + (0,) * nd, pipeline_mode=pl.Buffered(1))


def _cast_body(w_ref, o_ref):
    o_ref[...] = w_ref[...].astype(BF16)


def _cast_bf16(w):
    depth, k, n = w.shape
    bk = max(8, min(k, (1024 * 1024) // n))
    spec = pl.BlockSpec((None, bk, n), lambda l, i: (l, i, 0))
    return pl.pallas_call(
        _cast_body, grid=(depth, k // bk), in_specs=[spec], out_specs=spec,
        out_shape=jax.ShapeDtypeStruct(w.shape, BF16), compiler_params=_params(2), name="cast_bf16",
    )(w)


def _prep_w_in(w_in):
    o_ba = CONV_DIM + D_MODEL
    o_qb = o_ba + 2 * N_HEADS_A
    wm = jnp.concatenate([w_in[:, :, :o_ba], w_in[:, :, o_qb:]], axis=2).astype(BF16)
    wba = jnp.pad(w_in[:, :, o_ba:o_qb], ((0, 0), (0, 0), (0, GB_LANES - 2 * N_HEADS_A))).astype(BF16)
    return wm, wba


def _ada_body(c_ref, w_ref, b_ref, o_ref):
    s = _silu(c_ref[...]).astype(BF16)
    o_ref[...] = _dot(s, w_ref[...].astype(BF16)) + b_ref[...]


def _ada(c_all, w_ada, b_ada):
    depth = w_ada.shape[0]
    n_rows = c_all.shape[0]
    n_col = w_ada.shape[2] // D_MODEL
    return pl.pallas_call(
        _ada_body,
        grid=(depth, n_col),
        in_specs=[pl.BlockSpec((n_rows, D_MODEL), lambda l, n: (0, 0)),
                  pl.BlockSpec((None, D_MODEL, D_MODEL), lambda l, n: (l, 0, n)),
                  pl.BlockSpec((None, 1, D_MODEL), lambda l, n: (l, 0, n))],
        out_specs=pl.BlockSpec((None, n_rows, D_MODEL), lambda l, n: (l, 0, n)),
        out_shape=jax.ShapeDtypeStruct((depth, n_rows, w_ada.shape[2]), F32),
        compiler_params=_params(2),
        name="ada",
    )(c_all, w_ada, b_ada.reshape(depth, 1, -1))


def _bucket_table():
    r = np.arange(WINDOW)[:, None]
    c = np.arange(2 * WINDOW)[None, :]
    dist = WINDOW + r - c
    n = np.maximum(dist, 0)
    max_exact = N_BUCKETS // 2
    ratio = np.maximum(n, max_exact).astype(np.float32) / np.float32(max_exact)
    large = max_exact + (np.log(ratio) / np.float32(math.log(MAX_DISTANCE / max_exact))
                         * np.float32(N_BUCKETS - max_exact)).astype(np.int32)
    large = np.minimum(large, N_BUCKETS - 1)
    bucket = np.where(n < max_exact, n, large).astype(np.int32)
    valid = ((dist >= 0) & (dist < WINDOW)).astype(np.int32)
    return bucket, valid


def _bias_t_body(rb_ref, sink_ref, bucket_ref, valid_ref, o_ref):
    l = pl.program_id(0)
    h = pl.program_id(1)
    bucket = bucket_ref[...]
    acc = jnp.zeros(bucket.shape, F32)
    for j in range(N_BUCKETS):
        acc = jnp.where(bucket == j, rb_ref[j, h], acc)
    acc = jnp.where(valid_ref[...] > 0, acc, -jnp.inf)
    key = lax.broadcasted_iota(jnp.int32, bucket.shape, 0)
    o_ref[...] = jnp.where(key == 0, sink_ref[l, h], acc)


def _bias_table_t(rel_bias, sinks):
    depth = sinks.shape[0]
    bucket, valid = _bucket_table()
    return pl.pallas_call(
        _bias_t_body,
        grid=(depth, N_HEADS_B),
        in_specs=[pl.BlockSpec(memory_space=pltpu.SMEM), pl.BlockSpec(memory_space=pltpu.SMEM),
                  pl.BlockSpec((2 * WINDOW, WINDOW), lambda l, h: (0, 0)),
                  pl.BlockSpec((2 * WINDOW, WINDOW), lambda l, h: (0, 0))],
        out_specs=pl.BlockSpec((None, None, 2 * WINDOW, WINDOW), lambda l, h: (l, h // GROUP_B, 0, h % GROUP_B)),
        out_shape=jax.ShapeDtypeStruct((depth, N_KV_B, 2 * WINDOW, GROUP_B * WINDOW), F32),
        compiler_params=_params(2),
        name="bias_table_t",
    )(rel_bias.astype(F32), sinks.astype(F32), jnp.asarray(bucket.T.copy()), jnp.asarray(valid.T.copy()))


def _front_body(x_ref, mod_ref, st_ref, wm_ref, wba_ref, wconv_ref, gpar_ref,
                cs_ref, z_ref, gb_ref, qb_ref, k_ref, v_ref, gates_ref, cst_ref, cbuf, *, bt, tt):
    m = bt * tt
    sh1 = mod_ref[:, :, 0:D_MODEL]
    sc1 = mod_ref[:, :, D_MODEL:2 * D_MODEL]
    h = (x_ref[...] * (1.0 + sc1) + sh1).reshape(m, D_MODEL).astype(BF16)

    @pl.when(pl.program_id(1) == 0)
    def _():
        cbuf[:, 5:8, :] = st_ref[...]

    half = D_MODEL // 2

    def proj(off, n):
        return _dot(h, wm_ref[:, off:off + n])

    def conv_piece(i):
        cols = slice(i * half, (i + 1) * half)
        cbuf[:, 8:8 + tt, cols] = proj(i * half, half).reshape(bt, tt, half)
        y = cbuf[:, 5:5 + tt, cols] * wconv_ref[0:1, cols]
        for j in range(1, CONV_W):
            y = y + cbuf[:, 5 + j:5 + j + tt, cols] * wconv_ref[j:j + 1, cols]
        cs_ref[:, :, cols] = _silu(y).astype(cs_ref.dtype)

    def z_piece(i):
        z_ref[:, :, i * half:(i + 1) * half] = proj(OFF_Z + i * half, half).astype(z_ref.dtype).reshape(bt, tt, half)

    def qb_piece(i):
        q = proj(OFF_QB + i * half, half) * (HD_B ** -0.5)
        qb_ref[:, :, i * half:(i + 1) * half] = q.astype(BF16).reshape(bt, tt, half)

    def kv_piece(_):
        kv = proj(OFF_KVB, 2 * KV_B)
        k_ref[...] = kv[:, :KV_B].reshape(bt, tt, KV_B)
        v_ref[...] = kv[:, KV_B:].reshape(bt, tt, KV_B)

    def gates_piece(i):
        g = _sigmoid(proj(OFF_GATES + i * half, half))
        gates_ref[:, :, i * half:(i + 1) * half] = g.astype(gates_ref.dtype).reshape(bt, tt, half)

    for piece, i in [(conv_piece, 0), (z_piece, 0), (conv_piece, 1), (z_piece, 1), (conv_piece, 2), (qb_piece, 0),
                     (conv_piece, 3), (qb_piece, 1), (conv_piece, 4), (kv_piece, 0), (gates_piece, 0),
                     (conv_piece, 5), (gates_piece, 1), (gates_piece, 2), (gates_piece, 3)]:
        piece(i)
    tail = cbuf[:, tt + 5:tt + 8, :]
    cst_ref[...] = tail
    cbuf[:, 5:8, :] = tail

    ba = _dot(h, wba_ref[...])
    xg = ba + gpar_ref[1:2, :]
    softplus = jnp.maximum(xg, 0.0) + jnp.log(1.0 + jnp.exp(-jnp.abs(xg)))
    lane = lax.broadcasted_iota(jnp.int32, ba.shape, 1)
    gb = jnp.where(lane < N_HEADS_A, _sigmoid(ba), gpar_ref[0:1, :] * softplus)
    gb_ref[...] = gb.reshape(bt, tt, GB_LANES)


def _front(x, mod, conv_state, wm, wba, wconv, gpar, *, layer, bt, tt):
    b, l, _ = x.shape
    grid = (b // bt, l // tt)
    tok = lambda n, dt=F32: jax.ShapeDtypeStruct((b, l, n), dt)
    act = BF16 if tt % 16 == 0 else F32
    tok_spec = lambda n: pl.BlockSpec((bt, tt, n), lambda i, t: (i, t, 0))
    return pl.pallas_call(
        functools.partial(_front_body, bt=bt, tt=tt),
        grid=grid,
        in_specs=[tok_spec(D_MODEL),
                  pl.BlockSpec((bt, 1, 6 * D_MODEL), lambda i, t: (i, 0, 0)),
                  pl.BlockSpec((bt, CONV_W - 1, CONV_DIM), lambda i, t: (i, 0, 0)),
                  _layer_spec((D_MODEL, N_MAIN), layer),
                  _layer_spec((D_MODEL, GB_LANES), layer),
                  _const_spec((CONV_W, CONV_DIM)),
                  _const_spec((2, GB_LANES))],
        out_specs=[tok_spec(CONV_DIM), tok_spec(D_MODEL), tok_spec(GB_LANES), tok_spec(D_MODEL),
                   tok_spec(KV_B), tok_spec(KV_B), tok_spec(2 * D_MODEL),
                   pl.BlockSpec((bt, CONV_W - 1, CONV_DIM), lambda i, t: (i, 0, 0))],
        out_shape=[tok(CONV_DIM, act), tok(D_MODEL, act), tok(GB_LANES), tok(D_MODEL, BF16),
                   tok(KV_B), tok(KV_B), tok(2 * D_MODEL, act),
                   jax.ShapeDtypeStruct((b, CONV_W - 1, CONV_DIM), F32)],
        scratch_shapes=[pltpu.VMEM((bt, tt + 8, CONV_DIM), F32)],
        compiler_params=_params(2),
        name="front",
    )(x, mod, conv_state, wm, wba, wconv, gpar)


def _delta_masks(r, block):
    i = lax.broadcasted_iota(jnp.int32, (r, r), 0)
    j = lax.broadcasted_iota(jnp.int32, (r, r), 1)
    shift = int(math.log2(block))
    same = (i >> shift) == (j >> shift)
    levels = [((i >> (s + 1)) == (j >> (s + 1))) & ((i >> s) != (j >> s)) for s in range(shift)]
    return dict(tri=same & (i >= j), strict=same & (i > j), upper=same & (i < j), eye=(i == j).astype(F32),
                levels=levels)


def _split_dot(a01, x):
    a01 = a01.astype(BF16)
    hi = x.astype(BF16)
    rest = x - hi.astype(F32)
    mid = rest.astype(BF16)
    lo = (rest - mid.astype(F32)).astype(BF16)
    return _dot(a01, hi) + _dot(a01, mid) + _dot(a01, lo)


def _decay_sums(g, masks):
    return _split_dot(masks["tri"], g), _split_dot(masks["upper"], g)


def _delta_pre(problems, masks):
    tri, strict, eye, levels = masks["tri"], masks["strict"], masks["eye"], masks["levels"]
    n = len(problems)
    r = problems[0][0].shape[0]
    lhs, ks, rhs, decays, qgs, kds = [], [], [], [], [], []
    for qr, kr, v, beta_b, gc, rev, gc_row in problems:
        q = qr * lax.rsqrt(jnp.sum(qr * qr, -1, keepdims=True) + 1e-6) * (DK_A ** -0.5)
        k = kr * lax.rsqrt(jnp.sum(kr * kr, -1, keepdims=True) + 1e-6)
        gc_col = gc[:, :r] if r <= DK_A else jnp.concatenate([gc] * (r // DK_A), axis=1)
        decays.append(jnp.exp(jnp.where(tri, gc_col - gc_row, -jnp.inf)))
        eg = jnp.exp(gc)
        kb = k * beta_b
        lhs.append(jnp.concatenate([kb, q], axis=0).astype(BF16))
        ks.append(k.astype(BF16))
        rhs.append(jnp.concatenate([v * beta_b, kb * eg], axis=1).astype(BF16))
        qgs.append((q * eg).astype(BF16))
        kds.append((k * jnp.exp(rev)).astype(BF16))
    kk = [_dot_nt(lhs[i], ks[i]) for i in range(n)]
    mm = [jnp.where(strict, kk[i][:r] * decays[i], 0.0) for i in range(n)]
    qk = [(kk[i][r:] * decays[i]).astype(BF16) for i in range(n)]
    t = [eye - jnp.where(levels[0], mm[i], 0.0) for i in range(n)]
    for lvl in levels[1:]:
        tb = [t[i].astype(BF16) for i in range(n)]
        x = [_dot(tb[i], jnp.where(lvl, mm[i], 0.0).astype(BF16)).astype(BF16) for i in range(n)]
        y = [_dot(x[i], tb[i]) for i in range(n)]
        t = [t[i] - y[i] for i in range(n)]
    uw = [_dot(t[i].astype(BF16), rhs[i]) for i in range(n)]
    return [(uw[i][:, :DK_A], uw[i][:, DK_A:].astype(BF16), qgs[i], kds[i], qk[i]) for i in range(n)]


def _gated_norm(o, z, w_onorm):
    o = o * lax.rsqrt(jnp.mean(o * o, -1, keepdims=True) + RMS_EPS) * w_onorm
    return o * _silu(z)


def _delta_prompt_body(cs_ref, z_ref, gb_ref, wn_ref, o_ref, s_out_ref,
                       s_ref, u_s, o_s, wq_s, qkd_s, egl_s, *, tt):
    n_c = tt // CHUNK
    masks = _delta_masks(tt, CHUNK)

    @pl.when(pl.program_id(1) == 0)
    def _():
        s_ref[...] = jnp.zeros(s_ref.shape, F32)

    gbv = gb_ref[...]
    gc_all, rev_all = _decay_sums(gbv, masks)
    gc_t = gc_all.T
    heads = range(N_HEADS_A)
    hcols = [slice(hd * DK_A, (hd + 1) * DK_A) for hd in heads]
    problems = []
    for hd in heads:
        lg = N_HEADS_A + hd
        problems.append((cs_ref[:, hcols[hd]].astype(F32),
                         cs_ref[:, D_MODEL + hd * DK_A:D_MODEL + (hd + 1) * DK_A].astype(F32),
                         cs_ref[:, 2 * D_MODEL + hd * DK_A:2 * D_MODEL + (hd + 1) * DK_A].astype(F32),
                         jnp.broadcast_to(gbv[:, hd:hd + 1], (tt, DK_A)),
                         jnp.broadcast_to(gc_all[:, lg:lg + 1], (tt, DK_A)),
                         jnp.broadcast_to(rev_all[:, lg:lg + 1], (tt, DK_A)),
                         jnp.broadcast_to(gc_t[lg:lg + 1, :], (tt, tt))))
    for hd, (u, w, qg, kd, qk) in enumerate(_delta_pre(problems, masks)):
        u_s[:, hcols[hd]] = u
        kdt = kd.T
        for c in range(n_c):
            blk = slice(c * CHUNK, (c + 1) * CHUNK)
            idx = hd * n_c + c
            wq_s[idx, 0:CHUNK, :] = w[blk]
            wq_s[idx, CHUNK:2 * CHUNK, :] = qg[blk]
            qkd_s[idx, 0:CHUNK, :] = qk[blk, blk]
            qkd_s[idx, CHUNK:CHUNK + DK_A, :] = kdt[:, blk]
            last = (c + 1) * CHUNK - 1
            egl_s[idx] = jnp.broadcast_to(jnp.exp(problems[hd][4][last:last + 1, :]), (8, DK_A))

    states = [s_ref[hd] for hd in heads]
    for c in range(n_c):
        blk = slice(c * CHUNK, (c + 1) * CHUNK)
        idx = [hd * n_c + c for hd in heads]
        ws = [_dot(wq_s[idx[hd]], states[hd].astype(BF16)) for hd in heads]
        v_new = [(u_s[blk, hcols[hd]] - ws[hd][:CHUNK]).astype(BF16) for hd in heads]
        upd = [_dot(qkd_s[idx[hd]], v_new[hd]) for hd in heads]
        for hd in heads:
            o_s[blk, hcols[hd]] = ws[hd][CHUNK:] + upd[hd][:CHUNK]
            states[hd] = states[hd] * egl_s[idx[hd]][0:1, :] + upd[hd][CHUNK:]
    for hd in heads:
        s_ref[hd] = states[hd]
        o_ref[:, hcols[hd]] = _gated_norm(o_s[:, hcols[hd]], z_ref[:, hcols[hd]].astype(F32), wn_ref[...]).astype(BF16)

    @pl.when(pl.program_id(1) == pl.num_programs(1) - 1)
    def _():
        s_out_ref[...] = s_ref[...]


def _delta_prompt(cs, z, gb, w_onorm, *, tt):
    b, l, _ = cs.shape
    n_c = tt // CHUNK
    tok_spec = lambda n: pl.BlockSpec((None, tt, n), lambda i, t: (i, t, 0))
    return pl.pallas_call(
        functools.partial(_delta_prompt_body, tt=tt),
        grid=(b, l // tt),
        in_specs=[tok_spec(CONV_DIM), tok_spec(D_MODEL), tok_spec(GB_LANES), _const_spec((1, DK_A))],
        out_specs=[tok_spec(D_MODEL),
                   pl.BlockSpec((None, N_HEADS_A, DK_A, DK_A), lambda i, t: (i, 0, 0, 0))],
        out_shape=[jax.ShapeDtypeStruct((b, l, D_MODEL), BF16),
                   jax.ShapeDtypeStruct((b, N_HEADS_A, DK_A, DK_A), F32)],
        scratch_shapes=[pltpu.VMEM((N_HEADS_A, DK_A, DK_A), F32),
                        pltpu.VMEM((tt, D_MODEL), F32), pltpu.VMEM((tt, D_MODEL), F32),
                        pltpu.VMEM((N_HEADS_A * n_c, 2 * CHUNK, DK_A), BF16),
                        pltpu.VMEM((N_HEADS_A * n_c, CHUNK + DK_A, CHUNK), BF16),
                        pltpu.VMEM((N_HEADS_A * n_c, 8, DK_A), F32)],
        compiler_params=_params(2),
        name="delta_prompt",
    )(cs, z, gb, w_onorm)


def _delta_sample_body(cs_ref, z_ref, gb_ref, wn_ref, s0_ref, *rest, bt, nb, ls):
    o_ref, s_out_ref = rest[-2:]
    r = N_HEADS_A * ls
    masks = _delta_masks(r, ls)

    heads = range(N_HEADS_A)
    hrows = [slice(hd * ls, (hd + 1) * ls) for hd in heads]
    hcols = [slice(hd * DK_A, (hd + 1) * DK_A) for hd in heads]

    def per_group(gi, carry):
        bis = [gi * nb + d for d in range(nb)]
        problems, egl = [], []
        for bi in bis:
            cs = cs_ref[bi].astype(F32)
            gbv = gb_ref[bi]
            stack = lambda base: jnp.concatenate([cs[:, base + hd * DK_A: base + (hd + 1) * DK_A] for hd in heads],
                                                 axis=0)
            beta_b = jnp.concatenate([jnp.broadcast_to(gbv[:, hd:hd + 1], (ls, DK_A)) for hd in heads], axis=0)
            g_b = jnp.concatenate(
                [jnp.broadcast_to(gbv[:, N_HEADS_A + hd:N_HEADS_A + hd + 1], (ls, DK_A)) for hd in heads], axis=0)
            gc, rev = _decay_sums(g_b, masks)
            egl.append(jnp.exp(gc))
            problems.append((stack(0), stack(D_MODEL), stack(2 * D_MODEL), beta_b, gc, rev, gc.T[:r, :]))
        pre = _delta_pre(problems, masks)
        both = [[_dot(jnp.concatenate([pre[d][1][hrows[hd]], pre[d][2][hrows[hd]]], axis=0),
                      s0_ref[bis[d], hd].astype(BF16)) for hd in heads] for d in range(nb)]
        v_new = [(pre[d][0] - jnp.concatenate([both[d][hd][:ls] for hd in heads], axis=0)).astype(BF16)
                 for d in range(nb)]
        qkv = [_dot(pre[d][4], v_new[d]) for d in range(nb)]
        upd = [[_dot_tn(pre[d][3][hrows[hd]], v_new[d][hrows[hd]]) for hd in heads] for d in range(nb)]
        for d, bi in enumerate(bis):
            zb = z_ref[bi].astype(F32)
            for hd in heads:
                last = (hd + 1) * ls - 1
                s_out_ref[bi, hd] = s0_ref[bi, hd] * egl[d][last:last + 1, :] + upd[d][hd]
                o = both[d][hd][ls:] + qkv[d][hrows[hd]]
                o_ref[bi, :, hcols[hd]] = _gated_norm(o, zb[:, hcols[hd]], wn_ref[...]).astype(BF16)
        return carry

    lax.fori_loop(0, bt // nb, per_group, 0)


def _fill_layer(prev):
    if prev is None:
        return [], []
    return [prev], [pl.BlockSpec(memory_space=pl.ANY)]


def _delta_sample(cs, z, gb, w_onorm, s0_all, s_prev, *, layer, bt):
    b, ls, _ = cs.shape
    tok_spec = lambda n: pl.BlockSpec((bt, ls, n), lambda i: (i, 0, 0))
    st_spec = pl.BlockSpec((None, bt, N_HEADS_A, DK_A, DK_A), lambda i: (layer, i, 0, 0, 0))
    extra, extra_specs = _fill_layer(s_prev)
    return pl.pallas_call(
        functools.partial(_delta_sample_body, bt=bt, nb=min(bt, 4), ls=ls),
        grid=(b // bt,),
        in_specs=[tok_spec(CONV_DIM), tok_spec(D_MODEL), tok_spec(GB_LANES), _const_spec((1, DK_A)), st_spec]
        + extra_specs,
        out_specs=[tok_spec(D_MODEL), st_spec],
        out_shape=[jax.ShapeDtypeStruct((b, ls, D_MODEL), BF16), jax.ShapeDtypeStruct(s0_all.shape, F32)],
        input_output_aliases={5: 1} if extra else {},
        compiler_params=_params(1),
        name="delta_sample",
    )(cs, z, gb, w_onorm, s0_all, *extra)


def _softmax_keys(s):
    p = jnp.exp(s - jnp.max(s, axis=0, keepdims=True))
    return (p / jnp.sum(p, axis=0, keepdims=True)).astype(BF16)


def _zero_key0(x):
    return jnp.where(lax.broadcasted_iota(jnp.int32, x.shape, 0) == 0, 0.0, x)


def _kv_cols(kvh):
    return slice(kvh * HD_B, (kvh + 1) * HD_B)


def _group_queries(q, kvh):
    return jnp.concatenate([q[:, hh * HD_B:(hh + 1) * HD_B] for hh in range(kvh * GROUP_B, (kvh + 1) * GROUP_B)],
                           axis=0)


def _swa_prompt_body(q_ref, kp_ref, kc_ref, vp_ref, vc_ref, bias_ref, o_ref, ko_ref, vo_ref):
    @pl.when(pl.program_id(1) == pl.num_programs(1) - 1)
    def _():
        ko_ref[...] = kc_ref[...]
        vo_ref[...] = vc_ref[...]

    n_q = GROUP_B * WINDOW
    key = lax.broadcasted_iota(jnp.int32, (2 * WINDOW, n_q), 0)
    no_prev = (pl.program_id(1) == 0) & (key >= 1) & (key < WINDOW)
    k2 = _zero_key0(jnp.concatenate([kp_ref[...], kc_ref[...]], axis=0)).astype(BF16)
    v2t = _zero_key0(jnp.concatenate([vp_ref[...], vc_ref[...]], axis=0)).T.astype(BF16)
    q = q_ref[...]
    kvs = range(N_KV_B)
    s = [_dot_nt(k2[:, _kv_cols(kvh)], _group_queries(q, kvh)) for kvh in kvs]
    pn = [_softmax_keys(jnp.where(no_prev, -jnp.inf, s[kvh] + bias_ref[kvh])) for kvh in kvs]
    ot = [_dot(v2t[_kv_cols(kvh), :], pn[kvh]) for kvh in kvs]
    for kvh in kvs:
        for g in range(GROUP_B):
            hh = kvh * GROUP_B + g
            o_ref[:, hh * HD_B:(hh + 1) * HD_B] = ot[kvh][:, g * WINDOW:(g + 1) * WINDOW].T.astype(BF16)


def _swa_prompt(qb, k, v, bias_t):
    b, l, _ = qb.shape
    cur = lambda n: pl.BlockSpec((None, WINDOW, n), lambda i, t: (i, t, 0))
    prev = lambda n: pl.BlockSpec((None, WINDOW, n), lambda i, t: (i, jnp.maximum(t - 1, 0), 0))
    last = lambda n: pl.BlockSpec((None, WINDOW, n), lambda i, t: (i, 0, 0))
    return pl.pallas_call(
        _swa_prompt_body,
        grid=(b, l // WINDOW),
        in_specs=[cur(D_MODEL), prev(KV_B), cur(KV_B), prev(KV_B), cur(KV_B),
                  _const_spec((N_KV_B, 2 * WINDOW, GROUP_B * WINDOW))],
        out_specs=[cur(D_MODEL), last(KV_B), last(KV_B)],
        out_shape=[jax.ShapeDtypeStruct((b, l, D_MODEL), BF16),
                   jax.ShapeDtypeStruct((b, WINDOW, KV_B), F32), jax.ShapeDtypeStruct((b, WINDOW, KV_B), F32)],
        compiler_params=_params(2),
        name="swa_prompt",
    )(qb, k, k, v, v, bias_t)


def _swa_sample_body(q_ref, kn_ref, vn_ref, kc_ref, vc_ref, bias_ref, *rest, bt, nb, ls):
    o_ref, ko_ref, vo_ref = rest[-3:]
    kvs = range(N_KV_B)

    def per_group(gi, carry):
        bis = [gi * nb + d for d in range(nb)]
        ks, vs, s = [], [], []
        for bi in bis:
            kf = jnp.concatenate([kc_ref[bi], kn_ref[bi]], axis=0)
            vf = jnp.concatenate([vc_ref[bi], vn_ref[bi]], axis=0)
            ko_ref[bi] = kf[ls:, :]
            vo_ref[bi] = vf[ls:, :]
            ks.append(_zero_key0(kf).astype(BF16))
            vs.append(_zero_key0(vf).astype(BF16))
        for d, bi in enumerate(bis):
            q = q_ref[bi]
            s.append([_dot_nt(ks[d][:, _kv_cols(kvh)], _group_queries(q, kvh)) for kvh in kvs])
        pn = [[_softmax_keys(s[d][kvh] + bias_ref[kvh]) for kvh in kvs] for d in range(nb)]
        o = [[_dot_tn(pn[d][kvh], vs[d][:, _kv_cols(kvh)]) for kvh in kvs] for d in range(nb)]
        for d, bi in enumerate(bis):
            for kvh in kvs:
                for g in range(GROUP_B):
                    hh = kvh * GROUP_B + g
                    o_ref[bi, :, hh * HD_B:(hh + 1) * HD_B] = o[d][kvh][g * ls:(g + 1) * ls].astype(BF16)
        return carry

    lax.fori_loop(0, bt // nb, per_group, 0)


def _swa_sample(qb, k_new, v_new, k_cache_all, v_cache_all, bias_s, k_prev, v_prev, *, layer, bt):
    b, ls, _ = qb.shape
    n_keys = WINDOW + ls
    new = lambda n: pl.BlockSpec((bt, ls, n), lambda i: (i, 0, 0))
    cache = pl.BlockSpec((None, bt, WINDOW, KV_B), lambda i: (layer, i, 0, 0))
    extra_k, specs_k = _fill_layer(k_prev)
    extra_v, specs_v = _fill_layer(v_prev)
    return pl.pallas_call(
        functools.partial(_swa_sample_body, bt=bt, nb=min(bt, 4), ls=ls),
        grid=(b // bt,),
        in_specs=[new(D_MODEL), new(KV_B), new(KV_B), cache, cache,
                  _const_spec((N_KV_B, n_keys, GROUP_B * ls))] + specs_k + specs_v,
        out_specs=[new(D_MODEL), cache, cache],
        out_shape=[jax.ShapeDtypeStruct((b, ls, D_MODEL), BF16),
                   jax.ShapeDtypeStruct(k_cache_all.shape, F32), jax.ShapeDtypeStruct(v_cache_all.shape, F32)],
        input_output_aliases={6: 1, 7: 2} if extra_k else {},
        compiler_params=_params(1),
        name="swa_sample",
    )(qb, k_new, v_new, k_cache_all, v_cache_all, bias_s, *extra_k, *extra_v)


def _layer_norm(y, g, b):
    mu = jnp.mean(y, -1, keepdims=True)
    d = y - mu
    var = jnp.mean(d * d, -1, keepdims=True)
    return d * lax.rsqrt(var + LN_EPS) * g + b


def _merge_body(x_ref, oa_ref, ob_ref, gates_ref, mod_ref, wpa_ref, wpb_ref, wout_ref, lng_ref, lnb_ref,
                o_ref, *, bt, tt, alpha):
    m = bt * tt
    ga = gates_ref[:, :, 0:D_MODEL].astype(F32).reshape(m, D_MODEL)
    gb = gates_ref[:, :, D_MODEL:2 * D_MODEL].astype(F32).reshape(m, D_MODEL)
    mixed = ga * _dot(oa_ref[...].reshape(m, D_MODEL), wpa_ref[...]) \
        + gb * _dot(ob_ref[...].reshape(m, D_MODEL), wpb_ref[...])
    attn = _dot(mixed.astype(BF16), wout_ref[...]).reshape(bt, tt, D_MODEL)
    gt1 = mod_ref[:, :, 2 * D_MODEL:3 * D_MODEL]
    o_ref[...] = _layer_norm(alpha * x_ref[...] + gt1 * attn, lng_ref[...], lnb_ref[...])


def _merge(x, oa, ob, gates, mod, wpa, wpb, wout, lng, lnb, *, layer, bt, tt, alpha):
    b, l, _ = x.shape
    tok_spec = lambda n: pl.BlockSpec((bt, tt, n), lambda i, t: (i, t, 0))
    return pl.pallas_call(
        functools.partial(_merge_body, bt=bt, tt=tt, alpha=alpha),
        grid=(b // bt, l // tt),
        in_specs=[tok_spec(D_MODEL), tok_spec(D_MODEL), tok_spec(D_MODEL), tok_spec(2 * D_MODEL),
                  pl.BlockSpec((bt, 1, 6 * D_MODEL), lambda i, t: (i, 0, 0)),
                  _layer_spec((D_MODEL, D_MODEL), layer), _layer_spec((D_MODEL, D_MODEL), layer),
                  _layer_spec((D_MODEL, D_MODEL), layer),
                  _const_spec((1, D_MODEL)), _const_spec((1, D_MODEL))],
        out_specs=tok_spec(D_MODEL),
        out_shape=jax.ShapeDtypeStruct((b, l, D_MODEL), F32),
        compiler_params=_params(2),
        name="merge",
    )(x, oa, ob, gates, mod, wpa, wpb, wout, lng, lnb)


def _mlp_body(x_ref, mod_ref, wup_ref, wdn_ref, lng_ref, lnb_ref, o_ref, *, bt, tt, alpha):
    m = bt * tt
    x = x_ref[...]
    sh2 = mod_ref[:, :, 3 * D_MODEL:4 * D_MODEL]
    sc2 = mod_ref[:, :, 4 * D_MODEL:5 * D_MODEL]
    gt2 = mod_ref[:, :, 5 * D_MODEL:6 * D_MODEL]
    h2 = (x * (1.0 + sc2) + sh2).reshape(m, D_MODEL).astype(BF16)
    ff = jnp.zeros((m, D_MODEL), F32)
    for c in range(D_FF // D_MODEL):
        cols = slice(c * D_MODEL, (c + 1) * D_MODEL)
        a = jnp.maximum(_dot(h2, wup_ref[:, cols]), 0.0)
        ff = ff + _dot((a * a).astype(BF16), wdn_ref[cols, :])
    o_ref[...] = _layer_norm(alpha * x + gt2 * ff.reshape(bt, tt, D_MODEL), lng_ref[...], lnb_ref[...])


def _mlp(x, mod, wup, wdn, lng, lnb, *, layer, bt, tt, alpha):
    b, l, _ = x.shape
    tok_spec = pl.BlockSpec((bt, tt, D_MODEL), lambda i, t: (i, t, 0))
    return pl.pallas_call(
        functools.partial(_mlp_body, bt=bt, tt=tt, alpha=alpha),
        grid=(b // bt, l // tt),
        in_specs=[tok_spec, pl.BlockSpec((bt, 1, 6 * D_MODEL), lambda i, t: (i, 0, 0)),
                  _layer_spec((D_MODEL, D_FF), layer), _layer_spec((D_FF, D_MODEL), layer),
                  _const_spec((1, D_MODEL)), _const_spec((1, D_MODEL))],
        out_specs=tok_spec,
        out_shape=jax.ShapeDtypeStruct((b, l, D_MODEL), F32),
        compiler_params=_params(2),
        name="mlp",
    )(x, mod, wup, wdn, lng, lnb)


def _tiles(b, l, rows):
    tt = min(l, rows)
    bt = max(1, min(b, rows // tt))
    return bt, tt


def _layer_params(l, w_conv, a_log, dt_bias, w_onorm, ln1_g, ln1_b, ln2_g, ln2_b):
    gpar = jnp.zeros((2, GB_LANES), F32)
    gpar = gpar.at[0, N_HEADS_A:2 * N_HEADS_A].set(-jnp.exp(a_log[l].astype(F32)))
    gpar = gpar.at[1, N_HEADS_A:2 * N_HEADS_A].set(dt_bias[l].astype(F32))
    row = lambda a: a[l].reshape(1, -1).astype(F32)
    return dict(wconv=w_conv[l].astype(F32), gpar=gpar, w_onorm=row(w_onorm), ln1_g=row(ln1_g), ln1_b=row(ln1_b),
                ln2_g=row(ln2_g), ln2_b=row(ln2_b))


def _trunk_layer(x, mod, layer, w, p, alpha, conv_state, sample_state):
    b, l, _ = x.shape
    bt, tt = _tiles(b, l, 512 if l % 512 == 0 else 256)
    cs, z, gb, qb, k, v, gates, conv_out = _front(x, mod, conv_state, w["wm"], w["wba"], p["wconv"], p["gpar"],
                                                  layer=layer, bt=bt, tt=tt)
    if sample_state is None:
        oa, s_new = _delta_prompt(cs, z, gb, p["w_onorm"], tt=min(l, 256))
        ob, k_new, v_new = _swa_prompt(qb, k, v, p["bias_t"])
    else:
        s0_all, kc_all, vc_all, s_prev, k_prev, v_prev = sample_state
        oa, s_new = _delta_sample(cs, z, gb, p["w_onorm"], s0_all, s_prev, layer=layer, bt=min(b, 8))
        bias_s = p["bias_t"][:, :WINDOW + l, :].reshape(N_KV_B, WINDOW + l, GROUP_B, WINDOW)[..., :l]
        bias_s = bias_s.reshape(N_KV_B, WINDOW + l, GROUP_B * l)
        ob, k_new, v_new = _swa_sample(qb, k, v, kc_all, vc_all, bias_s, k_prev, v_prev, layer=layer, bt=min(b, 8))
    bt, tt = _tiles(b, l, 512)
    x1 = _merge(x, oa, ob, gates, mod, w["wpa"], w["wpb"], w["wout"], p["ln1_g"], p["ln1_b"],
                layer=layer, bt=bt, tt=tt, alpha=alpha)
    x2 = _mlp(x1, mod, w["wup"], w["wdn"], p["ln2_g"], p["ln2_b"], layer=layer, bt=bt, tt=tt, alpha=alpha)
    return x2, (s_new, conv_out, k_new, v_new)


def kernel(x_prompt, x_sample, state_delta, state_conv, cache_k, cache_v, c_prompt, c_sample, rel_bias, w_ada, b_ada, w_in, w_conv, a_log, dt_bias, w_onorm, sinks, w_pa, w_pb, w_out, ln1_g, ln1_b, w_up, w_down, ln2_g, ln2_b):
    depth = w_in.shape[0]
    alpha = (2 * depth) ** 0.25
    bp = x_prompt.shape[0]
    mod_all = _ada(jnp.concatenate([c_prompt, c_sample], axis=0), w_ada, b_ada)
    bias_t = _bias_table_t(rel_bias, sinks)
    wm, wba = _prep_w_in(w_in)
    w = dict(wm=wm, wba=wba, wpa=_cast_bf16(w_pa), wpb=_cast_bf16(w_pb), wout=_cast_bf16(w_out),
             wup=_cast_bf16(w_up), wdn=_cast_bf16(w_down))
    bs = x_sample.shape[0]
    kc_all = cache_k.reshape(depth, bs, WINDOW, KV_B)
    vc_all = cache_v.reshape(depth, bs, WINDOW, KV_B)
    yp, ys = x_prompt, x_sample
    prompt_outs = [[] for _ in range(4)]
    sample_conv = []
    s_all = k_all = v_all = None
    for l in range(depth):
        p = _layer_params(l, w_conv, a_log, dt_bias, w_onorm, ln1_g, ln1_b, ln2_g, ln2_b)
        p["bias_t"] = bias_t[l]
        mod_p = mod_all[l, :bp][:, None, :]
        mod_s = mod_all[l, bp:][:, None, :]
        zero_conv = jnp.zeros((bp, CONV_W - 1, CONV_DIM), x_prompt.dtype)
        yp, rest_p = _trunk_layer(yp, mod_p, l, w, p, alpha, zero_conv, None)
        ys, (s_all, conv_s, k_all, v_all) = _trunk_layer(ys, mod_s, l, w, p, alpha, state_conv[l],
                                                         (state_delta, kc_all, vc_all, s_all, k_all, v_all))
        for acc, val in zip(prompt_outs, rest_p):
            acc.append(val)
        sample_conv.append(conv_s)
    heads = lambda a: a.reshape(a.shape[:-1] + (N_KV_B, HD_B))
    pd, pc, pk, pv = (jnp.stack(a) for a in prompt_outs)
    return (yp, ys, pd, pc, heads(pk), heads(pv), s_all, jnp.stack(sample_conv), heads(k_all), heads(v_all))
```

```python
import functools
import math

import numpy as np
import jax
import jax.numpy as jnp
from jax import lax
from jax.experimental import pallas as pl
from jax.experimental.pallas import tpu as pltpu

F32 = jnp.float32
BF16 = jnp.bfloat16

D_MODEL = 1024
N_HEADS_A = 8
DK_A = 128
CONV_W = 4
CONV_DIM = 3 * D_MODEL
CHUNK = 64
HD_B = 64
N_HEADS_B = 16
N_KV_B = 4
GROUP_B = N_HEADS_B // N_KV_B
KV_B = N_KV_B * HD_B
WINDOW = 128
N_BUCKETS = 32
MAX_DISTANCE = 128
D_FF = 4 * D_MODEL
LN_EPS = 1e-5
RMS_EPS = 1e-6

OFF_Z = CONV_DIM
OFF_QB = OFF_Z + D_MODEL
OFF_KVB = OFF_QB + D_MODEL
OFF_GATES = OFF_KVB + 2 * KV_B
N_MAIN = OFF_GATES + 2 * D_MODEL
GB_LANES = 128

V7X_VMEM_BYTES = 64 * 1024 * 1024
VMEM_LIMIT = V7X_VMEM_BYTES - 8 * 1024 * 1024


def _params(n_grid):
    return pltpu.CompilerParams(dimension_semantics=("arbitrary",) * n_grid, vmem_limit_bytes=VMEM_LIMIT)


def _dot(a, b):
    return jnp.dot(a, b, preferred_element_type=F32)


def _dot_nt(a, b):
    return lax.dot_general(a, b, (((1,), (1,)), ((), ())), preferred_element_type=F32)


def _dot_tn(a, b):
    return lax.dot_general(a, b, (((0,), (0,)), ((), ())), preferred_element_type=F32)


def _sigmoid(x):
    return 1.0 / (1.0 + jnp.exp(-x))


def _silu(x):
    return x * _sigmoid(x)


def _const_spec(shape):
    nd = len(shape)
    return pl.BlockSpec(shape, lambda *_: (0,) * nd, pipeline_mode=pl.Buffered(1))


def _layer_spec(shape, layer):
    nd = len(shape)
    return pl.BlockSpec((None,) + tuple(shape), lambda *_: (layer,) + (0,) * nd, pipeline_mode=pl.Buffered(1))


def _cast_body(w_ref, o_ref):
    o_ref[...] = w_ref[...].astype(BF16)


def _cast_bf16(w):
    depth, k, n = w.shape
    bk = max(8, min(k, (1024 * 1024) // n))
    spec = pl.BlockSpec((None, bk, n), lambda l, i: (l, i, 0))
    return pl.pallas_call(
        _cast_body, grid=(depth, k // bk), in_specs=[spec], out_specs=spec,
        out_shape=jax.ShapeDtypeStruct(w.shape, BF16), compiler_params=_params(2), name="cast_bf16",
    )(w)


def _prep_w_in(w_in):
    o_ba = CONV_DIM + D_MODEL
    o_qb = o_ba + 2 * N_HEADS_A
    wm = jnp.concatenate([w_in[:, :, :o_ba], w_in[:, :, o_qb:]], axis=2).astype(BF16)
    wba = jnp.pad(w_in[:, :, o_ba:o_qb], ((0, 0), (0, 0), (0, GB_LANES - 2 * N_HEADS_A))).astype(BF16)
    return wm, wba


def _ada_body(c_ref, w_ref, b_ref, o_ref):
    s = _silu(c_ref[...]).astype(BF16)
    o_ref[...] = _dot(s, w_ref[...].astype(BF16)) + b_ref[...]


def _ada(c_all, w_ada, b_ada):
    depth = w_ada.shape[0]
    n_rows = c_all.shape[0]
    n_col = w_ada.shape[2] // D_MODEL
    return pl.pallas_call(
        _ada_body,
        grid=(depth, n_col),
        in_specs=[pl.BlockSpec((n_rows, D_MODEL), lambda l, n: (0, 0)),
                  pl.BlockSpec((None, D_MODEL, D_MODEL), lambda l, n: (l, 0, n)),
                  pl.BlockSpec((None, 1, D_MODEL), lambda l, n: (l, 0, n))],
        out_specs=pl.BlockSpec((None, n_rows, D_MODEL), lambda l, n: (l, 0, n)),
        out_shape=jax.ShapeDtypeStruct((depth, n_rows, w_ada.shape[2]), F32),
        compiler_params=_params(2),
        name="ada",
    )(c_all, w_ada, b_ada.reshape(depth, 1, -1))


def _bucket_table():
    r = np.arange(WINDOW)[:, None]
    c = np.arange(2 * WINDOW)[None, :]
    dist = WINDOW + r - c
    n = np.maximum(dist, 0)
    max_exact = N_BUCKETS // 2
    ratio = np.maximum(n, max_exact).astype(np.float32) / np.float32(max_exact)
    large = max_exact + (np.log(ratio) / np.float32(math.log(MAX_DISTANCE / max_exact))
                         * np.float32(N_BUCKETS - max_exact)).astype(np.int32)
    large = np.minimum(large, N_BUCKETS - 1)
    bucket = np.where(n < max_exact, n, large).astype(np.int32)
    valid = ((dist >= 0) & (dist < WINDOW)).astype(np.int32)
    return bucket, valid


def _bias_t_body(rb_ref, sink_ref, bucket_ref, valid_ref, o_ref):
    l = pl.program_id(0)
    h = pl.program_id(1)
    bucket = bucket_ref[...]
    acc = jnp.zeros(bucket.shape, F32)
    for j in range(N_BUCKETS):
        acc = jnp.where(bucket == j, rb_ref[j, h], acc)
    acc = jnp.where(valid_ref[...] > 0, acc, -jnp.inf)
    key = lax.broadcasted_iota(jnp.int32, bucket.shape, 0)
    o_ref[...] = jnp.where(key == 0, sink_ref[l, h], acc)


def _bias_table_t(rel_bias, sinks):
    depth = sinks.shape[0]
    bucket, valid = _bucket_table()
    return pl.pallas_call(
        _bias_t_body,
        grid=(depth, N_HEADS_B),
        in_specs=[pl.BlockSpec(memory_space=pltpu.SMEM), pl.BlockSpec(memory_space=pltpu.SMEM),
                  pl.BlockSpec((2 * WINDOW, WINDOW), lambda l, h: (0, 0)),
                  pl.BlockSpec((2 * WINDOW, WINDOW), lambda l, h: (0, 0))],
        out_specs=pl.BlockSpec((None, None, 2 * WINDOW, WINDOW), lambda l, h: (l, h // GROUP_B, 0, h % GROUP_B)),
        out_shape=jax.ShapeDtypeStruct((depth, N_KV_B, 2 * WINDOW, GROUP_B * WINDOW), F32),
        compiler_params=_params(2),
        name="bias_table_t",
    )(rel_bias.astype(F32), sinks.astype(F32), jnp.asarray(bucket.T.copy()), jnp.asarray(valid.T.copy()))


def _front_body(x_ref, mod_ref, st_ref, wm_ref, wba_ref, wconv_ref, gpar_ref,
                cs_ref, z_ref, gb_ref, qb_ref, k_ref, v_ref, gates_ref, cst_ref, cbuf, *, bt, tt):
    m = bt * tt
    sh1 = mod_ref[:, :, 0:D_MODEL]
    sc1 = mod_ref[:, :, D_MODEL:2 * D_MODEL]
    h = (x_ref[...] * (1.0 + sc1) + sh1).reshape(m, D_MODEL).astype(BF16)

    @pl.when(pl.program_id(1) == 0)
    def _():
        cbuf[:, 0:8, :] = jnp.zeros((bt, 8, CONV_DIM), F32)
        cbuf[:, 5:8, :] = st_ref[...]

    half = D_MODEL // 2

    def proj(off, n):
        return _dot(h, wm_ref[:, off:off + n])

    def conv_piece(i):
        cols = slice(i * half, (i + 1) * half)
        cbuf[:, 8:8 + tt, cols] = proj(i * half, half).reshape(bt, tt, half)
        u = cbuf[:, :, cols]
        u1 = pltpu.roll(u, 1, 1)
        w0, w1, w2, w3 = (wconv_ref[j:j + 1, cols] for j in range(CONV_W))
        y = (u * w3 + u1 * w2) + pltpu.roll(u * w1 + u1 * w0, 2, 1)
        cs_ref[:, :, cols] = _silu(y[:, 8:, :]).astype(cs_ref.dtype)

    def z_piece(i):
        z_ref[:, :, i * half:(i + 1) * half] = proj(OFF_Z + i * half, half).astype(z_ref.dtype).reshape(bt, tt, half)

    def qb_piece(i):
        q = proj(OFF_QB + i * half, half) * (HD_B ** -0.5)
        qb_ref[:, :, i * half:(i + 1) * half] = q.astype(BF16).reshape(bt, tt, half)

    def kv_piece(_):
        kv = proj(OFF_KVB, 2 * KV_B)
        k_ref[...] = kv[:, :KV_B].reshape(bt, tt, KV_B)
        v_ref[...] = kv[:, KV_B:].reshape(bt, tt, KV_B)

    def gates_piece(i):
        g = _sigmoid(proj(OFF_GATES + i * half, half))
        gates_ref[:, :, i * half:(i + 1) * half] = g.astype(gates_ref.dtype).reshape(bt, tt, half)

    for piece, i in [(conv_piece, 0), (z_piece, 0), (conv_piece, 1), (z_piece, 1), (conv_piece, 2), (qb_piece, 0),
                     (conv_piece, 3), (qb_piece, 1), (conv_piece, 4), (kv_piece, 0), (gates_piece, 0),
                     (conv_piece, 5), (gates_piece, 1), (gates_piece, 2), (gates_piece, 3)]:
        piece(i)
    tail = cbuf[:, tt + 5:tt + 8, :]
    cst_ref[...] = tail
    cbuf[:, 5:8, :] = tail

    ba = _dot(h, wba_ref[...])
    xg = ba + gpar_ref[1:2, :]
    softplus = jnp.maximum(xg, 0.0) + jnp.log(1.0 + jnp.exp(-jnp.abs(xg)))
    lane = lax.broadcasted_iota(jnp.int32, ba.shape, 1)
    gb = jnp.where(lane < N_HEADS_A, _sigmoid(ba), gpar_ref[0:1, :] * softplus)
    gb_ref[...] = gb.reshape(bt, tt, GB_LANES)


def _front(x, mod, conv_state, wm, wba, wconv, gpar, *, layer, bt, tt):
    b, l, _ = x.shape
    grid = (b // bt, l // tt)
    tok = lambda n, dt=F32: jax.ShapeDtypeStruct((b, l, n), dt)
    act = BF16 if tt % 16 == 0 else F32
    tok_spec = lambda n: pl.BlockSpec((bt, tt, n), lambda i, t: (i, t, 0))
    return pl.pallas_call(
        functools.partial(_front_body, bt=bt, tt=tt),
        grid=grid,
        in_specs=[tok_spec(D_MODEL),
                  pl.BlockSpec((bt, 1, 6 * D_MODEL), lambda i, t: (i, 0, 0)),
                  pl.BlockSpec((bt, CONV_W - 1, CONV_DIM), lambda i, t: (i, 0, 0)),
                  _layer_spec((D_MODEL, N_MAIN), layer),
                  _layer_spec((D_MODEL, GB_LANES), layer),
                  _const_spec((CONV_W, CONV_DIM)),
                  _const_spec((2, GB_LANES))],
        out_specs=[tok_spec(CONV_DIM), tok_spec(D_MODEL), tok_spec(GB_LANES), tok_spec(D_MODEL),
                   tok_spec(KV_B), tok_spec(KV_B), tok_spec(2 * D_MODEL),
                   pl.BlockSpec((bt, CONV_W - 1, CONV_DIM), lambda i, t: (i, 0, 0))],
        out_shape=[tok(CONV_DIM, act), tok(D_MODEL, act), tok(GB_LANES), tok(D_MODEL, BF16),
                   tok(KV_B), tok(KV_B), tok(2 * D_MODEL, act),
                   jax.ShapeDtypeStruct((b, CONV_W - 1, CONV_DIM), F32)],
        scratch_shapes=[pltpu.VMEM((bt, tt + 8, CONV_DIM), F32)],
        compiler_params=_params(2),
        name="front",
    )(x, mod, conv_state, wm, wba, wconv, gpar)


def _delta_masks(r, block):
    i = lax.broadcasted_iota(jnp.int32, (r, r), 0)
    j = lax.broadcasted_iota(jnp.int32, (r, r), 1)
    shift = int(math.log2(block))
    same = (i >> shift) == (j >> shift)
    levels = [((i >> (s + 1)) == (j >> (s + 1))) & ((i >> s) != (j >> s)) for s in range(shift)]
    return dict(tri=same & (i >= j), strict=same & (i > j), upper=same & (i < j), eye=(i == j).astype(F32),
                levels=levels)


def _split_dot(a01, x):
    a01 = a01.astype(BF16)
    hi = x.astype(BF16)
    rest = x - hi.astype(F32)
    mid = rest.astype(BF16)
    lo = (rest - mid.astype(F32)).astype(BF16)
    return _dot(a01, hi) + _dot(a01, mid) + _dot(a01, lo)


def _decay_sums(g, masks):
    return _split_dot(masks["tri"], g), _split_dot(masks["upper"], g)


def _delta_pre(problems, masks):
    tri, strict, eye, levels = masks["tri"], masks["strict"], masks["eye"], masks["levels"]
    n = len(problems)
    r = problems[0][0].shape[0]
    lhs, ks, rhs, decays, qgs, kds = [], [], [], [], [], []
    for qr, kr, v, beta_b, gc, rev, gc_row in problems:
        q = qr * lax.rsqrt(jnp.sum(qr * qr, -1, keepdims=True) + 1e-6) * (DK_A ** -0.5)
        k = kr * lax.rsqrt(jnp.sum(kr * kr, -1, keepdims=True) + 1e-6)
        gc_col = gc[:, :r] if r <= DK_A else jnp.concatenate([gc] * (r // DK_A), axis=1)
        decays.append(jnp.exp(jnp.where(tri, gc_col - gc_row, -jnp.inf)))
        eg = jnp.exp(gc)
        kb = k * beta_b
        lhs.append(jnp.concatenate([kb, q], axis=0).astype(BF16))
        ks.append(k.astype(BF16))
        rhs.append(jnp.concatenate([v * beta_b, kb * eg], axis=1).astype(BF16))
        qgs.append((q * eg).astype(BF16))
        kds.append((k * jnp.exp(rev)).astype(BF16))
    kk = [_dot_nt(lhs[i], ks[i]) for i in range(n)]
    mm = [jnp.where(strict, kk[i][:r] * decays[i], 0.0) for i in range(n)]
    qk = [(kk[i][r:] * decays[i]).astype(BF16) for i in range(n)]
    t = [eye - jnp.where(levels[0], mm[i], 0.0) for i in range(n)]
    for s, lvl in enumerate(levels[1:], start=1):
        blk = 2 ** s
        tb = [t[i].astype(BF16) for i in range(n)]
        mo = [jnp.where(lvl, mm[i], 0.0).astype(BF16) for i in range(n)]
        if blk % 8:
            x = [_dot(tb[i], mo[i]).astype(BF16) for i in range(n)]
            y = [_dot(x[i], tb[i]) for i in range(n)]
            t = [t[i] - y[i] for i in range(n)]
        else:
            split = [t[i].reshape(r // (2 * blk), 2, blk, r) for i in range(n)]
            lo = [split[i][:, 1].reshape(r // 2, r) for i in range(n)]
            x = [_dot(lo[i].astype(BF16), mo[i]).astype(BF16) for i in range(n)]
            y = [_dot(x[i], tb[i]) for i in range(n)]
            t = [jnp.stack([split[i][:, 0], (lo[i] - y[i]).reshape(r // (2 * blk), blk, r)], axis=1).reshape(r, r)
                 for i in range(n)]
    uw = [_dot(t[i].astype(BF16), rhs[i]) for i in range(n)]
    return [(uw[i][:, :DK_A], uw[i][:, DK_A:].astype(BF16), qgs[i], kds[i], qk[i]) for i in range(n)]


def _gated_norm(o, z, w_onorm):
    o = o * lax.rsqrt(jnp.mean(o * o, -1, keepdims=True) + RMS_EPS) * w_onorm
    return o * _silu(z)


def _delta_prompt_body(cs_ref, z_ref, gb_ref, wn_ref, o_ref, s_out_ref,
                       s_ref, u_s, o_s, wq_s, qkd_s, egl_s, *, tt):
    n_c = tt // CHUNK
    masks = _delta_masks(tt, CHUNK)

    @pl.when(pl.program_id(1) == 0)
    def _():
        s_ref[...] = jnp.zeros(s_ref.shape, F32)

    gbv = gb_ref[...]
    gc_all, rev_all = _decay_sums(gbv, masks)
    gc_t = gc_all.T
    heads = range(N_HEADS_A)
    hcols = [slice(hd * DK_A, (hd + 1) * DK_A) for hd in heads]
    problems = []
    for hd in heads:
        lg = N_HEADS_A + hd
        problems.append((cs_ref[:, hcols[hd]].astype(F32),
                         cs_ref[:, D_MODEL + hd * DK_A:D_MODEL + (hd + 1) * DK_A].astype(F32),
                         cs_ref[:, 2 * D_MODEL + hd * DK_A:2 * D_MODEL + (hd + 1) * DK_A].astype(F32),
                         jnp.broadcast_to(gbv[:, hd:hd + 1], (tt, DK_A)),
                         jnp.broadcast_to(gc_all[:, lg:lg + 1], (tt, DK_A)),
                         jnp.broadcast_to(rev_all[:, lg:lg + 1], (tt, DK_A)),
                         jnp.broadcast_to(gc_t[lg:lg + 1, :], (tt, tt))))
    for hd, (u, w, qg, kd, qk) in enumerate(_delta_pre(problems, masks)):
        u_s[:, hcols[hd]] = u
        kdt = kd.T
        for c in range(n_c):
            blk = slice(c * CHUNK, (c + 1) * CHUNK)
            idx = hd * n_c + c
            wq_s[idx, 0:CHUNK, :] = w[blk]
            wq_s[idx, CHUNK:2 * CHUNK, :] = qg[blk]
            qkd_s[idx, 0:CHUNK, :] = qk[blk, blk]
            qkd_s[idx, CHUNK:CHUNK + DK_A, :] = kdt[:, blk]
            last = (c + 1) * CHUNK - 1
            egl_s[idx] = jnp.broadcast_to(jnp.exp(problems[hd][4][last:last + 1, :]), (8, DK_A))

    states = [s_ref[hd] for hd in heads]
    for c in range(n_c):
        blk = slice(c * CHUNK, (c + 1) * CHUNK)
        idx = [hd * n_c + c for hd in heads]
        ws = [_dot(wq_s[idx[hd]], states[hd].astype(BF16)) for hd in heads]
        v_new = [(u_s[blk, hcols[hd]] - ws[hd][:CHUNK]).astype(BF16) for hd in heads]
        upd = [_dot(qkd_s[idx[hd]], v_new[hd]) for hd in heads]
        for hd in heads:
            o_s[blk, hcols[hd]] = ws[hd][CHUNK:] + upd[hd][:CHUNK]
            states[hd] = states[hd] * egl_s[idx[hd]][0:1, :] + upd[hd][CHUNK:]
    for hd in heads:
        s_ref[hd] = states[hd]
        o_ref[:, hcols[hd]] = _gated_norm(o_s[:, hcols[hd]], z_ref[:, hcols[hd]].astype(F32), wn_ref[...]).astype(BF16)

    @pl.when(pl.program_id(1) == pl.num_programs(1) - 1)
    def _():
        s_out_ref[...] = s_ref[...]


def _delta_prompt(cs, z, gb, w_onorm, *, tt):
    b, l, _ = cs.shape
    n_c = tt // CHUNK
    tok_spec = lambda n: pl.BlockSpec((None, tt, n), lambda i, t: (i, t, 0))
    return pl.pallas_call(
        functools.partial(_delta_prompt_body, tt=tt),
        grid=(b, l // tt),
        in_specs=[tok_spec(CONV_DIM), tok_spec(D_MODEL), tok_spec(GB_LANES), _const_spec((1, DK_A))],
        out_specs=[tok_spec(D_MODEL),
                   pl.BlockSpec((None, N_HEADS_A, DK_A, DK_A), lambda i, t: (i, 0, 0, 0))],
        out_shape=[jax.ShapeDtypeStruct((b, l, D_MODEL), BF16),
                   jax.ShapeDtypeStruct((b, N_HEADS_A, DK_A, DK_A), F32)],
        scratch_shapes=[pltpu.VMEM((N_HEADS_A, DK_A, DK_A), F32),
                        pltpu.VMEM((tt, D_MODEL), F32), pltpu.VMEM((tt, D_MODEL), F32),
                        pltpu.VMEM((N_HEADS_A * n_c, 2 * CHUNK, DK_A), BF16),
                        pltpu.VMEM((N_HEADS_A * n_c, CHUNK + DK_A, CHUNK), BF16),
                        pltpu.VMEM((N_HEADS_A * n_c, 8, DK_A), F32)],
        compiler_params=_params(2),
        name="delta_prompt",
    )(cs, z, gb, w_onorm)


def _delta_sample_body(cs_ref, z_ref, gb_ref, wn_ref, s0_ref, *rest, bt, nb, ls):
    o_ref, s_out_ref = rest[-2:]
    r = N_HEADS_A * ls
    masks = _delta_masks(r, ls)

    heads = range(N_HEADS_A)
    hrows = [slice(hd * ls, (hd + 1) * ls) for hd in heads]
    hcols = [slice(hd * DK_A, (hd + 1) * DK_A) for hd in heads]

    def per_group(gi, carry):
        bis = [gi * nb + d for d in range(nb)]
        problems, egl = [], []
        for bi in bis:
            cs = cs_ref[bi].astype(F32)
            gbv = gb_ref[bi]
            stack = lambda base: jnp.concatenate([cs[:, base + hd * DK_A: base + (hd + 1) * DK_A] for hd in heads],
                                                 axis=0)
            beta_b = jnp.concatenate([jnp.broadcast_to(gbv[:, hd:hd + 1], (ls, DK_A)) for hd in heads], axis=0)
            g_b = jnp.concatenate(
                [jnp.broadcast_to(gbv[:, N_HEADS_A + hd:N_HEADS_A + hd + 1], (ls, DK_A)) for hd in heads], axis=0)
            gc, rev = _decay_sums(g_b, masks)
            egl.append(jnp.exp(gc))
            problems.append((stack(0), stack(D_MODEL), stack(2 * D_MODEL), beta_b, gc, rev, gc.T[:r, :]))
        pre = _delta_pre(problems, masks)
        both = [[_dot(jnp.concatenate([pre[d][1][hrows[hd]], pre[d][2][hrows[hd]]], axis=0),
                      s0_ref[bis[d], hd].astype(BF16)) for hd in heads] for d in range(nb)]
        v_new = [(pre[d][0] - jnp.concatenate([both[d][hd][:ls] for hd in heads], axis=0)).astype(BF16)
                 for d in range(nb)]
        qkv = [_dot(pre[d][4], v_new[d]) for d in range(nb)]
        upd = [[_dot_tn(pre[d][3][hrows[hd]], v_new[d][hrows[hd]]) for hd in heads] for d in range(nb)]
        for d, bi in enumerate(bis):
            zb = z_ref[bi].astype(F32)
            for hd in heads:
                last = (hd + 1) * ls - 1
                s_out_ref[bi, hd] = s0_ref[bi, hd] * egl[d][last:last + 1, :] + upd[d][hd]
                o = both[d][hd][ls:] + qkv[d][hrows[hd]]
                o_ref[bi, :, hcols[hd]] = _gated_norm(o, zb[:, hcols[hd]], wn_ref[...]).astype(BF16)
        return carry

    lax.fori_loop(0, bt // nb, per_group, 0)


def _fill_layer(prev):
    if prev is None:
        return [], []
    return [prev], [pl.BlockSpec(memory_space=pl.ANY)]


def _delta_sample(cs, z, gb, w_onorm, s0_all, s_prev, *, layer, bt):
    b, ls, _ = cs.shape
    tok_spec = lambda n: pl.BlockSpec((bt, ls, n), lambda i: (i, 0, 0))
    st_spec = pl.BlockSpec((None, bt, N_HEADS_A, DK_A, DK_A), lambda i: (layer, i, 0, 0, 0))
    extra, extra_specs = _fill_layer(s_prev)
    return pl.pallas_call(
        functools.partial(_delta_sample_body, bt=bt, nb=min(bt, 4), ls=ls),
        grid=(b // bt,),
        in_specs=[tok_spec(CONV_DIM), tok_spec(D_MODEL), tok_spec(GB_LANES), _const_spec((1, DK_A)), st_spec]
        + extra_specs,
        out_specs=[tok_spec(D_MODEL), st_spec],
        out_shape=[jax.ShapeDtypeStruct((b, ls, D_MODEL), BF16), jax.ShapeDtypeStruct(s0_all.shape, F32)],
        input_output_aliases={5: 1} if extra else {},
        compiler_params=_params(1),
        name="delta_sample",
    )(cs, z, gb, w_onorm, s0_all, *extra)


def _softmax_keys(s):
    p = jnp.exp(s - jnp.max(s, axis=0, keepdims=True))
    return (p / jnp.sum(p, axis=0, keepdims=True)).astype(BF16)


def _zero_key0(x):
    return jnp.where(lax.broadcasted_iota(jnp.int32, x.shape, 0) == 0, 0.0, x)


def _kv_cols(kvh):
    return slice(kvh * HD_B, (kvh + 1) * HD_B)


def _group_queries(q, kvh):
    return jnp.concatenate([q[:, hh * HD_B:(hh + 1) * HD_B] for hh in range(kvh * GROUP_B, (kvh + 1) * GROUP_B)],
                           axis=0)


def _swa_prompt_body(q_ref, kp_ref, kc_ref, vp_ref, vc_ref, bias_ref, o_ref, ko_ref, vo_ref):
    @pl.when(pl.program_id(1) == pl.num_programs(1) - 1)
    def _():
        ko_ref[...] = kc_ref[...]
        vo_ref[...] = vc_ref[...]

    n_q = GROUP_B * WINDOW
    key = lax.broadcasted_iota(jnp.int32, (2 * WINDOW, n_q), 0)
    no_prev = (pl.program_id(1) == 0) & (key >= 1) & (key < WINDOW)
    k2 = _zero_key0(jnp.concatenate([kp_ref[...], kc_ref[...]], axis=0)).astype(BF16)
    v2t = _zero_key0(jnp.concatenate([vp_ref[...], vc_ref[...]], axis=0)).T.astype(BF16)
    q = q_ref[...]
    kvs = range(N_KV_B)
    s = [_dot_nt(k2[:, _kv_cols(kvh)], _group_queries(q, kvh)) for kvh in kvs]
    pn = [_softmax_keys(jnp.where(no_prev, -jnp.inf, s[kvh] + bias_ref[kvh])) for kvh in kvs]
    ot = [_dot(v2t[_kv_cols(kvh), :], pn[kvh]) for kvh in kvs]
    for kvh in kvs:
        for g in range(GROUP_B):
            hh = kvh * GROUP_B + g
            o_ref[:, hh * HD_B:(hh + 1) * HD_B] = ot[kvh][:, g * WINDOW:(g + 1) * WINDOW].T.astype(BF16)


def _swa_prompt(qb, k, v, bias_t):
    b, l, _ = qb.shape
    cur = lambda n: pl.BlockSpec((None, WINDOW, n), lambda i, t: (i, t, 0))
    prev = lambda n: pl.BlockSpec((None, WINDOW, n), lambda i, t: (i, jnp.maximum(t - 1, 0), 0))
    last = lambda n: pl.BlockSpec((None, WINDOW, n), lambda i, t: (i, 0, 0))
    return pl.pallas_call(
        _swa_prompt_body,
        grid=(b, l // WINDOW),
        in_specs=[cur(D_MODEL), prev(KV_B), cur(KV_B), prev(KV_B), cur(KV_B),
                  _const_spec((N_KV_B, 2 * WINDOW, GROUP_B * WINDOW))],
        out_specs=[cur(D_MODEL), last(KV_B), last(KV_B)],
        out_shape=[jax.ShapeDtypeStruct((b, l, D_MODEL), BF16),
                   jax.ShapeDtypeStruct((b, WINDOW, KV_B), F32), jax.ShapeDtypeStruct((b, WINDOW, KV_B), F32)],
        compiler_params=_params(2),
        name="swa_prompt",
    )(qb, k, k, v, v, bias_t)


def _swa_sample_body(q_ref, kn_ref, vn_ref, kc_ref, vc_ref, bias_ref, *rest, bt, nb, ls):
    o_ref, ko_ref, vo_ref = rest[-3:]
    kvs = range(N_KV_B)

    def per_group(gi, carry):
        bis = [gi * nb + d for d in range(nb)]
        ks, vs, s = [], [], []
        for bi in bis:
            kf = jnp.concatenate([kc_ref[bi], kn_ref[bi]], axis=0)
            vf = jnp.concatenate([vc_ref[bi], vn_ref[bi]], axis=0)
            ko_ref[bi] = kf[ls:, :]
            vo_ref[bi] = vf[ls:, :]
            ks.append(_zero_key0(kf).astype(BF16))
            vs.append(_zero_key0(vf).astype(BF16))
        for d, bi in enumerate(bis):
            q = q_ref[bi]
            s.append([_dot_nt(ks[d][:, _kv_cols(kvh)], _group_queries(q, kvh)) for kvh in kvs])
        pn = [[_softmax_keys(s[d][kvh] + bias_ref[kvh]) for kvh in kvs] for d in range(nb)]
        o = [[_dot_tn(pn[d][kvh], vs[d][:, _kv_cols(kvh)]) for kvh in kvs] for d in range(nb)]
        for d, bi in enumerate(bis):
            for kvh in kvs:
                for g in range(GROUP_B):
                    hh = kvh * GROUP_B + g
                    o_ref[bi, :, hh * HD_B:(hh + 1) * HD_B] = o[d][kvh][g * ls:(g + 1) * ls].astype(BF16)
        return carry

    lax.fori_loop(0, bt // nb, per_group, 0)


def _swa_sample(qb, k_new, v_new, k_cache_all, v_cache_all, bias_s, k_prev, v_prev, *, layer, bt):
    b, ls, _ = qb.shape
    n_keys = WINDOW + ls
    new = lambda n: pl.BlockSpec((bt, ls, n), lambda i: (i, 0, 0))
    cache = pl.BlockSpec((None, bt, WINDOW, KV_B), lambda i: (layer, i, 0, 0))
    extra_k, specs_k = _fill_layer(k_prev)
    extra_v, specs_v = _fill_layer(v_prev)
    return pl.pallas_call(
        functools.partial(_swa_sample_body, bt=bt, nb=min(bt, 4), ls=ls),
        grid=(b // bt,),
        in_specs=[new(D_MODEL), new(KV_B), new(KV_B), cache, cache,
                  _const_spec((N_KV_B, n_keys, GROUP_B * ls))] + specs_k + specs_v,
        out_specs=[new(D_MODEL), cache, cache],
        out_shape=[jax.ShapeDtypeStruct((b, ls, D_MODEL), BF16),
                   jax.ShapeDtypeStruct(k_cache_all.shape, F32), jax.ShapeDtypeStruct(v_cache_all.shape, F32)],
        input_output_aliases={6: 1, 7: 2} if extra_k else {},
        compiler_params=_params(1),
        name="swa_sample",
    )(qb, k_new, v_new, k_cache_all, v_cache_all, bias_s, *extra_k, *extra_v)


def _layer_norm(y, g, b):
    mu = jnp.mean(y, -1, keepdims=True)
    d = y - mu
    var = jnp.mean(d * d, -1, keepdims=True)
    return d * lax.rsqrt(var + LN_EPS) * g + b


def _tail_body(x_ref, oa_ref, ob_ref, gates_ref, mod_ref, wpa_ref, wpb_ref, wout_ref, ln1g_ref, ln1b_ref,
               wup_ref, wdn_ref, ln2g_ref, ln2b_ref, o_ref, *, bt, tt, alpha):
    m = bt * tt
    mod = lambda i: mod_ref[:, :, i * D_MODEL:(i + 1) * D_MODEL]
    ga = gates_ref[:, :, 0:D_MODEL].astype(F32).reshape(m, D_MODEL)
    gb = gates_ref[:, :, D_MODEL:2 * D_MODEL].astype(F32).reshape(m, D_MODEL)
    mixed = ga * _dot(oa_ref[...].reshape(m, D_MODEL), wpa_ref[...]) \
        + gb * _dot(ob_ref[...].reshape(m, D_MODEL), wpb_ref[...])
    attn = _dot(mixed.astype(BF16), wout_ref[...]).reshape(bt, tt, D_MODEL)
    x1 = _layer_norm(alpha * x_ref[...] + mod(2) * attn, ln1g_ref[...], ln1b_ref[...])
    h2 = (x1 * (1.0 + mod(4)) + mod(3)).reshape(m, D_MODEL).astype(BF16)
    ff = jnp.zeros((m, D_MODEL), F32)
    for c in range(D_FF // D_MODEL):
        cols = slice(c * D_MODEL, (c + 1) * D_MODEL)
        a = jnp.maximum(_dot(h2, wup_ref[:, cols]), 0.0)
        ff = ff + _dot((a * a).astype(BF16), wdn_ref[cols, :])
    o_ref[...] = _layer_norm(alpha * x1 + mod(5) * ff.reshape(bt, tt, D_MODEL), ln2g_ref[...], ln2b_ref[...])


def _tail(x, oa, ob, gates, mod, w, p, *, layer, bt, tt, alpha):
    b, l, _ = x.shape
    tok_spec = lambda n: pl.BlockSpec((bt, tt, n), lambda i, t: (i, t, 0))
    row = _const_spec((1, D_MODEL))
    sq = _layer_spec((D_MODEL, D_MODEL), layer)
    return pl.pallas_call(
        functools.partial(_tail_body, bt=bt, tt=tt, alpha=alpha),
        grid=(b // bt, l // tt),
        in_specs=[tok_spec(D_MODEL), tok_spec(D_MODEL), tok_spec(D_MODEL), tok_spec(2 * D_MODEL),
                  pl.BlockSpec((bt, 1, 6 * D_MODEL), lambda i, t: (i, 0, 0)),
                  sq, sq, sq, row, row,
                  _layer_spec((D_MODEL, D_FF), layer), _layer_spec((D_FF, D_MODEL), layer), row, row],
        out_specs=tok_spec(D_MODEL),
        out_shape=jax.ShapeDtypeStruct((b, l, D_MODEL), F32),
        compiler_params=_params(2),
        name="tail",
    )(x, oa, ob, gates, mod, w["wpa"], w["wpb"], w["wout"], p["ln1_g"], p["ln1_b"],
      w["wup"], w["wdn"], p["ln2_g"], p["ln2_b"])


def _tiles(b, l, rows):
    tt = min(l, rows)
    bt = max(1, min(b, rows // tt))
    return bt, tt


def _layer_params(l, w_conv, a_log, dt_bias, w_onorm, ln1_g, ln1_b, ln2_g, ln2_b):
    gpar = jnp.zeros((2, GB_LANES), F32)
    gpar = gpar.at[0, N_HEADS_A:2 * N_HEADS_A].set(-jnp.exp(a_log[l].astype(F32)))
    gpar = gpar.at[1, N_HEADS_A:2 * N_HEADS_A].set(dt_bias[l].astype(F32))
    row = lambda a: a[l].reshape(1, -1).astype(F32)
    return dict(wconv=w_conv[l].astype(F32), gpar=gpar, w_onorm=row(w_onorm), ln1_g=row(ln1_g), ln1_b=row(ln1_b),
                ln2_g=row(ln2_g), ln2_b=row(ln2_b))


def _trunk_layer(x, mod, layer, w, p, alpha, conv_state, sample_state):
    b, l, _ = x.shape
    bt, tt = _tiles(b, l, 512 if l % 512 == 0 else 256)
    cs, z, gb, qb, k, v, gates, conv_out = _front(x, mod, conv_state, w["wm"], w["wba"], p["wconv"], p["gpar"],
                                                  layer=layer, bt=bt, tt=tt)
    if sample_state is None:
        oa, s_new = _delta_prompt(cs, z, gb, p["w_onorm"], tt=min(l, 256))
        ob, k_new, v_new = _swa_prompt(qb, k, v, p["bias_t"])
    else:
        s0_all, kc_all, vc_all, s_prev, k_prev, v_prev = sample_state
        oa, s_new = _delta_sample(cs, z, gb, p["w_onorm"], s0_all, s_prev, layer=layer, bt=min(b, 8))
        bias_s = p["bias_t"][:, :WINDOW + l, :].reshape(N_KV_B, WINDOW + l, GROUP_B, WINDOW)[..., :l]
        bias_s = bias_s.reshape(N_KV_B, WINDOW + l, GROUP_B * l)
        ob, k_new, v_new = _swa_sample(qb, k, v, kc_all, vc_all, bias_s, k_prev, v_prev, layer=layer, bt=min(b, 8))
    bt, tt = _tiles(b, l, 512)
    x2 = _tail(x, oa, ob, gates, mod, w, p, layer=layer, bt=bt, tt=tt, alpha=alpha)
    return x2, (s_new, conv_out, k_new, v_new)


def kernel(x_prompt, x_sample, state_delta, state_conv, cache_k, cache_v, c_prompt, c_sample, rel_bias, w_ada, b_ada, w_in, w_conv, a_log, dt_bias, w_onorm, sinks, w_pa, w_pb, w_out, ln1_g, ln1_b, w_up, w_down, ln2_g, ln2_b):
    depth = w_in.shape[0]
    alpha = (2 * depth) ** 0.25
    bp = x_prompt.shape[0]
    mod_all = _ada(jnp.concatenate([c_prompt, c_sample], axis=0), w_ada, b_ada)
    bias_t = _bias_table_t(rel_bias, sinks)
    wm, wba = _prep_w_in(w_in)
    w = dict(wm=wm, wba=wba, wpa=_cast_bf16(w_pa), wpb=_cast_bf16(w_pb), wout=_cast_bf16(w_out),
             wup=_cast_bf16(w_up), wdn=_cast_bf16(w_down))
    bs = x_sample.shape[0]
    kc_all = cache_k.reshape(depth, bs, WINDOW, KV_B)
    vc_all = cache_v.reshape(depth, bs, WINDOW, KV_B)
    yp, ys = x_prompt, x_sample
    prompt_outs = [[] for _ in range(4)]
    sample_conv = []
    s_all = k_all = v_all = None
    for l in range(depth):
        p = _layer_params(l, w_conv, a_log, dt_bias, w_onorm, ln1_g, ln1_b, ln2_g, ln2_b)
        p["bias_t"] = bias_t[l]
        mod_p = mod_all[l, :bp][:, None, :]
        mod_s = mod_all[l, bp:][:, None, :]
        zero_conv = jnp.zeros((bp, CONV_W - 1, CONV_DIM), x_prompt.dtype)
        yp, rest_p = _trunk_layer(yp, mod_p, l, w, p, alpha, zero_conv, None)
        ys, (s_all, conv_s, k_all, v_all) = _trunk_layer(ys, mod_s, l, w, p, alpha, state_conv[l],
                                                         (state_delta, kc_all, vc_all, s_all, k_all, v_all))
        for acc, val in zip(prompt_outs, rest_p):
            acc.append(val)
        sample_conv.append(conv_s)
    heads = lambda a: a.reshape(a.shape[:-1] + (N_KV_B, HD_B))
    pd, pc, pk, pv = (jnp.stack(a) for a in prompt_outs)
    return (yp, ys, pd, pc, heads(pk), heads(pv), s_all, jnp.stack(sample_conv), heads(k_all), heads(v_all))
```

```python
import functools
import math

import numpy as np
import jax
import jax.numpy as jnp
from jax import lax
from jax.experimental import pallas as pl
from jax.experimental.pallas import tpu as pltpu

F32 = jnp.float32
BF16 = jnp.bfloat16

D_MODEL = 1024
N_HEADS_A = 8
DK_A = 128
CONV_W = 4
CONV_DIM = 3 * D_MODEL
CHUNK = 64
HD_B = 64
N_HEADS_B = 16
N_KV_B = 4
GROUP_B = N_HEADS_B // N_KV_B
KV_B = N_KV_B * HD_B
WINDOW = 128
N_BUCKETS = 32
MAX_DISTANCE = 128
D_FF = 4 * D_MODEL
LN_EPS = 1e-5
RMS_EPS = 1e-6

OFF_Z = CONV_DIM
OFF_QB = OFF_Z + D_MODEL
OFF_KVB = OFF_QB + D_MODEL
OFF_GATES = OFF_KVB + 2 * KV_B
N_MAIN = OFF_GATES + 2 * D_MODEL
GB_LANES = 128

V7X_VMEM_BYTES = 64 * 1024 * 1024
VMEM_LIMIT = V7X_VMEM_BYTES - 8 * 1024 * 1024


def _params(n_grid):
    return pltpu.CompilerParams(dimension_semantics=("arbitrary",) * n_grid, vmem_limit_bytes=VMEM_LIMIT)


def _dot(a, b):
    return jnp.dot(a, b, preferred_element_type=F32)


def _dot_nt(a, b):
    return lax.dot_general(a, b, (((1,), (1,)), ((), ())), preferred_element_type=F32)


def _dot_tn(a, b):
    return lax.dot_general(a, b, (((0,), (0,)), ((), ())), preferred_element_type=F32)


def _sigmoid(x):
    return 1.0 / (1.0 + jnp.exp(-x))


def _silu(x):
    return x * _sigmoid(x)


def _const_spec(shape):
    nd = len(shape)
    return pl.BlockSpec(shape, lambda *_: (0,) * nd, pipeline_mode=pl.Buffered(1))


def _layer_spec(shape, layer):
    nd = len(shape)
    return pl.BlockSpec((None,) + tuple(shape), lambda *_: (layer,) + (0,) * nd, pipeline_mode=pl.Buffered(1))


def _cast_body(w_ref, o_ref):
    o_ref[...] = w_ref[...].astype(BF16)


def _cast_bf16(w):
    depth, k, n = w.shape
    bk = max(8, min(k, (1024 * 1024) // n))
    spec = pl.BlockSpec((None, bk, n), lambda l, i: (l, i, 0))
    return pl.pallas_call(
        _cast_body, grid=(depth, k // bk), in_specs=[spec], out_specs=spec,
        out_shape=jax.ShapeDtypeStruct(w.shape, BF16), compiler_params=_params(2), name="cast_bf16",
    )(w)


def _prep_w_in(w_in):
    o_ba = CONV_DIM + D_MODEL
    o_qb = o_ba + 2 * N_HEADS_A
    wm = jnp.concatenate([w_in[:, :, :o_ba], w_in[:, :, o_qb:]], axis=2).astype(BF16)
    wba = jnp.pad(w_in[:, :, o_ba:o_qb], ((0, 0), (0, 0), (0, GB_LANES - 2 * N_HEADS_A))).astype(BF16)
    return wm, wba


def _ada_body(c_ref, w_ref, b_ref, o_ref):
    s = _silu(c_ref[...]).astype(BF16)
    o_ref[...] = _dot(s, w_ref[...].astype(BF16)) + b_ref[...]


def _ada(c_all, w_ada, b_ada):
    depth = w_ada.shape[0]
    n_rows = c_all.shape[0]
    n_col = w_ada.shape[2] // D_MODEL
    return pl.pallas_call(
        _ada_body,
        grid=(depth, n_col),
        in_specs=[pl.BlockSpec((n_rows, D_MODEL), lambda l, n: (0, 0)),
                  pl.BlockSpec((None, D_MODEL, D_MODEL), lambda l, n: (l, 0, n)),
                  pl.BlockSpec((None, 1, D_MODEL), lambda l, n: (l, 0, n))],
        out_specs=pl.BlockSpec((None, n_rows, D_MODEL), lambda l, n: (l, 0, n)),
        out_shape=jax.ShapeDtypeStruct((depth, n_rows, w_ada.shape[2]), F32),
        compiler_params=_params(2),
        name="ada",
    )(c_all, w_ada, b_ada.reshape(depth, 1, -1))


def _bucket_table():
    r = np.arange(WINDOW)[:, None]
    c = np.arange(2 * WINDOW)[None, :]
    dist = WINDOW + r - c
    n = np.maximum(dist, 0)
    max_exact = N_BUCKETS // 2
    ratio = np.maximum(n, max_exact).astype(np.float32) / np.float32(max_exact)
    large = max_exact + (np.log(ratio) / np.float32(math.log(MAX_DISTANCE / max_exact))
                         * np.float32(N_BUCKETS - max_exact)).astype(np.int32)
    large = np.minimum(large, N_BUCKETS - 1)
    bucket = np.where(n < max_exact, n, large).astype(np.int32)
    valid = ((dist >= 0) & (dist < WINDOW)).astype(np.int32)
    return bucket, valid


def _bias_t_body(rb_ref, sink_ref, bucket_ref, valid_ref, o_ref):
    l = pl.program_id(0)
    h = pl.program_id(1)
    bucket = bucket_ref[...]
    acc = jnp.zeros(bucket.shape, F32)
    for j in range(N_BUCKETS):
        acc = jnp.where(bucket == j, rb_ref[j, h], acc)
    acc = jnp.where(valid_ref[...] > 0, acc, -jnp.inf)
    key = lax.broadcasted_iota(jnp.int32, bucket.shape, 0)
    o_ref[...] = jnp.where(key == 0, sink_ref[l, h], acc)


def _bias_table_t(rel_bias, sinks):
    depth = sinks.shape[0]
    bucket, valid = _bucket_table()
    return pl.pallas_call(
        _bias_t_body,
        grid=(depth, N_HEADS_B),
        in_specs=[pl.BlockSpec(memory_space=pltpu.SMEM), pl.BlockSpec(memory_space=pltpu.SMEM),
                  pl.BlockSpec((2 * WINDOW, WINDOW), lambda l, h: (0, 0)),
                  pl.BlockSpec((2 * WINDOW, WINDOW), lambda l, h: (0, 0))],
        out_specs=pl.BlockSpec((None, None, 2 * WINDOW, WINDOW), lambda l, h: (l, h // GROUP_B, 0, h % GROUP_B)),
        out_shape=jax.ShapeDtypeStruct((depth, N_KV_B, 2 * WINDOW, GROUP_B * WINDOW), F32),
        compiler_params=_params(2),
        name="bias_table_t",
    )(rel_bias.astype(F32), sinks.astype(F32), jnp.asarray(bucket.T.copy()), jnp.asarray(valid.T.copy()))


def _front_body(x_ref, mod_ref, st_ref, wm_ref, wba_ref, wconv_ref, gpar_ref,
                cs_ref, z_ref, gb_ref, qb_ref, k_ref, v_ref, gates_ref, cst_ref, cbuf, *, bt, tt):
    m = bt * tt
    sh1 = mod_ref[:, :, 0:D_MODEL]
    sc1 = mod_ref[:, :, D_MODEL:2 * D_MODEL]
    h = (x_ref[...] * (1.0 + sc1) + sh1).reshape(m, D_MODEL).astype(BF16)

    @pl.when(pl.program_id(1) == 0)
    def _():
        cbuf[:, 0:8, :] = jnp.zeros((bt, 8, CONV_DIM), F32)
        cbuf[:, 5:8, :] = st_ref[...]

    def proj(off, n):
        return _dot(h, wm_ref[:, off:off + n])

    half = D_MODEL // 2

    def conv_piece(i):
        cols = slice(i * half, (i + 1) * half)
        cbuf[:, 8:8 + tt, cols] = proj(i * half, half).reshape(bt, tt, half)
        u = cbuf[:, :, cols]
        u1 = pltpu.roll(u, 1, 1)
        w0, w1, w2, w3 = (wconv_ref[j:j + 1, cols] for j in range(CONV_W))
        y = (u * w3 + u1 * w2) + pltpu.roll(u * w1 + u1 * w0, 2, 1)
        cs_ref[:, :, cols] = _silu(y[:, 8:, :]).astype(cs_ref.dtype)

    def z_piece(i):
        z_ref[:, :, i * half:(i + 1) * half] = proj(OFF_Z + i * half, half).astype(z_ref.dtype).reshape(bt, tt, half)

    def qb_piece(i):
        q = proj(OFF_QB + i * half, half) * (HD_B ** -0.5)
        qb_ref[:, :, i * half:(i + 1) * half] = q.astype(BF16).reshape(bt, tt, half)

    def kv_piece(_):
        kv = proj(OFF_KVB, 2 * KV_B)
        k_ref[...] = kv[:, :KV_B].reshape(bt, tt, KV_B)
        v_ref[...] = kv[:, KV_B:].reshape(bt, tt, KV_B)

    def gates_piece(i):
        g = _sigmoid(proj(OFF_GATES + i * half, half))
        gates_ref[:, :, i * half:(i + 1) * half] = g.astype(gates_ref.dtype).reshape(bt, tt, half)

    for piece, i in [(conv_piece, 0), (z_piece, 0), (conv_piece, 1), (z_piece, 1), (conv_piece, 2), (qb_piece, 0),
                     (conv_piece, 3), (qb_piece, 1), (conv_piece, 4), (kv_piece, 0), (gates_piece, 0),
                     (conv_piece, 5), (gates_piece, 1), (gates_piece, 2), (gates_piece, 3)]:
        piece(i)
    tail = cbuf[:, tt + 5:tt + 8, :]
    cst_ref[...] = tail
    cbuf[:, 5:8, :] = tail

    ba = _dot(h, wba_ref[...])
    xg = ba + gpar_ref[1:2, :]
    softplus = jnp.maximum(xg, 0.0) + jnp.log(1.0 + jnp.exp(-jnp.abs(xg)))
    lane = lax.broadcasted_iota(jnp.int32, ba.shape, 1)
    gb = jnp.where(lane < N_HEADS_A, _sigmoid(ba), gpar_ref[0:1, :] * softplus)
    gb_ref[...] = gb.reshape(bt, tt, GB_LANES)


def _front(x, mod, conv_state, wm, wba, wconv, gpar, *, layer, bt, tt):
    b, l, _ = x.shape
    grid = (b // bt, l // tt)
    tok = lambda n, dt=F32: jax.ShapeDtypeStruct((b, l, n), dt)
    act = BF16 if tt % 16 == 0 else F32
    tok_spec = lambda n: pl.BlockSpec((bt, tt, n), lambda i, t: (i, t, 0))
    return pl.pallas_call(
        functools.partial(_front_body, bt=bt, tt=tt),
        grid=grid,
        in_specs=[tok_spec(D_MODEL),
                  pl.BlockSpec((bt, 1, 6 * D_MODEL), lambda i, t: (i, 0, 0)),
                  pl.BlockSpec((bt, CONV_W - 1, CONV_DIM), lambda i, t: (i, 0, 0)),
                  _layer_spec((D_MODEL, N_MAIN), layer),
                  _layer_spec((D_MODEL, GB_LANES), layer),
                  _const_spec((CONV_W, CONV_DIM)),
                  _const_spec((2, GB_LANES))],
        out_specs=[tok_spec(CONV_DIM), tok_spec(D_MODEL), tok_spec(GB_LANES), tok_spec(D_MODEL),
                   tok_spec(KV_B), tok_spec(KV_B), tok_spec(2 * D_MODEL),
                   pl.BlockSpec((bt, CONV_W - 1, CONV_DIM), lambda i, t: (i, 0, 0))],
        out_shape=[tok(CONV_DIM, act), tok(D_MODEL, act), tok(GB_LANES), tok(D_MODEL, BF16),
                   tok(KV_B), tok(KV_B), tok(2 * D_MODEL, act),
                   jax.ShapeDtypeStruct((b, CONV_W - 1, CONV_DIM), F32)],
        scratch_shapes=[pltpu.VMEM((bt, tt + 8, CONV_DIM), F32)],
        compiler_params=_params(2),
        name="front",
    )(x, mod, conv_state, wm, wba, wconv, gpar)


def _delta_masks(r, block):
    i = lax.broadcasted_iota(jnp.int32, (r, r), 0)
    j = lax.broadcasted_iota(jnp.int32, (r, r), 1)
    shift = int(math.log2(block))
    same = (i >> shift) == (j >> shift)
    levels = [((i >> (s + 1)) == (j >> (s + 1))) & ((i >> s) != (j >> s)) for s in range(shift)]
    return dict(tri=same & (i >= j), strict=same & (i > j), upper=same & (i < j), eye=(i == j).astype(F32),
                levels=levels)


def _split_dot(a01, x):
    a01 = a01.astype(BF16)
    hi = x.astype(BF16)
    rest = x - hi.astype(F32)
    mid = rest.astype(BF16)
    lo = (rest - mid.astype(F32)).astype(BF16)
    return _dot(a01, hi) + _dot(a01, mid) + _dot(a01, lo)


def _decay_sums(g, masks):
    return _split_dot(masks["tri"], g), _split_dot(masks["upper"], g)


def _delta_pre(problems, masks):
    tri, strict, eye, levels = masks["tri"], masks["strict"], masks["eye"], masks["levels"]
    n = len(problems)
    r = problems[0][0].shape[0]
    lhs, ks, rhs, decays, qgs, kds = [], [], [], [], [], []
    for qr, kr, v, beta_b, gc, rev, gc_row in problems:
        q = qr * lax.rsqrt(jnp.sum(qr * qr, -1, keepdims=True) + 1e-6) * (DK_A ** -0.5)
        k = kr * lax.rsqrt(jnp.sum(kr * kr, -1, keepdims=True) + 1e-6)
        gc_col = gc[:, :r] if r <= DK_A else jnp.concatenate([gc] * (r // DK_A), axis=1)
        decays.append(jnp.exp(jnp.where(tri, gc_col - gc_row, -jnp.inf)))
        eg = jnp.exp(gc)
        kb = k * beta_b
        lhs.append(jnp.concatenate([kb, q], axis=0).astype(BF16))
        ks.append(k.astype(BF16))
        rhs.append(jnp.concatenate([v * beta_b, kb * eg], axis=1).astype(BF16))
        qgs.append((q * eg).astype(BF16))
        kds.append((k * jnp.exp(rev)).astype(BF16))
    kk = [_dot_nt(lhs[i], ks[i]) for i in range(n)]
    mm = [jnp.where(strict, kk[i][:r] * decays[i], 0.0) for i in range(n)]
    qk = [(kk[i][r:] * decays[i]).astype(BF16) for i in range(n)]
    t = [eye - jnp.where(levels[0], mm[i], 0.0) for i in range(n)]
    for s, lvl in enumerate(levels[1:], start=1):
        blk = 2 ** s
        tb = [t[i].astype(BF16) for i in range(n)]
        mo = [jnp.where(lvl, mm[i], 0.0).astype(BF16) for i in range(n)]
        if blk % 8:
            x = [_dot(tb[i], mo[i]).astype(BF16) for i in range(n)]
            y = [_dot(x[i], tb[i]) for i in range(n)]
            t = [t[i] - y[i] for i in range(n)]
        else:
            split = [t[i].reshape(r // (2 * blk), 2, blk, r) for i in range(n)]
            lo = [split[i][:, 1].reshape(r // 2, r) for i in range(n)]
            x = [_dot(lo[i].astype(BF16), mo[i]).astype(BF16) for i in range(n)]
            y = [_dot(x[i], tb[i]) for i in range(n)]
            t = [jnp.stack([split[i][:, 0], (lo[i] - y[i]).reshape(r // (2 * blk), blk, r)], axis=1).reshape(r, r)
                 for i in range(n)]
    uw = [_dot(t[i].astype(BF16), rhs[i]) for i in range(n)]
    return [(uw[i][:, :DK_A], uw[i][:, DK_A:].astype(BF16), qgs[i], kds[i], qk[i]) for i in range(n)]


def _gated_norm(o, z, w_onorm):
    o = o * lax.rsqrt(jnp.mean(o * o, -1, keepdims=True) + RMS_EPS) * w_onorm
    return o * _silu(z)


def _delta_prompt_body(cs_ref, z_ref, gb_ref, wn_ref, o_ref, s_out_ref,
                       s_ref, u_s, o_s, wq_s, qkd_s, egl_s, *, tt):
    n_c = tt // CHUNK
    masks = _delta_masks(tt, CHUNK)

    @pl.when(pl.program_id(1) == 0)
    def _():
        s_ref[...] = jnp.zeros(s_ref.shape, F32)

    gbv = gb_ref[...]
    gc_all, rev_all = _decay_sums(gbv, masks)
    gc_t = gc_all.T
    heads = range(N_HEADS_A)
    hcols = [slice(hd * DK_A, (hd + 1) * DK_A) for hd in heads]
    problems = []
    for hd in heads:
        lg = N_HEADS_A + hd
        problems.append((cs_ref[:, hcols[hd]].astype(F32),
                         cs_ref[:, D_MODEL + hd * DK_A:D_MODEL + (hd + 1) * DK_A].astype(F32),
                         cs_ref[:, 2 * D_MODEL + hd * DK_A:2 * D_MODEL + (hd + 1) * DK_A].astype(F32),
                         jnp.broadcast_to(gbv[:, hd:hd + 1], (tt, DK_A)),
                         jnp.broadcast_to(gc_all[:, lg:lg + 1], (tt, DK_A)),
                         jnp.broadcast_to(rev_all[:, lg:lg + 1], (tt, DK_A)),
                         jnp.broadcast_to(gc_t[lg:lg + 1, :], (tt, tt))))
    for hd, (u, w, qg, kd, qk) in enumerate(_delta_pre(problems, masks)):
        u_s[:, hcols[hd]] = u
        kdt = kd.T
        for c in range(n_c):
            blk = slice(c * CHUNK, (c + 1) * CHUNK)
            idx = hd * n_c + c
            wq_s[idx, 0:CHUNK, :] = w[blk]
            wq_s[idx, CHUNK:2 * CHUNK, :] = qg[blk]
            qkd_s[idx, 0:CHUNK, :] = qk[blk, blk]
            qkd_s[idx, CHUNK:CHUNK + DK_A, :] = kdt[:, blk]
            last = (c + 1) * CHUNK - 1
            egl_s[idx] = jnp.broadcast_to(jnp.exp(problems[hd][4][last:last + 1, :]), (8, DK_A))

    states = [s_ref[hd] for hd in heads]
    for c in range(n_c):
        blk = slice(c * CHUNK, (c + 1) * CHUNK)
        idx = [hd * n_c + c for hd in heads]
        ws = [_dot(wq_s[idx[hd]], states[hd].astype(BF16)) for hd in heads]
        v_new = [(u_s[blk, hcols[hd]] - ws[hd][:CHUNK]).astype(BF16) for hd in heads]
        upd = [_dot(qkd_s[idx[hd]], v_new[hd]) for hd in heads]
        for hd in heads:
            o_s[blk, hcols[hd]] = ws[hd][CHUNK:] + upd[hd][:CHUNK]
            states[hd] = states[hd] * egl_s[idx[hd]][0:1, :] + upd[hd][CHUNK:]
    for hd in heads:
        s_ref[hd] = states[hd]
        o_ref[:, hcols[hd]] = _gated_norm(o_s[:, hcols[hd]], z_ref[:, hcols[hd]].astype(F32), wn_ref[...]).astype(BF16)

    @pl.when(pl.program_id(1) == pl.num_programs(1) - 1)
    def _():
        s_out_ref[...] = s_ref[...]


def _delta_prompt(cs, z, gb, w_onorm, *, tt):
    b, l, _ = cs.shape
    n_c = tt // CHUNK
    tok_spec = lambda n: pl.BlockSpec((None, tt, n), lambda i, t: (i, t, 0))
    return pl.pallas_call(
        functools.partial(_delta_prompt_body, tt=tt),
        grid=(b, l // tt),
        in_specs=[tok_spec(CONV_DIM), tok_spec(D_MODEL), tok_spec(GB_LANES), _const_spec((1, DK_A))],
        out_specs=[tok_spec(D_MODEL),
                   pl.BlockSpec((None, N_HEADS_A, DK_A, DK_A), lambda i, t: (i, 0, 0, 0))],
        out_shape=[jax.ShapeDtypeStruct((b, l, D_MODEL), BF16),
                   jax.ShapeDtypeStruct((b, N_HEADS_A, DK_A, DK_A), F32)],
        scratch_shapes=[pltpu.VMEM((N_HEADS_A, DK_A, DK_A), F32),
                        pltpu.VMEM((tt, D_MODEL), F32), pltpu.VMEM((tt, D_MODEL), F32),
                        pltpu.VMEM((N_HEADS_A * n_c, 2 * CHUNK, DK_A), BF16),
                        pltpu.VMEM((N_HEADS_A * n_c, CHUNK + DK_A, CHUNK), BF16),
                        pltpu.VMEM((N_HEADS_A * n_c, 8, DK_A), F32)],
        compiler_params=_params(2),
        name="delta_prompt",
    )(cs, z, gb, w_onorm)


def _delta_sample_body(cs_ref, z_ref, gb_ref, wn_ref, s0_ref, *rest, bt, nb, ls):
    o_ref, s_out_ref = rest[-2:]
    r = N_HEADS_A * ls
    masks = _delta_masks(r, ls)

    heads = range(N_HEADS_A)
    hrows = [slice(hd * ls, (hd + 1) * ls) for hd in heads]
    hcols = [slice(hd * DK_A, (hd + 1) * DK_A) for hd in heads]

    def per_group(gi, carry):
        bis = [gi * nb + d for d in range(nb)]
        problems, egl = [], []
        for bi in bis:
            cs = cs_ref[bi].astype(F32)
            gbv = gb_ref[bi]
            stack = lambda base: jnp.concatenate([cs[:, base + hd * DK_A: base + (hd + 1) * DK_A] for hd in heads],
                                                 axis=0)
            beta_b = jnp.concatenate([jnp.broadcast_to(gbv[:, hd:hd + 1], (ls, DK_A)) for hd in heads], axis=0)
            g_b = jnp.concatenate(
                [jnp.broadcast_to(gbv[:, N_HEADS_A + hd:N_HEADS_A + hd + 1], (ls, DK_A)) for hd in heads], axis=0)
            gc, rev = _decay_sums(g_b, masks)
            egl.append(jnp.exp(gc))
            problems.append((stack(0), stack(D_MODEL), stack(2 * D_MODEL), beta_b, gc, rev, gc.T[:r, :]))
        pre = _delta_pre(problems, masks)
        both = [[_dot(jnp.concatenate([pre[d][1][hrows[hd]], pre[d][2][hrows[hd]]], axis=0),
                      s0_ref[bis[d], hd].astype(BF16)) for hd in heads] for d in range(nb)]
        v_new = [(pre[d][0] - jnp.concatenate([both[d][hd][:ls] for hd in heads], axis=0)).astype(BF16)
                 for d in range(nb)]
        qkv = [_dot(pre[d][4], v_new[d]) for d in range(nb)]
        upd = [[_dot_tn(pre[d][3][hrows[hd]], v_new[d][hrows[hd]]) for hd in heads] for d in range(nb)]
        for d, bi in enumerate(bis):
            zb = z_ref[bi].astype(F32)
            for hd in heads:
                last = (hd + 1) * ls - 1
                s_out_ref[bi, hd] = s0_ref[bi, hd] * egl[d][last:last + 1, :] + upd[d][hd]
                o = both[d][hd][ls:] + qkv[d][hrows[hd]]
                o_ref[bi, :, hcols[hd]] = _gated_norm(o, zb[:, hcols[hd]], wn_ref[...]).astype(BF16)
        return carry

    lax.fori_loop(0, bt // nb, per_group, 0)


def _fill_layer(prev):
    if prev is None:
        return [], []
    return [prev], [pl.BlockSpec(memory_space=pl.ANY)]


def _delta_sample(cs, z, gb, w_onorm, s0_all, s_prev, *, layer, bt):
    b, ls, _ = cs.shape
    tok_spec = lambda n: pl.BlockSpec((bt, ls, n), lambda i: (i, 0, 0))
    st_spec = pl.BlockSpec((None, bt, N_HEADS_A, DK_A, DK_A), lambda i: (layer, i, 0, 0, 0))
    extra, extra_specs = _fill_layer(s_prev)
    return pl.pallas_call(
        functools.partial(_delta_sample_body, bt=bt, nb=min(bt, 4), ls=ls),
        grid=(b // bt,),
        in_specs=[tok_spec(CONV_DIM), tok_spec(D_MODEL), tok_spec(GB_LANES), _const_spec((1, DK_A)), st_spec]
        + extra_specs,
        out_specs=[tok_spec(D_MODEL), st_spec],
        out_shape=[jax.ShapeDtypeStruct((b, ls, D_MODEL), BF16), jax.ShapeDtypeStruct(s0_all.shape, F32)],
        input_output_aliases={5: 1} if extra else {},
        compiler_params=_params(1),
        name="delta_sample",
    )(cs, z, gb, w_onorm, s0_all, *extra)


def _softmax_keys(s):
    p = jnp.exp(s - jnp.max(s, axis=0, keepdims=True))
    return (p / jnp.sum(p, axis=0, keepdims=True)).astype(BF16)


def _zero_key0(x):
    return jnp.where(lax.broadcasted_iota(jnp.int32, x.shape, 0) == 0, 0.0, x)


def _kv_cols(kvh):
    return slice(kvh * HD_B, (kvh + 1) * HD_B)


def _group_queries(q, kvh):
    return jnp.concatenate([q[:, hh * HD_B:(hh + 1) * HD_B] for hh in range(kvh * GROUP_B, (kvh + 1) * GROUP_B)],
                           axis=0)


def _swa_prompt_body(q_ref, kp_ref, kc_ref, vp_ref, vc_ref, bias_ref, o_ref, ko_ref, vo_ref, *, nq):
    @pl.when(pl.program_id(1) == pl.num_programs(1) - 1)
    def _():
        ko_ref[...] = kc_ref[(nq - 1) * WINDOW:, :]
        vo_ref[...] = vc_ref[(nq - 1) * WINDOW:, :]

    n_q = GROUP_B * WINDOW
    key = lax.broadcasted_iota(jnp.int32, (2 * WINDOW, n_q), 0)
    no_prev = (pl.program_id(1) == 0) & (key >= 1) & (key < WINDOW)
    kall = jnp.concatenate([kp_ref[...], kc_ref[...]], axis=0)
    vall = jnp.concatenate([vp_ref[...], vc_ref[...]], axis=0)
    probs = [(j, kvh) for j in range(nq) for kvh in range(N_KV_B)]
    k2 = [_zero_key0(kall[j * WINDOW:(j + 2) * WINDOW]).astype(BF16) for j in range(nq)]
    v2t = [_zero_key0(vall[j * WINDOW:(j + 2) * WINDOW]).T.astype(BF16) for j in range(nq)]
    q = [q_ref[j * WINDOW:(j + 1) * WINDOW, :] for j in range(nq)]
    s = [_dot_nt(k2[j][:, _kv_cols(kvh)], _group_queries(q[j], kvh)) for j, kvh in probs]
    pn = []
    for (j, kvh), sc in zip(probs, s):
        sc = sc + bias_ref[kvh]
        pn.append(_softmax_keys(jnp.where(no_prev, -jnp.inf, sc) if j == 0 else sc))
    ot = [_dot(v2t[j][_kv_cols(kvh), :], p) for (j, kvh), p in zip(probs, pn)]
    for (j, kvh), o in zip(probs, ot):
        for g in range(GROUP_B):
            hh = kvh * GROUP_B + g
            o_ref[j * WINDOW:(j + 1) * WINDOW, hh * HD_B:(hh + 1) * HD_B] = \
                o[:, g * WINDOW:(g + 1) * WINDOW].T.astype(BF16)


def _swa_prompt(qb, k, v, bias_t):
    b, l, _ = qb.shape
    nq = 2 if l % (2 * WINDOW) == 0 else 1
    cur = lambda n: pl.BlockSpec((None, nq * WINDOW, n), lambda i, t: (i, t, 0))
    prev = lambda n: pl.BlockSpec((None, WINDOW, n), lambda i, t: (i, jnp.maximum(nq * t - 1, 0), 0))
    last = lambda n: pl.BlockSpec((None, WINDOW, n), lambda i, t: (i, 0, 0))
    return pl.pallas_call(
        functools.partial(_swa_prompt_body, nq=nq),
        grid=(b, l // (nq * WINDOW)),
        in_specs=[cur(D_MODEL), prev(KV_B), cur(KV_B), prev(KV_B), cur(KV_B),
                  _const_spec((N_KV_B, 2 * WINDOW, GROUP_B * WINDOW))],
        out_specs=[cur(D_MODEL), last(KV_B), last(KV_B)],
        out_shape=[jax.ShapeDtypeStruct((b, l, D_MODEL), BF16),
                   jax.ShapeDtypeStruct((b, WINDOW, KV_B), F32), jax.ShapeDtypeStruct((b, WINDOW, KV_B), F32)],
        compiler_params=_params(2),
        name="swa_prompt",
    )(qb, k, k, v, v, bias_t)


def _swa_sample_body(q_ref, kn_ref, vn_ref, kc_ref, vc_ref, bias_ref, *rest, bt, nb, ls):
    o_ref, ko_ref, vo_ref = rest[-3:]
    kvs = range(N_KV_B)

    def per_group(gi, carry):
        bis = [gi * nb + d for d in range(nb)]
        ks, vs, s = [], [], []
        for bi in bis:
            kf = jnp.concatenate([kc_ref[bi], kn_ref[bi]], axis=0)
            vf = jnp.concatenate([vc_ref[bi], vn_ref[bi]], axis=0)
            ko_ref[bi] = kf[ls:, :]
            vo_ref[bi] = vf[ls:, :]
            ks.append(_zero_key0(kf).astype(BF16))
            vs.append(_zero_key0(vf).astype(BF16))
        for d, bi in enumerate(bis):
            q = q_ref[bi]
            s.append([_dot_nt(ks[d][:, _kv_cols(kvh)], _group_queries(q, kvh)) for kvh in kvs])
        pn = [[_softmax_keys(s[d][kvh] + bias_ref[kvh]) for kvh in kvs] for d in range(nb)]
        o = [[_dot_tn(pn[d][kvh], vs[d][:, _kv_cols(kvh)]) for kvh in kvs] for d in range(nb)]
        for d, bi in enumerate(bis):
            for kvh in kvs:
                for g in range(GROUP_B):
                    hh = kvh * GROUP_B + g
                    o_ref[bi, :, hh * HD_B:(hh + 1) * HD_B] = o[d][kvh][g * ls:(g + 1) * ls].astype(BF16)
        return carry

    lax.fori_loop(0, bt // nb, per_group, 0)


def _swa_sample(qb, k_new, v_new, k_cache_all, v_cache_all, bias_s, k_prev, v_prev, *, layer, bt):
    b, ls, _ = qb.shape
    n_keys = WINDOW + ls
    new = lambda n: pl.BlockSpec((bt, ls, n), lambda i: (i, 0, 0))
    cache = pl.BlockSpec((None, bt, WINDOW, KV_B), lambda i: (layer, i, 0, 0))
    extra_k, specs_k = _fill_layer(k_prev)
    extra_v, specs_v = _fill_layer(v_prev)
    return pl.pallas_call(
        functools.partial(_swa_sample_body, bt=bt, nb=min(bt, 4), ls=ls),
        grid=(b // bt,),
        in_specs=[new(D_MODEL), new(KV_B), new(KV_B), cache, cache,
                  _const_spec((N_KV_B, n_keys, GROUP_B * ls))] + specs_k + specs_v,
        out_specs=[new(D_MODEL), cache, cache],
        out_shape=[jax.ShapeDtypeStruct((b, ls, D_MODEL), BF16),
                   jax.ShapeDtypeStruct(k_cache_all.shape, F32), jax.ShapeDtypeStruct(v_cache_all.shape, F32)],
        input_output_aliases={6: 1, 7: 2} if extra_k else {},
        compiler_params=_params(1),
        name="swa_sample",
    )(qb, k_new, v_new, k_cache_all, v_cache_all, bias_s, *extra_k, *extra_v)


def _layer_norm(y, g, b):
    mu = jnp.mean(y, -1, keepdims=True)
    d = y - mu
    var = jnp.mean(d * d, -1, keepdims=True)
    return d * lax.rsqrt(var + LN_EPS) * g + b


def _tail_body(x_ref, oa_ref, ob_ref, gates_ref, mod_ref, wpa_ref, wpb_ref, wout_ref, ln1g_ref, ln1b_ref,
               wup_ref, wdn_ref, ln2g_ref, ln2b_ref, o_ref, *, bt, tt, alpha):
    m = bt * tt
    mod = lambda i: mod_ref[:, :, i * D_MODEL:(i + 1) * D_MODEL]
    ga = gates_ref[:, :, 0:D_MODEL].astype(F32).reshape(m, D_MODEL)
    gb = gates_ref[:, :, D_MODEL:2 * D_MODEL].astype(F32).reshape(m, D_MODEL)
    mixed = ga * _dot(oa_ref[...].reshape(m, D_MODEL), wpa_ref[...]) \
        + gb * _dot(ob_ref[...].reshape(m, D_MODEL), wpb_ref[...])
    attn = _dot(mixed.astype(BF16), wout_ref[...]).reshape(bt, tt, D_MODEL)
    x1 = _layer_norm(alpha * x_ref[...] + mod(2) * attn, ln1g_ref[...], ln1b_ref[...])
    h2 = (x1 * (1.0 + mod(4)) + mod(3)).reshape(m, D_MODEL).astype(BF16)
    ff = jnp.zeros((m, D_MODEL), F32)
    for c in range(D_FF // D_MODEL):
        cols = slice(c * D_MODEL, (c + 1) * D_MODEL)
        a = jnp.maximum(_dot(h2, wup_ref[:, cols]), 0.0)
        ff = ff + _dot((a * a).astype(BF16), wdn_ref[cols, :])
    o_ref[...] = _layer_norm(alpha * x1 + mod(5) * ff.reshape(bt, tt, D_MODEL), ln2g_ref[...], ln2b_ref[...])


def _tail(x, oa, ob, gates, mod, w, p, *, layer, bt, tt, alpha):
    b, l, _ = x.shape
    tok_spec = lambda n: pl.BlockSpec((bt, tt, n), lambda i, t: (i, t, 0))
    row = _const_spec((1, D_MODEL))
    sq = _layer_spec((D_MODEL, D_MODEL), layer)
    return pl.pallas_call(
        functools.partial(_tail_body, bt=bt, tt=tt, alpha=alpha),
        grid=(b // bt, l // tt),
        in_specs=[tok_spec(D_MODEL), tok_spec(D_MODEL), tok_spec(D_MODEL), tok_spec(2 * D_MODEL),
                  pl.BlockSpec((bt, 1, 6 * D_MODEL), lambda i, t: (i, 0, 0)),
                  sq, sq, sq, row, row,
                  _layer_spec((D_MODEL, D_FF), layer), _layer_spec((D_FF, D_MODEL), layer), row, row],
        out_specs=tok_spec(D_MODEL),
        out_shape=jax.ShapeDtypeStruct((b, l, D_MODEL), F32),
        compiler_params=_params(2),
        name="tail",
    )(x, oa, ob, gates, mod, w["wpa"], w["wpb"], w["wout"], p["ln1_g"], p["ln1_b"],
      w["wup"], w["wdn"], p["ln2_g"], p["ln2_b"])


def _tiles(b, l, rows):
    tt = min(l, rows)
    bt = max(1, min(b, rows // tt))
    return bt, tt


def _layer_params(l, w_conv, a_log, dt_bias, w_onorm, ln1_g, ln1_b, ln2_g, ln2_b):
    gpar = jnp.zeros((2, GB_LANES), F32)
    gpar = gpar.at[0, N_HEADS_A:2 * N_HEADS_A].set(-jnp.exp(a_log[l].astype(F32)))
    gpar = gpar.at[1, N_HEADS_A:2 * N_HEADS_A].set(dt_bias[l].astype(F32))
    row = lambda a: a[l].reshape(1, -1).astype(F32)
    return dict(wconv=w_conv[l].astype(F32), gpar=gpar, w_onorm=row(w_onorm), ln1_g=row(ln1_g), ln1_b=row(ln1_b),
                ln2_g=row(ln2_g), ln2_b=row(ln2_b))


def _trunk_layer(x, mod, layer, w, p, alpha, conv_state, sample_state):
    b, l, _ = x.shape
    bt, tt = _tiles(b, l, 512 if l % 512 == 0 else 256)
    cs, z, gb, qb, k, v, gates, conv_out = _front(x, mod, conv_state, w["wm"], w["wba"], p["wconv"], p["gpar"],
                                                  layer=layer, bt=bt, tt=tt)
    if sample_state is None:
        oa, s_new = _delta_prompt(cs, z, gb, p["w_onorm"], tt=min(l, 256))
        ob, k_new, v_new = _swa_prompt(qb, k, v, p["bias_t"])
    else:
        s0_all, kc_all, vc_all, s_prev, k_prev, v_prev = sample_state
        oa, s_new = _delta_sample(cs, z, gb, p["w_onorm"], s0_all, s_prev, layer=layer, bt=min(b, 8))
        bias_s = p["bias_t"][:, :WINDOW + l, :].reshape(N_KV_B, WINDOW + l, GROUP_B, WINDOW)[..., :l]
        bias_s = bias_s.reshape(N_KV_B, WINDOW + l, GROUP_B * l)
        ob, k_new, v_new = _swa_sample(qb, k, v, kc_all, vc_all, bias_s, k_prev, v_prev, layer=layer, bt=min(b, 8))
    bt, tt = _tiles(b, l, 512)
    x2 = _tail(x, oa, ob, gates, mod, w, p, layer=layer, bt=bt, tt=tt, alpha=alpha)
    return x2, (s_new, conv_out, k_new, v_new)


def kernel(x_prompt, x_sample, state_delta, state_conv, cache_k, cache_v, c_prompt, c_sample, rel_bias, w_ada, b_ada, w_in, w_conv, a_log, dt_bias, w_onorm, sinks, w_pa, w_pb, w_out, ln1_g, ln1_b, w_up, w_down, ln2_g, ln2_b):
    depth = w_in.shape[0]
    alpha = (2 * depth) ** 0.25
    bp = x_prompt.shape[0]
    mod_all = _ada(jnp.concatenate([c_prompt, c_sample], axis=0), w_ada, b_ada)
    bias_t = _bias_table_t(rel_bias, sinks)
    wm, wba = _prep_w_in(w_in)
    w = dict(wm=wm, wba=wba, wpa=_cast_bf16(w_pa), wpb=_cast_bf16(w_pb), wout=_cast_bf16(w_out),
             wup=_cast_bf16(w_up), wdn=_cast_bf16(w_down))
    bs = x_sample.shape[0]
    kc_all = cache_k.reshape(depth, bs, WINDOW, KV_B)
    vc_all = cache_v.reshape(depth, bs, WINDOW, KV_B)
    yp, ys = x_prompt, x_sample
    prompt_outs = [[] for _ in range(4)]
    sample_conv = []
    s_all = k_all = v_all = None
    for l in range(depth):
        p = _layer_params(l, w_conv, a_log, dt_bias, w_onorm, ln1_g, ln1_b, ln2_g, ln2_b)
        p["bias_t"] = bias_t[l]
        mod_p = mod_all[l, :bp][:, None, :]
        mod_s = mod_all[l, bp:][:, None, :]
        zero_conv = jnp.zeros((bp, CONV_W - 1, CONV_DIM), x_prompt.dtype)
        yp, rest_p = _trunk_layer(yp, mod_p, l, w, p, alpha, zero_conv, None)
        ys, (s_all, conv_s, k_all, v_all) = _trunk_layer(ys, mod_s, l, w, p, alpha, state_conv[l],
                                                         (state_delta, kc_all, vc_all, s_all, k_all, v_all))
        for acc, val in zip(prompt_outs, rest_p):
            acc.append(val)
        sample_conv.append(conv_s)
    heads = lambda a: a.reshape(a.shape[:-1] + (N_KV_B, HD_B))
    pd, pc, pk, pv = (jnp.stack(a) for a in prompt_outs)
    return (yp, ys, pd, pc, heads(pk), heads(pv), s_all, jnp.stack(sample_conv), heads(k_all), heads(v_all))
```

```python
import functools
import math

import numpy as np
import jax
import jax.numpy as jnp
from jax import lax
from jax.experimental import pallas as pl
from jax.experimental.pallas import tpu as pltpu

F32 = jnp.float32
BF16 = jnp.bfloat16

D_MODEL = 1024
N_HEADS_A = 8
DK_A = 128
CONV_W = 4
CONV_DIM = 3 * D_MODEL
CHUNK = 64
HD_B = 64
N_HEADS_B = 16
N_KV_B = 4
GROUP_B = N_HEADS_B // N_KV_B
KV_B = N_KV_B * HD_B
WINDOW = 128
N_BUCKETS = 32
MAX_DISTANCE = 128
D_FF = 4 * D_MODEL
LN_EPS = 1e-5
RMS_EPS = 1e-6

OFF_Z = CONV_DIM
OFF_QB = OFF_Z + D_MODEL
OFF_KVB = OFF_QB + D_MODEL
OFF_GATES = OFF_KVB + 2 * KV_B
N_MAIN = OFF_GATES + 2 * D_MODEL
GB_LANES = 128

V7X_VMEM_BYTES = 64 * 1024 * 1024
VMEM_LIMIT = V7X_VMEM_BYTES - 8 * 1024 * 1024


def _params(n_grid):
    return pltpu.CompilerParams(dimension_semantics=("arbitrary",) * n_grid, vmem_limit_bytes=VMEM_LIMIT)


def _dot(a, b):
    return jnp.dot(a, b, preferred_element_type=F32)


def _dot_nt(a, b):
    return lax.dot_general(a, b, (((1,), (1,)), ((), ())), preferred_element_type=F32)


def _dot_tn(a, b):
    return lax.dot_general(a, b, (((0,), (0,)), ((), ())), preferred_element_type=F32)


def _sigmoid(x):
    return 1.0 / (1.0 + jnp.exp(-x))


def _silu(x):
    return x * _sigmoid(x)


def _const_spec(shape):
    nd = len(shape)
    return pl.BlockSpec(shape, lambda *_: (0,) * nd, pipeline_mode=pl.Buffered(1))


def _layer_spec(shape, layer):
    nd = len(shape)
    return pl.BlockSpec((None,) + tuple(shape), lambda *_: (layer,) + (0,) * nd, pipeline_mode=pl.Buffered(1))


def _cast_body(w_ref, o_ref):
    o_ref[...] = w_ref[...].astype(BF16)


def _cast_bf16(w):
    depth, k, n = w.shape
    bk = max(8, min(k, (1024 * 1024) // n))
    spec = pl.BlockSpec((None, bk, n), lambda l, i: (l, i, 0))
    return pl.pallas_call(
        _cast_body, grid=(depth, k // bk), in_specs=[spec], out_specs=spec,
        out_shape=jax.ShapeDtypeStruct(w.shape, BF16), compiler_params=_params(2), name="cast_bf16",
    )(w)


def _prep_w_in(w_in):
    o_ba = CONV_DIM + D_MODEL
    o_qb = o_ba + 2 * N_HEADS_A
    wm = jnp.concatenate([w_in[:, :, :o_ba], w_in[:, :, o_qb:]], axis=2).astype(BF16)
    wba = jnp.pad(w_in[:, :, o_ba:o_qb], ((0, 0), (0, 0), (0, GB_LANES - 2 * N_HEADS_A))).astype(BF16)
    return wm, wba


def _ada_body(c_ref, w_ref, b_ref, o_ref):
    s = _silu(c_ref[...]).astype(BF16)
    o_ref[...] = _dot(s, w_ref[...].astype(BF16)) + b_ref[...]


def _ada(c_all, w_ada, b_ada):
    depth = w_ada.shape[0]
    n_rows = c_all.shape[0]
    n_col = w_ada.shape[2] // D_MODEL
    return pl.pallas_call(
        _ada_body,
        grid=(depth, n_col),
        in_specs=[pl.BlockSpec((n_rows, D_MODEL), lambda l, n: (0, 0)),
                  pl.BlockSpec((None, D_MODEL, D_MODEL), lambda l, n: (l, 0, n)),
                  pl.BlockSpec((None, 1, D_MODEL), lambda l, n: (l, 0, n))],
        out_specs=pl.BlockSpec((None, n_rows, D_MODEL), lambda l, n: (l, 0, n)),
        out_shape=jax.ShapeDtypeStruct((depth, n_rows, w_ada.shape[2]), F32),
        compiler_params=_params(2),
        name="ada",
    )(c_all, w_ada, b_ada.reshape(depth, 1, -1))


def _bucket_table():
    r = np.arange(WINDOW)[:, None]
    c = np.arange(2 * WINDOW)[None, :]
    dist = WINDOW + r - c
    n = np.maximum(dist, 0)
    max_exact = N_BUCKETS // 2
    ratio = np.maximum(n, max_exact).astype(np.float32) / np.float32(max_exact)
    large = max_exact + (np.log(ratio) / np.float32(math.log(MAX_DISTANCE / max_exact))
                         * np.float32(N_BUCKETS - max_exact)).astype(np.int32)
    large = np.minimum(large, N_BUCKETS - 1)
    bucket = np.where(n < max_exact, n, large).astype(np.int32)
    valid = ((dist >= 0) & (dist < WINDOW)).astype(np.int32)
    return bucket, valid


def _bias_t_body(rb_ref, sink_ref, bucket_ref, valid_ref, o_ref):
    l = pl.program_id(0)
    h = pl.program_id(1)
    bucket = bucket_ref[...]
    acc = jnp.zeros(bucket.shape, F32)
    for j in range(N_BUCKETS):
        acc = jnp.where(bucket == j, rb_ref[j, h], acc)
    acc = jnp.where(valid_ref[...] > 0, acc, -jnp.inf)
    key = lax.broadcasted_iota(jnp.int32, bucket.shape, 0)
    o_ref[...] = jnp.where(key == 0, sink_ref[l, h], acc)


def _bias_table_t(rel_bias, sinks):
    depth = sinks.shape[0]
    bucket, valid = _bucket_table()
    return pl.pallas_call(
        _bias_t_body,
        grid=(depth, N_HEADS_B),
        in_specs=[pl.BlockSpec(memory_space=pltpu.SMEM), pl.BlockSpec(memory_space=pltpu.SMEM),
                  pl.BlockSpec((2 * WINDOW, WINDOW), lambda l, h: (0, 0)),
                  pl.BlockSpec((2 * WINDOW, WINDOW), lambda l, h: (0, 0))],
        out_specs=pl.BlockSpec((None, None, 2 * WINDOW, WINDOW), lambda l, h: (l, h // GROUP_B, 0, h % GROUP_B)),
        out_shape=jax.ShapeDtypeStruct((depth, N_KV_B, 2 * WINDOW, GROUP_B * WINDOW), F32),
        compiler_params=_params(2),
        name="bias_table_t",
    )(rel_bias.astype(F32), sinks.astype(F32), jnp.asarray(bucket.T.copy()), jnp.asarray(valid.T.copy()))


def _front_body(x_ref, mod_ref, st_ref, wm_ref, wba_ref, wconv_ref, gpar_ref,
                cs_ref, z_ref, gb_ref, qb_ref, k_ref, v_ref, gates_ref, cst_ref, cbuf, *, bt, tt):
    m = bt * tt
    sh1 = mod_ref[:, :, 0:D_MODEL]
    sc1 = mod_ref[:, :, D_MODEL:2 * D_MODEL]
    h = (x_ref[...] * (1.0 + sc1) + sh1).reshape(m, D_MODEL).astype(BF16)

    @pl.when(pl.program_id(1) == 0)
    def _():
        cbuf[:, 0:8, :] = jnp.zeros((bt, 8, CONV_DIM), F32)
        cbuf[:, 5:8, :] = st_ref[...]

    def proj(off, n):
        return _dot(h, wm_ref[:, off:off + n])

    half = D_MODEL // 2

    def conv_piece(i):
        cols = slice(i * half, (i + 1) * half)
        cbuf[:, 8:8 + tt, cols] = proj(i * half, half).reshape(bt, tt, half)
        u = cbuf[:, :, cols]
        u1 = pltpu.roll(u, 1, 1)
        w0, w1, w2, w3 = (wconv_ref[j:j + 1, cols] for j in range(CONV_W))
        y = (u * w3 + u1 * w2) + pltpu.roll(u * w1 + u1 * w0, 2, 1)
        cs_ref[:, :, cols] = _silu(y[:, 8:, :]).astype(cs_ref.dtype)

    def z_piece(i):
        z_ref[:, :, i * half:(i + 1) * half] = proj(OFF_Z + i * half, half).astype(z_ref.dtype).reshape(bt, tt, half)

    def qb_piece(i):
        q = proj(OFF_QB + i * half, half) * (HD_B ** -0.5)
        qb_ref[:, :, i * half:(i + 1) * half] = q.astype(BF16).reshape(bt, tt, half)

    def kv_piece(_):
        kv = proj(OFF_KVB, 2 * KV_B)
        k_ref[...] = kv[:, :KV_B].reshape(bt, tt, KV_B)
        v_ref[...] = kv[:, KV_B:].reshape(bt, tt, KV_B)

    def gates_piece(i):
        g = _sigmoid(proj(OFF_GATES + i * half, half))
        gates_ref[:, :, i * half:(i + 1) * half] = g.astype(gates_ref.dtype).reshape(bt, tt, half)

    for piece, i in [(conv_piece, 0), (z_piece, 0), (conv_piece, 1), (z_piece, 1), (conv_piece, 2), (qb_piece, 0),
                     (conv_piece, 3), (qb_piece, 1), (conv_piece, 4), (kv_piece, 0), (gates_piece, 0),
                     (conv_piece, 5), (gates_piece, 1), (gates_piece, 2), (gates_piece, 3)]:
        piece(i)
    tail = cbuf[:, tt + 5:tt + 8, :]
    cst_ref[...] = tail
    cbuf[:, 5:8, :] = tail

    ba = _dot(h, wba_ref[...])
    xg = ba + gpar_ref[1:2, :]
    softplus = jnp.maximum(xg, 0.0) + jnp.log(1.0 + jnp.exp(-jnp.abs(xg)))
    lane = lax.broadcasted_iota(jnp.int32, ba.shape, 1)
    gb = jnp.where(lane < N_HEADS_A, _sigmoid(ba), gpar_ref[0:1, :] * softplus)
    gb_ref[...] = gb.reshape(bt, tt, GB_LANES)


def _front(x, mod, conv_state, wm, wba, wconv, gpar, *, layer, bt, tt):
    b, l, _ = x.shape
    grid = (b // bt, l // tt)
    tok = lambda n, dt=F32: jax.ShapeDtypeStruct((b, l, n), dt)
    act = BF16 if tt % 16 == 0 else F32
    tok_spec = lambda n: pl.BlockSpec((bt, tt, n), lambda i, t: (i, t, 0))
    return pl.pallas_call(
        functools.partial(_front_body, bt=bt, tt=tt),
        grid=grid,
        in_specs=[tok_spec(D_MODEL),
                  pl.BlockSpec((bt, 1, 6 * D_MODEL), lambda i, t: (i, 0, 0)),
                  pl.BlockSpec((bt, CONV_W - 1, CONV_DIM), lambda i, t: (i, 0, 0)),
                  _layer_spec((D_MODEL, N_MAIN), layer),
                  _layer_spec((D_MODEL, GB_LANES), layer),
                  _const_spec((CONV_W, CONV_DIM)),
                  _const_spec((2, GB_LANES))],
        out_specs=[tok_spec(CONV_DIM), tok_spec(D_MODEL), tok_spec(GB_LANES), tok_spec(D_MODEL),
                   tok_spec(KV_B), tok_spec(KV_B), tok_spec(2 * D_MODEL),
                   pl.BlockSpec((bt, CONV_W - 1, CONV_DIM), lambda i, t: (i, 0, 0))],
        out_shape=[tok(CONV_DIM, act), tok(D_MODEL, act), tok(GB_LANES), tok(D_MODEL, BF16),
                   tok(KV_B), tok(KV_B), tok(2 * D_MODEL, act),
                   jax.ShapeDtypeStruct((b, CONV_W - 1, CONV_DIM), F32)],
        scratch_shapes=[pltpu.VMEM((bt, tt + 8, CONV_DIM), F32)],
        compiler_params=_params(2),
        name="front",
    )(x, mod, conv_state, wm, wba, wconv, gpar)


def _delta_masks(r, block):
    i = lax.broadcasted_iota(jnp.int32, (r, r), 0)
    j = lax.broadcasted_iota(jnp.int32, (r, r), 1)
    shift = int(math.log2(block))
    same = (i >> shift) == (j >> shift)
    levels = [((i >> (s + 1)) == (j >> (s + 1))) & ((i >> s) != (j >> s)) for s in range(shift)]
    return dict(tri=same & (i >= j), strict=same & (i > j), upper=same & (i < j), eye=(i == j).astype(F32),
                levels=levels)


def _split_dot(a01, x):
    a01 = a01.astype(BF16)
    hi = x.astype(BF16)
    rest = x - hi.astype(F32)
    mid = rest.astype(BF16)
    lo = (rest - mid.astype(F32)).astype(BF16)
    return _dot(a01, hi) + _dot(a01, mid) + _dot(a01, lo)


def _decay_sums(g, masks):
    return _split_dot(masks["tri"], g), _split_dot(masks["upper"], g)


def _delta_pre(problems, masks):
    tri, strict, eye, levels = masks["tri"], masks["strict"], masks["eye"], masks["levels"]
    n = len(problems)
    r = problems[0][0].shape[0]
    lhs, ks, rhs, decays, qgs, kds = [], [], [], [], [], []
    for qr, kr, v, beta_b, gc, rev, gc_row in problems:
        q = qr * lax.rsqrt(jnp.sum(qr * qr, -1, keepdims=True) + 1e-6) * (DK_A ** -0.5)
        k = kr * lax.rsqrt(jnp.sum(kr * kr, -1, keepdims=True) + 1e-6)
        gc_col = gc[:, :r] if r <= DK_A else jnp.concatenate([gc] * (r // DK_A), axis=1)
        decays.append(jnp.exp(jnp.where(tri, gc_col - gc_row, -jnp.inf)))
        eg = jnp.exp(gc)
        kb = k * beta_b
        lhs.append(jnp.concatenate([kb, q], axis=0).astype(BF16))
        ks.append(k.astype(BF16))
        rhs.append(jnp.concatenate([v * beta_b, kb * eg], axis=1).astype(BF16))
        qgs.append((q * eg).astype(BF16))
        kds.append((k * jnp.exp(rev)).astype(BF16))
    kk = [_dot_nt(lhs[i], ks[i]) for i in range(n)]
    mm = [jnp.where(strict, kk[i][:r] * decays[i], 0.0) for i in range(n)]
    qk = [(kk[i][r:] * decays[i]).astype(BF16) for i in range(n)]
    t = [eye - jnp.where(levels[0], mm[i], 0.0) for i in range(n)]
    for s, lvl in enumerate(levels[1:], start=1):
        blk = 2 ** s
        tb = [t[i].astype(BF16) for i in range(n)]
        mo = [jnp.where(lvl, mm[i], 0.0).astype(BF16) for i in range(n)]
        if blk % 8:
            x = [_dot(tb[i], mo[i]).astype(BF16) for i in range(n)]
            y = [_dot(x[i], tb[i]) for i in range(n)]
            t = [t[i] - y[i] for i in range(n)]
        else:
            split = [t[i].reshape(r // (2 * blk), 2, blk, r) for i in range(n)]
            lo = [split[i][:, 1].reshape(r // 2, r) for i in range(n)]
            x = [_dot(lo[i].astype(BF16), mo[i]).astype(BF16) for i in range(n)]
            y = [_dot(x[i], tb[i]) for i in range(n)]
            t = [jnp.stack([split[i][:, 0], (lo[i] - y[i]).reshape(r // (2 * blk), blk, r)], axis=1).reshape(r, r)
                 for i in range(n)]
    uw = [_dot(t[i].astype(BF16), rhs[i]) for i in range(n)]
    return [(uw[i][:, :DK_A], uw[i][:, DK_A:].astype(BF16), qgs[i], kds[i], qk[i]) for i in range(n)]


def _gated_norm(o, z, w_onorm):
    o = o * lax.rsqrt(jnp.mean(o * o, -1, keepdims=True) + RMS_EPS) * w_onorm
    return o * _silu(z)


def _delta_prompt_body(cs_ref, z_ref, gb_ref, wn_ref, o_ref, s_out_ref,
                       s_ref, u_s, o_s, wq_s, qkd_s, egl_s, *, tt, side_work=None):
    n_c = tt // CHUNK
    masks = _delta_masks(tt, CHUNK)

    @pl.when(pl.program_id(1) == 0)
    def _():
        s_ref[...] = jnp.zeros(s_ref.shape, F32)

    finish_side_work = side_work() if side_work is not None else None

    gbv = gb_ref[...]
    gc_all, rev_all = _decay_sums(gbv, masks)
    gc_t = gc_all.T
    heads = range(N_HEADS_A)
    hcols = [slice(hd * DK_A, (hd + 1) * DK_A) for hd in heads]
    problems = []
    for hd in heads:
        lg = N_HEADS_A + hd
        problems.append((cs_ref[:, hcols[hd]].astype(F32),
                         cs_ref[:, D_MODEL + hd * DK_A:D_MODEL + (hd + 1) * DK_A].astype(F32),
                         cs_ref[:, 2 * D_MODEL + hd * DK_A:2 * D_MODEL + (hd + 1) * DK_A].astype(F32),
                         jnp.broadcast_to(gbv[:, hd:hd + 1], (tt, DK_A)),
                         jnp.broadcast_to(gc_all[:, lg:lg + 1], (tt, DK_A)),
                         jnp.broadcast_to(rev_all[:, lg:lg + 1], (tt, DK_A)),
                         jnp.broadcast_to(gc_t[lg:lg + 1, :], (tt, tt))))
    for hd, (u, w, qg, kd, qk) in enumerate(_delta_pre(problems, masks)):
        u_s[:, hcols[hd]] = u
        kdt = kd.T
        for c in range(n_c):
            blk = slice(c * CHUNK, (c + 1) * CHUNK)
            idx = hd * n_c + c
            wq_s[idx, 0:CHUNK, :] = w[blk]
            wq_s[idx, CHUNK:2 * CHUNK, :] = qg[blk]
            qkd_s[idx, 0:CHUNK, :] = qk[blk, blk]
            qkd_s[idx, CHUNK:CHUNK + DK_A, :] = kdt[:, blk]
            last = (c + 1) * CHUNK - 1
            egl_s[idx] = jnp.broadcast_to(jnp.exp(problems[hd][4][last:last + 1, :]), (8, DK_A))

    if finish_side_work is not None:
        finish_side_work()

    states = [s_ref[hd] for hd in heads]
    for c in range(n_c):
        blk = slice(c * CHUNK, (c + 1) * CHUNK)
        idx = [hd * n_c + c for hd in heads]
        ws = [_dot(wq_s[idx[hd]], states[hd].astype(BF16)) for hd in heads]
        v_new = [(u_s[blk, hcols[hd]] - ws[hd][:CHUNK]).astype(BF16) for hd in heads]
        upd = [_dot(qkd_s[idx[hd]], v_new[hd]) for hd in heads]
        for hd in heads:
            o_s[blk, hcols[hd]] = ws[hd][CHUNK:] + upd[hd][:CHUNK]
            states[hd] = states[hd] * egl_s[idx[hd]][0:1, :] + upd[hd][CHUNK:]
    for hd in heads:
        s_ref[hd] = states[hd]
        o_ref[:, hcols[hd]] = _gated_norm(o_s[:, hcols[hd]], z_ref[:, hcols[hd]].astype(F32), wn_ref[...]).astype(BF16)

    @pl.when(pl.program_id(1) == pl.num_programs(1) - 1)
    def _():
        s_out_ref[...] = s_ref[...]


def _delta_prompt(cs, z, gb, w_onorm, *, tt):
    b, l, _ = cs.shape
    n_c = tt // CHUNK
    tok_spec = lambda n: pl.BlockSpec((None, tt, n), lambda i, t: (i, t, 0))
    return pl.pallas_call(
        functools.partial(_delta_prompt_body, tt=tt),
        grid=(b, l // tt),
        in_specs=[tok_spec(CONV_DIM), tok_spec(D_MODEL), tok_spec(GB_LANES), _const_spec((1, DK_A))],
        out_specs=[tok_spec(D_MODEL),
                   pl.BlockSpec((None, N_HEADS_A, DK_A, DK_A), lambda i, t: (i, 0, 0, 0))],
        out_shape=[jax.ShapeDtypeStruct((b, l, D_MODEL), BF16),
                   jax.ShapeDtypeStruct((b, N_HEADS_A, DK_A, DK_A), F32)],
        scratch_shapes=[pltpu.VMEM((N_HEADS_A, DK_A, DK_A), F32),
                        pltpu.VMEM((tt, D_MODEL), F32), pltpu.VMEM((tt, D_MODEL), F32),
                        pltpu.VMEM((N_HEADS_A * n_c, 2 * CHUNK, DK_A), BF16),
                        pltpu.VMEM((N_HEADS_A * n_c, CHUNK + DK_A, CHUNK), BF16),
                        pltpu.VMEM((N_HEADS_A * n_c, 8, DK_A), F32)],
        compiler_params=_params(2),
        name="delta_prompt",
    )(cs, z, gb, w_onorm)


def _delta_sample_body(cs_ref, z_ref, gb_ref, wn_ref, s0_ref, *rest, bt, nb, ls):
    o_ref, s_out_ref = rest[-2:]
    r = N_HEADS_A * ls
    masks = _delta_masks(r, ls)

    heads = range(N_HEADS_A)
    hrows = [slice(hd * ls, (hd + 1) * ls) for hd in heads]
    hcols = [slice(hd * DK_A, (hd + 1) * DK_A) for hd in heads]

    def per_group(gi, carry):
        bis = [gi * nb + d for d in range(nb)]
        problems, egl = [], []
        for bi in bis:
            cs = cs_ref[bi].astype(F32)
            gbv = gb_ref[bi]
            stack = lambda base: jnp.concatenate([cs[:, base + hd * DK_A: base + (hd + 1) * DK_A] for hd in heads],
                                                 axis=0)
            beta_b = jnp.concatenate([jnp.broadcast_to(gbv[:, hd:hd + 1], (ls, DK_A)) for hd in heads], axis=0)
            g_b = jnp.concatenate(
                [jnp.broadcast_to(gbv[:, N_HEADS_A + hd:N_HEADS_A + hd + 1], (ls, DK_A)) for hd in heads], axis=0)
            gc, rev = _decay_sums(g_b, masks)
            egl.append(jnp.exp(gc))
            problems.append((stack(0), stack(D_MODEL), stack(2 * D_MODEL), beta_b, gc, rev, gc.T[:r, :]))
        pre = _delta_pre(problems, masks)
        both = [[_dot(jnp.concatenate([pre[d][1][hrows[hd]], pre[d][2][hrows[hd]]], axis=0),
                      s0_ref[bis[d], hd].astype(BF16)) for hd in heads] for d in range(nb)]
        v_new = [(pre[d][0] - jnp.concatenate([both[d][hd][:ls] for hd in heads], axis=0)).astype(BF16)
                 for d in range(nb)]
        qkv = [_dot(pre[d][4], v_new[d]) for d in range(nb)]
        upd = [[_dot_tn(pre[d][3][hrows[hd]], v_new[d][hrows[hd]]) for hd in heads] for d in range(nb)]
        for d, bi in enumerate(bis):
            zb = z_ref[bi].astype(F32)
            for hd in heads:
                last = (hd + 1) * ls - 1
                s_out_ref[bi, hd] = s0_ref[bi, hd] * egl[d][last:last + 1, :] + upd[d][hd]
                o = both[d][hd][ls:] + qkv[d][hrows[hd]]
                o_ref[bi, :, hcols[hd]] = _gated_norm(o, zb[:, hcols[hd]], wn_ref[...]).astype(BF16)
        return carry

    lax.fori_loop(0, bt // nb, per_group, 0)


def _fill_layer(prev):
    if prev is None:
        return [], []
    return [prev], [pl.BlockSpec(memory_space=pl.ANY)]


def _delta_sample(cs, z, gb, w_onorm, s0_all, s_prev, *, layer, bt):
    b, ls, _ = cs.shape
    tok_spec = lambda n: pl.BlockSpec((bt, ls, n), lambda i: (i, 0, 0))
    st_spec = pl.BlockSpec((None, bt, N_HEADS_A, DK_A, DK_A), lambda i: (layer, i, 0, 0, 0))
    extra, extra_specs = _fill_layer(s_prev)
    return pl.pallas_call(
        functools.partial(_delta_sample_body, bt=bt, nb=min(bt, 4), ls=ls),
        grid=(b // bt,),
        in_specs=[tok_spec(CONV_DIM), tok_spec(D_MODEL), tok_spec(GB_LANES), _const_spec((1, DK_A)), st_spec]
        + extra_specs,
        out_specs=[tok_spec(D_MODEL), st_spec],
        out_shape=[jax.ShapeDtypeStruct((b, ls, D_MODEL), BF16), jax.ShapeDtypeStruct(s0_all.shape, F32)],
        input_output_aliases={5: 1} if extra else {},
        compiler_params=_params(1),
        name="delta_sample",
    )(cs, z, gb, w_onorm, s0_all, *extra)


def _softmax_keys(s):
    p = jnp.exp(s - jnp.max(s, axis=0, keepdims=True))
    return (p / jnp.sum(p, axis=0, keepdims=True)).astype(BF16)


def _zero_key0(x):
    return jnp.where(lax.broadcasted_iota(jnp.int32, x.shape, 0) == 0, 0.0, x)


def _kv_cols(kvh):
    return slice(kvh * HD_B, (kvh + 1) * HD_B)


def _group_queries(q, kvh):
    return jnp.concatenate([q[:, hh * HD_B:(hh + 1) * HD_B] for hh in range(kvh * GROUP_B, (kvh + 1) * GROUP_B)],
                           axis=0)


def _swa_prompt_body(q_ref, kp_ref, kc_ref, vp_ref, vc_ref, bias_ref, o_ref, ko_ref, vo_ref, *, nq):
    _swa_prompt_cache(kc_ref, vc_ref, ko_ref, vo_ref, nq)
    _swa_prompt_start(q_ref, kp_ref, kc_ref, vp_ref, vc_ref, bias_ref, o_ref, nq)()


def _swa_prompt_cache(kc_ref, vc_ref, ko_ref, vo_ref, nq):
    @pl.when(pl.program_id(1) == pl.num_programs(1) - 1)
    def _():
        ko_ref[...] = kc_ref[(nq - 1) * WINDOW:, :]
        vo_ref[...] = vc_ref[(nq - 1) * WINDOW:, :]


def _swa_prompt_start(q_ref, kp_ref, kc_ref, vp_ref, vc_ref, bias_ref, o_ref, nq):
    n_q = GROUP_B * WINDOW
    key = lax.broadcasted_iota(jnp.int32, (2 * WINDOW, n_q), 0)
    no_prev = (pl.program_id(1) == 0) & (key >= 1) & (key < WINDOW)
    kall = jnp.concatenate([kp_ref[...], kc_ref[...]], axis=0)
    vall = jnp.concatenate([vp_ref[...], vc_ref[...]], axis=0)
    probs = [(j, kvh) for j in range(nq) for kvh in range(N_KV_B)]
    k2 = [_zero_key0(kall[j * WINDOW:(j + 2) * WINDOW]).astype(BF16) for j in range(nq)]
    v2t = [_zero_key0(vall[j * WINDOW:(j + 2) * WINDOW]).T.astype(BF16) for j in range(nq)]
    q = [q_ref[j * WINDOW:(j + 1) * WINDOW, :] for j in range(nq)]
    s = [_dot_nt(k2[j][:, _kv_cols(kvh)], _group_queries(q[j], kvh)) for j, kvh in probs]
    pn = []
    for (j, kvh), sc in zip(probs, s):
        sc = sc + bias_ref[kvh]
        pn.append(_softmax_keys(jnp.where(no_prev, -jnp.inf, sc) if j == 0 else sc))

    def finish():
        ot = [_dot(v2t[j][_kv_cols(kvh), :], p) for (j, kvh), p in zip(probs, pn)]
        for (j, kvh), o in zip(probs, ot):
            for g in range(GROUP_B):
                hh = kvh * GROUP_B + g
                o_ref[j * WINDOW:(j + 1) * WINDOW, hh * HD_B:(hh + 1) * HD_B] = \
                    o[:, g * WINDOW:(g + 1) * WINDOW].T.astype(BF16)

    return finish


def _mixer_prompt_body(cs_ref, z_ref, gb_ref, wn_ref, q_ref, kp_ref, kc_ref, vp_ref, vc_ref, bias_ref,
                       oa_ref, s_out_ref, ob_ref, ko_ref, vo_ref, *scratch, tt):
    nq = tt // WINDOW
    _swa_prompt_cache(kc_ref, vc_ref, ko_ref, vo_ref, nq)
    attention = functools.partial(_swa_prompt_start, q_ref, kp_ref, kc_ref, vp_ref, vc_ref, bias_ref, ob_ref, nq)
    _delta_prompt_body(cs_ref, z_ref, gb_ref, wn_ref, oa_ref, s_out_ref, *scratch, tt=tt, side_work=attention)


def _mixer_prompt(cs, z, gb, w_onorm, qb, k, v, bias_t, *, tt):
    b, l, _ = cs.shape
    n_c = tt // CHUNK
    nq = tt // WINDOW
    tok_spec = lambda n: pl.BlockSpec((None, tt, n), lambda i, t: (i, t, 0))
    prev = lambda n: pl.BlockSpec((None, WINDOW, n), lambda i, t: (i, jnp.maximum(nq * t - 1, 0), 0))
    last = lambda n: pl.BlockSpec((None, WINDOW, n), lambda i, t: (i, 0, 0))
    return pl.pallas_call(
        functools.partial(_mixer_prompt_body, tt=tt),
        grid=(b, l // tt),
        in_specs=[tok_spec(CONV_DIM), tok_spec(D_MODEL), tok_spec(GB_LANES), _const_spec((1, DK_A)),
                  tok_spec(D_MODEL), prev(KV_B), tok_spec(KV_B), prev(KV_B), tok_spec(KV_B),
                  _const_spec((N_KV_B, 2 * WINDOW, GROUP_B * WINDOW))],
        out_specs=[tok_spec(D_MODEL), pl.BlockSpec((None, N_HEADS_A, DK_A, DK_A), lambda i, t: (i, 0, 0, 0)),
                   tok_spec(D_MODEL), last(KV_B), last(KV_B)],
        out_shape=[jax.ShapeDtypeStruct((b, l, D_MODEL), BF16),
                   jax.ShapeDtypeStruct((b, N_HEADS_A, DK_A, DK_A), F32),
                   jax.ShapeDtypeStruct((b, l, D_MODEL), BF16),
                   jax.ShapeDtypeStruct((b, WINDOW, KV_B), F32), jax.ShapeDtypeStruct((b, WINDOW, KV_B), F32)],
        scratch_shapes=[pltpu.VMEM((N_HEADS_A, DK_A, DK_A), F32),
                        pltpu.VMEM((tt, D_MODEL), F32), pltpu.VMEM((tt, D_MODEL), F32),
                        pltpu.VMEM((N_HEADS_A * n_c, 2 * CHUNK, DK_A), BF16),
                        pltpu.VMEM((N_HEADS_A * n_c, CHUNK + DK_A, CHUNK), BF16),
                        pltpu.VMEM((N_HEADS_A * n_c, 8, DK_A), F32)],
        compiler_params=_params(2),
        name="mixer_prompt",
    )(cs, z, gb, w_onorm, qb, k, k, v, v, bias_t)


def _swa_prompt(qb, k, v, bias_t):
    b, l, _ = qb.shape
    nq = 2 if l % (2 * WINDOW) == 0 else 1
    cur = lambda n: pl.BlockSpec((None, nq * WINDOW, n), lambda i, t: (i, t, 0))
    prev = lambda n: pl.BlockSpec((None, WINDOW, n), lambda i, t: (i, jnp.maximum(nq * t - 1, 0), 0))
    last = lambda n: pl.BlockSpec((None, WINDOW, n), lambda i, t: (i, 0, 0))
    return pl.pallas_call(
        functools.partial(_swa_prompt_body, nq=nq),
        grid=(b, l // (nq * WINDOW)),
        in_specs=[cur(D_MODEL), prev(KV_B), cur(KV_B), prev(KV_B), cur(KV_B),
                  _const_spec((N_KV_B, 2 * WINDOW, GROUP_B * WINDOW))],
        out_specs=[cur(D_MODEL), last(KV_B), last(KV_B)],
        out_shape=[jax.ShapeDtypeStruct((b, l, D_MODEL), BF16),
                   jax.ShapeDtypeStruct((b, WINDOW, KV_B), F32), jax.ShapeDtypeStruct((b, WINDOW, KV_B), F32)],
        compiler_params=_params(2),
        name="swa_prompt",
    )(qb, k, k, v, v, bias_t)


def _swa_sample_body(q_ref, kn_ref, vn_ref, kc_ref, vc_ref, bias_ref, *rest, bt, nb, ls):
    o_ref, ko_ref, vo_ref = rest[-3:]
    kvs = range(N_KV_B)

    def per_group(gi, carry):
        bis = [gi * nb + d for d in range(nb)]
        ks, vs, s = [], [], []
        for bi in bis:
            kf = jnp.concatenate([kc_ref[bi], kn_ref[bi]], axis=0)
            vf = jnp.concatenate([vc_ref[bi], vn_ref[bi]], axis=0)
            ko_ref[bi] = kf[ls:, :]
            vo_ref[bi] = vf[ls:, :]
            ks.append(_zero_key0(kf).astype(BF16))
            vs.append(_zero_key0(vf).astype(BF16))
        for d, bi in enumerate(bis):
            q = q_ref[bi]
            s.append([_dot_nt(ks[d][:, _kv_cols(kvh)], _group_queries(q, kvh)) for kvh in kvs])
        pn = [[_softmax_keys(s[d][kvh] + bias_ref[kvh]) for kvh in kvs] for d in range(nb)]
        o = [[_dot_tn(pn[d][kvh], vs[d][:, _kv_cols(kvh)]) for kvh in kvs] for d in range(nb)]
        for d, bi in enumerate(bis):
            for kvh in kvs:
                for g in range(GROUP_B):
                    hh = kvh * GROUP_B + g
                    o_ref[bi, :, hh * HD_B:(hh + 1) * HD_B] = o[d][kvh][g * ls:(g + 1) * ls].astype(BF16)
        return carry

    lax.fori_loop(0, bt // nb, per_group, 0)


def _swa_sample(qb, k_new, v_new, k_cache_all, v_cache_all, bias_s, k_prev, v_prev, *, layer, bt):
    b, ls, _ = qb.shape
    n_keys = WINDOW + ls
    new = lambda n: pl.BlockSpec((bt, ls, n), lambda i: (i, 0, 0))
    cache = pl.BlockSpec((None, bt, WINDOW, KV_B), lambda i: (layer, i, 0, 0))
    extra_k, specs_k = _fill_layer(k_prev)
    extra_v, specs_v = _fill_layer(v_prev)
    return pl.pallas_call(
        functools.partial(_swa_sample_body, bt=bt, nb=min(bt, 4), ls=ls),
        grid=(b // bt,),
        in_specs=[new(D_MODEL), new(KV_B), new(KV_B), cache, cache,
                  _const_spec((N_KV_B, n_keys, GROUP_B * ls))] + specs_k + specs_v,
        out_specs=[new(D_MODEL), cache, cache],
        out_shape=[jax.ShapeDtypeStruct((b, ls, D_MODEL), BF16),
                   jax.ShapeDtypeStruct(k_cache_all.shape, F32), jax.ShapeDtypeStruct(v_cache_all.shape, F32)],
        input_output_aliases={6: 1, 7: 2} if extra_k else {},
        compiler_params=_params(1),
        name="swa_sample",
    )(qb, k_new, v_new, k_cache_all, v_cache_all, bias_s, *extra_k, *extra_v)


def _layer_norm(y, g, b):
    mu = jnp.mean(y, -1, keepdims=True)
    d = y - mu
    var = jnp.mean(d * d, -1, keepdims=True)
    return d * lax.rsqrt(var + LN_EPS) * g + b


def _tail_body(x_ref, oa_ref, ob_ref, gates_ref, mod_ref, wpa_ref, wpb_ref, wout_ref, ln1g_ref, ln1b_ref,
               wup_ref, wdn_ref, ln2g_ref, ln2b_ref, o_ref, *, bt, tt, alpha):
    m = bt * tt
    mod = lambda i: mod_ref[:, :, i * D_MODEL:(i + 1) * D_MODEL]
    ga = gates_ref[:, :, 0:D_MODEL].astype(F32).reshape(m, D_MODEL)
    gb = gates_ref[:, :, D_MODEL:2 * D_MODEL].astype(F32).reshape(m, D_MODEL)
    mixed = ga * _dot(oa_ref[...].reshape(m, D_MODEL), wpa_ref[...]) \
        + gb * _dot(ob_ref[...].reshape(m, D_MODEL), wpb_ref[...])
    attn = _dot(mixed.astype(BF16), wout_ref[...]).reshape(bt, tt, D_MODEL)
    x1 = _layer_norm(alpha * x_ref[...] + mod(2) * attn, ln1g_ref[...], ln1b_ref[...])
    h2 = (x1 * (1.0 + mod(4)) + mod(3)).reshape(m, D_MODEL).astype(BF16)
    ff = jnp.zeros((m, D_MODEL), F32)
    for c in range(D_FF // D_MODEL):
        cols = slice(c * D_MODEL, (c + 1) * D_MODEL)
        a = jnp.maximum(_dot(h2, wup_ref[:, cols]), 0.0)
        ff = ff + _dot((a * a).astype(BF16), wdn_ref[cols, :])
    o_ref[...] = _layer_norm(alpha * x1 + mod(5) * ff.reshape(bt, tt, D_MODEL), ln2g_ref[...], ln2b_ref[...])


def _tail(x, oa, ob, gates, mod, w, p, *, layer, bt, tt, alpha):
    b, l, _ = x.shape
    tok_spec = lambda n: pl.BlockSpec((bt, tt, n), lambda i, t: (i, t, 0))
    row = _const_spec((1, D_MODEL))
    sq = _layer_spec((D_MODEL, D_MODEL), layer)
    return pl.pallas_call(
        functools.partial(_tail_body, bt=bt, tt=tt, alpha=alpha),
        grid=(b // bt, l // tt),
        in_specs=[tok_spec(D_MODEL), tok_spec(D_MODEL), tok_spec(D_MODEL), tok_spec(2 * D_MODEL),
                  pl.BlockSpec((bt, 1, 6 * D_MODEL), lambda i, t: (i, 0, 0)),
                  sq, sq, sq, row, row,
                  _layer_spec((D_MODEL, D_FF), layer), _layer_spec((D_FF, D_MODEL), layer), row, row],
        out_specs=tok_spec(D_MODEL),
        out_shape=jax.ShapeDtypeStruct((b, l, D_MODEL), F32),
        compiler_params=_params(2),
        name="tail",
    )(x, oa, ob, gates, mod, w["wpa"], w["wpb"], w["wout"], p["ln1_g"], p["ln1_b"],
      w["wup"], w["wdn"], p["ln2_g"], p["ln2_b"])


def _tiles(b, l, rows):
    tt = min(l, rows)
    bt = max(1, min(b, rows // tt))
    return bt, tt


def _layer_params(l, w_conv, a_log, dt_bias, w_onorm, ln1_g, ln1_b, ln2_g, ln2_b):
    gpar = jnp.zeros((2, GB_LANES), F32)
    gpar = gpar.at[0, N_HEADS_A:2 * N_HEADS_A].set(-jnp.exp(a_log[l].astype(F32)))
    gpar = gpar.at[1, N_HEADS_A:2 * N_HEADS_A].set(dt_bias[l].astype(F32))
    row = lambda a: a[l].reshape(1, -1).astype(F32)
    return dict(wconv=w_conv[l].astype(F32), gpar=gpar, w_onorm=row(w_onorm), ln1_g=row(ln1_g), ln1_b=row(ln1_b),
                ln2_g=row(ln2_g), ln2_b=row(ln2_b))


def _trunk_layer(x, mod, layer, w, p, alpha, conv_state, sample_state):
    b, l, _ = x.shape
    bt, tt = _tiles(b, l, 512 if l % 512 == 0 else 256)
    cs, z, gb, qb, k, v, gates, conv_out = _front(x, mod, conv_state, w["wm"], w["wba"], p["wconv"], p["gpar"],
                                                  layer=layer, bt=bt, tt=tt)
    if sample_state is None:
        oa, s_new, ob, k_new, v_new = _mixer_prompt(cs, z, gb, p["w_onorm"], qb, k, v, p["bias_t"], tt=min(l, 256))
    else:
        s0_all, kc_all, vc_all, s_prev, k_prev, v_prev = sample_state
        oa, s_new = _delta_sample(cs, z, gb, p["w_onorm"], s0_all, s_prev, layer=layer, bt=min(b, 8))
        bias_s = p["bias_t"][:, :WINDOW + l, :].reshape(N_KV_B, WINDOW + l, GROUP_B, WINDOW)[..., :l]
        bias_s = bias_s.reshape(N_KV_B, WINDOW + l, GROUP_B * l)
        ob, k_new, v_new = _swa_sample(qb, k, v, kc_all, vc_all, bias_s, k_prev, v_prev, layer=layer, bt=min(b, 8))
    bt, tt = _tiles(b, l, 512)
    x2 = _tail(x, oa, ob, gates, mod, w, p, layer=layer, bt=bt, tt=tt, alpha=alpha)
    return x2, (s_new, conv_out, k_new, v_new)


def kernel(x_prompt, x_sample, state_delta, state_conv, cache_k, cache_v, c_prompt, c_sample, rel_bias, w_ada, b_ada, w_in, w_conv, a_log, dt_bias, w_onorm, sinks, w_pa, w_pb, w_out, ln1_g, ln1_b, w_up, w_down, ln2_g, ln2_b):
    depth = w_in.shape[0]
    alpha = (2 * depth) ** 0.25
    bp = x_prompt.shape[0]
    mod_all = _ada(jnp.concatenate([c_prompt, c_sample], axis=0), w_ada, b_ada)
    bias_t = _bias_table_t(rel_bias, sinks)
    wm, wba = _prep_w_in(w_in)
    w = dict(wm=wm, wba=wba, wpa=_cast_bf16(w_pa), wpb=_cast_bf16(w_pb), wout=_cast_bf16(w_out),
             wup=_cast_bf16(w_up), wdn=_cast_bf16(w_down))
    bs = x_sample.shape[0]
    kc_all = cache_k.reshape(depth, bs, WINDOW, KV_B)
    vc_all = cache_v.reshape(depth, bs, WINDOW, KV_B)
    yp, ys = x_prompt, x_sample
    prompt_outs = [[] for _ in range(4)]
    sample_conv = []
    s_all = k_all = v_all = None
    for l in range(depth):
        p = _layer_params(l, w_conv, a_log, dt_bias, w_onorm, ln1_g, ln1_b, ln2_g, ln2_b)
        p["bias_t"] = bias_t[l]
        mod_p = mod_all[l, :bp][:, None, :]
        mod_s = mod_all[l, bp:][:, None, :]
        zero_conv = jnp.zeros((bp, CONV_W - 1, CONV_DIM), x_prompt.dtype)
        yp, rest_p = _trunk_layer(yp, mod_p, l, w, p, alpha, zero_conv, None)
        ys, (s_all, conv_s, k_all, v_all) = _trunk_layer(ys, mod_s, l, w, p, alpha, state_conv[l],
                                                         (state_delta, kc_all, vc_all, s_all, k_all, v_all))
        for acc, val in zip(prompt_outs, rest_p):
            acc.append(val)
        sample_conv.append(conv_s)
    heads = lambda a: a.reshape(a.shape[:-1] + (N_KV_B, HD_B))
    pd, pc, pk, pv = (jnp.stack(a) for a in prompt_outs)
    return (yp, ys, pd, pc, heads(pk), heads(pv), s_all, jnp.stack(sample_conv), heads(k_all), heads(v_all))
```

```python
import functools
import math

import numpy as np
import jax
import jax.numpy as jnp
from jax import lax
from jax.experimental import pallas as pl
from jax.experimental.pallas import tpu as pltpu

F32 = jnp.float32
BF16 = jnp.bfloat16

D_MODEL = 1024
N_HEADS_A = 8
DK_A = 128
CONV_W = 4
CONV_DIM = 3 * D_MODEL
CHUNK = 64
HD_B = 64
N_HEADS_B = 16
N_KV_B = 4
GROUP_B = N_HEADS_B // N_KV_B
KV_B = N_KV_B * HD_B
WINDOW = 128
N_BUCKETS = 32
MAX_DISTANCE = 128
D_FF = 4 * D_MODEL
LN_EPS = 1e-5
RMS_EPS = 1e-6

OFF_Z = CONV_DIM
OFF_QB = OFF_Z + D_MODEL
OFF_KVB = OFF_QB + D_MODEL
OFF_GATES = OFF_KVB + 2 * KV_B
N_MAIN = OFF_GATES + 2 * D_MODEL
GB_LANES = 128

V7X_VMEM_BYTES = 64 * 1024 * 1024
VMEM_LIMIT = V7X_VMEM_BYTES - 8 * 1024 * 1024


def _params(n_grid):
    return pltpu.CompilerParams(dimension_semantics=("arbitrary",) * n_grid, vmem_limit_bytes=VMEM_LIMIT)


def _dot(a, b):
    return jnp.dot(a, b, preferred_element_type=F32)


def _dot_nt(a, b):
    return lax.dot_general(a, b, (((1,), (1,)), ((), ())), preferred_element_type=F32)


def _dot_tn(a, b):
    return lax.dot_general(a, b, (((0,), (0,)), ((), ())), preferred_element_type=F32)


def _sigmoid(x):
    return 1.0 / (1.0 + jnp.exp(-x))


def _silu(x):
    return x * _sigmoid(x)


def _const_spec(shape):
    nd = len(shape)
    return pl.BlockSpec(shape, lambda *_: (0,) * nd, pipeline_mode=pl.Buffered(1))


def _layer_spec(shape, layer):
    nd = len(shape)
    return pl.BlockSpec((None,) + tuple(shape), lambda *_: (layer,) + (0,) * nd, pipeline_mode=pl.Buffered(1))


def _cast_body(w_ref, o_ref):
    o_ref[...] = w_ref[...].astype(BF16)


def _cast_bf16(w):
    depth, k, n = w.shape
    bk = max(8, min(k, (1024 * 1024) // n))
    spec = pl.BlockSpec((None, bk, n), lambda l, i: (l, i, 0))
    return pl.pallas_call(
        _cast_body, grid=(depth, k // bk), in_specs=[spec], out_specs=spec,
        out_shape=jax.ShapeDtypeStruct(w.shape, BF16), compiler_params=_params(2), name="cast_bf16",
    )(w)


def _prep_w_in(w_in):
    o_ba = CONV_DIM + D_MODEL
    o_qb = o_ba + 2 * N_HEADS_A
    wba = jnp.pad(w_in[:, :, o_ba:o_qb], ((0, 0), (0, 0), (0, GB_LANES - 2 * N_HEADS_A))).astype(BF16)
    return w_in[:, :, :o_ba].astype(BF16), w_in[:, :, o_qb:].astype(BF16), wba


def _ada_body(c_ref, w_ref, b_ref, o_ref):
    s = _silu(c_ref[...]).astype(BF16)
    o_ref[...] = _dot(s, w_ref[...].astype(BF16)) + b_ref[...]


def _ada(c_all, w_ada, b_ada):
    depth = w_ada.shape[0]
    n_rows = c_all.shape[0]
    n_col = w_ada.shape[2] // D_MODEL
    return pl.pallas_call(
        _ada_body,
        grid=(depth, n_col),
        in_specs=[pl.BlockSpec((n_rows, D_MODEL), lambda l, n: (0, 0)),
                  pl.BlockSpec((None, D_MODEL, D_MODEL), lambda l, n: (l, 0, n)),
                  pl.BlockSpec((None, 1, D_MODEL), lambda l, n: (l, 0, n))],
        out_specs=pl.BlockSpec((None, n_rows, D_MODEL), lambda l, n: (l, 0, n)),
        out_shape=jax.ShapeDtypeStruct((depth, n_rows, w_ada.shape[2]), F32),
        compiler_params=_params(2),
        name="ada",
    )(c_all, w_ada, b_ada.reshape(depth, 1, -1))


def _bucket_table():
    r = np.arange(WINDOW)[:, None]
    c = np.arange(2 * WINDOW)[None, :]
    dist = WINDOW + r - c
    n = np.maximum(dist, 0)
    max_exact = N_BUCKETS // 2
    ratio = np.maximum(n, max_exact).astype(np.float32) / np.float32(max_exact)
    large = max_exact + (np.log(ratio) / np.float32(math.log(MAX_DISTANCE / max_exact))
                         * np.float32(N_BUCKETS - max_exact)).astype(np.int32)
    large = np.minimum(large, N_BUCKETS - 1)
    bucket = np.where(n < max_exact, n, large).astype(np.int32)
    valid = ((dist >= 0) & (dist < WINDOW)).astype(np.int32)
    return bucket, valid


def _bias_t_body(rb_ref, sink_ref, bucket_ref, valid_ref, o_ref):
    l = pl.program_id(0)
    h = pl.program_id(1)
    bucket = bucket_ref[...]
    acc = jnp.zeros(bucket.shape, F32)
    for j in range(N_BUCKETS):
        acc = jnp.where(bucket == j, rb_ref[j, h], acc)
    acc = jnp.where(valid_ref[...] > 0, acc, -jnp.inf)
    key = lax.broadcasted_iota(jnp.int32, bucket.shape, 0)
    o_ref[...] = jnp.where(key == 0, sink_ref[l, h], acc)


def _bias_table_t(rel_bias, sinks):
    depth = sinks.shape[0]
    bucket, valid = _bucket_table()
    return pl.pallas_call(
        _bias_t_body,
        grid=(depth, N_HEADS_B),
        in_specs=[pl.BlockSpec(memory_space=pltpu.SMEM), pl.BlockSpec(memory_space=pltpu.SMEM),
                  pl.BlockSpec((2 * WINDOW, WINDOW), lambda l, h: (0, 0)),
                  pl.BlockSpec((2 * WINDOW, WINDOW), lambda l, h: (0, 0))],
        out_specs=pl.BlockSpec((None, None, 2 * WINDOW, WINDOW), lambda l, h: (l, h // GROUP_B, 0, h % GROUP_B)),
        out_shape=jax.ShapeDtypeStruct((depth, N_KV_B, 2 * WINDOW, GROUP_B * WINDOW), F32),
        compiler_params=_params(2),
        name="bias_table_t",
    )(rel_bias.astype(F32), sinks.astype(F32), jnp.asarray(bucket.T.copy()), jnp.asarray(valid.T.copy()))


def _front_body(x_ref, mod_ref, st_ref, wa_ref, wb_ref, wba_ref, wconv_ref, gpar_ref,
                cs_ref, z_ref, gb_ref, qb_ref, k_ref, v_ref, gates_ref, cst_ref, cbuf, *, bt, tt):
    m = bt * tt
    sh1 = mod_ref[:, :, 0:D_MODEL]
    sc1 = mod_ref[:, :, D_MODEL:2 * D_MODEL]
    h = (x_ref[...] * (1.0 + sc1) + sh1).reshape(m, D_MODEL).astype(BF16)

    @pl.when(pl.program_id(1) == 0)
    def _():
        cbuf[:, 0:8, :] = jnp.zeros((bt, 8, CONV_DIM), F32)
        cbuf[:, 5:8, :] = st_ref[...]

    def proj(off, n):
        if off < OFF_QB:
            return _dot(h, wa_ref[:, off:off + n])
        return _dot(h, wb_ref[:, off - OFF_QB:off - OFF_QB + n])

    half = D_MODEL // 2

    def conv_piece(i):
        cols = slice(i * half, (i + 1) * half)
        cbuf[:, 8:8 + tt, cols] = proj(i * half, half).reshape(bt, tt, half)
        u = cbuf[:, :, cols]
        u1 = pltpu.roll(u, 1, 1)
        w0, w1, w2, w3 = (wconv_ref[j:j + 1, cols] for j in range(CONV_W))
        y = (u * w3 + u1 * w2) + pltpu.roll(u * w1 + u1 * w0, 2, 1)
        cs_ref[:, :, cols] = _silu(y[:, 8:, :]).astype(cs_ref.dtype)

    def z_piece(i):
        z_ref[:, :, i * half:(i + 1) * half] = proj(OFF_Z + i * half, half).astype(z_ref.dtype).reshape(bt, tt, half)

    def qb_piece(i):
        q = proj(OFF_QB + i * half, half) * (HD_B ** -0.5)
        qb_ref[:, :, i * half:(i + 1) * half] = q.astype(BF16).reshape(bt, tt, half)

    def kv_piece(_):
        kv = proj(OFF_KVB, 2 * KV_B)
        k_ref[...] = kv[:, :KV_B].reshape(bt, tt, KV_B)
        v_ref[...] = kv[:, KV_B:].reshape(bt, tt, KV_B)

    def gates_piece(i):
        g = _sigmoid(proj(OFF_GATES + i * half, half))
        gates_ref[:, :, i * half:(i + 1) * half] = g.astype(gates_ref.dtype).reshape(bt, tt, half)

    for piece, i in [(conv_piece, 0), (z_piece, 0), (conv_piece, 1), (z_piece, 1), (conv_piece, 2), (qb_piece, 0),
                     (conv_piece, 3), (qb_piece, 1), (conv_piece, 4), (kv_piece, 0), (gates_piece, 0),
                     (conv_piece, 5), (gates_piece, 1), (gates_piece, 2), (gates_piece, 3)]:
        piece(i)
    tail = cbuf[:, tt + 5:tt + 8, :]
    cst_ref[...] = tail
    cbuf[:, 5:8, :] = tail

    ba = _dot(h, wba_ref[...])
    xg = ba + gpar_ref[1:2, :]
    softplus = jnp.maximum(xg, 0.0) + jnp.log(1.0 + jnp.exp(-jnp.abs(xg)))
    lane = lax.broadcasted_iota(jnp.int32, ba.shape, 1)
    gb = jnp.where(lane < N_HEADS_A, _sigmoid(ba), gpar_ref[0:1, :] * softplus)
    gb_ref[...] = gb.reshape(bt, tt, GB_LANES)


def _front(x, mod, conv_state, wa, wb, wba, wconv, gpar, *, layer, bt, tt):
    b, l, _ = x.shape
    grid = (b // bt, l // tt)
    tok = lambda n, dt=F32: jax.ShapeDtypeStruct((b, l, n), dt)
    act = BF16 if tt % 16 == 0 else F32
    tok_spec = lambda n: pl.BlockSpec((bt, tt, n), lambda i, t: (i, t, 0))
    return pl.pallas_call(
        functools.partial(_front_body, bt=bt, tt=tt),
        grid=grid,
        in_specs=[tok_spec(D_MODEL),
                  pl.BlockSpec((bt, 1, 6 * D_MODEL), lambda i, t: (i, 0, 0)),
                  pl.BlockSpec((bt, CONV_W - 1, CONV_DIM), lambda i, t: (i, 0, 0)),
                  _layer_spec((D_MODEL, OFF_QB), layer), _layer_spec((D_MODEL, N_MAIN - OFF_QB), layer),
                  _layer_spec((D_MODEL, GB_LANES), layer),
                  _const_spec((CONV_W, CONV_DIM)),
                  _const_spec((2, GB_LANES))],
        out_specs=[tok_spec(CONV_DIM), tok_spec(D_MODEL), tok_spec(GB_LANES), tok_spec(D_MODEL),
                   tok_spec(KV_B), tok_spec(KV_B), tok_spec(2 * D_MODEL),
                   pl.BlockSpec((bt, CONV_W - 1, CONV_DIM), lambda i, t: (i, 0, 0))],
        out_shape=[tok(CONV_DIM, act), tok(D_MODEL, act), tok(GB_LANES), tok(D_MODEL, BF16),
                   tok(KV_B), tok(KV_B), tok(2 * D_MODEL, act),
                   jax.ShapeDtypeStruct((b, CONV_W - 1, CONV_DIM), F32)],
        scratch_shapes=[pltpu.VMEM((bt, tt + 8, CONV_DIM), F32)],
        compiler_params=_params(2),
        name="front",
    )(x, mod, conv_state, wa, wb, wba, wconv, gpar)


def _delta_masks(r, block):
    i = lax.broadcasted_iota(jnp.int32, (r, r), 0)
    j = lax.broadcasted_iota(jnp.int32, (r, r), 1)
    shift = int(math.log2(block))
    same = (i >> shift) == (j >> shift)
    levels = [((i >> (s + 1)) == (j >> (s + 1))) & ((i >> s) != (j >> s)) for s in range(shift)]
    return dict(tri=same & (i >= j), strict=same & (i > j), upper=same & (i < j), eye=(i == j).astype(F32),
                levels=levels)


def _split_dot(a01, x):
    a01 = a01.astype(BF16)
    hi = x.astype(BF16)
    rest = x - hi.astype(F32)
    mid = rest.astype(BF16)
    lo = (rest - mid.astype(F32)).astype(BF16)
    return _dot(a01, hi) + _dot(a01, mid) + _dot(a01, lo)


def _decay_sums(g, masks):
    return _split_dot(masks["tri"], g), _split_dot(masks["upper"], g)


def _delta_pre(problems, masks):
    tri, strict, eye, levels = masks["tri"], masks["strict"], masks["eye"], masks["levels"]
    n = len(problems)
    r = problems[0][0].shape[0]
    lhs, ks, rhs, decays, qgs, kds = [], [], [], [], [], []
    for qr, kr, v, beta_b, gc, rev, gc_row in problems:
        q = qr * lax.rsqrt(jnp.sum(qr * qr, -1, keepdims=True) + 1e-6) * (DK_A ** -0.5)
        k = kr * lax.rsqrt(jnp.sum(kr * kr, -1, keepdims=True) + 1e-6)
        gc_col = gc[:, :r] if r <= DK_A else jnp.concatenate([gc] * (r // DK_A), axis=1)
        decays.append(jnp.exp(jnp.where(tri, gc_col - gc_row, -jnp.inf)))
        eg = jnp.exp(gc)
        kb = k * beta_b
        lhs.append(jnp.concatenate([kb, q], axis=0).astype(BF16))
        ks.append(k.astype(BF16))
        rhs.append(jnp.concatenate([v * beta_b, kb * eg], axis=1).astype(BF16))
        qgs.append((q * eg).astype(BF16))
        kds.append((k * jnp.exp(rev)).astype(BF16))
    kk = [_dot_nt(lhs[i], ks[i]) for i in range(n)]
    mm = [jnp.where(strict, kk[i][:r] * decays[i], 0.0) for i in range(n)]
    qk = [(kk[i][r:] * decays[i]).astype(BF16) for i in range(n)]
    t = [eye - jnp.where(levels[0], mm[i], 0.0) for i in range(n)]
    for s, lvl in enumerate(levels[1:], start=1):
        blk = 2 ** s
        tb = [t[i].astype(BF16) for i in range(n)]
        mo = [jnp.where(lvl, mm[i], 0.0).astype(BF16) for i in range(n)]
        if blk % 8:
            x = [_dot(tb[i], mo[i]).astype(BF16) for i in range(n)]
            y = [_dot(x[i], tb[i]) for i in range(n)]
            t = [t[i] - y[i] for i in range(n)]
        else:
            split = [t[i].reshape(r // (2 * blk), 2, blk, r) for i in range(n)]
            lo = [split[i][:, 1].reshape(r // 2, r) for i in range(n)]
            x = [_dot(lo[i].astype(BF16), mo[i]).astype(BF16) for i in range(n)]
            y = [_dot(x[i], tb[i]) for i in range(n)]
            t = [jnp.stack([split[i][:, 0], (lo[i] - y[i]).reshape(r // (2 * blk), blk, r)], axis=1).reshape(r, r)
                 for i in range(n)]
    uw = [_dot(t[i].astype(BF16), rhs[i]) for i in range(n)]
    return [(uw[i][:, :DK_A], uw[i][:, DK_A:].astype(BF16), qgs[i], kds[i], qk[i]) for i in range(n)]


def _gated_norm(o, z, w_onorm):
    o = o * lax.rsqrt(jnp.mean(o * o, -1, keepdims=True) + RMS_EPS) * w_onorm
    return o * _silu(z)


def _delta_prompt_body(cs_ref, z_ref, gb_ref, wn_ref, o_ref, s_out_ref,
                       s_ref, u_s, o_s, wq_s, qkd_s, egl_s, *, tt, side_work=None):
    n_c = tt // CHUNK
    masks = _delta_masks(tt, CHUNK)

    @pl.when(pl.program_id(1) == 0)
    def _():
        s_ref[...] = jnp.zeros(s_ref.shape, F32)

    finish_side_work = side_work() if side_work is not None else None

    gbv = gb_ref[...]
    gc_all, rev_all = _decay_sums(gbv, masks)
    gc_t = gc_all.T
    heads = range(N_HEADS_A)
    hcols = [slice(hd * DK_A, (hd + 1) * DK_A) for hd in heads]
    problems = []
    for hd in heads:
        lg = N_HEADS_A + hd
        problems.append((cs_ref[:, hcols[hd]].astype(F32),
                         cs_ref[:, D_MODEL + hd * DK_A:D_MODEL + (hd + 1) * DK_A].astype(F32),
                         cs_ref[:, 2 * D_MODEL + hd * DK_A:2 * D_MODEL + (hd + 1) * DK_A].astype(F32),
                         jnp.broadcast_to(gbv[:, hd:hd + 1], (tt, DK_A)),
                         jnp.broadcast_to(gc_all[:, lg:lg + 1], (tt, DK_A)),
                         jnp.broadcast_to(rev_all[:, lg:lg + 1], (tt, DK_A)),
                         jnp.broadcast_to(gc_t[lg:lg + 1, :], (tt, tt))))
    for hd, (u, w, qg, kd, qk) in enumerate(_delta_pre(problems, masks)):
        u_s[:, hcols[hd]] = u
        kdt = kd.T
        for c in range(n_c):
            blk = slice(c * CHUNK, (c + 1) * CHUNK)
            idx = hd * n_c + c
            wq_s[idx, 0:CHUNK, :] = w[blk]
            wq_s[idx, CHUNK:2 * CHUNK, :] = qg[blk]
            qkd_s[idx, 0:CHUNK, :] = qk[blk, blk]
            qkd_s[idx, CHUNK:CHUNK + DK_A, :] = kdt[:, blk]
            last = (c + 1) * CHUNK - 1
            egl_s[idx] = jnp.broadcast_to(jnp.exp(problems[hd][4][last:last + 1, :]), (8, DK_A))

    if finish_side_work is not None:
        finish_side_work()

    states = [s_ref[hd] for hd in heads]
    for c in range(n_c):
        blk = slice(c * CHUNK, (c + 1) * CHUNK)
        idx = [hd * n_c + c for hd in heads]
        ws = [_dot(wq_s[idx[hd]], states[hd].astype(BF16)) for hd in heads]
        v_new = [(u_s[blk, hcols[hd]] - ws[hd][:CHUNK]).astype(BF16) for hd in heads]
        upd = [_dot(qkd_s[idx[hd]], v_new[hd]) for hd in heads]
        for hd in heads:
            o_s[blk, hcols[hd]] = ws[hd][CHUNK:] + upd[hd][:CHUNK]
            states[hd] = states[hd] * egl_s[idx[hd]][0:1, :] + upd[hd][CHUNK:]
    for hd in heads:
        s_ref[hd] = states[hd]
        o_ref[:, hcols[hd]] = _gated_norm(o_s[:, hcols[hd]], z_ref[:, hcols[hd]].astype(F32), wn_ref[...]).astype(BF16)

    @pl.when(pl.program_id(1) == pl.num_programs(1) - 1)
    def _():
        s_out_ref[...] = s_ref[...]


def _delta_prompt(cs, z, gb, w_onorm, *, tt):
    b, l, _ = cs.shape
    n_c = tt // CHUNK
    tok_spec = lambda n: pl.BlockSpec((None, tt, n), lambda i, t: (i, t, 0))
    return pl.pallas_call(
        functools.partial(_delta_prompt_body, tt=tt),
        grid=(b, l // tt),
        in_specs=[tok_spec(CONV_DIM), tok_spec(D_MODEL), tok_spec(GB_LANES), _const_spec((1, DK_A))],
        out_specs=[tok_spec(D_MODEL),
                   pl.BlockSpec((None, N_HEADS_A, DK_A, DK_A), lambda i, t: (i, 0, 0, 0))],
        out_shape=[jax.ShapeDtypeStruct((b, l, D_MODEL), BF16),
                   jax.ShapeDtypeStruct((b, N_HEADS_A, DK_A, DK_A), F32)],
        scratch_shapes=[pltpu.VMEM((N_HEADS_A, DK_A, DK_A), F32),
                        pltpu.VMEM((tt, D_MODEL), F32), pltpu.VMEM((tt, D_MODEL), F32),
                        pltpu.VMEM((N_HEADS_A * n_c, 2 * CHUNK, DK_A), BF16),
                        pltpu.VMEM((N_HEADS_A * n_c, CHUNK + DK_A, CHUNK), BF16),
                        pltpu.VMEM((N_HEADS_A * n_c, 8, DK_A), F32)],
        compiler_params=_params(2),
        name="delta_prompt",
    )(cs, z, gb, w_onorm)


def _delta_sample_body(cs_ref, z_ref, gb_ref, wn_ref, s0_ref, *rest, bt, nb, ls):
    o_ref, s_out_ref = rest[-2:]
    r = N_HEADS_A * ls
    masks = _delta_masks(r, ls)

    heads = range(N_HEADS_A)
    hrows = [slice(hd * ls, (hd + 1) * ls) for hd in heads]
    hcols = [slice(hd * DK_A, (hd + 1) * DK_A) for hd in heads]

    def per_group(gi, carry):
        bis = [gi * nb + d for d in range(nb)]
        problems, egl = [], []
        for bi in bis:
            cs = cs_ref[bi].astype(F32)
            gbv = gb_ref[bi]
            stack = lambda base: jnp.concatenate([cs[:, base + hd * DK_A: base + (hd + 1) * DK_A] for hd in heads],
                                                 axis=0)
            beta_b = jnp.concatenate([jnp.broadcast_to(gbv[:, hd:hd + 1], (ls, DK_A)) for hd in heads], axis=0)
            g_b = jnp.concatenate(
                [jnp.broadcast_to(gbv[:, N_HEADS_A + hd:N_HEADS_A + hd + 1], (ls, DK_A)) for hd in heads], axis=0)
            gc, rev = _decay_sums(g_b, masks)
            egl.append(jnp.exp(gc))
            problems.append((stack(0), stack(D_MODEL), stack(2 * D_MODEL), beta_b, gc, rev, gc.T[:r, :]))
        pre = _delta_pre(problems, masks)
        both = [[_dot(jnp.concatenate([pre[d][1][hrows[hd]], pre[d][2][hrows[hd]]], axis=0),
                      s0_ref[bis[d], hd].astype(BF16)) for hd in heads] for d in range(nb)]
        v_new = [(pre[d][0] - jnp.concatenate([both[d][hd][:ls] for hd in heads], axis=0)).astype(BF16)
                 for d in range(nb)]
        qkv = [_dot(pre[d][4], v_new[d]) for d in range(nb)]
        upd = [[_dot_tn(pre[d][3][hrows[hd]], v_new[d][hrows[hd]]) for hd in heads] for d in range(nb)]
        for d, bi in enumerate(bis):
            zb = z_ref[bi].astype(F32)
            for hd in heads:
                last = (hd + 1) * ls - 1
                s_out_ref[bi, hd] = s0_ref[bi, hd] * egl[d][last:last + 1, :] + upd[d][hd]
                o = both[d][hd][ls:] + qkv[d][hrows[hd]]
                o_ref[bi, :, hcols[hd]] = _gated_norm(o, zb[:, hcols[hd]], wn_ref[...]).astype(BF16)
        return carry

    lax.fori_loop(0, bt // nb, per_group, 0)


def _fill_layer(prev):
    if prev is None:
        return [], []
    return [prev], [pl.BlockSpec(memory_space=pl.ANY)]


def _delta_sample(cs, z, gb, w_onorm, s0_all, s_prev, *, layer, bt):
    b, ls, _ = cs.shape
    tok_spec = lambda n: pl.BlockSpec((bt, ls, n), lambda i: (i, 0, 0))
    st_spec = pl.BlockSpec((None, bt, N_HEADS_A, DK_A, DK_A), lambda i: (layer, i, 0, 0, 0))
    extra, extra_specs = _fill_layer(s_prev)
    return pl.pallas_call(
        functools.partial(_delta_sample_body, bt=bt, nb=min(bt, 4), ls=ls),
        grid=(b // bt,),
        in_specs=[tok_spec(CONV_DIM), tok_spec(D_MODEL), tok_spec(GB_LANES), _const_spec((1, DK_A)), st_spec]
        + extra_specs,
        out_specs=[tok_spec(D_MODEL), st_spec],
        out_shape=[jax.ShapeDtypeStruct((b, ls, D_MODEL), BF16), jax.ShapeDtypeStruct(s0_all.shape, F32)],
        input_output_aliases={5: 1} if extra else {},
        compiler_params=_params(1),
        name="delta_sample",
    )(cs, z, gb, w_onorm, s0_all, *extra)


def _softmax_keys(s):
    p = jnp.exp(s - jnp.max(s, axis=0, keepdims=True))
    return (p / jnp.sum(p, axis=0, keepdims=True)).astype(BF16)


def _zero_key0(x):
    return jnp.where(lax.broadcasted_iota(jnp.int32, x.shape, 0) == 0, 0.0, x)


def _kv_cols(kvh):
    return slice(kvh * HD_B, (kvh + 1) * HD_B)


def _group_queries(q, kvh):
    return jnp.concatenate([q[:, hh * HD_B:(hh + 1) * HD_B] for hh in range(kvh * GROUP_B, (kvh + 1) * GROUP_B)],
                           axis=0)


def _swa_prompt_body(q_ref, kp_ref, kc_ref, vp_ref, vc_ref, bias_ref, o_ref, ko_ref, vo_ref, *, nq):
    _swa_prompt_cache(kc_ref, vc_ref, ko_ref, vo_ref, nq)
    _swa_prompt_start(q_ref, kp_ref, kc_ref, vp_ref, vc_ref, bias_ref, o_ref, nq)()


def _swa_prompt_cache(kc_ref, vc_ref, ko_ref, vo_ref, nq):
    @pl.when(pl.program_id(1) == pl.num_programs(1) - 1)
    def _():
        ko_ref[...] = kc_ref[(nq - 1) * WINDOW:, :]
        vo_ref[...] = vc_ref[(nq - 1) * WINDOW:, :]


def _swa_prompt_start(q_ref, kp_ref, kc_ref, vp_ref, vc_ref, bias_ref, o_ref, nq):
    n_q = GROUP_B * WINDOW
    key = lax.broadcasted_iota(jnp.int32, (2 * WINDOW, n_q), 0)
    no_prev = (pl.program_id(1) == 0) & (key >= 1) & (key < WINDOW)
    kall = jnp.concatenate([kp_ref[...], kc_ref[...]], axis=0)
    vall = jnp.concatenate([vp_ref[...], vc_ref[...]], axis=0)
    probs = [(j, kvh) for j in range(nq) for kvh in range(N_KV_B)]
    k2 = [_zero_key0(kall[j * WINDOW:(j + 2) * WINDOW]).astype(BF16) for j in range(nq)]
    v2t = [_zero_key0(vall[j * WINDOW:(j + 2) * WINDOW]).T.astype(BF16) for j in range(nq)]
    q = [q_ref[j * WINDOW:(j + 1) * WINDOW, :] for j in range(nq)]
    s = [_dot_nt(k2[j][:, _kv_cols(kvh)], _group_queries(q[j], kvh)) for j, kvh in probs]
    pn = []
    for (j, kvh), sc in zip(probs, s):
        sc = sc + bias_ref[kvh]
        pn.append(_softmax_keys(jnp.where(no_prev, -jnp.inf, sc) if j == 0 else sc))

    def finish():
        ot = [_dot(v2t[j][_kv_cols(kvh), :], p) for (j, kvh), p in zip(probs, pn)]
        for (j, kvh), o in zip(probs, ot):
            for g in range(GROUP_B):
                hh = kvh * GROUP_B + g
                o_ref[j * WINDOW:(j + 1) * WINDOW, hh * HD_B:(hh + 1) * HD_B] = \
                    o[:, g * WINDOW:(g + 1) * WINDOW].T.astype(BF16)

    return finish


def _mixer_prompt_body(cs_ref, z_ref, gb_ref, wn_ref, q_ref, kp_ref, kc_ref, vp_ref, vc_ref, bias_ref,
                       oa_ref, s_out_ref, ob_ref, ko_ref, vo_ref, *scratch, tt):
    nq = tt // WINDOW
    _swa_prompt_cache(kc_ref, vc_ref, ko_ref, vo_ref, nq)
    attention = functools.partial(_swa_prompt_start, q_ref, kp_ref, kc_ref, vp_ref, vc_ref, bias_ref, ob_ref, nq)
    _delta_prompt_body(cs_ref, z_ref, gb_ref, wn_ref, oa_ref, s_out_ref, *scratch, tt=tt, side_work=attention)


def _mixer_prompt(cs, z, gb, w_onorm, qb, k, v, bias_t, *, tt):
    b, l, _ = cs.shape
    n_c = tt // CHUNK
    nq = tt // WINDOW
    tok_spec = lambda n: pl.BlockSpec((None, tt, n), lambda i, t: (i, t, 0))
    prev = lambda n: pl.BlockSpec((None, WINDOW, n), lambda i, t: (i, jnp.maximum(nq * t - 1, 0), 0))
    last = lambda n: pl.BlockSpec((None, WINDOW, n), lambda i, t: (i, 0, 0))
    return pl.pallas_call(
        functools.partial(_mixer_prompt_body, tt=tt),
        grid=(b, l // tt),
        in_specs=[tok_spec(CONV_DIM), tok_spec(D_MODEL), tok_spec(GB_LANES), _const_spec((1, DK_A)),
                  tok_spec(D_MODEL), prev(KV_B), tok_spec(KV_B), prev(KV_B), tok_spec(KV_B),
                  _const_spec((N_KV_B, 2 * WINDOW, GROUP_B * WINDOW))],
        out_specs=[tok_spec(D_MODEL), pl.BlockSpec((None, N_HEADS_A, DK_A, DK_A), lambda i, t: (i, 0, 0, 0)),
                   tok_spec(D_MODEL), last(KV_B), last(KV_B)],
        out_shape=[jax.ShapeDtypeStruct((b, l, D_MODEL), BF16),
                   jax.ShapeDtypeStruct((b, N_HEADS_A, DK_A, DK_A), F32),
                   jax.ShapeDtypeStruct((b, l, D_MODEL), BF16),
                   jax.ShapeDtypeStruct((b, WINDOW, KV_B), F32), jax.ShapeDtypeStruct((b, WINDOW, KV_B), F32)],
        scratch_shapes=[pltpu.VMEM((N_HEADS_A, DK_A, DK_A), F32),
                        pltpu.VMEM((tt, D_MODEL), F32), pltpu.VMEM((tt, D_MODEL), F32),
                        pltpu.VMEM((N_HEADS_A * n_c, 2 * CHUNK, DK_A), BF16),
                        pltpu.VMEM((N_HEADS_A * n_c, CHUNK + DK_A, CHUNK), BF16),
                        pltpu.VMEM((N_HEADS_A * n_c, 8, DK_A), F32)],
        compiler_params=_params(2),
        name="mixer_prompt",
    )(cs, z, gb, w_onorm, qb, k, k, v, v, bias_t)


def _swa_prompt(qb, k, v, bias_t):
    b, l, _ = qb.shape
    nq = 2 if l % (2 * WINDOW) == 0 else 1
    cur = lambda n: pl.BlockSpec((None, nq * WINDOW, n), lambda i, t: (i, t, 0))
    prev = lambda n: pl.BlockSpec((None, WINDOW, n), lambda i, t: (i, jnp.maximum(nq * t - 1, 0), 0))
    last = lambda n: pl.BlockSpec((None, WINDOW, n), lambda i, t: (i, 0, 0))
    return pl.pallas_call(
        functools.partial(_swa_prompt_body, nq=nq),
        grid=(b, l // (nq * WINDOW)),
        in_specs=[cur(D_MODEL), prev(KV_B), cur(KV_B), prev(KV_B), cur(KV_B),
                  _const_spec((N_KV_B, 2 * WINDOW, GROUP_B * WINDOW))],
        out_specs=[cur(D_MODEL), last(KV_B), last(KV_B)],
        out_shape=[jax.ShapeDtypeStruct((b, l, D_MODEL), BF16),
                   jax.ShapeDtypeStruct((b, WINDOW, KV_B), F32), jax.ShapeDtypeStruct((b, WINDOW, KV_B), F32)],
        compiler_params=_params(2),
        name="swa_prompt",
    )(qb, k, k, v, v, bias_t)


def _swa_sample_body(q_ref, kn_ref, vn_ref, kc_ref, vc_ref, bias_ref, *rest, bt, nb, ls):
    o_ref, ko_ref, vo_ref = rest[-3:]
    kvs = range(N_KV_B)

    def per_group(gi, carry):
        bis = [gi * nb + d for d in range(nb)]
        ks, vs, s = [], [], []
        for bi in bis:
            kf = jnp.concatenate([kc_ref[bi], kn_ref[bi]], axis=0)
            vf = jnp.concatenate([vc_ref[bi], vn_ref[bi]], axis=0)
            ko_ref[bi] = kf[ls:, :]
            vo_ref[bi] = vf[ls:, :]
            ks.append(_zero_key0(kf).astype(BF16))
            vs.append(_zero_key0(vf).astype(BF16))
        for d, bi in enumerate(bis):
            q = q_ref[bi]
            s.append([_dot_nt(ks[d][:, _kv_cols(kvh)], _group_queries(q, kvh)) for kvh in kvs])
        pn = [[_softmax_keys(s[d][kvh] + bias_ref[kvh]) for kvh in kvs] for d in range(nb)]
        o = [[_dot_tn(pn[d][kvh], vs[d][:, _kv_cols(kvh)]) for kvh in kvs] for d in range(nb)]
        for d, bi in enumerate(bis):
            for kvh in kvs:
                for g in range(GROUP_B):
                    hh = kvh * GROUP_B + g
                    o_ref[bi, :, hh * HD_B:(hh + 1) * HD_B] = o[d][kvh][g * ls:(g + 1) * ls].astype(BF16)
        return carry

    lax.fori_loop(0, bt // nb, per_group, 0)


def _swa_sample(qb, k_new, v_new, k_cache_all, v_cache_all, bias_s, k_prev, v_prev, *, layer, bt):
    b, ls, _ = qb.shape
    n_keys = WINDOW + ls
    new = lambda n: pl.BlockSpec((bt, ls, n), lambda i: (i, 0, 0))
    cache = pl.BlockSpec((None, bt, WINDOW, KV_B), lambda i: (layer, i, 0, 0))
    extra_k, specs_k = _fill_layer(k_prev)
    extra_v, specs_v = _fill_layer(v_prev)
    return pl.pallas_call(
        functools.partial(_swa_sample_body, bt=bt, nb=min(bt, 4), ls=ls),
        grid=(b // bt,),
        in_specs=[new(D_MODEL), new(KV_B), new(KV_B), cache, cache,
                  _const_spec((N_KV_B, n_keys, GROUP_B * ls))] + specs_k + specs_v,
        out_specs=[new(D_MODEL), cache, cache],
        out_shape=[jax.ShapeDtypeStruct((b, ls, D_MODEL), BF16),
                   jax.ShapeDtypeStruct(k_cache_all.shape, F32), jax.ShapeDtypeStruct(v_cache_all.shape, F32)],
        input_output_aliases={6: 1, 7: 2} if extra_k else {},
        compiler_params=_params(1),
        name="swa_sample",
    )(qb, k_new, v_new, k_cache_all, v_cache_all, bias_s, *extra_k, *extra_v)


def _mixer_sample_body(cs_ref, z_ref, gb_ref, wn_ref, s0_ref, q_ref, kn_ref, vn_ref, kc_ref, vc_ref, bias_ref, *rest,
                       bt, nb, ls):
    oa_ref, s_out_ref, ob_ref, ko_ref, vo_ref = rest[-5:]
    r = N_HEADS_A * ls
    masks = _delta_masks(r, ls)
    heads = range(N_HEADS_A)
    hrows = [slice(hd * ls, (hd + 1) * ls) for hd in heads]
    hcols = [slice(hd * DK_A, (hd + 1) * DK_A) for hd in heads]
    kvs = range(N_KV_B)

    def per_group(gi, carry):
        bis = [gi * nb + d for d in range(nb)]
        ks, vs, scores = [], [], []
        for bi in bis:
            kf = jnp.concatenate([kc_ref[bi], kn_ref[bi]], axis=0)
            vf = jnp.concatenate([vc_ref[bi], vn_ref[bi]], axis=0)
            ko_ref[bi] = kf[ls:, :]
            vo_ref[bi] = vf[ls:, :]
            ks.append(_zero_key0(kf).astype(BF16))
            vs.append(_zero_key0(vf).astype(BF16))
        for d, bi in enumerate(bis):
            q = q_ref[bi]
            scores.append([_dot_nt(ks[d][:, _kv_cols(kvh)], _group_queries(q, kvh)) for kvh in kvs])
        pn = [[_softmax_keys(scores[d][kvh] + bias_ref[kvh]) for kvh in kvs] for d in range(nb)]

        problems, egl = [], []
        for bi in bis:
            cs = cs_ref[bi].astype(F32)
            gbv = gb_ref[bi]
            stack = lambda base: jnp.concatenate([cs[:, base + hd * DK_A: base + (hd + 1) * DK_A] for hd in heads],
                                                 axis=0)
            beta_b = jnp.concatenate([jnp.broadcast_to(gbv[:, hd:hd + 1], (ls, DK_A)) for hd in heads], axis=0)
            g_b = jnp.concatenate(
                [jnp.broadcast_to(gbv[:, N_HEADS_A + hd:N_HEADS_A + hd + 1], (ls, DK_A)) for hd in heads], axis=0)
            gc, rev = _decay_sums(g_b, masks)
            egl.append(jnp.exp(gc))
            problems.append((stack(0), stack(D_MODEL), stack(2 * D_MODEL), beta_b, gc, rev, gc.T[:r, :]))
        pre = _delta_pre(problems, masks)
        both = [[_dot(jnp.concatenate([pre[d][1][hrows[hd]], pre[d][2][hrows[hd]]], axis=0),
                      s0_ref[bis[d], hd].astype(BF16)) for hd in heads] for d in range(nb)]
        v_new = [(pre[d][0] - jnp.concatenate([both[d][hd][:ls] for hd in heads], axis=0)).astype(BF16)
                 for d in range(nb)]
        qkv = [_dot(pre[d][4], v_new[d]) for d in range(nb)]
        upd = [[_dot_tn(pre[d][3][hrows[hd]], v_new[d][hrows[hd]]) for hd in heads] for d in range(nb)]
        att = [[_dot_tn(pn[d][kvh], vs[d][:, _kv_cols(kvh)]) for kvh in kvs] for d in range(nb)]
        for d, bi in enumerate(bis):
            zb = z_ref[bi].astype(F32)
            for hd in heads:
                last = (hd + 1) * ls - 1
                s_out_ref[bi, hd] = s0_ref[bi, hd] * egl[d][last:last + 1, :] + upd[d][hd]
                o = both[d][hd][ls:] + qkv[d][hrows[hd]]
                oa_ref[bi, :, hcols[hd]] = _gated_norm(o, zb[:, hcols[hd]], wn_ref[...]).astype(BF16)
            for kvh in kvs:
                for g in range(GROUP_B):
                    hh = kvh * GROUP_B + g
                    ob_ref[bi, :, hh * HD_B:(hh + 1) * HD_B] = att[d][kvh][g * ls:(g + 1) * ls].astype(BF16)
        return carry

    lax.fori_loop(0, bt // nb, per_group, 0)


def _mixer_sample(cs, z, gb, w_onorm, s0_all, qb, k_new, v_new, k_cache_all, v_cache_all, bias_s, prev, *, layer, bt):
    b, ls, _ = cs.shape
    tok_spec = lambda n: pl.BlockSpec((bt, ls, n), lambda i: (i, 0, 0))
    st_spec = pl.BlockSpec((None, bt, N_HEADS_A, DK_A, DK_A), lambda i: (layer, i, 0, 0, 0))
    cache = pl.BlockSpec((None, bt, WINDOW, KV_B), lambda i: (layer, i, 0, 0))
    extra, extra_specs = ([], []) if prev is None else (list(prev), [pl.BlockSpec(memory_space=pl.ANY)] * 3)
    n_in = 11
    return pl.pallas_call(
        functools.partial(_mixer_sample_body, bt=bt, nb=min(bt, 4), ls=ls),
        grid=(b // bt,),
        in_specs=[tok_spec(CONV_DIM), tok_spec(D_MODEL), tok_spec(GB_LANES), _const_spec((1, DK_A)), st_spec,
                  tok_spec(D_MODEL), tok_spec(KV_B), tok_spec(KV_B), cache, cache,
                  _const_spec((N_KV_B, WINDOW + ls, GROUP_B * ls))] + extra_specs,
        out_specs=[tok_spec(D_MODEL), st_spec, tok_spec(D_MODEL), cache, cache],
        out_shape=[jax.ShapeDtypeStruct((b, ls, D_MODEL), BF16), jax.ShapeDtypeStruct(s0_all.shape, F32),
                   jax.ShapeDtypeStruct((b, ls, D_MODEL), BF16),
                   jax.ShapeDtypeStruct(k_cache_all.shape, F32), jax.ShapeDtypeStruct(v_cache_all.shape, F32)],
        input_output_aliases={n_in: 1, n_in + 1: 3, n_in + 2: 4} if extra else {},
        compiler_params=_params(1),
        name="mixer_sample",
    )(cs, z, gb, w_onorm, s0_all, qb, k_new, v_new, k_cache_all, v_cache_all, bias_s, *extra)


def _layer_norm(y, g, b):
    mu = jnp.mean(y, -1, keepdims=True)
    d = y - mu
    var = jnp.mean(d * d, -1, keepdims=True)
    return d * lax.rsqrt(var + LN_EPS) * g + b


def _tail_body(x_ref, oa_ref, ob_ref, gates_ref, mod_ref, wpa_ref, wpb_ref, wout_ref, ln1g_ref, ln1b_ref,
               wup_ref, wdn_ref, ln2g_ref, ln2b_ref, o_ref, *, bt, tt, alpha):
    m = bt * tt
    mod = lambda i: mod_ref[:, :, i * D_MODEL:(i + 1) * D_MODEL]
    ga = gates_ref[:, :, 0:D_MODEL].astype(F32).reshape(m, D_MODEL)
    gb = gates_ref[:, :, D_MODEL:2 * D_MODEL].astype(F32).reshape(m, D_MODEL)
    mixed = ga * _dot(oa_ref[...].reshape(m, D_MODEL), wpa_ref[...]) \
        + gb * _dot(ob_ref[...].reshape(m, D_MODEL), wpb_ref[...])
    attn = _dot(mixed.astype(BF16), wout_ref[...]).reshape(bt, tt, D_MODEL)
    x1 = _layer_norm(alpha * x_ref[...] + mod(2) * attn, ln1g_ref[...], ln1b_ref[...])
    h2 = (x1 * (1.0 + mod(4)) + mod(3)).reshape(m, D_MODEL).astype(BF16)
    ff = jnp.zeros((m, D_MODEL), F32)
    for c in range(D_FF // D_MODEL):
        cols = slice(c * D_MODEL, (c + 1) * D_MODEL)
        a = jnp.maximum(_dot(h2, wup_ref[:, cols]), 0.0)
        ff = ff + _dot((a * a).astype(BF16), wdn_ref[cols, :])
    o_ref[...] = _layer_norm(alpha * x1 + mod(5) * ff.reshape(bt, tt, D_MODEL), ln2g_ref[...], ln2b_ref[...])


def _tail(x, oa, ob, gates, mod, w, p, *, layer, bt, tt, alpha):
    b, l, _ = x.shape
    tok_spec = lambda n: pl.BlockSpec((bt, tt, n), lambda i, t: (i, t, 0))
    row = _const_spec((1, D_MODEL))
    sq = _layer_spec((D_MODEL, D_MODEL), layer)
    return pl.pallas_call(
        functools.partial(_tail_body, bt=bt, tt=tt, alpha=alpha),
        grid=(b // bt, l // tt),
        in_specs=[tok_spec(D_MODEL), tok_spec(D_MODEL), tok_spec(D_MODEL), tok_spec(2 * D_MODEL),
                  pl.BlockSpec((bt, 1, 6 * D_MODEL), lambda i, t: (i, 0, 0)),
                  sq, sq, sq, row, row,
                  _layer_spec((D_MODEL, D_FF), layer), _layer_spec((D_FF, D_MODEL), layer), row, row],
        out_specs=tok_spec(D_MODEL),
        out_shape=jax.ShapeDtypeStruct((b, l, D_MODEL), F32),
        compiler_params=_params(2),
        name="tail",
    )(x, oa, ob, gates, mod, w["wpa"], w["wpb"], w["wout"], p["ln1_g"], p["ln1_b"],
      w["wup"], w["wdn"], p["ln2_g"], p["ln2_b"])


def _tiles(b, l, rows):
    tt = min(l, rows)
    bt = max(1, min(b, rows // tt))
    return bt, tt


def _layer_params(l, w_conv, a_log, dt_bias, w_onorm, ln1_g, ln1_b, ln2_g, ln2_b):
    gpar = jnp.zeros((2, GB_LANES), F32)
    gpar = gpar.at[0, N_HEADS_A:2 * N_HEADS_A].set(-jnp.exp(a_log[l].astype(F32)))
    gpar = gpar.at[1, N_HEADS_A:2 * N_HEADS_A].set(dt_bias[l].astype(F32))
    row = lambda a: a[l].reshape(1, -1).astype(F32)
    return dict(wconv=w_conv[l].astype(F32), gpar=gpar, w_onorm=row(w_onorm), ln1_g=row(ln1_g), ln1_b=row(ln1_b),
                ln2_g=row(ln2_g), ln2_b=row(ln2_b))


def _trunk_layer(x, mod, layer, w, p, alpha, conv_state, sample_state):
    b, l, _ = x.shape
    bt, tt = _tiles(b, l, 512 if l % 512 == 0 else 256)
    cs, z, gb, qb, k, v, gates, conv_out = _front(x, mod, conv_state, w["wa"], w["wb"], w["wba"], p["wconv"], p["gpar"],
                                                  layer=layer, bt=bt, tt=tt)
    if sample_state is None:
        oa, s_new, ob, k_new, v_new = _mixer_prompt(cs, z, gb, p["w_onorm"], qb, k, v, p["bias_t"], tt=min(l, 256))
    else:
        s0_all, kc_all, vc_all, s_prev, k_prev, v_prev = sample_state
        bias_s = p["bias_t"][:, :WINDOW + l, :].reshape(N_KV_B, WINDOW + l, GROUP_B, WINDOW)[..., :l]
        bias_s = bias_s.reshape(N_KV_B, WINDOW + l, GROUP_B * l)
        prev = None if s_prev is None else (s_prev, k_prev, v_prev)
        oa, s_new, ob, k_new, v_new = _mixer_sample(cs, z, gb, p["w_onorm"], s0_all, qb, k, v, kc_all, vc_all, bias_s,
                                                    prev, layer=layer, bt=min(b, 8))
    bt, tt = _tiles(b, l, 512)
    x2 = _tail(x, oa, ob, gates, mod, w, p, layer=layer, bt=bt, tt=tt, alpha=alpha)
    return x2, (s_new, conv_out, k_new, v_new)


def kernel(x_prompt, x_sample, state_delta, state_conv, cache_k, cache_v, c_prompt, c_sample, rel_bias, w_ada, b_ada, w_in, w_conv, a_log, dt_bias, w_onorm, sinks, w_pa, w_pb, w_out, ln1_g, ln1_b, w_up, w_down, ln2_g, ln2_b):
    depth = w_in.shape[0]
    alpha = (2 * depth) ** 0.25
    bp = x_prompt.shape[0]
    mod_all = _ada(jnp.concatenate([c_prompt, c_sample], axis=0), w_ada, b_ada)
    bias_t = _bias_table_t(rel_bias, sinks)
    wa, wb, wba = _prep_w_in(w_in)
    w = dict(wa=wa, wb=wb, wba=wba, wpa=_cast_bf16(w_pa), wpb=_cast_bf16(w_pb), wout=_cast_bf16(w_out),
             wup=_cast_bf16(w_up), wdn=_cast_bf16(w_down))
    bs = x_sample.shape[0]
    kc_all = cache_k.reshape(depth, bs, WINDOW, KV_B)
    vc_all = cache_v.reshape(depth, bs, WINDOW, KV_B)
    yp, ys = x_prompt, x_sample
    prompt_outs = [[] for _ in range(4)]
    sample_conv = []
    s_all = k_all = v_all = None
    for l in range(depth):
        p = _layer_params(l, w_conv, a_log, dt_bias, w_onorm, ln1_g, ln1_b, ln2_g, ln2_b)
        p["bias_t"] = bias_t[l]
        mod_p = mod_all[l, :bp][:, None, :]
        mod_s = mod_all[l, bp:][:, None, :]
        zero_conv = jnp.zeros((bp, CONV_W - 1, CONV_DIM), x_prompt.dtype)
        yp, rest_p = _trunk_layer(yp, mod_p, l, w, p, alpha, zero_conv, None)
        ys, (s_all, conv_s, k_all, v_all) = _trunk_layer(ys, mod_s, l, w, p, alpha, state_conv[l],
                                                         (state_delta, kc_all, vc_all, s_all, k_all, v_all))
        for acc, val in zip(prompt_outs, rest_p):
            acc.append(val)
        sample_conv.append(conv_s)
    heads = lambda a: a.reshape(a.shape[:-1] + (N_KV_B, HD_B))
    pd, pc, pk, pv = (jnp.stack(a) for a in prompt_outs)
    return (yp, ys, pd, pc, heads(pk), heads(pv), s_all, jnp.stack(sample_conv), heads(k_all), heads(v_all))
```

```python
import functools
import math

import numpy as np
import jax
import jax.numpy as jnp
from jax import lax
from jax.experimental import pallas as pl
from jax.experimental.pallas import tpu as pltpu

F32 = jnp.float32
BF16 = jnp.bfloat16

D_MODEL = 1024
N_HEADS_A = 8
DK_A = 128
CONV_W = 4
CONV_DIM = 3 * D_MODEL
CHUNK = 64
HD_B = 64
N_HEADS_B = 16
N_KV_B = 4
GROUP_B = N_HEADS_B // N_KV_B
KV_B = N_KV_B * HD_B
WINDOW = 128
N_BUCKETS = 32
MAX_DISTANCE = 128
D_FF = 4 * D_MODEL
LN_EPS = 1e-5
RMS_EPS = 1e-6

OFF_Z = CONV_DIM
OFF_QB = OFF_Z + D_MODEL
OFF_KVB = OFF_QB + D_MODEL
OFF_GATES = OFF_KVB + 2 * KV_B
N_MAIN = OFF_GATES + 2 * D_MODEL
GB_LANES = 128

F32_SUBLANES = 8
BF16_SUBLANES = 16
V7X_VMEM_BYTES = 64 * 1024 * 1024
VMEM_LIMIT = V7X_VMEM_BYTES - 8 * 1024 * 1024


def _params(n_grid):
    return pltpu.CompilerParams(dimension_semantics=("arbitrary",) * n_grid, vmem_limit_bytes=VMEM_LIMIT)


def _dot(a, b):
    return jnp.dot(a, b, preferred_element_type=F32)


def _dot_nt(a, b):
    return lax.dot_general(a, b, (((1,), (1,)), ((), ())), preferred_element_type=F32)


def _dot_tn(a, b):
    return lax.dot_general(a, b, (((0,), (0,)), ((), ())), preferred_element_type=F32)


def _sigmoid(x):
    return 1.0 / (1.0 + jnp.exp(-x))


def _silu(x):
    return x * _sigmoid(x)


def _const_spec(shape):
    nd = len(shape)
    return pl.BlockSpec(shape, lambda *_: (0,) * nd, pipeline_mode=pl.Buffered(1))


def _layer_spec(shape, layer):
    nd = len(shape)
    return pl.BlockSpec((None,) + tuple(shape), lambda *_: (layer,) + (0,) * nd, pipeline_mode=pl.Buffered(1))


def _cast_body(w_ref, o_ref):
    o_ref[...] = w_ref[...].astype(BF16)


def _cast_bf16(w):
    depth, k, n = w.shape
    bk = max(8, min(k, (1024 * 1024) // n))
    spec = pl.BlockSpec((None, bk, n), lambda l, i: (l, i, 0))
    return pl.pallas_call(
        _cast_body, grid=(depth, k // bk), in_specs=[spec], out_specs=spec,
        out_shape=jax.ShapeDtypeStruct(w.shape, BF16), compiler_params=_params(2), name="cast_bf16",
    )(w)


def _prep_w_in(w_in):
    o_ba = CONV_DIM + D_MODEL
    o_qb = o_ba + 2 * N_HEADS_A
    wm = jnp.concatenate([w_in[:, :, :o_ba], w_in[:, :, o_qb:]], axis=2).astype(BF16)
    wba = jnp.pad(w_in[:, :, o_ba:o_qb], ((0, 0), (0, 0), (0, GB_LANES - 2 * N_HEADS_A))).astype(BF16)
    return wm, wba


def _ada_body(c_ref, w_ref, b_ref, o_ref):
    s = _silu(c_ref[...]).astype(BF16)
    o_ref[...] = _dot(s, w_ref[...].astype(BF16)) + b_ref[...]


def _ada(c_all, w_ada, b_ada):
    depth = w_ada.shape[0]
    n_rows = c_all.shape[0]
    n_col = w_ada.shape[2] // D_MODEL
    return pl.pallas_call(
        _ada_body,
        grid=(depth, n_col),
        in_specs=[pl.BlockSpec((n_rows, D_MODEL), lambda l, n: (0, 0)),
                  pl.BlockSpec((None, D_MODEL, D_MODEL), lambda l, n: (l, 0, n)),
                  pl.BlockSpec((None, 1, D_MODEL), lambda l, n: (l, 0, n))],
        out_specs=pl.BlockSpec((None, n_rows, D_MODEL), lambda l, n: (l, 0, n)),
        out_shape=jax.ShapeDtypeStruct((depth, n_rows, w_ada.shape[2]), F32),
        compiler_params=_params(2),
        name="ada",
    )(c_all, w_ada, b_ada.reshape(depth, 1, -1))


def _bucket_table():
    r = np.arange(WINDOW)[:, None]
    c = np.arange(2 * WINDOW)[None, :]
    dist = WINDOW + r - c
    n = np.maximum(dist, 0)
    max_exact = N_BUCKETS // 2
    ratio = np.maximum(n, max_exact).astype(np.float32) / np.float32(max_exact)
    large = max_exact + (np.log(ratio) / np.float32(math.log(MAX_DISTANCE / max_exact))
                         * np.float32(N_BUCKETS - max_exact)).astype(np.int32)
    large = np.minimum(large, N_BUCKETS - 1)
    bucket = np.where(n < max_exact, n, large).astype(np.int32)
    valid = ((dist >= 0) & (dist < WINDOW)).astype(np.int32)
    return bucket, valid


def _bias_t_body(rb_ref, sink_ref, bucket_ref, valid_ref, o_ref):
    l = pl.program_id(0)
    h = pl.program_id(1)
    bucket = bucket_ref[...]
    acc = jnp.zeros(bucket.shape, F32)
    for j in range(N_BUCKETS):
        acc = jnp.where(bucket == j, rb_ref[j, h], acc)
    acc = jnp.where(valid_ref[...] > 0, acc, -jnp.inf)
    key = lax.broadcasted_iota(jnp.int32, bucket.shape, 0)
    o_ref[...] = jnp.where(key == 0, sink_ref[l, h], acc)


def _bias_table_t(rel_bias, sinks):
    depth = sinks.shape[0]
    bucket, valid = _bucket_table()
    return pl.pallas_call(
        _bias_t_body,
        grid=(depth, N_HEADS_B),
        in_specs=[pl.BlockSpec(memory_space=pltpu.SMEM), pl.BlockSpec(memory_space=pltpu.SMEM),
                  pl.BlockSpec((2 * WINDOW, WINDOW), lambda l, h: (0, 0)),
                  pl.BlockSpec((2 * WINDOW, WINDOW), lambda l, h: (0, 0))],
        out_specs=pl.BlockSpec((None, None, 2 * WINDOW, WINDOW), lambda l, h: (l, h // GROUP_B, 0, h % GROUP_B)),
        out_shape=jax.ShapeDtypeStruct((depth, N_KV_B, 2 * WINDOW, GROUP_B * WINDOW), F32),
        compiler_params=_params(2),
        name="bias_table_t",
    )(rel_bias.astype(F32), sinks.astype(F32), jnp.asarray(bucket.T.copy()), jnp.asarray(valid.T.copy()))


def _front_body(x_ref, mod_ref, st_ref, wm_ref, wba_ref, wconv_ref, gpar_ref,
                cs_ref, z_ref, gb_ref, qb_ref, k_ref, v_ref, gates_ref, cst_ref, cbuf, *, bt, tt):
    m = bt * tt
    pad, hist = F32_SUBLANES, CONV_W - 1
    sh1 = mod_ref[:, :, 0:D_MODEL]
    sc1 = mod_ref[:, :, D_MODEL:2 * D_MODEL]
    h = (x_ref[...] * (1.0 + sc1) + sh1).reshape(m, D_MODEL).astype(BF16)

    @pl.when(pl.program_id(1) == 0)
    def _():
        cbuf[:, 0:pad, :] = jnp.zeros((bt, pad, CONV_DIM), F32)
        cbuf[:, pad - hist:pad, :] = st_ref[...]

    def proj(off, n):
        return _dot(h, wm_ref[:, off:off + n])

    half = D_MODEL // 2

    def conv_piece(i):
        cols = slice(i * half, (i + 1) * half)
        cbuf[:, pad:pad + tt, cols] = proj(i * half, half).reshape(bt, tt, half)
        u = cbuf[:, :, cols]
        u1 = pltpu.roll(u, 1, 1)
        w0, w1, w2, w3 = (wconv_ref[j:j + 1, cols] for j in range(CONV_W))
        y = (u * w3 + u1 * w2) + pltpu.roll(u * w1 + u1 * w0, 2, 1)
        cs_ref[:, :, cols] = _silu(y[:, pad:, :]).astype(cs_ref.dtype)

    def z_piece(i):
        z_ref[:, :, i * half:(i + 1) * half] = proj(OFF_Z + i * half, half).astype(z_ref.dtype).reshape(bt, tt, half)

    def qb_piece(i):
        q = proj(OFF_QB + i * half, half) * (HD_B ** -0.5)
        qb_ref[:, :, i * half:(i + 1) * half] = q.astype(BF16).reshape(bt, tt, half)

    def kv_piece(_):
        kv = proj(OFF_KVB, 2 * KV_B)
        k_ref[...] = kv[:, :KV_B].reshape(bt, tt, KV_B)
        v_ref[...] = kv[:, KV_B:].reshape(bt, tt, KV_B)

    def gates_piece(i):
        g = _sigmoid(proj(OFF_GATES + i * half, half))
        gates_ref[:, :, i * half:(i + 1) * half] = g.astype(gates_ref.dtype).reshape(bt, tt, half)

    for piece, i in [(conv_piece, 0), (z_piece, 0), (conv_piece, 1), (z_piece, 1), (conv_piece, 2), (qb_piece, 0),
                     (conv_piece, 3), (qb_piece, 1), (conv_piece, 4), (kv_piece, 0), (gates_piece, 0),
                     (conv_piece, 5), (gates_piece, 1), (gates_piece, 2), (gates_piece, 3)]:
        piece(i)
    tail = cbuf[:, tt + pad - hist:tt + pad, :]
    cst_ref[...] = tail
    cbuf[:, pad - hist:pad, :] = tail

    ba = _dot(h, wba_ref[...])
    xg = ba + gpar_ref[1:2, :]
    softplus = jnp.maximum(xg, 0.0) + jnp.log(1.0 + jnp.exp(-jnp.abs(xg)))
    lane = lax.broadcasted_iota(jnp.int32, ba.shape, 1)
    gb = jnp.where(lane < N_HEADS_A, _sigmoid(ba), gpar_ref[0:1, :] * softplus)
    gb_ref[...] = gb.reshape(bt, tt, GB_LANES)


def _front(x, mod, conv_state, wm, wba, wconv, gpar, *, layer, bt, tt):
    b, l, _ = x.shape
    grid = (b // bt, l // tt)
    tok = lambda n, dt=F32: jax.ShapeDtypeStruct((b, l, n), dt)
    act = BF16 if tt % BF16_SUBLANES == 0 else F32
    tok_spec = lambda n: pl.BlockSpec((bt, tt, n), lambda i, t: (i, t, 0))
    return pl.pallas_call(
        functools.partial(_front_body, bt=bt, tt=tt),
        grid=grid,
        in_specs=[tok_spec(D_MODEL),
                  pl.BlockSpec((bt, 1, 6 * D_MODEL), lambda i, t: (i, 0, 0)),
                  pl.BlockSpec((bt, CONV_W - 1, CONV_DIM), lambda i, t: (i, 0, 0)),
                  _layer_spec((D_MODEL, N_MAIN), layer),
                  _layer_spec((D_MODEL, GB_LANES), layer),
                  _const_spec((CONV_W, CONV_DIM)),
                  _const_spec((2, GB_LANES))],
        out_specs=[tok_spec(CONV_DIM), tok_spec(D_MODEL), tok_spec(GB_LANES), tok_spec(D_MODEL),
                   tok_spec(KV_B), tok_spec(KV_B), tok_spec(2 * D_MODEL),
                   pl.BlockSpec((bt, CONV_W - 1, CONV_DIM), lambda i, t: (i, 0, 0))],
        out_shape=[tok(CONV_DIM, act), tok(D_MODEL, act), tok(GB_LANES), tok(D_MODEL, BF16),
                   tok(KV_B), tok(KV_B), tok(2 * D_MODEL, act),
                   jax.ShapeDtypeStruct((b, CONV_W - 1, CONV_DIM), F32)],
        scratch_shapes=[pltpu.VMEM((bt, tt + F32_SUBLANES, CONV_DIM), F32)],
        compiler_params=_params(2),
        name="front",
    )(x, mod, conv_state, wm, wba, wconv, gpar)


def _delta_masks(r, block):
    i = lax.broadcasted_iota(jnp.int32, (r, r), 0)
    j = lax.broadcasted_iota(jnp.int32, (r, r), 1)
    shift = int(math.log2(block))
    same = (i >> shift) == (j >> shift)
    levels = [((i >> (s + 1)) == (j >> (s + 1))) & ((i >> s) != (j >> s)) for s in range(shift)]
    return dict(tri=same & (i >= j), strict=same & (i > j), upper=same & (i < j), eye=(i == j).astype(F32),
                levels=levels)


def _split_dot(a01, x):
    a01 = a01.astype(BF16)
    hi = x.astype(BF16)
    rest = x - hi.astype(F32)
    mid = rest.astype(BF16)
    lo = (rest - mid.astype(F32)).astype(BF16)
    return _dot(a01, hi) + _dot(a01, mid) + _dot(a01, lo)


def _decay_sums(g, masks):
    return _split_dot(masks["tri"], g), _split_dot(masks["upper"], g)


def _delta_pre(problems, masks):
    tri, strict, eye, levels = masks["tri"], masks["strict"], masks["eye"], masks["levels"]
    n = len(problems)
    r = problems[0][0].shape[0]
    lhs, ks, rhs, decays, qgs, kds = [], [], [], [], [], []
    for qr, kr, v, beta_b, gc, rev, gc_row in problems:
        q = qr * lax.rsqrt(jnp.sum(qr * qr, -1, keepdims=True) + 1e-6) * (DK_A ** -0.5)
        k = kr * lax.rsqrt(jnp.sum(kr * kr, -1, keepdims=True) + 1e-6)
        gc_col = gc[:, :r] if r <= DK_A else jnp.concatenate([gc] * (r // DK_A), axis=1)
        decays.append(jnp.exp(jnp.where(tri, gc_col - gc_row, -jnp.inf)))
        eg = jnp.exp(gc)
        kb = k * beta_b
        lhs.append(jnp.concatenate([kb, q], axis=0).astype(BF16))
        ks.append(k.astype(BF16))
        rhs.append(jnp.concatenate([v * beta_b, kb * eg], axis=1).astype(BF16))
        qgs.append((q * eg).astype(BF16))
        kds.append((k * jnp.exp(rev)).astype(BF16))
    kk = [_dot_nt(lhs[i], ks[i]) for i in range(n)]
    mm = [jnp.where(strict, kk[i][:r] * decays[i], 0.0) for i in range(n)]
    qk = [(kk[i][r:] * decays[i]).astype(BF16) for i in range(n)]
    t = [eye - jnp.where(levels[0], mm[i], 0.0) for i in range(n)]
    for s, lvl in enumerate(levels[1:], start=1):
        blk = 2 ** s
        tb = [t[i].astype(BF16) for i in range(n)]
        mo = [jnp.where(lvl, mm[i], 0.0).astype(BF16) for i in range(n)]
        if blk % 8:
            x = [_dot(tb[i], mo[i]).astype(BF16) for i in range(n)]
            y = [_dot(x[i], tb[i]) for i in range(n)]
            t = [t[i] - y[i] for i in range(n)]
        else:
            split = [t[i].reshape(r // (2 * blk), 2, blk, r) for i in range(n)]
            lo = [split[i][:, 1].reshape(r // 2, r) for i in range(n)]
            x = [_dot(lo[i].astype(BF16), mo[i]).astype(BF16) for i in range(n)]
            y = [_dot(x[i], tb[i]) for i in range(n)]
            t = [jnp.stack([split[i][:, 0], (lo[i] - y[i]).reshape(r // (2 * blk), blk, r)], axis=1).reshape(r, r)
                 for i in range(n)]
    uw = [_dot(t[i].astype(BF16), rhs[i]) for i in range(n)]
    return [(uw[i][:, :DK_A], uw[i][:, DK_A:].astype(BF16), qgs[i], kds[i], qk[i]) for i in range(n)]


def _gated_norm(o, z, w_onorm):
    o = o * lax.rsqrt(jnp.mean(o * o, -1, keepdims=True) + RMS_EPS) * w_onorm
    return o * _silu(z)


def _delta_prompt_body(cs_ref, z_ref, gb_ref, wn_ref, o_ref, s_out_ref,
                       s_ref, u_s, o_s, wq_s, qkd_s, egl_s, *, tt, side_work=None):
    n_c = tt // CHUNK
    masks = _delta_masks(tt, CHUNK)

    @pl.when(pl.program_id(1) == 0)
    def _():
        s_ref[...] = jnp.zeros(s_ref.shape, F32)

    finish_side_work = side_work() if side_work is not None else None

    gbv = gb_ref[...]
    gc_all, rev_all = _decay_sums(gbv, masks)
    gc_t = gc_all.T
    heads = range(N_HEADS_A)
    hcols = [slice(hd * DK_A, (hd + 1) * DK_A) for hd in heads]
    problems = []
    for hd in heads:
        lg = N_HEADS_A + hd
        problems.append((cs_ref[:, hcols[hd]].astype(F32),
                         cs_ref[:, D_MODEL + hd * DK_A:D_MODEL + (hd + 1) * DK_A].astype(F32),
                         cs_ref[:, 2 * D_MODEL + hd * DK_A:2 * D_MODEL + (hd + 1) * DK_A].astype(F32),
                         jnp.broadcast_to(gbv[:, hd:hd + 1], (tt, DK_A)),
                         jnp.broadcast_to(gc_all[:, lg:lg + 1], (tt, DK_A)),
                         jnp.broadcast_to(rev_all[:, lg:lg + 1], (tt, DK_A)),
                         jnp.broadcast_to(gc_t[lg:lg + 1, :], (tt, tt))))
    for hd, (u, w, qg, kd, qk) in enumerate(_delta_pre(problems, masks)):
        u_s[:, hcols[hd]] = u
        kdt = kd.T
        for c in range(n_c):
            blk = slice(c * CHUNK, (c + 1) * CHUNK)
            idx = hd * n_c + c
            wq_s[idx, 0:CHUNK, :] = w[blk]
            wq_s[idx, CHUNK:2 * CHUNK, :] = qg[blk]
            qkd_s[idx, 0:CHUNK, :] = qk[blk, blk]
            qkd_s[idx, CHUNK:CHUNK + DK_A, :] = kdt[:, blk]
            last = (c + 1) * CHUNK - 1
            egl_s[idx] = jnp.broadcast_to(jnp.exp(problems[hd][4][last:last + 1, :]), (8, DK_A))

    if finish_side_work is not None:
        finish_side_work()

    states = [s_ref[hd] for hd in heads]
    for c in range(n_c):
        blk = slice(c * CHUNK, (c + 1) * CHUNK)
        idx = [hd * n_c + c for hd in heads]
        ws = [_dot(wq_s[idx[hd]], states[hd].astype(BF16)) for hd in heads]
        v_new = [(u_s[blk, hcols[hd]] - ws[hd][:CHUNK]).astype(BF16) for hd in heads]
        upd = [_dot(qkd_s[idx[hd]], v_new[hd]) for hd in heads]
        for hd in heads:
            o_s[blk, hcols[hd]] = ws[hd][CHUNK:] + upd[hd][:CHUNK]
            states[hd] = states[hd] * egl_s[idx[hd]][0:1, :] + upd[hd][CHUNK:]
    for hd in heads:
        s_ref[hd] = states[hd]
        o_ref[:, hcols[hd]] = _gated_norm(o_s[:, hcols[hd]], z_ref[:, hcols[hd]].astype(F32), wn_ref[...]).astype(BF16)

    @pl.when(pl.program_id(1) == pl.num_programs(1) - 1)
    def _():
        s_out_ref[...] = s_ref[...]


def _softmax_keys(s):
    p = jnp.exp(s - jnp.max(s, axis=0, keepdims=True))
    return (p / jnp.sum(p, axis=0, keepdims=True)).astype(BF16)


def _zero_key0(x):
    return jnp.where(lax.broadcasted_iota(jnp.int32, x.shape, 0) == 0, 0.0, x)


def _kv_cols(kvh):
    return slice(kvh * HD_B, (kvh + 1) * HD_B)


def _group_queries(q, kvh):
    return jnp.concatenate([q[:, hh * HD_B:(hh + 1) * HD_B] for hh in range(kvh * GROUP_B, (kvh + 1) * GROUP_B)],
                           axis=0)


def _swa_prompt_cache(kc_ref, vc_ref, ko_ref, vo_ref, nq):
    @pl.when(pl.program_id(1) == pl.num_programs(1) - 1)
    def _():
        ko_ref[...] = kc_ref[(nq - 1) * WINDOW:, :]
        vo_ref[...] = vc_ref[(nq - 1) * WINDOW:, :]


def _swa_prompt_start(q_ref, kp_ref, kc_ref, vp_ref, vc_ref, bias_ref, o_ref, nq):
    n_q = GROUP_B * WINDOW
    key = lax.broadcasted_iota(jnp.int32, (2 * WINDOW, n_q), 0)
    no_prev = (pl.program_id(1) == 0) & (key >= 1) & (key < WINDOW)
    kall = jnp.concatenate([kp_ref[...], kc_ref[...]], axis=0)
    vall = jnp.concatenate([vp_ref[...], vc_ref[...]], axis=0)
    probs = [(j, kvh) for j in range(nq) for kvh in range(N_KV_B)]
    k2 = [_zero_key0(kall[j * WINDOW:(j + 2) * WINDOW]).astype(BF16) for j in range(nq)]
    v2t = [_zero_key0(vall[j * WINDOW:(j + 2) * WINDOW]).T.astype(BF16) for j in range(nq)]
    q = [q_ref[j * WINDOW:(j + 1) * WINDOW, :] for j in range(nq)]
    s = [_dot_nt(k2[j][:, _kv_cols(kvh)], _group_queries(q[j], kvh)) for j, kvh in probs]
    pn = []
    for (j, kvh), sc in zip(probs, s):
        sc = sc + bias_ref[kvh]
        pn.append(_softmax_keys(jnp.where(no_prev, -jnp.inf, sc) if j == 0 else sc))

    def finish():
        ot = [_dot(v2t[j][_kv_cols(kvh), :], p) for (j, kvh), p in zip(probs, pn)]
        for (j, kvh), o in zip(probs, ot):
            for g in range(GROUP_B):
                hh = kvh * GROUP_B + g
                o_ref[j * WINDOW:(j + 1) * WINDOW, hh * HD_B:(hh + 1) * HD_B] = \
                    o[:, g * WINDOW:(g + 1) * WINDOW].T.astype(BF16)

    return finish


def _mixer_prompt_body(cs_ref, z_ref, gb_ref, wn_ref, q_ref, kp_ref, kc_ref, vp_ref, vc_ref, bias_ref,
                       oa_ref, s_out_ref, ob_ref, ko_ref, vo_ref, *scratch, tt):
    nq = tt // WINDOW
    _swa_prompt_cache(kc_ref, vc_ref, ko_ref, vo_ref, nq)
    attention = functools.partial(_swa_prompt_start, q_ref, kp_ref, kc_ref, vp_ref, vc_ref, bias_ref, ob_ref, nq)
    _delta_prompt_body(cs_ref, z_ref, gb_ref, wn_ref, oa_ref, s_out_ref, *scratch, tt=tt, side_work=attention)


def _mixer_prompt(cs, z, gb, w_onorm, qb, k, v, bias_t, *, tt):
    b, l, _ = cs.shape
    n_c = tt // CHUNK
    nq = tt // WINDOW
    tok_spec = lambda n: pl.BlockSpec((None, tt, n), lambda i, t: (i, t, 0))
    prev = lambda n: pl.BlockSpec((None, WINDOW, n), lambda i, t: (i, jnp.maximum(nq * t - 1, 0), 0))
    last = lambda n: pl.BlockSpec((None, WINDOW, n), lambda i, t: (i, 0, 0))
    return pl.pallas_call(
        functools.partial(_mixer_prompt_body, tt=tt),
        grid=(b, l // tt),
        in_specs=[tok_spec(CONV_DIM), tok_spec(D_MODEL), tok_spec(GB_LANES), _const_spec((1, DK_A)),
                  tok_spec(D_MODEL), prev(KV_B), tok_spec(KV_B), prev(KV_B), tok_spec(KV_B),
                  _const_spec((N_KV_B, 2 * WINDOW, GROUP_B * WINDOW))],
        out_specs=[tok_spec(D_MODEL), pl.BlockSpec((None, N_HEADS_A, DK_A, DK_A), lambda i, t: (i, 0, 0, 0)),
                   tok_spec(D_MODEL), last(KV_B), last(KV_B)],
        out_shape=[jax.ShapeDtypeStruct((b, l, D_MODEL), BF16),
                   jax.ShapeDtypeStruct((b, N_HEADS_A, DK_A, DK_A), F32),
                   jax.ShapeDtypeStruct((b, l, D_MODEL), BF16),
                   jax.ShapeDtypeStruct((b, WINDOW, KV_B), F32), jax.ShapeDtypeStruct((b, WINDOW, KV_B), F32)],
        scratch_shapes=[pltpu.VMEM((N_HEADS_A, DK_A, DK_A), F32),
                        pltpu.VMEM((tt, D_MODEL), F32), pltpu.VMEM((tt, D_MODEL), F32),
                        pltpu.VMEM((N_HEADS_A * n_c, 2 * CHUNK, DK_A), BF16),
                        pltpu.VMEM((N_HEADS_A * n_c, CHUNK + DK_A, CHUNK), BF16),
                        pltpu.VMEM((N_HEADS_A * n_c, 8, DK_A), F32)],
        compiler_params=_params(2),
        name="mixer_prompt",
    )(cs, z, gb, w_onorm, qb, k, k, v, v, bias_t)


def _mixer_sample_body(cs_ref, z_ref, gb_ref, wn_ref, s0_ref, q_ref, kn_ref, vn_ref, kc_ref, vc_ref, bias_ref, *rest,
                       bt, nb, ls):
    oa_ref, s_out_ref, ob_ref, ko_ref, vo_ref = rest[-5:]
    r = N_HEADS_A * ls
    masks = _delta_masks(r, ls)
    heads = range(N_HEADS_A)
    hrows = [slice(hd * ls, (hd + 1) * ls) for hd in heads]
    hcols = [slice(hd * DK_A, (hd + 1) * DK_A) for hd in heads]
    kvs = range(N_KV_B)

    def per_group(gi, carry):
        bis = [gi * nb + d for d in range(nb)]
        ks, vs, scores = [], [], []
        for bi in bis:
            kf = jnp.concatenate([kc_ref[bi], kn_ref[bi]], axis=0)
            vf = jnp.concatenate([vc_ref[bi], vn_ref[bi]], axis=0)
            ko_ref[bi] = kf[ls:, :]
            vo_ref[bi] = vf[ls:, :]
            ks.append(_zero_key0(kf).astype(BF16))
            vs.append(_zero_key0(vf).astype(BF16))
        for d, bi in enumerate(bis):
            q = q_ref[bi]
            scores.append([_dot_nt(ks[d][:, _kv_cols(kvh)], _group_queries(q, kvh)) for kvh in kvs])
        pn = [[_softmax_keys(scores[d][kvh] + bias_ref[kvh]) for kvh in kvs] for d in range(nb)]

        problems, egl = [], []
        for bi in bis:
            cs = cs_ref[bi].astype(F32)
            gbv = gb_ref[bi]
            stack = lambda base: jnp.concatenate([cs[:, base + hd * DK_A: base + (hd + 1) * DK_A] for hd in heads],
                                                 axis=0)
            beta_b = jnp.concatenate([jnp.broadcast_to(gbv[:, hd:hd + 1], (ls, DK_A)) for hd in heads], axis=0)
            g_b = jnp.concatenate(
                [jnp.broadcast_to(gbv[:, N_HEADS_A + hd:N_HEADS_A + hd + 1], (ls, DK_A)) for hd in heads], axis=0)
            gc, rev = _decay_sums(g_b, masks)
            egl.append(jnp.exp(gc))
            problems.append((stack(0), stack(D_MODEL), stack(2 * D_MODEL), beta_b, gc, rev, gc.T[:r, :]))
        pre = _delta_pre(problems, masks)
        both = [[_dot(jnp.concatenate([pre[d][1][hrows[hd]], pre[d][2][hrows[hd]]], axis=0),
                      s0_ref[bis[d], hd].astype(BF16)) for hd in heads] for d in range(nb)]
        v_new = [(pre[d][0] - jnp.concatenate([both[d][hd][:ls] for hd in heads], axis=0)).astype(BF16)
                 for d in range(nb)]
        qkv = [_dot(pre[d][4], v_new[d]) for d in range(nb)]
        upd = [[_dot_tn(pre[d][3][hrows[hd]], v_new[d][hrows[hd]]) for hd in heads] for d in range(nb)]
        att = [[_dot_tn(pn[d][kvh], vs[d][:, _kv_cols(kvh)]) for kvh in kvs] for d in range(nb)]
        for d, bi in enumerate(bis):
            zb = z_ref[bi].astype(F32)
            for hd in heads:
                last = (hd + 1) * ls - 1
                s_out_ref[bi, hd] = s0_ref[bi, hd] * egl[d][last:last + 1, :] + upd[d][hd]
                o = both[d][hd][ls:] + qkv[d][hrows[hd]]
                oa_ref[bi, :, hcols[hd]] = _gated_norm(o, zb[:, hcols[hd]], wn_ref[...]).astype(BF16)
            for kvh in kvs:
                for g in range(GROUP_B):
                    hh = kvh * GROUP_B + g
                    ob_ref[bi, :, hh * HD_B:(hh + 1) * HD_B] = att[d][kvh][g * ls:(g + 1) * ls].astype(BF16)
        return carry

    lax.fori_loop(0, bt // nb, per_group, 0)


def _mixer_sample(cs, z, gb, w_onorm, s0_all, qb, k_new, v_new, k_cache_all, v_cache_all, bias_s, prev, *, layer, bt):
    b, ls, _ = cs.shape
    tok_spec = lambda n: pl.BlockSpec((bt, ls, n), lambda i: (i, 0, 0))
    st_spec = pl.BlockSpec((None, bt, N_HEADS_A, DK_A, DK_A), lambda i: (layer, i, 0, 0, 0))
    cache = pl.BlockSpec((None, bt, WINDOW, KV_B), lambda i: (layer, i, 0, 0))
    extra, extra_specs = ([], []) if prev is None else (list(prev), [pl.BlockSpec(memory_space=pl.ANY)] * 3)
    n_in = 11
    return pl.pallas_call(
        functools.partial(_mixer_sample_body, bt=bt, nb=min(bt, MIXER_SAMPLE_ROWS_PER_TRIP), ls=ls),
        grid=(b // bt,),
        in_specs=[tok_spec(CONV_DIM), tok_spec(D_MODEL), tok_spec(GB_LANES), _const_spec((1, DK_A)), st_spec,
                  tok_spec(D_MODEL), tok_spec(KV_B), tok_spec(KV_B), cache, cache,
                  _const_spec((N_KV_B, WINDOW + ls, GROUP_B * ls))] + extra_specs,
        out_specs=[tok_spec(D_MODEL), st_spec, tok_spec(D_MODEL), cache, cache],
        out_shape=[jax.ShapeDtypeStruct((b, ls, D_MODEL), BF16), jax.ShapeDtypeStruct(s0_all.shape, F32),
                   jax.ShapeDtypeStruct((b, ls, D_MODEL), BF16),
                   jax.ShapeDtypeStruct(k_cache_all.shape, F32), jax.ShapeDtypeStruct(v_cache_all.shape, F32)],
        input_output_aliases={n_in: 1, n_in + 1: 3, n_in + 2: 4} if extra else {},
        compiler_params=_params(1),
        name="mixer_sample",
    )(cs, z, gb, w_onorm, s0_all, qb, k_new, v_new, k_cache_all, v_cache_all, bias_s, *extra)


def _layer_norm(y, g, b):
    mu = jnp.mean(y, -1, keepdims=True)
    d = y - mu
    var = jnp.mean(d * d, -1, keepdims=True)
    return d * lax.rsqrt(var + LN_EPS) * g + b


def _tail_body(x_ref, oa_ref, ob_ref, gates_ref, mod_ref, wpa_ref, wpb_ref, wout_ref, ln1g_ref, ln1b_ref,
               wup_ref, wdn_ref, ln2g_ref, ln2b_ref, o_ref, *, bt, tt, alpha):
    m = bt * tt
    mod = lambda i: mod_ref[:, :, i * D_MODEL:(i + 1) * D_MODEL]
    ga = gates_ref[:, :, 0:D_MODEL].astype(F32).reshape(m, D_MODEL)
    gb = gates_ref[:, :, D_MODEL:2 * D_MODEL].astype(F32).reshape(m, D_MODEL)
    mixed = ga * _dot(oa_ref[...].reshape(m, D_MODEL), wpa_ref[...]) \
        + gb * _dot(ob_ref[...].reshape(m, D_MODEL), wpb_ref[...])
    attn = _dot(mixed.astype(BF16), wout_ref[...]).reshape(bt, tt, D_MODEL)
    x1 = _layer_norm(alpha * x_ref[...] + mod(2) * attn, ln1g_ref[...], ln1b_ref[...])
    h2 = (x1 * (1.0 + mod(4)) + mod(3)).reshape(m, D_MODEL).astype(BF16)
    ff = jnp.zeros((m, D_MODEL), F32)
    for c in range(D_FF // D_MODEL):
        cols = slice(c * D_MODEL, (c + 1) * D_MODEL)
        a = jnp.maximum(_dot(h2, wup_ref[:, cols]), 0.0)
        ff = ff + _dot((a * a).astype(BF16), wdn_ref[cols, :])
    o_ref[...] = _layer_norm(alpha * x1 + mod(5) * ff.reshape(bt, tt, D_MODEL), ln2g_ref[...], ln2b_ref[...])


def _tail(x, oa, ob, gates, mod, w, p, *, layer, bt, tt, alpha):
    b, l, _ = x.shape
    tok_spec = lambda n: pl.BlockSpec((bt, tt, n), lambda i, t: (i, t, 0))
    row = _const_spec((1, D_MODEL))
    sq = _layer_spec((D_MODEL, D_MODEL), layer)
    return pl.pallas_call(
        functools.partial(_tail_body, bt=bt, tt=tt, alpha=alpha),
        grid=(b // bt, l // tt),
        in_specs=[tok_spec(D_MODEL), tok_spec(D_MODEL), tok_spec(D_MODEL), tok_spec(2 * D_MODEL),
                  pl.BlockSpec((bt, 1, 6 * D_MODEL), lambda i, t: (i, 0, 0)),
                  sq, sq, sq, row, row,
                  _layer_spec((D_MODEL, D_FF), layer), _layer_spec((D_FF, D_MODEL), layer), row, row],
        out_specs=tok_spec(D_MODEL),
        out_shape=jax.ShapeDtypeStruct((b, l, D_MODEL), F32),
        compiler_params=_params(2),
        name="tail",
    )(x, oa, ob, gates, mod, w["wpa"], w["wpb"], w["wout"], p["ln1_g"], p["ln1_b"],
      w["wup"], w["wdn"], p["ln2_g"], p["ln2_b"])


def _tiles(b, l, rows):
    tt = min(l, rows)
    bt = max(1, min(b, rows // tt))
    return bt, tt


FRONT_ROWS_LONG, FRONT_ROWS_SHORT = 512, 256
MIXER_PROMPT_ROWS = 2 * WINDOW
MIXER_SAMPLE_BATCH_ROWS = 8
MIXER_SAMPLE_ROWS_PER_TRIP = 4
TAIL_ROWS = 512


def _layer_params(l, w_conv, a_log, dt_bias, w_onorm, ln1_g, ln1_b, ln2_g, ln2_b):
    gpar = jnp.zeros((2, GB_LANES), F32)
    gpar = gpar.at[0, N_HEADS_A:2 * N_HEADS_A].set(-jnp.exp(a_log[l].astype(F32)))
    gpar = gpar.at[1, N_HEADS_A:2 * N_HEADS_A].set(dt_bias[l].astype(F32))
    row = lambda a: a[l].reshape(1, -1).astype(F32)
    return dict(wconv=w_conv[l].astype(F32), gpar=gpar, w_onorm=row(w_onorm), ln1_g=row(ln1_g), ln1_b=row(ln1_b),
                ln2_g=row(ln2_g), ln2_b=row(ln2_b))


def _trunk_layer(x, mod, layer, w, p, alpha, conv_state, sample_state):
    b, l, _ = x.shape
    bt, tt = _tiles(b, l, FRONT_ROWS_LONG if l % FRONT_ROWS_LONG == 0 else FRONT_ROWS_SHORT)
    cs, z, gb, qb, k, v, gates, conv_out = _front(x, mod, conv_state, w["wm"], w["wba"], p["wconv"], p["gpar"],
                                                  layer=layer, bt=bt, tt=tt)
    if sample_state is None:
        oa, s_new, ob, k_new, v_new = _mixer_prompt(cs, z, gb, p["w_onorm"], qb, k, v, p["bias_t"],
                                                    tt=min(l, MIXER_PROMPT_ROWS))
    else:
        s0_all, kc_all, vc_all, s_prev, k_prev, v_prev = sample_state
        bias_s = p["bias_t"][:, :WINDOW + l, :].reshape(N_KV_B, WINDOW + l, GROUP_B, WINDOW)[..., :l]
        bias_s = bias_s.reshape(N_KV_B, WINDOW + l, GROUP_B * l)
        prev = None if s_prev is None else (s_prev, k_prev, v_prev)
        oa, s_new, ob, k_new, v_new = _mixer_sample(cs, z, gb, p["w_onorm"], s0_all, qb, k, v, kc_all, vc_all, bias_s,
                                                    prev, layer=layer, bt=min(b, MIXER_SAMPLE_BATCH_ROWS))
    bt, tt = _tiles(b, l, TAIL_ROWS)
    x2 = _tail(x, oa, ob, gates, mod, w, p, layer=layer, bt=bt, tt=tt, alpha=alpha)
    return x2, (s_new, conv_out, k_new, v_new)


def kernel(x_prompt, x_sample, state_delta, state_conv, cache_k, cache_v, c_prompt, c_sample, rel_bias, w_ada, b_ada, w_in, w_conv, a_log, dt_bias, w_onorm, sinks, w_pa, w_pb, w_out, ln1_g, ln1_b, w_up, w_down, ln2_g, ln2_b):
    depth = w_in.shape[0]
    alpha = (2 * depth) ** 0.25
    bp = x_prompt.shape[0]
    mod_all = _ada(jnp.concatenate([c_prompt, c_sample], axis=0), w_ada, b_ada)
    bias_t = _bias_table_t(rel_bias, sinks)
    wm, wba = _prep_w_in(w_in)
    w = dict(wm=wm, wba=wba, wpa=_cast_bf16(w_pa), wpb=_cast_bf16(w_pb), wout=_cast_bf16(w_out),
             wup=_cast_bf16(w_up), wdn=_cast_bf16(w_down))
    bs = x_sample.shape[0]
    kc_all = cache_k.reshape(depth, bs, WINDOW, KV_B)
    vc_all = cache_v.reshape(depth, bs, WINDOW, KV_B)
    yp, ys = x_prompt, x_sample
    prompt_outs = [[] for _ in range(4)]
    sample_conv = []
    s_all = k_all = v_all = None
    for l in range(depth):
        p = _layer_params(l, w_conv, a_log, dt_bias, w_onorm, ln1_g, ln1_b, ln2_g, ln2_b)
        p["bias_t"] = bias_t[l]
        mod_p = mod_all[l, :bp][:, None, :]
        mod_s = mod_all[l, bp:][:, None, :]
        zero_conv = jnp.zeros((bp, CONV_W - 1, CONV_DIM), x_prompt.dtype)
        yp, rest_p = _trunk_layer(yp, mod_p, l, w, p, alpha, zero_conv, None)
        ys, (s_all, conv_s, k_all, v_all) = _trunk_layer(ys, mod_s, l, w, p, alpha, state_conv[l],
                                                         (state_delta, kc_all, vc_all, s_all, k_all, v_all))
        for acc, val in zip(prompt_outs, rest_p):
            acc.append(val)
        sample_conv.append(conv_s)
    heads = lambda a: a.reshape(a.shape[:-1] + (N_KV_B, HD_B))
    pd, pc, pk, pv = (jnp.stack(a) for a in prompt_outs)
    return (yp, ys, pd, pc, heads(pk), heads(pv), s_all, jnp.stack(sample_conv), heads(k_all), heads(v_all))
```

```python
import functools
import math

import numpy as np
import jax
import jax.numpy as jnp
from jax import lax
from jax.experimental import pallas as pl
from jax.experimental.pallas import tpu as pltpu

F32 = jnp.float32
BF16 = jnp.bfloat16

D_MODEL = 1024
N_HEADS_A = 8
DK_A = 128
CONV_W = 4
CONV_DIM = 3 * D_MODEL
CHUNK = 64
HD_B = 64
N_HEADS_B = 16
N_KV_B = 4
GROUP_B = N_HEADS_B // N_KV_B
KV_B = N_KV_B * HD_B
WINDOW = 128
N_BUCKETS = 32
MAX_DISTANCE = 128
D_FF = 4 * D_MODEL
LN_EPS = 1e-5
RMS_EPS = 1e-6

OFF_Z = CONV_DIM
OFF_QB = OFF_Z + D_MODEL
OFF_KVB = OFF_QB + D_MODEL
OFF_GATES = OFF_KVB + 2 * KV_B
N_MAIN = OFF_GATES + 2 * D_MODEL
GB_LANES = 128

F32_SUBLANES = 8
BF16_SUBLANES = 16
V7X_VMEM_BYTES = 64 * 1024 * 1024
VMEM_LIMIT = V7X_VMEM_BYTES - 8 * 1024 * 1024


def _params(n_grid):
    return pltpu.CompilerParams(dimension_semantics=("arbitrary",) * n_grid, vmem_limit_bytes=VMEM_LIMIT)


def _dot(a, b):
    return jnp.dot(a, b, preferred_element_type=F32)


def _dot_nt(a, b):
    return lax.dot_general(a, b, (((1,), (1,)), ((), ())), preferred_element_type=F32)


def _dot_tn(a, b):
    return lax.dot_general(a, b, (((0,), (0,)), ((), ())), preferred_element_type=F32)


def _sigmoid(x):
    return 1.0 / (1.0 + jnp.exp(-x))


def _silu(x):
    return x * _sigmoid(x)


def _const_spec(shape):
    nd = len(shape)
    return pl.BlockSpec(shape, lambda *_: (0,) * nd, pipeline_mode=pl.Buffered(1))


def _layer_spec(shape, layer):
    nd = len(shape)
    return pl.BlockSpec((None,) + tuple(shape), lambda *_: (layer,) + (0,) * nd, pipeline_mode=pl.Buffered(1))


def _cast_body(w_ref, o_ref):
    o_ref[...] = w_ref[...].astype(BF16)


def _cast_bf16(w):
    depth, k, n = w.shape
    bk = max(8, min(k, (1024 * 1024) // n))
    spec = pl.BlockSpec((None, bk, n), lambda l, i: (l, i, 0))
    return pl.pallas_call(
        _cast_body, grid=(depth, k // bk), in_specs=[spec], out_specs=spec,
        out_shape=jax.ShapeDtypeStruct(w.shape, BF16), compiler_params=_params(2), name="cast_bf16",
    )(w)


def _prep_w_in(w_in):
    o_ba = CONV_DIM + D_MODEL
    o_qb = o_ba + 2 * N_HEADS_A
    wm = jnp.concatenate([w_in[:, :, :o_ba], w_in[:, :, o_qb:]], axis=2).astype(BF16)
    wba = jnp.pad(w_in[:, :, o_ba:o_qb], ((0, 0), (0, 0), (0, GB_LANES - 2 * N_HEADS_A))).astype(BF16)
    return wm, wba


def _ada_body(c_ref, w_ref, b_ref, o_ref):
    s = _silu(c_ref[...]).astype(BF16)
    o_ref[...] = _dot(s, w_ref[...].astype(BF16)) + b_ref[...]


def _ada(c_all, w_ada, b_ada):
    depth = w_ada.shape[0]
    n_rows = c_all.shape[0]
    n_col = w_ada.shape[2] // D_MODEL
    return pl.pallas_call(
        _ada_body,
        grid=(depth, n_col),
        in_specs=[pl.BlockSpec((n_rows, D_MODEL), lambda l, n: (0, 0)),
                  pl.BlockSpec((None, D_MODEL, D_MODEL), lambda l, n: (l, 0, n)),
                  pl.BlockSpec((None, 1, D_MODEL), lambda l, n: (l, 0, n))],
        out_specs=pl.BlockSpec((None, n_rows, D_MODEL), lambda l, n: (l, 0, n)),
        out_shape=jax.ShapeDtypeStruct((depth, n_rows, w_ada.shape[2]), F32),
        compiler_params=_params(2),
        name="ada",
    )(c_all, w_ada, b_ada.reshape(depth, 1, -1))


def _bucket_table():
    r = np.arange(WINDOW)[:, None]
    c = np.arange(2 * WINDOW)[None, :]
    dist = WINDOW + r - c
    n = np.maximum(dist, 0)
    max_exact = N_BUCKETS // 2
    ratio = np.maximum(n, max_exact).astype(np.float32) / np.float32(max_exact)
    large = max_exact + (np.log(ratio) / np.float32(math.log(MAX_DISTANCE / max_exact))
                         * np.float32(N_BUCKETS - max_exact)).astype(np.int32)
    large = np.minimum(large, N_BUCKETS - 1)
    bucket = np.where(n < max_exact, n, large).astype(np.int32)
    valid = ((dist >= 0) & (dist < WINDOW)).astype(np.int32)
    return bucket, valid


def _bias_t_body(rb_ref, sink_ref, bucket_ref, valid_ref, o_ref):
    l = pl.program_id(0)
    h = pl.program_id(1)
    bucket = bucket_ref[...]
    acc = jnp.zeros(bucket.shape, F32)
    for j in range(N_BUCKETS):
        acc = jnp.where(bucket == j, rb_ref[j, h], acc)
    acc = jnp.where(valid_ref[...] > 0, acc, -jnp.inf)
    key = lax.broadcasted_iota(jnp.int32, bucket.shape, 0)
    o_ref[...] = jnp.where(key == 0, sink_ref[l, h], acc)


def _bias_table_t(rel_bias, sinks):
    depth = sinks.shape[0]
    bucket, valid = _bucket_table()
    return pl.pallas_call(
        _bias_t_body,
        grid=(depth, N_HEADS_B),
        in_specs=[pl.BlockSpec(memory_space=pltpu.SMEM), pl.BlockSpec(memory_space=pltpu.SMEM),
                  pl.BlockSpec((2 * WINDOW, WINDOW), lambda l, h: (0, 0)),
                  pl.BlockSpec((2 * WINDOW, WINDOW), lambda l, h: (0, 0))],
        out_specs=pl.BlockSpec((None, None, 2 * WINDOW, WINDOW), lambda l, h: (l, h // GROUP_B, 0, h % GROUP_B)),
        out_shape=jax.ShapeDtypeStruct((depth, N_KV_B, 2 * WINDOW, GROUP_B * WINDOW), F32),
        compiler_params=_params(2),
        name="bias_table_t",
    )(rel_bias.astype(F32), sinks.astype(F32), jnp.asarray(bucket.T.copy()), jnp.asarray(valid.T.copy()))


def _front_body(x_ref, mod_ref, st_ref, wm_ref, wba_ref, wconv_ref, gpar_ref,
                cs_ref, z_ref, gb_ref, qb_ref, k_ref, v_ref, gates_ref, cst_ref, cbuf, *, bt, tt):
    m = bt * tt
    pad, hist = F32_SUBLANES, CONV_W - 1
    sh1 = mod_ref[:, :, 0:D_MODEL]
    sc1 = mod_ref[:, :, D_MODEL:2 * D_MODEL]
    h = (x_ref[...] * (1.0 + sc1) + sh1).reshape(m, D_MODEL).astype(BF16)

    @pl.when(pl.program_id(1) == 0)
    def _():
        cbuf[:, 0:pad, :] = jnp.zeros((bt, pad, CONV_DIM), F32)
        cbuf[:, pad - hist:pad, :] = st_ref[...]

    def proj(off, n):
        return _dot(h, wm_ref[:, off:off + n])

    half = D_MODEL // 2

    def conv_piece(i):
        cols = slice(i * half, (i + 1) * half)
        cbuf[:, pad:pad + tt, cols] = proj(i * half, half).reshape(bt, tt, half)
        u = cbuf[:, :, cols]
        u1 = pltpu.roll(u, 1, 1)
        w0, w1, w2, w3 = (wconv_ref[j:j + 1, cols] for j in range(CONV_W))
        y = (u * w3 + u1 * w2) + pltpu.roll(u * w1 + u1 * w0, 2, 1)
        cs_ref[:, :, cols] = _silu(y[:, pad:, :]).astype(cs_ref.dtype)

    def z_piece(i):
        z_ref[:, :, i * half:(i + 1) * half] = proj(OFF_Z + i * half, half).astype(z_ref.dtype).reshape(bt, tt, half)

    def qb_piece(i):
        q = proj(OFF_QB + i * half, half) * (HD_B ** -0.5)
        qb_ref[:, :, i * half:(i + 1) * half] = q.astype(BF16).reshape(bt, tt, half)

    def kv_piece(_):
        kv = proj(OFF_KVB, 2 * KV_B)
        k_ref[...] = kv[:, :KV_B].reshape(bt, tt, KV_B)
        v_ref[...] = kv[:, KV_B:].reshape(bt, tt, KV_B)

    def gates_piece(i):
        g = _sigmoid(proj(OFF_GATES + i * half, half))
        gates_ref[:, :, i * half:(i + 1) * half] = g.astype(gates_ref.dtype).reshape(bt, tt, half)

    for piece, i in [(conv_piece, 0), (z_piece, 0), (conv_piece, 1), (z_piece, 1), (conv_piece, 2), (qb_piece, 0),
                     (conv_piece, 3), (qb_piece, 1), (conv_piece, 4), (kv_piece, 0), (gates_piece, 0),
                     (conv_piece, 5), (gates_piece, 1), (gates_piece, 2), (gates_piece, 3)]:
        piece(i)
    tail = cbuf[:, tt + pad - hist:tt + pad, :]
    cst_ref[...] = tail
    cbuf[:, pad - hist:pad, :] = tail

    ba = _dot(h, wba_ref[...])
    xg = ba + gpar_ref[1:2, :]
    softplus = jnp.maximum(xg, 0.0) + jnp.log(1.0 + jnp.exp(-jnp.abs(xg)))
    lane = lax.broadcasted_iota(jnp.int32, ba.shape, 1)
    gb = jnp.where(lane < N_HEADS_A, _sigmoid(ba), gpar_ref[0:1, :] * softplus)
    gb_ref[...] = gb.reshape(bt, tt, GB_LANES)


def _front(x, mod, conv_state, wm, wba, wconv, gpar, *, layer, bt, tt):
    b, l, _ = x.shape
    grid = (b // bt, l // tt)
    tok = lambda n, dt=F32: jax.ShapeDtypeStruct((b, l, n), dt)
    act = BF16 if tt % BF16_SUBLANES == 0 else F32
    tok_spec = lambda n: pl.BlockSpec((bt, tt, n), lambda i, t: (i, t, 0))
    return pl.pallas_call(
        functools.partial(_front_body, bt=bt, tt=tt),
        grid=grid,
        in_specs=[tok_spec(D_MODEL),
                  pl.BlockSpec((bt, 1, 6 * D_MODEL), lambda i, t: (i, 0, 0)),
                  pl.BlockSpec((bt, CONV_W - 1, CONV_DIM), lambda i, t: (i, 0, 0)),
                  _layer_spec((D_MODEL, N_MAIN), layer),
                  _layer_spec((D_MODEL, GB_LANES), layer),
                  _const_spec((CONV_W, CONV_DIM)),
                  _const_spec((2, GB_LANES))],
        out_specs=[tok_spec(CONV_DIM), tok_spec(D_MODEL), tok_spec(GB_LANES), tok_spec(D_MODEL),
                   tok_spec(KV_B), tok_spec(KV_B), tok_spec(2 * D_MODEL),
                   pl.BlockSpec((bt, CONV_W - 1, CONV_DIM), lambda i, t: (i, 0, 0))],
        out_shape=[tok(CONV_DIM, act), tok(D_MODEL, act), tok(GB_LANES), tok(D_MODEL, BF16),
                   tok(KV_B), tok(KV_B), tok(2 * D_MODEL, act),
                   jax.ShapeDtypeStruct((b, CONV_W - 1, CONV_DIM), F32)],
        scratch_shapes=[pltpu.VMEM((bt, tt + F32_SUBLANES, CONV_DIM), F32)],
        compiler_params=_params(2),
        name="front",
    )(x, mod, conv_state, wm, wba, wconv, gpar)


def _delta_masks(r, block):
    i = lax.broadcasted_iota(jnp.int32, (r, r), 0)
    j = lax.broadcasted_iota(jnp.int32, (r, r), 1)
    shift = int(math.log2(block))
    same = (i >> shift) == (j >> shift)
    levels = [((i >> (s + 1)) == (j >> (s + 1))) & ((i >> s) != (j >> s)) for s in range(shift)]
    return dict(tri=same & (i >= j), strict=same & (i > j), upper=same & (i < j), eye=(i == j).astype(F32),
                levels=levels)


def _split_dot(a01, x):
    a01 = a01.astype(BF16)
    hi = x.astype(BF16)
    rest = x - hi.astype(F32)
    mid = rest.astype(BF16)
    lo = (rest - mid.astype(F32)).astype(BF16)
    return _dot(a01, hi) + _dot(a01, mid) + _dot(a01, lo)


def _decay_sums(g, masks):
    return _split_dot(masks["tri"], g), _split_dot(masks["upper"], g)


def _delta_pre(problems, masks):
    tri, strict, eye, levels = masks["tri"], masks["strict"], masks["eye"], masks["levels"]
    n = len(problems)
    r = problems[0][0].shape[0]
    lhs, ks, rhs, decays, qgs, kds = [], [], [], [], [], []
    for qr, kr, v, beta_b, gc, rev, gc_row in problems:
        q = qr * lax.rsqrt(jnp.sum(qr * qr, -1, keepdims=True) + 1e-6) * (DK_A ** -0.5)
        k = kr * lax.rsqrt(jnp.sum(kr * kr, -1, keepdims=True) + 1e-6)
        gc_col = gc[:, :r] if r <= DK_A else jnp.concatenate([gc] * (r // DK_A), axis=1)
        decays.append(jnp.exp(jnp.where(tri, gc_col - gc_row, -jnp.inf)))
        eg = jnp.exp(gc)
        kb = k * beta_b
        lhs.append(jnp.concatenate([kb, q], axis=0).astype(BF16))
        ks.append(k.astype(BF16))
        rhs.append(jnp.concatenate([v * beta_b, kb * eg], axis=1).astype(BF16))
        qgs.append((q * eg).astype(BF16))
        kds.append((k * jnp.exp(rev)).astype(BF16))
    kk = [_dot_nt(lhs[i], ks[i]) for i in range(n)]
    mm = [jnp.where(strict, kk[i][:r] * decays[i], 0.0) for i in range(n)]
    qk = [(kk[i][r:] * decays[i]).astype(BF16) for i in range(n)]
    t = [eye - jnp.where(levels[0], mm[i], 0.0) for i in range(n)]
    for s, lvl in enumerate(levels[1:], start=1):
        blk = 2 ** s
        tb = [t[i].astype(BF16) for i in range(n)]
        mo = [jnp.where(lvl, mm[i], 0.0).astype(BF16) for i in range(n)]
        if blk % 8:
            x = [_dot(tb[i], mo[i]).astype(BF16) for i in range(n)]
            y = [_dot(x[i], tb[i]) for i in range(n)]
            t = [t[i] - y[i] for i in range(n)]
        else:
            split = [t[i].reshape(r // (2 * blk), 2, blk, r) for i in range(n)]
            lo = [split[i][:, 1].reshape(r // 2, r) for i in range(n)]
            x = [_dot(lo[i].astype(BF16), mo[i]).astype(BF16) for i in range(n)]
            y = [_dot(x[i], tb[i]) for i in range(n)]
            t = [jnp.stack([split[i][:, 0], (lo[i] - y[i]).reshape(r // (2 * blk), blk, r)], axis=1).reshape(r, r)
                 for i in range(n)]
    uw = [_dot(t[i].astype(BF16), rhs[i]) for i in range(n)]
    return [(uw[i][:, :DK_A], uw[i][:, DK_A:].astype(BF16), qgs[i], kds[i], qk[i]) for i in range(n)]


def _gated_norm(o, z, w_onorm):
    o = o * lax.rsqrt(jnp.mean(o * o, -1, keepdims=True) + RMS_EPS) * w_onorm
    return o * _silu(z)


def _delta_prompt_body(cs_ref, z_ref, gb_ref, wn_ref, o_ref, s_out_ref,
                       s_ref, u_s, o_s, wq_s, qkd_s, egl_s, *, tt, side_work=None):
    n_c = tt // CHUNK
    masks = _delta_masks(tt, CHUNK)

    @pl.when(pl.program_id(1) == 0)
    def _():
        s_ref[...] = jnp.zeros(s_ref.shape, F32)

    finish_side_work = side_work() if side_work is not None else None

    gbv = gb_ref[...]
    gc_all, rev_all = _decay_sums(gbv, masks)
    gc_t = gc_all.T
    heads = range(N_HEADS_A)
    hcols = [slice(hd * DK_A, (hd + 1) * DK_A) for hd in heads]
    problems = []
    for hd in heads:
        lg = N_HEADS_A + hd
        problems.append((cs_ref[:, hcols[hd]].astype(F32),
                         cs_ref[:, D_MODEL + hd * DK_A:D_MODEL + (hd + 1) * DK_A].astype(F32),
                         cs_ref[:, 2 * D_MODEL + hd * DK_A:2 * D_MODEL + (hd + 1) * DK_A].astype(F32),
                         jnp.broadcast_to(gbv[:, hd:hd + 1], (tt, DK_A)),
                         jnp.broadcast_to(gc_all[:, lg:lg + 1], (tt, DK_A)),
                         jnp.broadcast_to(rev_all[:, lg:lg + 1], (tt, DK_A)),
                         jnp.broadcast_to(gc_t[lg:lg + 1, :], (tt, tt))))
    for hd, (u, w, qg, kd, qk) in enumerate(_delta_pre(problems, masks)):
        u_s[:, hcols[hd]] = u
        kdt = kd.T
        for c in range(n_c):
            blk = slice(c * CHUNK, (c + 1) * CHUNK)
            idx = hd * n_c + c
            wq_s[idx, 0:CHUNK, :] = w[blk]
            wq_s[idx, CHUNK:2 * CHUNK, :] = qg[blk]
            qkd_s[idx, 0:CHUNK, :] = qk[blk, blk]
            qkd_s[idx, CHUNK:CHUNK + DK_A, :] = kdt[:, blk]
            last = (c + 1) * CHUNK - 1
            egl_s[idx] = jnp.broadcast_to(jnp.exp(problems[hd][4][last:last + 1, :]), (8, DK_A))

    if finish_side_work is not None:
        finish_side_work()

    states = [s_ref[hd] for hd in heads]
    for c in range(n_c):
        blk = slice(c * CHUNK, (c + 1) * CHUNK)
        idx = [hd * n_c + c for hd in heads]
        ws = [_dot(wq_s[idx[hd]], states[hd].astype(BF16)) for hd in heads]
        v_new = [(u_s[blk, hcols[hd]] - ws[hd][:CHUNK]).astype(BF16) for hd in heads]
        upd = [_dot(qkd_s[idx[hd]], v_new[hd]) for hd in heads]
        for hd in heads:
            o_s[blk, hcols[hd]] = ws[hd][CHUNK:] + upd[hd][:CHUNK]
            states[hd] = states[hd] * egl_s[idx[hd]][0:1, :] + upd[hd][CHUNK:]
    for hd in heads:
        s_ref[hd] = states[hd]
        o_ref[:, hcols[hd]] = _gated_norm(o_s[:, hcols[hd]], z_ref[:, hcols[hd]].astype(F32), wn_ref[...]).astype(BF16)

    @pl.when(pl.program_id(1) == pl.num_programs(1) - 1)
    def _():
        s_out_ref[...] = s_ref[...]


def _softmax_keys(s):
    p = jnp.exp(s - jnp.max(s, axis=0, keepdims=True))
    return (p / jnp.sum(p, axis=0, keepdims=True)).astype(BF16)


def _zero_key0(x):
    return jnp.where(lax.broadcasted_iota(jnp.int32, x.shape, 0) == 0, 0.0, x)


def _kv_cols(kvh):
    return slice(kvh * HD_B, (kvh + 1) * HD_B)


def _group_queries(q, kvh):
    return jnp.concatenate([q[:, hh * HD_B:(hh + 1) * HD_B] for hh in range(kvh * GROUP_B, (kvh + 1) * GROUP_B)],
                           axis=0)


def _swa_prompt_cache(kc_ref, vc_ref, ko_ref, vo_ref, nq):
    @pl.when(pl.program_id(1) == pl.num_programs(1) - 1)
    def _():
        ko_ref[...] = kc_ref[(nq - 1) * WINDOW:, :]
        vo_ref[...] = vc_ref[(nq - 1) * WINDOW:, :]


def _swa_prompt_start(q_ref, kp_ref, kc_ref, vp_ref, vc_ref, bias_ref, o_ref, nq):
    n_q = GROUP_B * WINDOW
    key = lax.broadcasted_iota(jnp.int32, (2 * WINDOW, n_q), 0)
    no_prev = (pl.program_id(1) == 0) & (key >= 1) & (key < WINDOW)
    kall = jnp.concatenate([kp_ref[...], kc_ref[...]], axis=0)
    vall = jnp.concatenate([vp_ref[...], vc_ref[...]], axis=0)
    probs = [(j, kvh) for j in range(nq) for kvh in range(N_KV_B)]
    k2 = [_zero_key0(kall[j * WINDOW:(j + 2) * WINDOW]).astype(BF16) for j in range(nq)]
    v2t = [_zero_key0(vall[j * WINDOW:(j + 2) * WINDOW]).T.astype(BF16) for j in range(nq)]
    q = [q_ref[j * WINDOW:(j + 1) * WINDOW, :] for j in range(nq)]
    s = [_dot_nt(k2[j][:, _kv_cols(kvh)], _group_queries(q[j], kvh)) for j, kvh in probs]
    pn = []
    for (j, kvh), sc in zip(probs, s):
        sc = sc + bias_ref[kvh]
        pn.append(_softmax_keys(jnp.where(no_prev, -jnp.inf, sc) if j == 0 else sc))

    def finish():
        ot = [_dot(v2t[j][_kv_cols(kvh), :], p) for (j, kvh), p in zip(probs, pn)]
        for (j, kvh), o in zip(probs, ot):
            for g in range(GROUP_B):
                hh = kvh * GROUP_B + g
                o_ref[j * WINDOW:(j + 1) * WINDOW, hh * HD_B:(hh + 1) * HD_B] = \
                    o[:, g * WINDOW:(g + 1) * WINDOW].T.astype(BF16)

    return finish


def _mixer_prompt_body(cs_ref, z_ref, gb_ref, wn_ref, q_ref, kp_ref, kc_ref, vp_ref, vc_ref, bias_ref,
                       oa_ref, s_out_ref, ob_ref, ko_ref, vo_ref, *scratch, tt):
    nq = tt // WINDOW
    _swa_prompt_cache(kc_ref, vc_ref, ko_ref, vo_ref, nq)
    attention = functools.partial(_swa_prompt_start, q_ref, kp_ref, kc_ref, vp_ref, vc_ref, bias_ref, ob_ref, nq)
    _delta_prompt_body(cs_ref, z_ref, gb_ref, wn_ref, oa_ref, s_out_ref, *scratch, tt=tt, side_work=attention)


def _mixer_prompt(cs, z, gb, w_onorm, qb, k, v, bias_t, *, tt):
    b, l, _ = cs.shape
    n_c = tt // CHUNK
    nq = tt // WINDOW
    tok_spec = lambda n: pl.BlockSpec((None, tt, n), lambda i, t: (i, t, 0))
    prev = lambda n: pl.BlockSpec((None, WINDOW, n), lambda i, t: (i, jnp.maximum(nq * t - 1, 0), 0))
    last = lambda n: pl.BlockSpec((None, WINDOW, n), lambda i, t: (i, 0, 0))
    return pl.pallas_call(
        functools.partial(_mixer_prompt_body, tt=tt),
        grid=(b, l // tt),
        in_specs=[tok_spec(CONV_DIM), tok_spec(D_MODEL), tok_spec(GB_LANES), _const_spec((1, DK_A)),
                  tok_spec(D_MODEL), prev(KV_B), tok_spec(KV_B), prev(KV_B), tok_spec(KV_B),
                  _const_spec((N_KV_B, 2 * WINDOW, GROUP_B * WINDOW))],
        out_specs=[tok_spec(D_MODEL), pl.BlockSpec((None, N_HEADS_A, DK_A, DK_A), lambda i, t: (i, 0, 0, 0)),
                   tok_spec(D_MODEL), last(KV_B), last(KV_B)],
        out_shape=[jax.ShapeDtypeStruct((b, l, D_MODEL), BF16),
                   jax.ShapeDtypeStruct((b, N_HEADS_A, DK_A, DK_A), F32),
                   jax.ShapeDtypeStruct((b, l, D_MODEL), BF16),
                   jax.ShapeDtypeStruct((b, WINDOW, KV_B), F32), jax.ShapeDtypeStruct((b, WINDOW, KV_B), F32)],
        scratch_shapes=[pltpu.VMEM((N_HEADS_A, DK_A, DK_A), F32),
                        pltpu.VMEM((tt, D_MODEL), F32), pltpu.VMEM((tt, D_MODEL), F32),
                        pltpu.VMEM((N_HEADS_A * n_c, 2 * CHUNK, DK_A), BF16),
                        pltpu.VMEM((N_HEADS_A * n_c, CHUNK + DK_A, CHUNK), BF16),
                        pltpu.VMEM((N_HEADS_A * n_c, 8, DK_A), F32)],
        compiler_params=_params(2),
        name="mixer_prompt",
    )(cs, z, gb, w_onorm, qb, k, k, v, v, bias_t)


def _mixer_sample_body(cs_ref, z_ref, gb_ref, wn_ref, s0_ref, q_ref, kn_ref, vn_ref, kc_ref, vc_ref, bias_ref, *rest,
                       bt, nb, ls, fill_layers):
    oa_ref, s_out_ref, ob_ref, ko_ref, vo_ref = rest[-5:]

    def put(ref, idx, val):
        idx = idx if isinstance(idx, tuple) else (idx,)
        if fill_layers:
            for layer_slot in range(fill_layers):
                ref[(layer_slot,) + idx] = val
        else:
            ref[idx] = val
    r = N_HEADS_A * ls
    masks = _delta_masks(r, ls)
    heads = range(N_HEADS_A)
    hrows = [slice(hd * ls, (hd + 1) * ls) for hd in heads]
    hcols = [slice(hd * DK_A, (hd + 1) * DK_A) for hd in heads]
    kvs = range(N_KV_B)

    def per_group(gi, carry):
        bis = [gi * nb + d for d in range(nb)]
        ks, vs, scores = [], [], []
        for bi in bis:
            kf = jnp.concatenate([kc_ref[bi], kn_ref[bi]], axis=0)
            vf = jnp.concatenate([vc_ref[bi], vn_ref[bi]], axis=0)
            put(ko_ref, bi, kf[ls:, :])
            put(vo_ref, bi, vf[ls:, :])
            ks.append(_zero_key0(kf).astype(BF16))
            vs.append(_zero_key0(vf).astype(BF16))
        for d, bi in enumerate(bis):
            q = q_ref[bi]
            scores.append([_dot_nt(ks[d][:, _kv_cols(kvh)], _group_queries(q, kvh)) for kvh in kvs])
        pn = [[_softmax_keys(scores[d][kvh] + bias_ref[kvh]) for kvh in kvs] for d in range(nb)]

        problems, egl = [], []
        for bi in bis:
            cs = cs_ref[bi].astype(F32)
            gbv = gb_ref[bi]
            stack = lambda base: jnp.concatenate([cs[:, base + hd * DK_A: base + (hd + 1) * DK_A] for hd in heads],
                                                 axis=0)
            beta_b = jnp.concatenate([jnp.broadcast_to(gbv[:, hd:hd + 1], (ls, DK_A)) for hd in heads], axis=0)
            g_b = jnp.concatenate(
                [jnp.broadcast_to(gbv[:, N_HEADS_A + hd:N_HEADS_A + hd + 1], (ls, DK_A)) for hd in heads], axis=0)
            gc, rev = _decay_sums(g_b, masks)
            egl.append(jnp.exp(gc))
            problems.append((stack(0), stack(D_MODEL), stack(2 * D_MODEL), beta_b, gc, rev, gc.T[:r, :]))
        pre = _delta_pre(problems, masks)
        both = [[_dot(jnp.concatenate([pre[d][1][hrows[hd]], pre[d][2][hrows[hd]]], axis=0),
                      s0_ref[bis[d], hd].astype(BF16)) for hd in heads] for d in range(nb)]
        v_new = [(pre[d][0] - jnp.concatenate([both[d][hd][:ls] for hd in heads], axis=0)).astype(BF16)
                 for d in range(nb)]
        qkv = [_dot(pre[d][4], v_new[d]) for d in range(nb)]
        upd = [[_dot_tn(pre[d][3][hrows[hd]], v_new[d][hrows[hd]]) for hd in heads] for d in range(nb)]
        att = [[_dot_tn(pn[d][kvh], vs[d][:, _kv_cols(kvh)]) for kvh in kvs] for d in range(nb)]
        for d, bi in enumerate(bis):
            zb = z_ref[bi].astype(F32)
            for hd in heads:
                last = (hd + 1) * ls - 1
                put(s_out_ref, (bi, hd), s0_ref[bi, hd] * egl[d][last:last + 1, :] + upd[d][hd])
                o = both[d][hd][ls:] + qkv[d][hrows[hd]]
                oa_ref[bi, :, hcols[hd]] = _gated_norm(o, zb[:, hcols[hd]], wn_ref[...]).astype(BF16)
            for kvh in kvs:
                for g in range(GROUP_B):
                    hh = kvh * GROUP_B + g
                    ob_ref[bi, :, hh * HD_B:(hh + 1) * HD_B] = att[d][kvh][g * ls:(g + 1) * ls].astype(BF16)
        return carry

    lax.fori_loop(0, bt // nb, per_group, 0)


def _mixer_sample(cs, z, gb, w_onorm, s0_all, qb, k_new, v_new, k_cache_all, v_cache_all, bias_s, prev, *, layer, bt):
    b, ls, _ = cs.shape
    tok_spec = lambda n: pl.BlockSpec((bt, ls, n), lambda i: (i, 0, 0))
    st_spec = pl.BlockSpec((None, bt, N_HEADS_A, DK_A, DK_A), lambda i: (layer, i, 0, 0, 0))
    cache = pl.BlockSpec((None, bt, WINDOW, KV_B), lambda i: (layer, i, 0, 0))
    extra, extra_specs = ([], []) if prev is None else (list(prev), [pl.BlockSpec(memory_space=pl.ANY)] * 3)
    depth = s0_all.shape[0]
    if extra:
        st_out, cache_out = st_spec, cache
    else:
        st_out = pl.BlockSpec((depth, bt, N_HEADS_A, DK_A, DK_A), lambda i: (0, i, 0, 0, 0))
        cache_out = pl.BlockSpec((depth, bt, WINDOW, KV_B), lambda i: (0, i, 0, 0))
    n_in = 11
    return pl.pallas_call(
        functools.partial(_mixer_sample_body, bt=bt, nb=min(bt, MIXER_SAMPLE_ROWS_PER_TRIP), ls=ls,
                          fill_layers=0 if extra else depth),
        grid=(b // bt,),
        in_specs=[tok_spec(CONV_DIM), tok_spec(D_MODEL), tok_spec(GB_LANES), _const_spec((1, DK_A)), st_spec,
                  tok_spec(D_MODEL), tok_spec(KV_B), tok_spec(KV_B), cache, cache,
                  _const_spec((N_KV_B, WINDOW + ls, GROUP_B * ls))] + extra_specs,
        out_specs=[tok_spec(D_MODEL), st_out, tok_spec(D_MODEL), cache_out, cache_out],
        out_shape=[jax.ShapeDtypeStruct((b, ls, D_MODEL), BF16), jax.ShapeDtypeStruct(s0_all.shape, F32),
                   jax.ShapeDtypeStruct((b, ls, D_MODEL), BF16),
                   jax.ShapeDtypeStruct(k_cache_all.shape, F32), jax.ShapeDtypeStruct(v_cache_all.shape, F32)],
        input_output_aliases={n_in: 1, n_in + 1: 3, n_in + 2: 4} if extra else {},
        compiler_params=_params(1),
        name="mixer_sample",
    )(cs, z, gb, w_onorm, s0_all, qb, k_new, v_new, k_cache_all, v_cache_all, bias_s, *extra)


def _layer_norm(y, g, b):
    mu = jnp.mean(y, -1, keepdims=True)
    d = y - mu
    var = jnp.mean(d * d, -1, keepdims=True)
    return d * lax.rsqrt(var + LN_EPS) * g + b


def _tail_body(x_ref, oa_ref, ob_ref, gates_ref, mod_ref, wpa_ref, wpb_ref, wout_ref, ln1g_ref, ln1b_ref,
               wup_ref, wdn_ref, ln2g_ref, ln2b_ref, o_ref, *, bt, tt, alpha):
    m = bt * tt
    mod = lambda i: mod_ref[:, :, i * D_MODEL:(i + 1) * D_MODEL]
    ga = gates_ref[:, :, 0:D_MODEL].astype(F32).reshape(m, D_MODEL)
    gb = gates_ref[:, :, D_MODEL:2 * D_MODEL].astype(F32).reshape(m, D_MODEL)
    mixed = ga * _dot(oa_ref[...].reshape(m, D_MODEL), wpa_ref[...]) \
        + gb * _dot(ob_ref[...].reshape(m, D_MODEL), wpb_ref[...])
    attn = _dot(mixed.astype(BF16), wout_ref[...]).reshape(bt, tt, D_MODEL)
    x1 = _layer_norm(alpha * x_ref[...] + mod(2) * attn, ln1g_ref[...], ln1b_ref[...])
    h2 = (x1 * (1.0 + mod(4)) + mod(3)).reshape(m, D_MODEL).astype(BF16)
    ff = jnp.zeros((m, D_MODEL), F32)
    for c in range(D_FF // D_MODEL):
        cols = slice(c * D_MODEL, (c + 1) * D_MODEL)
        a = jnp.maximum(_dot(h2, wup_ref[:, cols]), 0.0)
        ff = ff + _dot((a * a).astype(BF16), wdn_ref[cols, :])
    o_ref[...] = _layer_norm(alpha * x1 + mod(5) * ff.reshape(bt, tt, D_MODEL), ln2g_ref[...], ln2b_ref[...])


def _tail(x, oa, ob, gates, mod, w, p, *, layer, bt, tt, alpha):
    b, l, _ = x.shape
    tok_spec = lambda n: pl.BlockSpec((bt, tt, n), lambda i, t: (i, t, 0))
    row = _const_spec((1, D_MODEL))
    sq = _layer_spec((D_MODEL, D_MODEL), layer)
    return pl.pallas_call(
        functools.partial(_tail_body, bt=bt, tt=tt, alpha=alpha),
        grid=(b // bt, l // tt),
        in_specs=[tok_spec(D_MODEL), tok_spec(D_MODEL), tok_spec(D_MODEL), tok_spec(2 * D_MODEL),
                  pl.BlockSpec((bt, 1, 6 * D_MODEL), lambda i, t: (i, 0, 0)),
                  sq, sq, sq, row, row,
                  _layer_spec((D_MODEL, D_FF), layer), _layer_spec((D_FF, D_MODEL), layer), row, row],
        out_specs=tok_spec(D_MODEL),
        out_shape=jax.ShapeDtypeStruct((b, l, D_MODEL), F32),
        compiler_params=_params(2),
        name="tail",
    )(x, oa, ob, gates, mod, w["wpa"], w["wpb"], w["wout"], p["ln1_g"], p["ln1_b"],
      w["wup"], w["wdn"], p["ln2_g"], p["ln2_b"])


def _tiles(b, l, rows):
    tt = min(l, rows)
    bt = max(1, min(b, rows // tt))
    return bt, tt


FRONT_ROWS_LONG, FRONT_ROWS_SHORT = 512, 256
MIXER_PROMPT_ROWS = 2 * WINDOW
MIXER_SAMPLE_BATCH_ROWS = 8
MIXER_SAMPLE_ROWS_PER_TRIP = 4
TAIL_ROWS = 512


def _layer_params(l, w_conv, a_log, dt_bias, w_onorm, ln1_g, ln1_b, ln2_g, ln2_b):
    gpar = jnp.zeros((2, GB_LANES), F32)
    gpar = gpar.at[0, N_HEADS_A:2 * N_HEADS_A].set(-jnp.exp(a_log[l].astype(F32)))
    gpar = gpar.at[1, N_HEADS_A:2 * N_HEADS_A].set(dt_bias[l].astype(F32))
    row = lambda a: a[l].reshape(1, -1).astype(F32)
    return dict(wconv=w_conv[l].astype(F32), gpar=gpar, w_onorm=row(w_onorm), ln1_g=row(ln1_g), ln1_b=row(ln1_b),
                ln2_g=row(ln2_g), ln2_b=row(ln2_b))


def _trunk_layer(x, mod, layer, w, p, alpha, conv_state, sample_state):
    b, l, _ = x.shape
    bt, tt = _tiles(b, l, FRONT_ROWS_LONG if l % FRONT_ROWS_LONG == 0 else FRONT_ROWS_SHORT)
    cs, z, gb, qb, k, v, gates, conv_out = _front(x, mod, conv_state, w["wm"], w["wba"], p["wconv"], p["gpar"],
                                                  layer=layer, bt=bt, tt=tt)
    if sample_state is None:
        oa, s_new, ob, k_new, v_new = _mixer_prompt(cs, z, gb, p["w_onorm"], qb, k, v, p["bias_t"],
                                                    tt=min(l, MIXER_PROMPT_ROWS))
    else:
        s0_all, kc_all, vc_all, s_prev, k_prev, v_prev = sample_state
        bias_s = p["bias_t"][:, :WINDOW + l, :].reshape(N_KV_B, WINDOW + l, GROUP_B, WINDOW)[..., :l]
        bias_s = bias_s.reshape(N_KV_B, WINDOW + l, GROUP_B * l)
        prev = None if s_prev is None else (s_prev, k_prev, v_prev)
        oa, s_new, ob, k_new, v_new = _mixer_sample(cs, z, gb, p["w_onorm"], s0_all, qb, k, v, kc_all, vc_all, bias_s,
                                                    prev, layer=layer, bt=min(b, MIXER_SAMPLE_BATCH_ROWS))
    bt, tt = _tiles(b, l, TAIL_ROWS)
    x2 = _tail(x, oa, ob, gates, mod, w, p, layer=layer, bt=bt, tt=tt, alpha=alpha)
    return x2, (s_new, conv_out, k_new, v_new)


def kernel(x_prompt, x_sample, state_delta, state_conv, cache_k, cache_v, c_prompt, c_sample, rel_bias, w_ada, b_ada, w_in, w_conv, a_log, dt_bias, w_onorm, sinks, w_pa, w_pb, w_out, ln1_g, ln1_b, w_up, w_down, ln2_g, ln2_b):
    depth = w_in.shape[0]
    alpha = (2 * depth) ** 0.25
    bp = x_prompt.shape[0]
    mod_all = _ada(jnp.concatenate([c_prompt, c_sample], axis=0), w_ada, b_ada)
    bias_t = _bias_table_t(rel_bias, sinks)
    wm, wba = _prep_w_in(w_in)
    w = dict(wm=wm, wba=wba, wpa=_cast_bf16(w_pa), wpb=_cast_bf16(w_pb), wout=_cast_bf16(w_out),
             wup=_cast_bf16(w_up), wdn=_cast_bf16(w_down))
    bs = x_sample.shape[0]
    kc_all = cache_k.reshape(depth, bs, WINDOW, KV_B)
    vc_all = cache_v.reshape(depth, bs, WINDOW, KV_B)
    yp, ys = x_prompt, x_sample
    prompt_outs = [[] for _ in range(4)]
    sample_conv = []
    s_all = k_all = v_all = None
    for l in range(depth):
        p = _layer_params(l, w_conv, a_log, dt_bias, w_onorm, ln1_g, ln1_b, ln2_g, ln2_b)
        p["bias_t"] = bias_t[l]
        mod_p = mod_all[l, :bp][:, None, :]
        mod_s = mod_all[l, bp:][:, None, :]
        zero_conv = jnp.zeros((bp, CONV_W - 1, CONV_DIM), x_prompt.dtype)
        yp, rest_p = _trunk_layer(yp, mod_p, l, w, p, alpha, zero_conv, None)
        ys, (s_all, conv_s, k_all, v_all) = _trunk_layer(ys, mod_s, l, w, p, alpha, state_conv[l],
                                                         (state_delta, kc_all, vc_all, s_all, k_all, v_all))
        for acc, val in zip(prompt_outs, rest_p):
            acc.append(val)
        sample_conv.append(conv_s)
    heads = lambda a: a.reshape(a.shape[:-1] + (N_KV_B, HD_B))
    pd, pc, pk, pv = (jnp.stack(a) for a in prompt_outs)
    return (yp, ys, pd, pc, heads(pk), heads(pv), s_all, jnp.stack(sample_conv), heads(k_all), heads(v_all))
```

```python
import functools
import math

import numpy as np
import jax
import jax.numpy as jnp
from jax import lax
from jax.experimental import pallas as pl
from jax.experimental.pallas import tpu as pltpu

F32 = jnp.float32
BF16 = jnp.bfloat16

D_MODEL = 1024
N_HEADS_A = 8
DK_A = 128
CONV_W = 4
CONV_DIM = 3 * D_MODEL
CHUNK = 64
HD_B = 64
N_HEADS_B = 16
N_KV_B = 4
GROUP_B = N_HEADS_B // N_KV_B
KV_B = N_KV_B * HD_B
WINDOW = 128
N_BUCKETS = 32
MAX_DISTANCE = 128
D_FF = 4 * D_MODEL
LN_EPS = 1e-5
RMS_EPS = 1e-6

OFF_Z = CONV_DIM
OFF_QB = OFF_Z + D_MODEL
OFF_KVB = OFF_QB + D_MODEL
OFF_GATES = OFF_KVB + 2 * KV_B
N_MAIN = OFF_GATES + 2 * D_MODEL
GB_LANES = 128

F32_SUBLANES = 8
BF16_SUBLANES = 16
V7X_VMEM_BYTES = 64 * 1024 * 1024
VMEM_LIMIT = V7X_VMEM_BYTES - 8 * 1024 * 1024


def _params(n_grid):
    return pltpu.CompilerParams(dimension_semantics=("arbitrary",) * n_grid, vmem_limit_bytes=VMEM_LIMIT)


def _dot(a, b):
    return jnp.dot(a, b, preferred_element_type=F32)


def _dot_nt(a, b):
    return lax.dot_general(a, b, (((1,), (1,)), ((), ())), preferred_element_type=F32)


def _dot_tn(a, b):
    return lax.dot_general(a, b, (((0,), (0,)), ((), ())), preferred_element_type=F32)


def _sigmoid(x):
    return 1.0 / (1.0 + jnp.exp(-x))


def _silu(x):
    return x * _sigmoid(x)


def _const_spec(shape):
    nd = len(shape)
    return pl.BlockSpec(shape, lambda *_: (0,) * nd, pipeline_mode=pl.Buffered(1))


def _layer_spec(shape, layer):
    nd = len(shape)
    return pl.BlockSpec((None,) + tuple(shape), lambda *_: (layer,) + (0,) * nd, pipeline_mode=pl.Buffered(1))


def _cast_body(w_ref, o_ref):
    o_ref[...] = w_ref[...].astype(BF16)


def _cast_bf16(w):
    depth, k, n = w.shape
    bk = max(8, min(k, (1024 * 1024) // n))
    spec = pl.BlockSpec((None, bk, n), lambda l, i: (l, i, 0))
    return pl.pallas_call(
        _cast_body, grid=(depth, k // bk), in_specs=[spec], out_specs=spec,
        out_shape=jax.ShapeDtypeStruct(w.shape, BF16), compiler_params=_params(2), name="cast_bf16",
    )(w)


def _prep_w_in(w_in):
    o_ba = CONV_DIM + D_MODEL
    o_qb = o_ba + 2 * N_HEADS_A
    wm = jnp.concatenate([w_in[:, :, :o_ba], w_in[:, :, o_qb:]], axis=2).astype(BF16)
    wba = jnp.pad(w_in[:, :, o_ba:o_qb], ((0, 0), (0, 0), (0, GB_LANES - 2 * N_HEADS_A))).astype(BF16)
    return wm, wba


def _ada_body(c_ref, w_ref, b_ref, o_ref):
    s = _silu(c_ref[...]).astype(BF16)
    o_ref[...] = _dot(s, w_ref[...].astype(BF16)) + b_ref[...]


def _ada(c_all, w_ada, b_ada):
    depth = w_ada.shape[0]
    n_rows = c_all.shape[0]
    n_col = w_ada.shape[2] // D_MODEL
    return pl.pallas_call(
        _ada_body,
        grid=(depth, n_col),
        in_specs=[pl.BlockSpec((n_rows, D_MODEL), lambda l, n: (0, 0)),
                  pl.BlockSpec((None, D_MODEL, D_MODEL), lambda l, n: (l, 0, n)),
                  pl.BlockSpec((None, 1, D_MODEL), lambda l, n: (l, 0, n))],
        out_specs=pl.BlockSpec((None, n_rows, D_MODEL), lambda l, n: (l, 0, n)),
        out_shape=jax.ShapeDtypeStruct((depth, n_rows, w_ada.shape[2]), F32),
        compiler_params=_params(2),
        name="ada",
    )(c_all, w_ada, b_ada.reshape(depth, 1, -1))


def _bucket_table():
    r = np.arange(WINDOW)[:, None]
    c = np.arange(2 * WINDOW)[None, :]
    dist = WINDOW + r - c
    n = np.maximum(dist, 0)
    max_exact = N_BUCKETS // 2
    ratio = np.maximum(n, max_exact).astype(np.float32) / np.float32(max_exact)
    large = max_exact + (np.log(ratio) / np.float32(math.log(MAX_DISTANCE / max_exact))
                         * np.float32(N_BUCKETS - max_exact)).astype(np.int32)
    large = np.minimum(large, N_BUCKETS - 1)
    bucket = np.where(n < max_exact, n, large).astype(np.int32)
    valid = ((dist >= 0) & (dist < WINDOW)).astype(np.int32)
    return bucket, valid


def _bias_t_body(rb_ref, sink_ref, bucket_ref, valid_ref, o_ref):
    l = pl.program_id(0)
    h = pl.program_id(1)
    bucket = bucket_ref[...]
    acc = jnp.zeros(bucket.shape, F32)
    for j in range(N_BUCKETS):
        acc = jnp.where(bucket == j, rb_ref[j, h], acc)
    acc = jnp.where(valid_ref[...] > 0, acc, -jnp.inf)
    key = lax.broadcasted_iota(jnp.int32, bucket.shape, 0)
    o_ref[...] = jnp.where(key == 0, sink_ref[l, h], acc)


def _bias_table_t(rel_bias, sinks):
    depth = sinks.shape[0]
    bucket, valid = _bucket_table()
    return pl.pallas_call(
        _bias_t_body,
        grid=(depth, N_HEADS_B),
        in_specs=[pl.BlockSpec(memory_space=pltpu.SMEM), pl.BlockSpec(memory_space=pltpu.SMEM),
                  pl.BlockSpec((2 * WINDOW, WINDOW), lambda l, h: (0, 0)),
                  pl.BlockSpec((2 * WINDOW, WINDOW), lambda l, h: (0, 0))],
        out_specs=pl.BlockSpec((None, None, 2 * WINDOW, WINDOW), lambda l, h: (l, h // GROUP_B, 0, h % GROUP_B)),
        out_shape=jax.ShapeDtypeStruct((depth, N_KV_B, 2 * WINDOW, GROUP_B * WINDOW), F32),
        compiler_params=_params(2),
        name="bias_table_t",
    )(rel_bias.astype(F32), sinks.astype(F32), jnp.asarray(bucket.T.copy()), jnp.asarray(valid.T.copy()))


def _front_body(x_ref, mod_ref, st_ref, wm_ref, wba_ref, wconv_ref, gpar_ref,
                cs_ref, z_ref, gb_ref, qb_ref, k_ref, v_ref, gates_ref, cst_ref, cbuf, *, bt, tt):
    m = bt * tt
    pad, hist = F32_SUBLANES, CONV_W - 1
    sh1 = mod_ref[:, :, 0:D_MODEL]
    sc1 = mod_ref[:, :, D_MODEL:2 * D_MODEL]
    h = (x_ref[...] * (1.0 + sc1) + sh1).reshape(m, D_MODEL).astype(BF16)

    @pl.when(pl.program_id(1) == 0)
    def _():
        cbuf[:, 0:pad, :] = jnp.zeros((bt, pad, CONV_DIM), F32)
        cbuf[:, pad - hist:pad, :] = st_ref[...]

    def proj(off, n):
        return _dot(h, wm_ref[:, off:off + n])

    half = D_MODEL // 2

    def conv_piece(i):
        cols = slice(i * half, (i + 1) * half)
        cbuf[:, pad:pad + tt, cols] = proj(i * half, half).reshape(bt, tt, half)
        u = cbuf[:, :, cols]
        u1 = pltpu.roll(u, 1, 1)
        w0, w1, w2, w3 = (wconv_ref[j:j + 1, cols] for j in range(CONV_W))
        y = (u * w3 + u1 * w2) + pltpu.roll(u * w1 + u1 * w0, 2, 1)
        cs_ref[:, :, cols] = _silu(y[:, pad:, :]).astype(cs_ref.dtype)

    def z_piece(i):
        z_ref[:, :, i * half:(i + 1) * half] = proj(OFF_Z + i * half, half).astype(z_ref.dtype).reshape(bt, tt, half)

    def qb_piece(i):
        q = proj(OFF_QB + i * half, half) * (HD_B ** -0.5)
        qb_ref[:, :, i * half:(i + 1) * half] = q.astype(BF16).reshape(bt, tt, half)

    def kv_piece(_):
        kv = proj(OFF_KVB, 2 * KV_B)
        k_ref[...] = kv[:, :KV_B].reshape(bt, tt, KV_B)
        v_ref[...] = kv[:, KV_B:].reshape(bt, tt, KV_B)

    def gates_piece(i):
        g = _sigmoid(proj(OFF_GATES + i * half, half))
        gates_ref[:, :, i * half:(i + 1) * half] = g.astype(gates_ref.dtype).reshape(bt, tt, half)

    for piece, i in [(conv_piece, 0), (z_piece, 0), (conv_piece, 1), (z_piece, 1), (conv_piece, 2), (qb_piece, 0),
                     (conv_piece, 3), (qb_piece, 1), (conv_piece, 4), (kv_piece, 0), (gates_piece, 0),
                     (conv_piece, 5), (gates_piece, 1), (gates_piece, 2), (gates_piece, 3)]:
        piece(i)
    tail = cbuf[:, tt + pad - hist:tt + pad, :]
    cst_ref[...] = tail
    cbuf[:, pad - hist:pad, :] = tail

    ba = _dot(h, wba_ref[...])
    xg = ba + gpar_ref[1:2, :]
    softplus = jnp.maximum(xg, 0.0) + jnp.log(1.0 + jnp.exp(-jnp.abs(xg)))
    lane = lax.broadcasted_iota(jnp.int32, ba.shape, 1)
    gb = jnp.where(lane < N_HEADS_A, _sigmoid(ba), gpar_ref[0:1, :] * softplus)
    gb_ref[...] = gb.reshape(bt, tt, GB_LANES)


def _front(x, mod, conv_state, wm, wba, wconv, gpar, *, layer, bt, tt):
    b, l, _ = x.shape
    grid = (b // bt, l // tt)
    tok = lambda n, dt=F32: jax.ShapeDtypeStruct((b, l, n), dt)
    act = BF16 if tt % BF16_SUBLANES == 0 else F32
    tok_spec = lambda n: pl.BlockSpec((bt, tt, n), lambda i, t: (i, t, 0))
    return pl.pallas_call(
        functools.partial(_front_body, bt=bt, tt=tt),
        grid=grid,
        in_specs=[tok_spec(D_MODEL),
                  pl.BlockSpec((bt, 1, 6 * D_MODEL), lambda i, t: (i, 0, 0)),
                  pl.BlockSpec((bt, CONV_W - 1, CONV_DIM), lambda i, t: (i, 0, 0)),
                  _layer_spec((D_MODEL, N_MAIN), layer),
                  _layer_spec((D_MODEL, GB_LANES), layer),
                  _const_spec((CONV_W, CONV_DIM)),
                  _const_spec((2, GB_LANES))],
        out_specs=[tok_spec(CONV_DIM), tok_spec(D_MODEL), tok_spec(GB_LANES), tok_spec(D_MODEL),
                   tok_spec(KV_B), tok_spec(KV_B), tok_spec(2 * D_MODEL),
                   pl.BlockSpec((bt, CONV_W - 1, CONV_DIM), lambda i, t: (i, 0, 0))],
        out_shape=[tok(CONV_DIM, act), tok(D_MODEL, act), tok(GB_LANES), tok(D_MODEL, BF16),
                   tok(KV_B), tok(KV_B), tok(2 * D_MODEL, act),
                   jax.ShapeDtypeStruct((b, CONV_W - 1, CONV_DIM), F32)],
        scratch_shapes=[pltpu.VMEM((bt, tt + F32_SUBLANES, CONV_DIM), F32)],
        compiler_params=_params(2),
        name="front",
    )(x, mod, conv_state, wm, wba, wconv, gpar)


def _delta_masks(r, block):
    i = lax.broadcasted_iota(jnp.int32, (r, r), 0)
    j = lax.broadcasted_iota(jnp.int32, (r, r), 1)
    shift = int(math.log2(block))
    same = (i >> shift) == (j >> shift)
    levels = [((i >> (s + 1)) == (j >> (s + 1))) & ((i >> s) != (j >> s)) for s in range(shift)]
    return dict(tri=same & (i >= j), strict=same & (i > j), upper=same & (i < j), eye=(i == j).astype(F32),
                levels=levels)


def _split_dot(a01, x):
    a01 = a01.astype(BF16)
    hi = x.astype(BF16)
    rest = x - hi.astype(F32)
    mid = rest.astype(BF16)
    lo = (rest - mid.astype(F32)).astype(BF16)
    return _dot(a01, hi) + _dot(a01, mid) + _dot(a01, lo)


def _decay_sums(g, masks):
    return _split_dot(masks["tri"], g), _split_dot(masks["upper"], g)


def _delta_pre(problems, masks):
    tri, strict, eye, levels = masks["tri"], masks["strict"], masks["eye"], masks["levels"]
    n = len(problems)
    r = problems[0][0].shape[0]
    lhs, ks, rhs, decays, qgs, kds = [], [], [], [], [], []
    for qr, kr, v, beta_b, gc, rev, gc_row in problems:
        q = qr * lax.rsqrt(jnp.sum(qr * qr, -1, keepdims=True) + 1e-6) * (DK_A ** -0.5)
        k = kr * lax.rsqrt(jnp.sum(kr * kr, -1, keepdims=True) + 1e-6)
        gc_col = gc[:, :r] if r <= DK_A else jnp.concatenate([gc] * (r // DK_A), axis=1)
        decays.append(jnp.exp(jnp.where(tri, gc_col - gc_row, -jnp.inf)))
        eg = jnp.exp(gc)
        kb = k * beta_b
        lhs.append(jnp.concatenate([kb, q], axis=0).astype(BF16))
        ks.append(k.astype(BF16))
        rhs.append(jnp.concatenate([v * beta_b, kb * eg], axis=1).astype(BF16))
        qgs.append((q * eg).astype(BF16))
        kds.append((k * jnp.exp(rev)).astype(BF16))
    kk = [_dot_nt(lhs[i], ks[i]) for i in range(n)]
    mm = [jnp.where(strict, kk[i][:r] * decays[i], 0.0) for i in range(n)]
    qk = [(kk[i][r:] * decays[i]).astype(BF16) for i in range(n)]
    t = [eye - jnp.where(levels[0], mm[i], 0.0) for i in range(n)]
    for s, lvl in enumerate(levels[1:], start=1):
        blk = 2 ** s
        tb = [t[i].astype(BF16) for i in range(n)]
        mo = [jnp.where(lvl, mm[i], 0.0).astype(BF16) for i in range(n)]
        if blk % 8:
            x = [_dot(tb[i], mo[i]).astype(BF16) for i in range(n)]
            y = [_dot(x[i], tb[i]) for i in range(n)]
            t = [t[i] - y[i] for i in range(n)]
        else:
            split = [t[i].reshape(r // (2 * blk), 2, blk, r) for i in range(n)]
            lo = [split[i][:, 1].reshape(r // 2, r) for i in range(n)]
            x = [_dot(lo[i].astype(BF16), mo[i]).astype(BF16) for i in range(n)]
            y = [_dot(x[i], tb[i]) for i in range(n)]
            t = [jnp.stack([split[i][:, 0], (lo[i] - y[i]).reshape(r // (2 * blk), blk, r)], axis=1).reshape(r, r)
                 for i in range(n)]
    uw = [_dot(t[i].astype(BF16), rhs[i]) for i in range(n)]
    return [(uw[i][:, :DK_A], uw[i][:, DK_A:].astype(BF16), qgs[i], kds[i], qk[i]) for i in range(n)]


def _gated_norm(o, z, w_onorm):
    o = o * lax.rsqrt(jnp.mean(o * o, -1, keepdims=True) + RMS_EPS) * w_onorm
    return o * _silu(z)


def _delta_prompt_body(cs_ref, z_ref, gb_ref, wn_ref, o_ref, s_out_ref,
                       s_ref, u_s, o_s, wq_s, qkd_s, egl_s, *, tt, side_work=None):
    n_c = tt // CHUNK
    masks = _delta_masks(tt, CHUNK)

    @pl.when(pl.program_id(1) == 0)
    def _():
        s_ref[...] = jnp.zeros(s_ref.shape, F32)

    finish_side_work = side_work() if side_work is not None else None

    gbv = gb_ref[...]
    gc_all, rev_all = _decay_sums(gbv, masks)
    gc_t = gc_all.T
    heads = range(N_HEADS_A)
    hcols = [slice(hd * DK_A, (hd + 1) * DK_A) for hd in heads]
    problems = []
    for hd in heads:
        lg = N_HEADS_A + hd
        problems.append((cs_ref[:, hcols[hd]].astype(F32),
                         cs_ref[:, D_MODEL + hd * DK_A:D_MODEL + (hd + 1) * DK_A].astype(F32),
                         cs_ref[:, 2 * D_MODEL + hd * DK_A:2 * D_MODEL + (hd + 1) * DK_A].astype(F32),
                         jnp.broadcast_to(gbv[:, hd:hd + 1], (tt, DK_A)),
                         jnp.broadcast_to(gc_all[:, lg:lg + 1], (tt, DK_A)),
                         jnp.broadcast_to(rev_all[:, lg:lg + 1], (tt, DK_A)),
                         jnp.broadcast_to(gc_t[lg:lg + 1, :], (tt, tt))))
    for hd, (u, w, qg, kd, qk) in enumerate(_delta_pre(problems, masks)):
        u_s[:, hcols[hd]] = u
        kdt = kd.T
        for c in range(n_c):
            blk = slice(c * CHUNK, (c + 1) * CHUNK)
            idx = hd * n_c + c
            wq_s[idx, 0:CHUNK, :] = w[blk]
            wq_s[idx, CHUNK:2 * CHUNK, :] = qg[blk]
            qkd_s[idx, 0:CHUNK, :] = qk[blk, blk]
            qkd_s[idx, CHUNK:CHUNK + DK_A, :] = kdt[:, blk]
            last = (c + 1) * CHUNK - 1
            egl_s[idx] = jnp.broadcast_to(jnp.exp(problems[hd][4][last:last + 1, :]), (8, DK_A))

    if finish_side_work is not None:
        finish_side_work()

    states = [s_ref[hd] for hd in heads]
    for c in range(n_c):
        blk = slice(c * CHUNK, (c + 1) * CHUNK)
        idx = [hd * n_c + c for hd in heads]
        ws = [_dot(wq_s[idx[hd]], states[hd].astype(BF16)) for hd in heads]
        v_new = [(u_s[blk, hcols[hd]] - ws[hd][:CHUNK]).astype(BF16) for hd in heads]
        upd = [_dot(qkd_s[idx[hd]], v_new[hd]) for hd in heads]
        for hd in heads:
            o_s[blk, hcols[hd]] = ws[hd][CHUNK:] + upd[hd][:CHUNK]
            states[hd] = states[hd] * egl_s[idx[hd]][0:1, :] + upd[hd][CHUNK:]
    for hd in heads:
        s_ref[hd] = states[hd]
        o_ref[:, hcols[hd]] = _gated_norm(o_s[:, hcols[hd]], z_ref[:, hcols[hd]].astype(F32), wn_ref[...]).astype(BF16)

    @pl.when(pl.program_id(1) == pl.num_programs(1) - 1)
    def _():
        s_out_ref[...] = s_ref[...]


def _softmax_keys(s):
    p = jnp.exp(s - jnp.max(s, axis=0, keepdims=True))
    return (p / jnp.sum(p, axis=0, keepdims=True)).astype(BF16)


def _zero_key0(x):
    return jnp.where(lax.broadcasted_iota(jnp.int32, x.shape, 0) == 0, 0.0, x)


def _kv_cols(kvh):
    return slice(kvh * HD_B, (kvh + 1) * HD_B)


def _group_queries(q, kvh):
    return jnp.concatenate([q[:, hh * HD_B:(hh + 1) * HD_B] for hh in range(kvh * GROUP_B, (kvh + 1) * GROUP_B)],
                           axis=0)


def _swa_prompt_cache(kc_ref, vc_ref, ko_ref, vo_ref, nq):
    @pl.when(pl.program_id(1) == pl.num_programs(1) - 1)
    def _():
        ko_ref[...] = kc_ref[(nq - 1) * WINDOW:, :]
        vo_ref[...] = vc_ref[(nq - 1) * WINDOW:, :]


def _swa_prompt_start(q_ref, kp_ref, kc_ref, vp_ref, vc_ref, bias_ref, o_ref, nq):
    n_q = GROUP_B * WINDOW
    key = lax.broadcasted_iota(jnp.int32, (2 * WINDOW, n_q), 0)
    no_prev = (pl.program_id(1) == 0) & (key >= 1) & (key < WINDOW)
    kall = jnp.concatenate([kp_ref[...], kc_ref[...]], axis=0)
    vall = jnp.concatenate([vp_ref[...], vc_ref[...]], axis=0)
    probs = [(j, kvh) for j in range(nq) for kvh in range(N_KV_B)]
    k2 = [_zero_key0(kall[j * WINDOW:(j + 2) * WINDOW]).astype(BF16) for j in range(nq)]
    v2t = [_zero_key0(vall[j * WINDOW:(j + 2) * WINDOW]).T.astype(BF16) for j in range(nq)]
    q = [q_ref[j * WINDOW:(j + 1) * WINDOW, :] for j in range(nq)]
    s = [_dot_nt(k2[j][:, _kv_cols(kvh)], _group_queries(q[j], kvh)) for j, kvh in probs]
    pn = []
    for (j, kvh), sc in zip(probs, s):
        sc = sc + bias_ref[kvh]
        pn.append(_softmax_keys(jnp.where(no_prev, -jnp.inf, sc) if j == 0 else sc))

    def finish():
        ot = [_dot(v2t[j][_kv_cols(kvh), :], p) for (j, kvh), p in zip(probs, pn)]
        for (j, kvh), o in zip(probs, ot):
            for g in range(GROUP_B):
                hh = kvh * GROUP_B + g
                o_ref[j * WINDOW:(j + 1) * WINDOW, hh * HD_B:(hh + 1) * HD_B] = \
                    o[:, g * WINDOW:(g + 1) * WINDOW].T.astype(BF16)

    return finish


def _mixer_prompt_body(cs_ref, z_ref, gb_ref, wn_ref, q_ref, kp_ref, kc_ref, vp_ref, vc_ref, bias_ref,
                       oa_ref, s_out_ref, ob_ref, ko_ref, vo_ref, *scratch, tt):
    nq = tt // WINDOW
    _swa_prompt_cache(kc_ref, vc_ref, ko_ref, vo_ref, nq)
    attention = functools.partial(_swa_prompt_start, q_ref, kp_ref, kc_ref, vp_ref, vc_ref, bias_ref, ob_ref, nq)
    _delta_prompt_body(cs_ref, z_ref, gb_ref, wn_ref, oa_ref, s_out_ref, *scratch, tt=tt, side_work=attention)


def _mixer_prompt(cs, z, gb, w_onorm, qb, k, v, bias_t, *, tt):
    b, l, _ = cs.shape
    n_c = tt // CHUNK
    nq = tt // WINDOW
    tok_spec = lambda n: pl.BlockSpec((None, tt, n), lambda i, t: (i, t, 0))
    prev = lambda n: pl.BlockSpec((None, WINDOW, n), lambda i, t: (i, jnp.maximum(nq * t - 1, 0), 0))
    last = lambda n: pl.BlockSpec((None, WINDOW, n), lambda i, t: (i, 0, 0))
    return pl.pallas_call(
        functools.partial(_mixer_prompt_body, tt=tt),
        grid=(b, l // tt),
        in_specs=[tok_spec(CONV_DIM), tok_spec(D_MODEL), tok_spec(GB_LANES), _const_spec((1, DK_A)),
                  tok_spec(D_MODEL), prev(KV_B), tok_spec(KV_B), prev(KV_B), tok_spec(KV_B),
                  _const_spec((N_KV_B, 2 * WINDOW, GROUP_B * WINDOW))],
        out_specs=[tok_spec(D_MODEL), pl.BlockSpec((None, N_HEADS_A, DK_A, DK_A), lambda i, t: (i, 0, 0, 0)),
                   tok_spec(D_MODEL), last(KV_B), last(KV_B)],
        out_shape=[jax.ShapeDtypeStruct((b, l, D_MODEL), BF16),
                   jax.ShapeDtypeStruct((b, N_HEADS_A, DK_A, DK_A), F32),
                   jax.ShapeDtypeStruct((b, l, D_MODEL), BF16),
                   jax.ShapeDtypeStruct((b, WINDOW, KV_B), F32), jax.ShapeDtypeStruct((b, WINDOW, KV_B), F32)],
        scratch_shapes=[pltpu.VMEM((N_HEADS_A, DK_A, DK_A), F32),
                        pltpu.VMEM((tt, D_MODEL), F32), pltpu.VMEM((tt, D_MODEL), F32),
                        pltpu.VMEM((N_HEADS_A * n_c, 2 * CHUNK, DK_A), BF16),
                        pltpu.VMEM((N_HEADS_A * n_c, CHUNK + DK_A, CHUNK), BF16),
                        pltpu.VMEM((N_HEADS_A * n_c, 8, DK_A), F32)],
        compiler_params=_params(2),
        name="mixer_prompt",
    )(cs, z, gb, w_onorm, qb, k, k, v, v, bias_t)


def _mixer_sample_body(cs_ref, z_ref, gb_ref, wn_ref, s0_ref, q_ref, kn_ref, vn_ref, kc_ref, vc_ref, bias_ref, *rest,
                       bt, nb, ls, fill_layers):
    oa_ref, s_out_ref, ob_ref, ko_ref, vo_ref = rest[-5:]

    def put(ref, idx, val):
        idx = idx if isinstance(idx, tuple) else (idx,)
        if fill_layers:
            for layer_slot in range(fill_layers):
                ref[(layer_slot,) + idx] = val
        else:
            ref[idx] = val
    r = N_HEADS_A * ls
    masks = _delta_masks(r, ls)
    heads = range(N_HEADS_A)
    hrows = [slice(hd * ls, (hd + 1) * ls) for hd in heads]
    hcols = [slice(hd * DK_A, (hd + 1) * DK_A) for hd in heads]
    kvs = range(N_KV_B)

    def per_group(gi, carry):
        bis = [gi * nb + d for d in range(nb)]
        ks, vs, scores = [], [], []
        for bi in bis:
            kf = jnp.concatenate([kc_ref[bi], kn_ref[bi]], axis=0)
            vf = jnp.concatenate([vc_ref[bi], vn_ref[bi]], axis=0)
            put(ko_ref, bi, kf[ls:, :])
            put(vo_ref, bi, vf[ls:, :])
            ks.append(_zero_key0(kf).astype(BF16))
            vs.append(_zero_key0(vf).astype(BF16))
        for d, bi in enumerate(bis):
            q = q_ref[bi]
            scores.append([_dot_nt(ks[d][:, _kv_cols(kvh)], _group_queries(q, kvh)) for kvh in kvs])
        pn = [[_softmax_keys(scores[d][kvh] + bias_ref[kvh]) for kvh in kvs] for d in range(nb)]

        problems, egl = [], []
        for bi in bis:
            cs = cs_ref[bi].astype(F32)
            gbv = gb_ref[bi]
            stack = lambda base: jnp.concatenate([cs[:, base + hd * DK_A: base + (hd + 1) * DK_A] for hd in heads],
                                                 axis=0)
            beta_b = jnp.concatenate([jnp.broadcast_to(gbv[:, hd:hd + 1], (ls, DK_A)) for hd in heads], axis=0)
            g_b = jnp.concatenate(
                [jnp.broadcast_to(gbv[:, N_HEADS_A + hd:N_HEADS_A + hd + 1], (ls, DK_A)) for hd in heads], axis=0)
            gc, rev = _decay_sums(g_b, masks)
            egl.append(jnp.exp(gc))
            problems.append((stack(0), stack(D_MODEL), stack(2 * D_MODEL), beta_b, gc, rev, gc.T[:r, :]))
        pre = _delta_pre(problems, masks)
        both = [[_dot(jnp.concatenate([pre[d][1][hrows[hd]], pre[d][2][hrows[hd]]], axis=0),
                      s0_ref[bis[d], hd].astype(BF16)) for hd in heads] for d in range(nb)]
        v_new = [(pre[d][0] - jnp.concatenate([both[d][hd][:ls] for hd in heads], axis=0)).astype(BF16)
                 for d in range(nb)]
        qkv = [_dot(pre[d][4], v_new[d]) for d in range(nb)]
        upd = [[_dot_tn(pre[d][3][hrows[hd]], v_new[d][hrows[hd]]) for hd in heads] for d in range(nb)]
        att = [[_dot_tn(pn[d][kvh], vs[d][:, _kv_cols(kvh)]) for kvh in kvs] for d in range(nb)]
        for d, bi in enumerate(bis):
            zb = z_ref[bi].astype(F32)
            for hd in heads:
                last = (hd + 1) * ls - 1
                put(s_out_ref, (bi, hd), s0_ref[bi, hd] * egl[d][last:last + 1, :] + upd[d][hd])
                o = both[d][hd][ls:] + qkv[d][hrows[hd]]
                oa_ref[bi, :, hcols[hd]] = _gated_norm(o, zb[:, hcols[hd]], wn_ref[...]).astype(BF16)
            for kvh in kvs:
                for g in range(GROUP_B):
                    hh = kvh * GROUP_B + g
                    ob_ref[bi, :, hh * HD_B:(hh + 1) * HD_B] = att[d][kvh][g * ls:(g + 1) * ls].astype(BF16)
        return carry

    lax.fori_loop(0, bt // nb, per_group, 0)


def _mixer_sample(cs, z, gb, w_onorm, s0_all, qb, k_new, v_new, k_cache_all, v_cache_all, bias_s, prev, *, layer, bt):
    b, ls, _ = cs.shape
    tok_spec = lambda n: pl.BlockSpec((bt, ls, n), lambda i: (i, 0, 0))
    st_spec = pl.BlockSpec((None, bt, N_HEADS_A, DK_A, DK_A), lambda i: (layer, i, 0, 0, 0))
    cache = pl.BlockSpec((None, bt, WINDOW, KV_B), lambda i: (layer, i, 0, 0))
    extra, extra_specs = ([], []) if prev is None else (list(prev), [pl.BlockSpec(memory_space=pl.ANY)] * 3)
    depth = s0_all.shape[0]
    if extra:
        st_out, cache_out = st_spec, cache
    else:
        st_out = pl.BlockSpec((depth, bt, N_HEADS_A, DK_A, DK_A), lambda i: (0, i, 0, 0, 0))
        cache_out = pl.BlockSpec((depth, bt, WINDOW, KV_B), lambda i: (0, i, 0, 0))
    n_in = 11
    return pl.pallas_call(
        functools.partial(_mixer_sample_body, bt=bt, nb=min(bt, MIXER_SAMPLE_ROWS_PER_TRIP), ls=ls,
                          fill_layers=0 if extra else depth),
        grid=(b // bt,),
        in_specs=[tok_spec(CONV_DIM), tok_spec(D_MODEL), tok_spec(GB_LANES), _const_spec((1, DK_A)), st_spec,
                  tok_spec(D_MODEL), tok_spec(KV_B), tok_spec(KV_B), cache, cache,
                  _const_spec((N_KV_B, WINDOW + ls, GROUP_B * ls))] + extra_specs,
        out_specs=[tok_spec(D_MODEL), st_out, tok_spec(D_MODEL), cache_out, cache_out],
        out_shape=[jax.ShapeDtypeStruct((b, ls, D_MODEL), BF16), jax.ShapeDtypeStruct(s0_all.shape, F32),
                   jax.ShapeDtypeStruct((b, ls, D_MODEL), BF16),
                   jax.ShapeDtypeStruct(k_cache_all.shape, F32), jax.ShapeDtypeStruct(v_cache_all.shape, F32)],
        input_output_aliases={n_in: 1, n_in + 1: 3, n_in + 2: 4} if extra else {},
        compiler_params=_params(1),
        name="mixer_sample",
    )(cs, z, gb, w_onorm, s0_all, qb, k_new, v_new, k_cache_all, v_cache_all, bias_s, *extra)


def _layer_norm(y, g, b):
    mu = jnp.mean(y, -1, keepdims=True)
    d = y - mu
    var = jnp.mean(d * d, -1, keepdims=True)
    return d * lax.rsqrt(var + LN_EPS) * g + b


def _tail_body(x_ref, oa_ref, ob_ref, gates_ref, mod_ref, wpa_ref, wpb_ref, wout_ref, ln1g_ref, ln1b_ref,
               wup_ref, wdn_ref, ln2g_ref, ln2b_ref, o_ref, *, bt, tt, alpha):
    mod = lambda i: mod_ref[:, :, i * D_MODEL:(i + 1) * D_MODEL]
    parts = [slice(0, tt // 2), slice(tt // 2, tt)] if bt == 1 and tt % (2 * BF16_SUBLANES) == 0 else [slice(0, tt)]
    ms = [bt * (p.stop - p.start) for p in parts]
    two = range(len(parts))
    flat = lambda a, i: a.reshape(ms[i], a.shape[-1])
    pa = [_dot(flat(oa_ref[:, p, :], i), wpa_ref[...]) for i, p in enumerate(parts)]
    pb = [_dot(flat(ob_ref[:, p, :], i), wpb_ref[...]) for i, p in enumerate(parts)]
    mixed = [flat(gates_ref[:, p, 0:D_MODEL].astype(F32), i) * pa[i]
             + flat(gates_ref[:, p, D_MODEL:2 * D_MODEL].astype(F32), i) * pb[i] for i, p in enumerate(parts)]
    attn = [_dot(mixed[i].astype(BF16), wout_ref[...]) for i in two]
    x1 = [_layer_norm(alpha * x_ref[:, p, :] + mod(2) * attn[i].reshape(bt, -1, D_MODEL), ln1g_ref[...], ln1b_ref[...])
          for i, p in enumerate(parts)]
    h2 = [flat(x1[i] * (1.0 + mod(4)) + mod(3), i).astype(BF16) for i in two]
    ff = [jnp.zeros((ms[i], D_MODEL), F32) for i in two]
    for c in range(D_FF // D_MODEL):
        cols = slice(c * D_MODEL, (c + 1) * D_MODEL)
        a = [jnp.maximum(_dot(h2[i], wup_ref[:, cols]), 0.0) for i in two]
        ff = [ff[i] + _dot((a[i] * a[i]).astype(BF16), wdn_ref[cols, :]) for i in two]
    for i, p in enumerate(parts):
        o_ref[:, p, :] = _layer_norm(alpha * x1[i] + mod(5) * ff[i].reshape(bt, -1, D_MODEL),
                                     ln2g_ref[...], ln2b_ref[...])


def _tail(x, oa, ob, gates, mod, w, p, *, layer, bt, tt, alpha):
    b, l, _ = x.shape
    tok_spec = lambda n: pl.BlockSpec((bt, tt, n), lambda i, t: (i, t, 0))
    row = _const_spec((1, D_MODEL))
    sq = _layer_spec((D_MODEL, D_MODEL), layer)
    return pl.pallas_call(
        functools.partial(_tail_body, bt=bt, tt=tt, alpha=alpha),
        grid=(b // bt, l // tt),
        in_specs=[tok_spec(D_MODEL), tok_spec(D_MODEL), tok_spec(D_MODEL), tok_spec(2 * D_MODEL),
                  pl.BlockSpec((bt, 1, 6 * D_MODEL), lambda i, t: (i, 0, 0)),
                  sq, sq, sq, row, row,
                  _layer_spec((D_MODEL, D_FF), layer), _layer_spec((D_FF, D_MODEL), layer), row, row],
        out_specs=tok_spec(D_MODEL),
        out_shape=jax.ShapeDtypeStruct((b, l, D_MODEL), F32),
        compiler_params=_params(2),
        name="tail",
    )(x, oa, ob, gates, mod, w["wpa"], w["wpb"], w["wout"], p["ln1_g"], p["ln1_b"],
      w["wup"], w["wdn"], p["ln2_g"], p["ln2_b"])


def _tiles(b, l, rows):
    tt = min(l, rows)
    bt = max(1, min(b, rows // tt))
    return bt, tt


FRONT_ROWS_LONG, FRONT_ROWS_SHORT = 512, 256
MIXER_PROMPT_ROWS = 2 * WINDOW
MIXER_SAMPLE_BATCH_ROWS = 8
MIXER_SAMPLE_ROWS_PER_TRIP = 4
TAIL_ROWS = 512


def _layer_params(l, w_conv, a_log, dt_bias, w_onorm, ln1_g, ln1_b, ln2_g, ln2_b):
    gpar = jnp.zeros((2, GB_LANES), F32)
    gpar = gpar.at[0, N_HEADS_A:2 * N_HEADS_A].set(-jnp.exp(a_log[l].astype(F32)))
    gpar = gpar.at[1, N_HEADS_A:2 * N_HEADS_A].set(dt_bias[l].astype(F32))
    row = lambda a: a[l].reshape(1, -1).astype(F32)
    return dict(wconv=w_conv[l].astype(F32), gpar=gpar, w_onorm=row(w_onorm), ln1_g=row(ln1_g), ln1_b=row(ln1_b),
                ln2_g=row(ln2_g), ln2_b=row(ln2_b))


def _trunk_layer(x, mod, layer, w, p, alpha, conv_state, sample_state):
    b, l, _ = x.shape
    bt, tt = _tiles(b, l, FRONT_ROWS_LONG if l % FRONT_ROWS_LONG == 0 else FRONT_ROWS_SHORT)
    cs, z, gb, qb, k, v, gates, conv_out = _front(x, mod, conv_state, w["wm"], w["wba"], p["wconv"], p["gpar"],
                                                  layer=layer, bt=bt, tt=tt)
    if sample_state is None:
        oa, s_new, ob, k_new, v_new = _mixer_prompt(cs, z, gb, p["w_onorm"], qb, k, v, p["bias_t"],
                                                    tt=min(l, MIXER_PROMPT_ROWS))
    else:
        s0_all, kc_all, vc_all, s_prev, k_prev, v_prev = sample_state
        bias_s = p["bias_t"][:, :WINDOW + l, :].reshape(N_KV_B, WINDOW + l, GROUP_B, WINDOW)[..., :l]
        bias_s = bias_s.reshape(N_KV_B, WINDOW + l, GROUP_B * l)
        prev = None if s_prev is None else (s_prev, k_prev, v_prev)
        oa, s_new, ob, k_new, v_new = _mixer_sample(cs, z, gb, p["w_onorm"], s0_all, qb, k, v, kc_all, vc_all, bias_s,
                                                    prev, layer=layer, bt=min(b, MIXER_SAMPLE_BATCH_ROWS))
    bt, tt = _tiles(b, l, TAIL_ROWS)
    x2 = _tail(x, oa, ob, gates, mod, w, p, layer=layer, bt=bt, tt=tt, alpha=alpha)
    return x2, (s_new, conv_out, k_new, v_new)


def kernel(x_prompt, x_sample, state_delta, state_conv, cache_k, cache_v, c_prompt, c_sample, rel_bias, w_ada, b_ada, w_in, w_conv, a_log, dt_bias, w_onorm, sinks, w_pa, w_pb, w_out, ln1_g, ln1_b, w_up, w_down, ln2_g, ln2_b):
    depth = w_in.shape[0]
    alpha = (2 * depth) ** 0.25
    bp = x_prompt.shape[0]
    mod_all = _ada(jnp.concatenate([c_prompt, c_sample], axis=0), w_ada, b_ada)
    bias_t = _bias_table_t(rel_bias, sinks)
    wm, wba = _prep_w_in(w_in)
    w = dict(wm=wm, wba=wba, wpa=_cast_bf16(w_pa), wpb=_cast_bf16(w_pb), wout=_cast_bf16(w_out),
             wup=_cast_bf16(w_up), wdn=_cast_bf16(w_down))
    bs = x_sample.shape[0]
    kc_all = cache_k.reshape(depth, bs, WINDOW, KV_B)
    vc_all = cache_v.reshape(depth, bs, WINDOW, KV_B)
    yp, ys = x_prompt, x_sample
    prompt_outs = [[] for _ in range(4)]
    sample_conv = []
    s_all = k_all = v_all = None
    for l in range(depth):
        p = _layer_params(l, w_conv, a_log, dt_bias, w_onorm, ln1_g, ln1_b, ln2_g, ln2_b)
        p["bias_t"] = bias_t[l]
        mod_p = mod_all[l, :bp][:, None, :]
        mod_s = mod_all[l, bp:][:, None, :]
        zero_conv = jnp.zeros((bp, CONV_W - 1, CONV_DIM), x_prompt.dtype)
        yp, rest_p = _trunk_layer(yp, mod_p, l, w, p, alpha, zero_conv, None)
        ys, (s_all, conv_s, k_all, v_all) = _trunk_layer(ys, mod_s, l, w, p, alpha, state_conv[l],
                                                         (state_delta, kc_all, vc_all, s_all, k_all, v_all))
        for acc, val in zip(prompt_outs, rest_p):
            acc.append(val)
        sample_conv.append(conv_s)
    heads = lambda a: a.reshape(a.shape[:-1] + (N_KV_B, HD_B))
    pd, pc, pk, pv = (jnp.stack(a) for a in prompt_outs)
    return (yp, ys, pd, pc, heads(pk), heads(pv), s_all, jnp.stack(sample_conv), heads(k_all), heads(v_all))
```

```python
import functools
import math

import numpy as np
import jax
import jax.numpy as jnp
from jax import lax
from jax.experimental import pallas as pl
from jax.experimental.pallas import tpu as pltpu

F32 = jnp.float32
BF16 = jnp.bfloat16

D_MODEL = 1024
N_HEADS_A = 8
DK_A = 128
CONV_W = 4
CONV_DIM = 3 * D_MODEL
CHUNK = 64
HD_B = 64
N_HEADS_B = 16
N_KV_B = 4
GROUP_B = N_HEADS_B // N_KV_B
KV_B = N_KV_B * HD_B
WINDOW = 128
N_BUCKETS = 32
MAX_DISTANCE = 128
D_FF = 4 * D_MODEL
LN_EPS = 1e-5
RMS_EPS = 1e-6

OFF_Z = CONV_DIM
OFF_QB = OFF_Z + D_MODEL
OFF_KVB = OFF_QB + D_MODEL
OFF_GATES = OFF_KVB + 2 * KV_B
N_MAIN = OFF_GATES + 2 * D_MODEL
GB_LANES = 128

F32_SUBLANES = 8
BF16_SUBLANES = 16
V7X_VMEM_BYTES = 64 * 1024 * 1024
VMEM_LIMIT = V7X_VMEM_BYTES - 8 * 1024 * 1024


def _params(n_grid):
    return pltpu.CompilerParams(dimension_semantics=("arbitrary",) * n_grid, vmem_limit_bytes=VMEM_LIMIT)


def _dot(a, b):
    return jnp.dot(a, b, preferred_element_type=F32)


def _dot_nt(a, b):
    return lax.dot_general(a, b, (((1,), (1,)), ((), ())), preferred_element_type=F32)


def _dot_tn(a, b):
    return lax.dot_general(a, b, (((0,), (0,)), ((), ())), preferred_element_type=F32)


def _sigmoid(x):
    return 1.0 / (1.0 + jnp.exp(-x))


def _silu(x):
    return x * _sigmoid(x)


def _const_spec(shape):
    nd = len(shape)
    return pl.BlockSpec(shape, lambda *_: (0,) * nd, pipeline_mode=pl.Buffered(1))


def _layer_spec(shape, layer):
    nd = len(shape)
    return pl.BlockSpec((None,) + tuple(shape), lambda *_: (layer,) + (0,) * nd, pipeline_mode=pl.Buffered(1))


def _cast_body(w_ref, o_ref):
    o_ref[...] = w_ref[...].astype(BF16)


def _cast_bf16(w):
    depth, k, n = w.shape
    bk = max(8, min(k, (1024 * 1024) // n))
    spec = pl.BlockSpec((None, bk, n), lambda l, i: (l, i, 0))
    return pl.pallas_call(
        _cast_body, grid=(depth, k // bk), in_specs=[spec], out_specs=spec,
        out_shape=jax.ShapeDtypeStruct(w.shape, BF16), compiler_params=_params(2), name="cast_bf16",
    )(w)


def _prep_w_in(w_in):
    o_ba = CONV_DIM + D_MODEL
    o_qb = o_ba + 2 * N_HEADS_A
    wm = jnp.concatenate([w_in[:, :, :o_ba], w_in[:, :, o_qb:]], axis=2).astype(BF16)
    wba = jnp.pad(w_in[:, :, o_ba:o_qb], ((0, 0), (0, 0), (0, GB_LANES - 2 * N_HEADS_A))).astype(BF16)
    return wm, wba


def _ada_body(c_ref, w_ref, b_ref, o_ref):
    s = _silu(c_ref[...]).astype(BF16)
    o_ref[...] = _dot(s, w_ref[...].astype(BF16)) + b_ref[...]


def _ada(c_all, w_ada, b_ada):
    depth = w_ada.shape[0]
    n_rows = c_all.shape[0]
    n_col = w_ada.shape[2] // D_MODEL
    return pl.pallas_call(
        _ada_body,
        grid=(depth, n_col),
        in_specs=[pl.BlockSpec((n_rows, D_MODEL), lambda l, n: (0, 0)),
                  pl.BlockSpec((None, D_MODEL, D_MODEL), lambda l, n: (l, 0, n)),
                  pl.BlockSpec((None, 1, D_MODEL), lambda l, n: (l, 0, n))],
        out_specs=pl.BlockSpec((None, n_rows, D_MODEL), lambda l, n: (l, 0, n)),
        out_shape=jax.ShapeDtypeStruct((depth, n_rows, w_ada.shape[2]), F32),
        compiler_params=_params(2),
        name="ada",
    )(c_all, w_ada, b_ada.reshape(depth, 1, -1))


def _bucket_table():
    r = np.arange(WINDOW)[:, None]
    c = np.arange(2 * WINDOW)[None, :]
    dist = WINDOW + r - c
    n = np.maximum(dist, 0)
    max_exact = N_BUCKETS // 2
    ratio = np.maximum(n, max_exact).astype(np.float32) / np.float32(max_exact)
    large = max_exact + (np.log(ratio) / np.float32(math.log(MAX_DISTANCE / max_exact))
                         * np.float32(N_BUCKETS - max_exact)).astype(np.int32)
    large = np.minimum(large, N_BUCKETS - 1)
    bucket = np.where(n < max_exact, n, large).astype(np.int32)
    valid = ((dist >= 0) & (dist < WINDOW)).astype(np.int32)
    return bucket, valid


def _bias_t_body(rb_ref, sink_ref, bucket_ref, valid_ref, o_ref):
    l = pl.program_id(0)
    h = pl.program_id(1)
    bucket = bucket_ref[...]
    acc = jnp.zeros(bucket.shape, F32)
    for j in range(N_BUCKETS):
        acc = jnp.where(bucket == j, rb_ref[j, h], acc)
    acc = jnp.where(valid_ref[...] > 0, acc, -jnp.inf)
    key = lax.broadcasted_iota(jnp.int32, bucket.shape, 0)
    o_ref[...] = jnp.where(key == 0, sink_ref[l, h], acc)


def _bias_table_t(rel_bias, sinks):
    depth = sinks.shape[0]
    bucket, valid = _bucket_table()
    return pl.pallas_call(
        _bias_t_body,
        grid=(depth, N_HEADS_B),
        in_specs=[pl.BlockSpec(memory_space=pltpu.SMEM), pl.BlockSpec(memory_space=pltpu.SMEM),
                  pl.BlockSpec((2 * WINDOW, WINDOW), lambda l, h: (0, 0)),
                  pl.BlockSpec((2 * WINDOW, WINDOW), lambda l, h: (0, 0))],
        out_specs=pl.BlockSpec((None, None, 2 * WINDOW, WINDOW), lambda l, h: (l, h // GROUP_B, 0, h % GROUP_B)),
        out_shape=jax.ShapeDtypeStruct((depth, N_KV_B, 2 * WINDOW, GROUP_B * WINDOW), F32),
        compiler_params=_params(2),
        name="bias_table_t",
    )(rel_bias.astype(F32), sinks.astype(F32), jnp.asarray(bucket.T.copy()), jnp.asarray(valid.T.copy()))


def _front_body(x_ref, mod_ref, st_ref, wm_ref, wba_ref, wconv_ref, gpar_ref,
                cs_ref, z_ref, gb_ref, qb_ref, k_ref, v_ref, gates_ref, cst_ref, cbuf, *, bt, tt):
    m = bt * tt
    pad, hist = F32_SUBLANES, CONV_W - 1
    sh1 = mod_ref[:, :, 0:D_MODEL]
    sc1 = mod_ref[:, :, D_MODEL:2 * D_MODEL]
    h = (x_ref[...] * (1.0 + sc1) + sh1).reshape(m, D_MODEL).astype(BF16)

    @pl.when(pl.program_id(1) == 0)
    def _():
        cbuf[:, 0:pad, :] = jnp.zeros((bt, pad, CONV_DIM), F32)
        cbuf[:, pad - hist:pad, :] = st_ref[...]

    def proj(off, n):
        w = wm_ref[:, off:off + n]
        return jnp.concatenate([_dot(h[r:r + PROJ_ROWS], w) for r in range(0, m, PROJ_ROWS)], axis=0)

    half = D_MODEL // 2

    def conv_piece(i):
        cols = slice(i * half, (i + 1) * half)
        cbuf[:, pad:pad + tt, cols] = proj(i * half, half).reshape(bt, tt, half)
        u = cbuf[:, :, cols]
        u1 = pltpu.roll(u, 1, 1)
        w0, w1, w2, w3 = (wconv_ref[j:j + 1, cols] for j in range(CONV_W))
        y = (u * w3 + u1 * w2) + pltpu.roll(u * w1 + u1 * w0, 2, 1)
        cs_ref[:, :, cols] = _silu(y[:, pad:, :]).astype(cs_ref.dtype)

    def z_piece(i):
        z_ref[:, :, i * half:(i + 1) * half] = proj(OFF_Z + i * half, half).astype(z_ref.dtype).reshape(bt, tt, half)

    def qb_piece(i):
        q = proj(OFF_QB + i * half, half) * (HD_B ** -0.5)
        qb_ref[:, :, i * half:(i + 1) * half] = q.astype(BF16).reshape(bt, tt, half)

    def kv_piece(_):
        kv = proj(OFF_KVB, 2 * KV_B)
        k_ref[...] = kv[:, :KV_B].reshape(bt, tt, KV_B)
        v_ref[...] = kv[:, KV_B:].reshape(bt, tt, KV_B)

    def gates_piece(i):
        g = _sigmoid(proj(OFF_GATES + i * half, half))
        gates_ref[:, :, i * half:(i + 1) * half] = g.astype(gates_ref.dtype).reshape(bt, tt, half)

    for piece, i in [(conv_piece, 0), (z_piece, 0), (conv_piece, 1), (z_piece, 1), (conv_piece, 2), (qb_piece, 0),
                     (conv_piece, 3), (qb_piece, 1), (conv_piece, 4), (kv_piece, 0), (gates_piece, 0),
                     (conv_piece, 5), (gates_piece, 1), (gates_piece, 2), (gates_piece, 3)]:
        piece(i)
    tail = cbuf[:, tt + pad - hist:tt + pad, :]
    cst_ref[...] = tail
    cbuf[:, pad - hist:pad, :] = tail

    ba = _dot(h, wba_ref[...])
    xg = ba + gpar_ref[1:2, :]
    softplus = jnp.maximum(xg, 0.0) + jnp.log(1.0 + jnp.exp(-jnp.abs(xg)))
    lane = lax.broadcasted_iota(jnp.int32, ba.shape, 1)
    gb = jnp.where(lane < N_HEADS_A, _sigmoid(ba), gpar_ref[0:1, :] * softplus)
    gb_ref[...] = gb.reshape(bt, tt, GB_LANES)


def _front(x, mod, conv_state, wm, wba, wconv, gpar, *, layer, bt, tt):
    b, l, _ = x.shape
    grid = (b // bt, l // tt)
    tok = lambda n, dt=F32: jax.ShapeDtypeStruct((b, l, n), dt)
    act = BF16 if tt % BF16_SUBLANES == 0 else F32
    tok_spec = lambda n: pl.BlockSpec((bt, tt, n), lambda i, t: (i, t, 0))
    return pl.pallas_call(
        functools.partial(_front_body, bt=bt, tt=tt),
        grid=grid,
        in_specs=[tok_spec(D_MODEL),
                  pl.BlockSpec((bt, 1, 6 * D_MODEL), lambda i, t: (i, 0, 0)),
                  pl.BlockSpec((bt, CONV_W - 1, CONV_DIM), lambda i, t: (i, 0, 0)),
                  _layer_spec((D_MODEL, N_MAIN), layer),
                  _layer_spec((D_MODEL, GB_LANES), layer),
                  _const_spec((CONV_W, CONV_DIM)),
                  _const_spec((2, GB_LANES))],
        out_specs=[tok_spec(CONV_DIM), tok_spec(D_MODEL), tok_spec(GB_LANES), tok_spec(D_MODEL),
                   tok_spec(KV_B), tok_spec(KV_B), tok_spec(2 * D_MODEL),
                   pl.BlockSpec((bt, CONV_W - 1, CONV_DIM), lambda i, t: (i, 0, 0))],
        out_shape=[tok(CONV_DIM, act), tok(D_MODEL, act), tok(GB_LANES), tok(D_MODEL, BF16),
                   tok(KV_B), tok(KV_B), tok(2 * D_MODEL, act),
                   jax.ShapeDtypeStruct((b, CONV_W - 1, CONV_DIM), F32)],
        scratch_shapes=[pltpu.VMEM((bt, tt + F32_SUBLANES, CONV_DIM), F32)],
        compiler_params=_params(2),
        name="front",
    )(x, mod, conv_state, wm, wba, wconv, gpar)


def _delta_masks(r, block):
    i = lax.broadcasted_iota(jnp.int32, (r, r), 0)
    j = lax.broadcasted_iota(jnp.int32, (r, r), 1)
    shift = int(math.log2(block))
    same = (i >> shift) == (j >> shift)
    levels = [((i >> (s + 1)) == (j >> (s + 1))) & ((i >> s) != (j >> s)) for s in range(shift)]
    return dict(tri=same & (i >= j), strict=same & (i > j), upper=same & (i < j), eye=(i == j).astype(F32),
                levels=levels)


def _split_dot(a01, x):
    a01 = a01.astype(BF16)
    hi = x.astype(BF16)
    rest = x - hi.astype(F32)
    mid = rest.astype(BF16)
    lo = (rest - mid.astype(F32)).astype(BF16)
    return _dot(a01, hi) + _dot(a01, mid) + _dot(a01, lo)


def _decay_sums(g, masks):
    return _split_dot(masks["tri"], g), _split_dot(masks["upper"], g)


def _delta_pre(problems, masks):
    tri, strict, eye, levels = masks["tri"], masks["strict"], masks["eye"], masks["levels"]
    n = len(problems)
    r = problems[0][0].shape[0]
    lhs, ks, rhs, decays, qgs, kds = [], [], [], [], [], []
    for qr, kr, v, beta_b, gc, rev, gc_row in problems:
        q = qr * lax.rsqrt(jnp.sum(qr * qr, -1, keepdims=True) + 1e-6) * (DK_A ** -0.5)
        k = kr * lax.rsqrt(jnp.sum(kr * kr, -1, keepdims=True) + 1e-6)
        gc_col = gc[:, :r] if r <= DK_A else jnp.concatenate([gc] * (r // DK_A), axis=1)
        decays.append(jnp.exp(jnp.where(tri, gc_col - gc_row, -jnp.inf)))
        eg = jnp.exp(gc)
        kb = k * beta_b
        lhs.append(jnp.concatenate([kb, q], axis=0).astype(BF16))
        ks.append(k.astype(BF16))
        rhs.append(jnp.concatenate([v * beta_b, kb * eg], axis=1).astype(BF16))
        qgs.append((q * eg).astype(BF16))
        kds.append((k * jnp.exp(rev)).astype(BF16))
    kk = [_dot_nt(lhs[i], ks[i]) for i in range(n)]
    mm = [jnp.where(strict, kk[i][:r] * decays[i], 0.0) for i in range(n)]
    qk = [(kk[i][r:] * decays[i]).astype(BF16) for i in range(n)]
    t = [eye - jnp.where(levels[0], mm[i], 0.0) for i in range(n)]
    for s, lvl in enumerate(levels[1:], start=1):
        blk = 2 ** s
        tb = [t[i].astype(BF16) for i in range(n)]
        mo = [jnp.where(lvl, mm[i], 0.0).astype(BF16) for i in range(n)]
        if blk % 8:
            x = [_dot(tb[i], mo[i]).astype(BF16) for i in range(n)]
            y = [_dot(x[i], tb[i]) for i in range(n)]
            t = [t[i] - y[i] for i in range(n)]
        else:
            split = [t[i].reshape(r // (2 * blk), 2, blk, r) for i in range(n)]
            lo = [split[i][:, 1].reshape(r // 2, r) for i in range(n)]
            x = [_dot(lo[i].astype(BF16), mo[i]).astype(BF16) for i in range(n)]
            y = [_dot(x[i], tb[i]) for i in range(n)]
            t = [jnp.stack([split[i][:, 0], (lo[i] - y[i]).reshape(r // (2 * blk), blk, r)], axis=1).reshape(r, r)
                 for i in range(n)]
    uw = [_dot(t[i].astype(BF16), rhs[i]) for i in range(n)]
    return [(uw[i][:, :DK_A], uw[i][:, DK_A:].astype(BF16), qgs[i], kds[i], qk[i]) for i in range(n)]


def _gated_norm(o, z, w_onorm):
    o = o * lax.rsqrt(jnp.mean(o * o, -1, keepdims=True) + RMS_EPS) * w_onorm
    return o * _silu(z)


def _delta_prompt_body(cs_ref, z_ref, gb_ref, wn_ref, o_ref, s_out_ref,
                       s_ref, u_s, o_s, wq_s, qkd_s, egl_s, *, tt, side_work=None):
    n_c = tt // CHUNK
    masks = _delta_masks(tt, CHUNK)

    @pl.when(pl.program_id(1) == 0)
    def _():
        s_ref[...] = jnp.zeros(s_ref.shape, F32)

    finish_side_work = side_work() if side_work is not None else None

    gbv = gb_ref[...]
    gc_all, rev_all = _decay_sums(gbv, masks)
    gc_t = gc_all.T
    heads = range(N_HEADS_A)
    hcols = [slice(hd * DK_A, (hd + 1) * DK_A) for hd in heads]
    problems = []
    for hd in heads:
        lg = N_HEADS_A + hd
        problems.append((cs_ref[:, hcols[hd]].astype(F32),
                         cs_ref[:, D_MODEL + hd * DK_A:D_MODEL + (hd + 1) * DK_A].astype(F32),
                         cs_ref[:, 2 * D_MODEL + hd * DK_A:2 * D_MODEL + (hd + 1) * DK_A].astype(F32),
                         jnp.broadcast_to(gbv[:, hd:hd + 1], (tt, DK_A)),
                         jnp.broadcast_to(gc_all[:, lg:lg + 1], (tt, DK_A)),
                         jnp.broadcast_to(rev_all[:, lg:lg + 1], (tt, DK_A)),
                         jnp.broadcast_to(gc_t[lg:lg + 1, :], (tt, tt))))
    for hd, (u, w, qg, kd, qk) in enumerate(_delta_pre(problems, masks)):
        u_s[:, hcols[hd]] = u
        kdt = kd.T
        for c in range(n_c):
            blk = slice(c * CHUNK, (c + 1) * CHUNK)
            idx = hd * n_c + c
            wq_s[idx, 0:CHUNK, :] = w[blk]
            wq_s[idx, CHUNK:2 * CHUNK, :] = qg[blk]
            qkd_s[idx, 0:CHUNK, :] = qk[blk, blk]
            qkd_s[idx, CHUNK:CHUNK + DK_A, :] = kdt[:, blk]
            last = (c + 1) * CHUNK - 1
            egl_s[idx] = jnp.broadcast_to(jnp.exp(problems[hd][4][last:last + 1, :]), (8, DK_A))

    if finish_side_work is not None:
        finish_side_work()

    states = [s_ref[hd] for hd in heads]
    for c in range(n_c):
        blk = slice(c * CHUNK, (c + 1) * CHUNK)
        idx = [hd * n_c + c for hd in heads]
        ws = [_dot(wq_s[idx[hd]], states[hd].astype(BF16)) for hd in heads]
        v_new = [(u_s[blk, hcols[hd]] - ws[hd][:CHUNK]).astype(BF16) for hd in heads]
        upd = [_dot(qkd_s[idx[hd]], v_new[hd]) for hd in heads]
        for hd in heads:
            o_s[blk, hcols[hd]] = ws[hd][CHUNK:] + upd[hd][:CHUNK]
            states[hd] = states[hd] * egl_s[idx[hd]][0:1, :] + upd[hd][CHUNK:]
    for hd in heads:
        s_ref[hd] = states[hd]
        o_ref[:, hcols[hd]] = _gated_norm(o_s[:, hcols[hd]], z_ref[:, hcols[hd]].astype(F32), wn_ref[...]).astype(BF16)

    @pl.when(pl.program_id(1) == pl.num_programs(1) - 1)
    def _():
        s_out_ref[...] = s_ref[...]


def _softmax_keys(s):
    p = jnp.exp(s - jnp.max(s, axis=0, keepdims=True))
    return (p / jnp.sum(p, axis=0, keepdims=True)).astype(BF16)


def _zero_key0(x):
    return jnp.where(lax.broadcasted_iota(jnp.int32, x.shape, 0) == 0, 0.0, x)


def _kv_cols(kvh):
    return slice(kvh * HD_B, (kvh + 1) * HD_B)


def _group_queries(q, kvh):
    return jnp.concatenate([q[:, hh * HD_B:(hh + 1) * HD_B] for hh in range(kvh * GROUP_B, (kvh + 1) * GROUP_B)],
                           axis=0)


def _swa_prompt_cache(kc_ref, vc_ref, ko_ref, vo_ref, nq):
    @pl.when(pl.program_id(1) == pl.num_programs(1) - 1)
    def _():
        ko_ref[...] = kc_ref[(nq - 1) * WINDOW:, :]
        vo_ref[...] = vc_ref[(nq - 1) * WINDOW:, :]


def _swa_prompt_start(q_ref, kp_ref, kc_ref, vp_ref, vc_ref, bias_ref, o_ref, nq):
    n_q = GROUP_B * WINDOW
    key = lax.broadcasted_iota(jnp.int32, (2 * WINDOW, n_q), 0)
    no_prev = (pl.program_id(1) == 0) & (key >= 1) & (key < WINDOW)
    kall = jnp.concatenate([kp_ref[...], kc_ref[...]], axis=0)
    vall = jnp.concatenate([vp_ref[...], vc_ref[...]], axis=0)
    probs = [(j, kvh) for j in range(nq) for kvh in range(N_KV_B)]
    k2 = [_zero_key0(kall[j * WINDOW:(j + 2) * WINDOW]).astype(BF16) for j in range(nq)]
    v2t = [_zero_key0(vall[j * WINDOW:(j + 2) * WINDOW]).T.astype(BF16) for j in range(nq)]
    q = [q_ref[j * WINDOW:(j + 1) * WINDOW, :] for j in range(nq)]
    s = [_dot_nt(k2[j][:, _kv_cols(kvh)], _group_queries(q[j], kvh)) for j, kvh in probs]
    pn = []
    for (j, kvh), sc in zip(probs, s):
        sc = sc + bias_ref[kvh]
        pn.append(_softmax_keys(jnp.where(no_prev, -jnp.inf, sc) if j == 0 else sc))

    def finish():
        ot = [_dot(v2t[j][_kv_cols(kvh), :], p) for (j, kvh), p in zip(probs, pn)]
        for (j, kvh), o in zip(probs, ot):
            for g in range(GROUP_B):
                hh = kvh * GROUP_B + g
                o_ref[j * WINDOW:(j + 1) * WINDOW, hh * HD_B:(hh + 1) * HD_B] = \
                    o[:, g * WINDOW:(g + 1) * WINDOW].T.astype(BF16)

    return finish


def _mixer_prompt_body(cs_ref, z_ref, gb_ref, wn_ref, q_ref, kp_ref, kc_ref, vp_ref, vc_ref, bias_ref,
                       oa_ref, s_out_ref, ob_ref, ko_ref, vo_ref, *scratch, tt):
    nq = tt // WINDOW
    _swa_prompt_cache(kc_ref, vc_ref, ko_ref, vo_ref, nq)
    attention = functools.partial(_swa_prompt_start, q_ref, kp_ref, kc_ref, vp_ref, vc_ref, bias_ref, ob_ref, nq)
    _delta_prompt_body(cs_ref, z_ref, gb_ref, wn_ref, oa_ref, s_out_ref, *scratch, tt=tt, side_work=attention)


def _mixer_prompt(cs, z, gb, w_onorm, qb, k, v, bias_t, *, tt):
    b, l, _ = cs.shape
    n_c = tt // CHUNK
    nq = tt // WINDOW
    tok_spec = lambda n: pl.BlockSpec((None, tt, n), lambda i, t: (i, t, 0))
    prev = lambda n: pl.BlockSpec((None, WINDOW, n), lambda i, t: (i, jnp.maximum(nq * t - 1, 0), 0))
    last = lambda n: pl.BlockSpec((None, WINDOW, n), lambda i, t: (i, 0, 0))
    return pl.pallas_call(
        functools.partial(_mixer_prompt_body, tt=tt),
        grid=(b, l // tt),
        in_specs=[tok_spec(CONV_DIM), tok_spec(D_MODEL), tok_spec(GB_LANES), _const_spec((1, DK_A)),
                  tok_spec(D_MODEL), prev(KV_B), tok_spec(KV_B), prev(KV_B), tok_spec(KV_B),
                  _const_spec((N_KV_B, 2 * WINDOW, GROUP_B * WINDOW))],
        out_specs=[tok_spec(D_MODEL), pl.BlockSpec((None, N_HEADS_A, DK_A, DK_A), lambda i, t: (i, 0, 0, 0)),
                   tok_spec(D_MODEL), last(KV_B), last(KV_B)],
        out_shape=[jax.ShapeDtypeStruct((b, l, D_MODEL), BF16),
                   jax.ShapeDtypeStruct((b, N_HEADS_A, DK_A, DK_A), F32),
                   jax.ShapeDtypeStruct((b, l, D_MODEL), BF16),
                   jax.ShapeDtypeStruct((b, WINDOW, KV_B), F32), jax.ShapeDtypeStruct((b, WINDOW, KV_B), F32)],
        scratch_shapes=[pltpu.VMEM((N_HEADS_A, DK_A, DK_A), F32),
                        pltpu.VMEM((tt, D_MODEL), F32), pltpu.VMEM((tt, D_MODEL), F32),
                        pltpu.VMEM((N_HEADS_A * n_c, 2 * CHUNK, DK_A), BF16),
                        pltpu.VMEM((N_HEADS_A * n_c, CHUNK + DK_A, CHUNK), BF16),
                        pltpu.VMEM((N_HEADS_A * n_c, 8, DK_A), F32)],
        compiler_params=_params(2),
        name="mixer_prompt",
    )(cs, z, gb, w_onorm, qb, k, k, v, v, bias_t)


def _mixer_sample_body(cs_ref, z_ref, gb_ref, wn_ref, s0_ref, q_ref, kn_ref, vn_ref, kc_ref, vc_ref, bias_ref, *rest,
                       bt, nb, ls, fill_layers):
    oa_ref, s_out_ref, ob_ref, ko_ref, vo_ref = rest[-5:]

    def put(ref, idx, val):
        idx = idx if isinstance(idx, tuple) else (idx,)
        if fill_layers:
            for layer_slot in range(fill_layers):
                ref[(layer_slot,) + idx] = val
        else:
            ref[idx] = val
    r = N_HEADS_A * ls
    masks = _delta_masks(r, ls)
    heads = range(N_HEADS_A)
    hrows = [slice(hd * ls, (hd + 1) * ls) for hd in heads]
    hcols = [slice(hd * DK_A, (hd + 1) * DK_A) for hd in heads]
    kvs = range(N_KV_B)

    def per_group(gi, carry):
        bis = [gi * nb + d for d in range(nb)]
        ks, vs, scores = [], [], []
        for bi in bis:
            kf = jnp.concatenate([kc_ref[bi], kn_ref[bi]], axis=0)
            vf = jnp.concatenate([vc_ref[bi], vn_ref[bi]], axis=0)
            put(ko_ref, bi, kf[ls:, :])
            put(vo_ref, bi, vf[ls:, :])
            ks.append(_zero_key0(kf).astype(BF16))
            vs.append(_zero_key0(vf).astype(BF16))
        for d, bi in enumerate(bis):
            q = q_ref[bi]
            scores.append([_dot_nt(ks[d][:, _kv_cols(kvh)], _group_queries(q, kvh)) for kvh in kvs])
        pn = [[_softmax_keys(scores[d][kvh] + bias_ref[kvh]) for kvh in kvs] for d in range(nb)]

        problems, egl = [], []
        for bi in bis:
            cs = cs_ref[bi].astype(F32)
            gbv = gb_ref[bi]
            stack = lambda base: jnp.concatenate([cs[:, base + hd * DK_A: base + (hd + 1) * DK_A] for hd in heads],
                                                 axis=0)
            beta_b = jnp.concatenate([jnp.broadcast_to(gbv[:, hd:hd + 1], (ls, DK_A)) for hd in heads], axis=0)
            g_b = jnp.concatenate(
                [jnp.broadcast_to(gbv[:, N_HEADS_A + hd:N_HEADS_A + hd + 1], (ls, DK_A)) for hd in heads], axis=0)
            gc, rev = _decay_sums(g_b, masks)
            egl.append(jnp.exp(gc))
            problems.append((stack(0), stack(D_MODEL), stack(2 * D_MODEL), beta_b, gc, rev, gc.T[:r, :]))
        pre = _delta_pre(problems, masks)
        both = [[_dot(jnp.concatenate([pre[d][1][hrows[hd]], pre[d][2][hrows[hd]]], axis=0),
                      s0_ref[bis[d], hd].astype(BF16)) for hd in heads] for d in range(nb)]
        v_new = [(pre[d][0] - jnp.concatenate([both[d][hd][:ls] for hd in heads], axis=0)).astype(BF16)
                 for d in range(nb)]
        qkv = [_dot(pre[d][4], v_new[d]) for d in range(nb)]
        upd = [[_dot_tn(pre[d][3][hrows[hd]], v_new[d][hrows[hd]]) for hd in heads] for d in range(nb)]
        att = [[_dot_tn(pn[d][kvh], vs[d][:, _kv_cols(kvh)]) for kvh in kvs] for d in range(nb)]
        for d, bi in enumerate(bis):
            zb = z_ref[bi].astype(F32)
            for hd in heads:
                last = (hd + 1) * ls - 1
                put(s_out_ref, (bi, hd), s0_ref[bi, hd] * egl[d][last:last + 1, :] + upd[d][hd])
                o = both[d][hd][ls:] + qkv[d][hrows[hd]]
                oa_ref[bi, :, hcols[hd]] = _gated_norm(o, zb[:, hcols[hd]], wn_ref[...]).astype(BF16)
            for kvh in kvs:
                for g in range(GROUP_B):
                    hh = kvh * GROUP_B + g
                    ob_ref[bi, :, hh * HD_B:(hh + 1) * HD_B] = att[d][kvh][g * ls:(g + 1) * ls].astype(BF16)
        return carry

    lax.fori_loop(0, bt // nb, per_group, 0)


def _mixer_sample(cs, z, gb, w_onorm, s0_all, qb, k_new, v_new, k_cache_all, v_cache_all, bias_s, prev, *, layer, bt):
    b, ls, _ = cs.shape
    tok_spec = lambda n: pl.BlockSpec((bt, ls, n), lambda i: (i, 0, 0))
    st_spec = pl.BlockSpec((None, bt, N_HEADS_A, DK_A, DK_A), lambda i: (layer, i, 0, 0, 0))
    cache = pl.BlockSpec((None, bt, WINDOW, KV_B), lambda i: (layer, i, 0, 0))
    extra, extra_specs = ([], []) if prev is None else (list(prev), [pl.BlockSpec(memory_space=pl.ANY)] * 3)
    depth = s0_all.shape[0]
    if extra:
        st_out, cache_out = st_spec, cache
    else:
        st_out = pl.BlockSpec((depth, bt, N_HEADS_A, DK_A, DK_A), lambda i: (0, i, 0, 0, 0))
        cache_out = pl.BlockSpec((depth, bt, WINDOW, KV_B), lambda i: (0, i, 0, 0))
    n_in = 11
    return pl.pallas_call(
        functools.partial(_mixer_sample_body, bt=bt, nb=min(bt, MIXER_SAMPLE_ROWS_PER_TRIP), ls=ls,
                          fill_layers=0 if extra else depth),
        grid=(b // bt,),
        in_specs=[tok_spec(CONV_DIM), tok_spec(D_MODEL), tok_spec(GB_LANES), _const_spec((1, DK_A)), st_spec,
                  tok_spec(D_MODEL), tok_spec(KV_B), tok_spec(KV_B), cache, cache,
                  _const_spec((N_KV_B, WINDOW + ls, GROUP_B * ls))] + extra_specs,
        out_specs=[tok_spec(D_MODEL), st_out, tok_spec(D_MODEL), cache_out, cache_out],
        out_shape=[jax.ShapeDtypeStruct((b, ls, D_MODEL), BF16), jax.ShapeDtypeStruct(s0_all.shape, F32),
                   jax.ShapeDtypeStruct((b, ls, D_MODEL), BF16),
                   jax.ShapeDtypeStruct(k_cache_all.shape, F32), jax.ShapeDtypeStruct(v_cache_all.shape, F32)],
        input_output_aliases={n_in: 1, n_in + 1: 3, n_in + 2: 4} if extra else {},
        compiler_params=_params(1),
        name="mixer_sample",
    )(cs, z, gb, w_onorm, s0_all, qb, k_new, v_new, k_cache_all, v_cache_all, bias_s, *extra)


def _layer_norm(y, g, b):
    mu = jnp.mean(y, -1, keepdims=True)
    d = y - mu
    var = jnp.mean(d * d, -1, keepdims=True)
    return d * lax.rsqrt(var + LN_EPS) * g + b


def _tail_body(x_ref, oa_ref, ob_ref, gates_ref, mod_ref, wpa_ref, wpb_ref, wout_ref, ln1g_ref, ln1b_ref,
               wup_ref, wdn_ref, ln2g_ref, ln2b_ref, o_ref, *, bt, tt, alpha):
    mod = lambda i: mod_ref[:, :, i * D_MODEL:(i + 1) * D_MODEL]
    parts = [slice(0, tt // 2), slice(tt // 2, tt)] if bt == 1 and tt % (2 * BF16_SUBLANES) == 0 else [slice(0, tt)]
    ms = [bt * (p.stop - p.start) for p in parts]
    two = range(len(parts))
    flat = lambda a, i: a.reshape(ms[i], a.shape[-1])
    pa = [_dot(flat(oa_ref[:, p, :], i), wpa_ref[...]) for i, p in enumerate(parts)]
    pb = [_dot(flat(ob_ref[:, p, :], i), wpb_ref[...]) for i, p in enumerate(parts)]
    mixed = [flat(gates_ref[:, p, 0:D_MODEL].astype(F32), i) * pa[i]
             + flat(gates_ref[:, p, D_MODEL:2 * D_MODEL].astype(F32), i) * pb[i] for i, p in enumerate(parts)]
    attn = [_dot(mixed[i].astype(BF16), wout_ref[...]) for i in two]
    x1 = [_layer_norm(alpha * x_ref[:, p, :] + mod(2) * attn[i].reshape(bt, -1, D_MODEL), ln1g_ref[...], ln1b_ref[...])
          for i, p in enumerate(parts)]
    h2 = [flat(x1[i] * (1.0 + mod(4)) + mod(3), i).astype(BF16) for i in two]
    ff = [jnp.zeros((ms[i], D_MODEL), F32) for i in two]
    for c in range(D_FF // D_MODEL):
        cols = slice(c * D_MODEL, (c + 1) * D_MODEL)
        a = [jnp.maximum(_dot(h2[i], wup_ref[:, cols]), 0.0) for i in two]
        ff = [ff[i] + _dot((a[i] * a[i]).astype(BF16), wdn_ref[cols, :]) for i in two]
    for i, p in enumerate(parts):
        o_ref[:, p, :] = _layer_norm(alpha * x1[i] + mod(5) * ff[i].reshape(bt, -1, D_MODEL),
                                     ln2g_ref[...], ln2b_ref[...])


def _tail(x, oa, ob, gates, mod, w, p, *, layer, bt, tt, alpha):
    b, l, _ = x.shape
    tok_spec = lambda n: pl.BlockSpec((bt, tt, n), lambda i, t: (i, t, 0))
    row = _const_spec((1, D_MODEL))
    sq = _layer_spec((D_MODEL, D_MODEL), layer)
    return pl.pallas_call(
        functools.partial(_tail_body, bt=bt, tt=tt, alpha=alpha),
        grid=(b // bt, l // tt),
        in_specs=[tok_spec(D_MODEL), tok_spec(D_MODEL), tok_spec(D_MODEL), tok_spec(2 * D_MODEL),
                  pl.BlockSpec((bt, 1, 6 * D_MODEL), lambda i, t: (i, 0, 0)),
                  sq, sq, sq, row, row,
                  _layer_spec((D_MODEL, D_FF), layer), _layer_spec((D_FF, D_MODEL), layer), row, row],
        out_specs=tok_spec(D_MODEL),
        out_shape=jax.ShapeDtypeStruct((b, l, D_MODEL), F32),
        compiler_params=_params(2),
        name="tail",
    )(x, oa, ob, gates, mod, w["wpa"], w["wpb"], w["wout"], p["ln1_g"], p["ln1_b"],
      w["wup"], w["wdn"], p["ln2_g"], p["ln2_b"])


def _tiles(b, l, rows):
    tt = min(l, rows)
    bt = max(1, min(b, rows // tt))
    return bt, tt


FRONT_ROWS_LONG, FRONT_ROWS_SHORT = 512, 256
PROJ_ROWS = 128
MIXER_PROMPT_ROWS = 2 * WINDOW
MIXER_SAMPLE_BATCH_ROWS = 8
MIXER_SAMPLE_ROWS_PER_TRIP = 4
TAIL_ROWS = 512


def _layer_params(l, w_conv, a_log, dt_bias, w_onorm, ln1_g, ln1_b, ln2_g, ln2_b):
    gpar = jnp.zeros((2, GB_LANES), F32)
    gpar = gpar.at[0, N_HEADS_A:2 * N_HEADS_A].set(-jnp.exp(a_log[l].astype(F32)))
    gpar = gpar.at[1, N_HEADS_A:2 * N_HEADS_A].set(dt_bias[l].astype(F32))
    row = lambda a: a[l].reshape(1, -1).astype(F32)
    return dict(wconv=w_conv[l].astype(F32), gpar=gpar, w_onorm=row(w_onorm), ln1_g=row(ln1_g), ln1_b=row(ln1_b),
                ln2_g=row(ln2_g), ln2_b=row(ln2_b))


def _trunk_layer(x, mod, layer, w, p, alpha, conv_state, sample_state):
    b, l, _ = x.shape
    bt, tt = _tiles(b, l, FRONT_ROWS_LONG if l % FRONT_ROWS_LONG == 0 else FRONT_ROWS_SHORT)
    cs, z, gb, qb, k, v, gates, conv_out = _front(x, mod, conv_state, w["wm"], w["wba"], p["wconv"], p["gpar"],
                                                  layer=layer, bt=bt, tt=tt)
    if sample_state is None:
        oa, s_new, ob, k_new, v_new = _mixer_prompt(cs, z, gb, p["w_onorm"], qb, k, v, p["bias_t"],
                                                    tt=min(l, MIXER_PROMPT_ROWS))
    else:
        s0_all, kc_all, vc_all, s_prev, k_prev, v_prev = sample_state
        bias_s = p["bias_t"][:, :WINDOW + l, :].reshape(N_KV_B, WINDOW + l, GROUP_B, WINDOW)[..., :l]
        bias_s = bias_s.reshape(N_KV_B, WINDOW + l, GROUP_B * l)
        prev = None if s_prev is None else (s_prev, k_prev, v_prev)
        oa, s_new, ob, k_new, v_new = _mixer_sample(cs, z, gb, p["w_onorm"], s0_all, qb, k, v, kc_all, vc_all, bias_s,
                                                    prev, layer=layer, bt=min(b, MIXER_SAMPLE_BATCH_ROWS))
    bt, tt = _tiles(b, l, TAIL_ROWS)
    x2 = _tail(x, oa, ob, gates, mod, w, p, layer=layer, bt=bt, tt=tt, alpha=alpha)
    return x2, (s_new, conv_out, k_new, v_new)


def kernel(x_prompt, x_sample, state_delta, state_conv, cache_k, cache_v, c_prompt, c_sample, rel_bias, w_ada, b_ada, w_in, w_conv, a_log, dt_bias, w_onorm, sinks, w_pa, w_pb, w_out, ln1_g, ln1_b, w_up, w_down, ln2_g, ln2_b):
    depth = w_in.shape[0]
    alpha = (2 * depth) ** 0.25
    bp = x_prompt.shape[0]
    mod_all = _ada(jnp.concatenate([c_prompt, c_sample], axis=0), w_ada, b_ada)
    bias_t = _bias_table_t(rel_bias, sinks)
    wm, wba = _prep_w_in(w_in)
    w = dict(wm=wm, wba=wba, wpa=_cast_bf16(w_pa), wpb=_cast_bf16(w_pb), wout=_cast_bf16(w_out),
             wup=_cast_bf16(w_up), wdn=_cast_bf16(w_down))
    bs = x_sample.shape[0]
    kc_all = cache_k.reshape(depth, bs, WINDOW, KV_B)
    vc_all = cache_v.reshape(depth, bs, WINDOW, KV_B)
    yp, ys = x_prompt, x_sample
    prompt_outs = [[] for _ in range(4)]
    sample_conv = []
    s_all = k_all = v_all = None
    for l in range(depth):
        p = _layer_params(l, w_conv, a_log, dt_bias, w_onorm, ln1_g, ln1_b, ln2_g, ln2_b)
        p["bias_t"] = bias_t[l]
        mod_p = mod_all[l, :bp][:, None, :]
        mod_s = mod_all[l, bp:][:, None, :]
        zero_conv = jnp.zeros((bp, CONV_W - 1, CONV_DIM), x_prompt.dtype)
        yp, rest_p = _trunk_layer(yp, mod_p, l, w, p, alpha, zero_conv, None)
        ys, (s_all, conv_s, k_all, v_all) = _trunk_layer(ys, mod_s, l, w, p, alpha, state_conv[l],
                                                         (state_delta, kc_all, vc_all, s_all, k_all, v_all))
        for acc, val in zip(prompt_outs, rest_p):
            acc.append(val)
        sample_conv.append(conv_s)
    heads = lambda a: a.reshape(a.shape[:-1] + (N_KV_B, HD_B))
    pd, pc, pk, pv = (jnp.stack(a) for a in prompt_outs)
    return (yp, ys, pd, pc, heads(pk), heads(pv), s_all, jnp.stack(sample_conv), heads(k_all), heads(v_all))
```

```python
import functools
import math

import numpy as np
import jax
import jax.numpy as jnp
from jax import lax
from jax.experimental import pallas as pl
from jax.experimental.pallas import tpu as pltpu

F32 = jnp.float32
BF16 = jnp.bfloat16

D_MODEL = 1024
N_HEADS_A = 8
DK_A = 128
CONV_W = 4
CONV_DIM = 3 * D_MODEL
CHUNK = 64
HD_B = 64
N_HEADS_B = 16
N_KV_B = 4
GROUP_B = N_HEADS_B // N_KV_B
KV_B = N_KV_B * HD_B
WINDOW = 128
N_BUCKETS = 32
MAX_DISTANCE = 128
D_FF = 4 * D_MODEL
LN_EPS = 1e-5
RMS_EPS = 1e-6

OFF_Z = CONV_DIM
OFF_QB = OFF_Z + D_MODEL
OFF_KVB = OFF_QB + D_MODEL
OFF_GATES = OFF_KVB + 2 * KV_B
N_MAIN = OFF_GATES + 2 * D_MODEL
GB_LANES = 128

F32_SUBLANES = 8
BF16_SUBLANES = 16
V7X_VMEM_BYTES = 64 * 1024 * 1024
VMEM_LIMIT = V7X_VMEM_BYTES - 8 * 1024 * 1024


def _params(n_grid):
    return pltpu.CompilerParams(dimension_semantics=("arbitrary",) * n_grid, vmem_limit_bytes=VMEM_LIMIT)


def _dot(a, b):
    return jnp.dot(a, b, preferred_element_type=F32)


def _dot_rows(a, b):
    m = a.shape[0]
    if m <= PROJ_ROWS or m % PROJ_ROWS:
        return _dot(a, b)
    return jnp.concatenate([_dot(a[r:r + PROJ_ROWS], b) for r in range(0, m, PROJ_ROWS)], axis=0)


def _dot_nt(a, b):
    return lax.dot_general(a, b, (((1,), (1,)), ((), ())), preferred_element_type=F32)


def _dot_tn(a, b):
    return lax.dot_general(a, b, (((0,), (0,)), ((), ())), preferred_element_type=F32)


def _sigmoid(x):
    return 1.0 / (1.0 + jnp.exp(-x))


def _silu(x):
    return x * _sigmoid(x)


def _const_spec(shape):
    nd = len(shape)
    return pl.BlockSpec(shape, lambda *_: (0,) * nd, pipeline_mode=pl.Buffered(1))


def _layer_spec(shape, layer):
    nd = len(shape)
    return pl.BlockSpec((None,) + tuple(shape), lambda *_: (layer,) + (0,) * nd, pipeline_mode=pl.Buffered(1))


def _cast_body(w_ref, o_ref):
    o_ref[...] = w_ref[...].astype(BF16)


def _cast_bf16(w):
    depth, k, n = w.shape
    bk = max(8, min(k, (1024 * 1024) // n))
    spec = pl.BlockSpec((None, bk, n), lambda l, i: (l, i, 0))
    return pl.pallas_call(
        _cast_body, grid=(depth, k // bk), in_specs=[spec], out_specs=spec,
        out_shape=jax.ShapeDtypeStruct(w.shape, BF16), compiler_params=_params(2), name="cast_bf16",
    )(w)


def _prep_w_in(w_in):
    o_ba = CONV_DIM + D_MODEL
    o_qb = o_ba + 2 * N_HEADS_A
    wm = jnp.concatenate([w_in[:, :, :o_ba], w_in[:, :, o_qb:]], axis=2).astype(BF16)
    wba = jnp.pad(w_in[:, :, o_ba:o_qb], ((0, 0), (0, 0), (0, GB_LANES - 2 * N_HEADS_A))).astype(BF16)
    return wm, wba


def _ada_body(c_ref, w_ref, b_ref, o_ref):
    s = _silu(c_ref[...]).astype(BF16)
    o_ref[...] = _dot(s, w_ref[...].astype(BF16)) + b_ref[...]


def _ada(c_all, w_ada, b_ada):
    depth = w_ada.shape[0]
    n_rows = c_all.shape[0]
    n_col = w_ada.shape[2] // D_MODEL
    return pl.pallas_call(
        _ada_body,
        grid=(depth, n_col),
        in_specs=[pl.BlockSpec((n_rows, D_MODEL), lambda l, n: (0, 0)),
                  pl.BlockSpec((None, D_MODEL, D_MODEL), lambda l, n: (l, 0, n)),
                  pl.BlockSpec((None, 1, D_MODEL), lambda l, n: (l, 0, n))],
        out_specs=pl.BlockSpec((None, n_rows, D_MODEL), lambda l, n: (l, 0, n)),
        out_shape=jax.ShapeDtypeStruct((depth, n_rows, w_ada.shape[2]), F32),
        compiler_params=_params(2),
        name="ada",
    )(c_all, w_ada, b_ada.reshape(depth, 1, -1))


def _bucket_table():
    r = np.arange(WINDOW)[:, None]
    c = np.arange(2 * WINDOW)[None, :]
    dist = WINDOW + r - c
    n = np.maximum(dist, 0)
    max_exact = N_BUCKETS // 2
    ratio = np.maximum(n, max_exact).astype(np.float32) / np.float32(max_exact)
    large = max_exact + (np.log(ratio) / np.float32(math.log(MAX_DISTANCE / max_exact))
                         * np.float32(N_BUCKETS - max_exact)).astype(np.int32)
    large = np.minimum(large, N_BUCKETS - 1)
    bucket = np.where(n < max_exact, n, large).astype(np.int32)
    valid = ((dist >= 0) & (dist < WINDOW)).astype(np.int32)
    return bucket, valid


def _bias_t_body(rb_ref, sink_ref, bucket_ref, valid_ref, o_ref):
    l = pl.program_id(0)
    h = pl.program_id(1)
    bucket = bucket_ref[...]
    acc = jnp.zeros(bucket.shape, F32)
    for j in range(N_BUCKETS):
        acc = jnp.where(bucket == j, rb_ref[j, h], acc)
    acc = jnp.where(valid_ref[...] > 0, acc, -jnp.inf)
    key = lax.broadcasted_iota(jnp.int32, bucket.shape, 0)
    o_ref[...] = jnp.where(key == 0, sink_ref[l, h], acc)


def _bias_table_t(rel_bias, sinks):
    depth = sinks.shape[0]
    bucket, valid = _bucket_table()
    return pl.pallas_call(
        _bias_t_body,
        grid=(depth, N_HEADS_B),
        in_specs=[pl.BlockSpec(memory_space=pltpu.SMEM), pl.BlockSpec(memory_space=pltpu.SMEM),
                  pl.BlockSpec((2 * WINDOW, WINDOW), lambda l, h: (0, 0)),
                  pl.BlockSpec((2 * WINDOW, WINDOW), lambda l, h: (0, 0))],
        out_specs=pl.BlockSpec((None, None, 2 * WINDOW, WINDOW), lambda l, h: (l, h // GROUP_B, 0, h % GROUP_B)),
        out_shape=jax.ShapeDtypeStruct((depth, N_KV_B, 2 * WINDOW, GROUP_B * WINDOW), F32),
        compiler_params=_params(2),
        name="bias_table_t",
    )(rel_bias.astype(F32), sinks.astype(F32), jnp.asarray(bucket.T.copy()), jnp.asarray(valid.T.copy()))


def _front_body(x_ref, mod_ref, st_ref, wm_ref, wba_ref, wconv_ref, gpar_ref,
                cs_ref, z_ref, gb_ref, qb_ref, k_ref, v_ref, gates_ref, cst_ref, cbuf, *, bt, tt):
    m = bt * tt
    pad, hist = F32_SUBLANES, CONV_W - 1
    sh1 = mod_ref[:, :, 0:D_MODEL]
    sc1 = mod_ref[:, :, D_MODEL:2 * D_MODEL]
    h = (x_ref[...] * (1.0 + sc1) + sh1).reshape(m, D_MODEL).astype(BF16)

    @pl.when(pl.program_id(1) == 0)
    def _():
        cbuf[:, 0:pad, :] = jnp.zeros((bt, pad, CONV_DIM), F32)
        cbuf[:, pad - hist:pad, :] = st_ref[...]

    def proj(off, n):
        return _dot_rows(h, wm_ref[:, off:off + n])

    half = D_MODEL // 2

    def conv_piece(i):
        cols = slice(i * half, (i + 1) * half)
        cbuf[:, pad:pad + tt, cols] = proj(i * half, half).reshape(bt, tt, half)
        u = cbuf[:, :, cols]
        u1 = pltpu.roll(u, 1, 1)
        w0, w1, w2, w3 = (wconv_ref[j:j + 1, cols] for j in range(CONV_W))
        y = (u * w3 + u1 * w2) + pltpu.roll(u * w1 + u1 * w0, 2, 1)
        cs_ref[:, :, cols] = _silu(y[:, pad:, :]).astype(cs_ref.dtype)

    def z_piece(i):
        z_ref[:, :, i * half:(i + 1) * half] = proj(OFF_Z + i * half, half).astype(z_ref.dtype).reshape(bt, tt, half)

    def qb_piece(i):
        q = proj(OFF_QB + i * half, half) * (HD_B ** -0.5)
        qb_ref[:, :, i * half:(i + 1) * half] = q.astype(BF16).reshape(bt, tt, half)

    def kv_piece(_):
        kv = proj(OFF_KVB, 2 * KV_B)
        k_ref[...] = kv[:, :KV_B].reshape(bt, tt, KV_B)
        v_ref[...] = kv[:, KV_B:].reshape(bt, tt, KV_B)

    def gates_piece(i):
        g = _sigmoid(proj(OFF_GATES + i * half, half))
        gates_ref[:, :, i * half:(i + 1) * half] = g.astype(gates_ref.dtype).reshape(bt, tt, half)

    for piece, i in [(conv_piece, 0), (z_piece, 0), (conv_piece, 1), (z_piece, 1), (conv_piece, 2), (qb_piece, 0),
                     (conv_piece, 3), (qb_piece, 1), (conv_piece, 4), (kv_piece, 0), (gates_piece, 0),
                     (conv_piece, 5), (gates_piece, 1), (gates_piece, 2), (gates_piece, 3)]:
        piece(i)
    tail = cbuf[:, tt + pad - hist:tt + pad, :]
    cst_ref[...] = tail
    cbuf[:, pad - hist:pad, :] = tail

    ba = _dot(h, wba_ref[...])
    xg = ba + gpar_ref[1:2, :]
    softplus = jnp.maximum(xg, 0.0) + jnp.log(1.0 + jnp.exp(-jnp.abs(xg)))
    lane = lax.broadcasted_iota(jnp.int32, ba.shape, 1)
    gb = jnp.where(lane < N_HEADS_A, _sigmoid(ba), gpar_ref[0:1, :] * softplus)
    gb_ref[...] = gb.reshape(bt, tt, GB_LANES)


def _front(x, mod, conv_state, wm, wba, wconv, gpar, *, layer, bt, tt):
    b, l, _ = x.shape
    grid = (b // bt, l // tt)
    tok = lambda n, dt=F32: jax.ShapeDtypeStruct((b, l, n), dt)
    act = BF16 if tt % BF16_SUBLANES == 0 else F32
    tok_spec = lambda n: pl.BlockSpec((bt, tt, n), lambda i, t: (i, t, 0))
    return pl.pallas_call(
        functools.partial(_front_body, bt=bt, tt=tt),
        grid=grid,
        in_specs=[tok_spec(D_MODEL),
                  pl.BlockSpec((bt, 1, 6 * D_MODEL), lambda i, t: (i, 0, 0)),
                  pl.BlockSpec((bt, CONV_W - 1, CONV_DIM), lambda i, t: (i, 0, 0)),
                  _layer_spec((D_MODEL, N_MAIN), layer),
                  _layer_spec((D_MODEL, GB_LANES), layer),
                  _const_spec((CONV_W, CONV_DIM)),
                  _const_spec((2, GB_LANES))],
        out_specs=[tok_spec(CONV_DIM), tok_spec(D_MODEL), tok_spec(GB_LANES), tok_spec(D_MODEL),
                   tok_spec(KV_B), tok_spec(KV_B), tok_spec(2 * D_MODEL),
                   pl.BlockSpec((bt, CONV_W - 1, CONV_DIM), lambda i, t: (i, 0, 0))],
        out_shape=[tok(CONV_DIM, act), tok(D_MODEL, act), tok(GB_LANES), tok(D_MODEL, BF16),
                   tok(KV_B), tok(KV_B), tok(2 * D_MODEL, act),
                   jax.ShapeDtypeStruct((b, CONV_W - 1, CONV_DIM), F32)],
        scratch_shapes=[pltpu.VMEM((bt, tt + F32_SUBLANES, CONV_DIM), F32)],
        compiler_params=_params(2),
        name="front",
    )(x, mod, conv_state, wm, wba, wconv, gpar)


def _delta_masks(r, block):
    i = lax.broadcasted_iota(jnp.int32, (r, r), 0)
    j = lax.broadcasted_iota(jnp.int32, (r, r), 1)
    shift = int(math.log2(block))
    same = (i >> shift) == (j >> shift)
    levels = [((i >> (s + 1)) == (j >> (s + 1))) & ((i >> s) != (j >> s)) for s in range(shift)]
    return dict(tri=same & (i >= j), strict=same & (i > j), upper=same & (i < j), eye=(i == j).astype(F32),
                levels=levels)


def _split_dot(a01, x):
    a01 = a01.astype(BF16)
    hi = x.astype(BF16)
    rest = x - hi.astype(F32)
    mid = rest.astype(BF16)
    lo = (rest - mid.astype(F32)).astype(BF16)
    return _dot(a01, hi) + _dot(a01, mid) + _dot(a01, lo)


def _decay_sums(g, masks):
    return _split_dot(masks["tri"], g), _split_dot(masks["upper"], g)


def _delta_pre(problems, masks):
    tri, strict, eye, levels = masks["tri"], masks["strict"], masks["eye"], masks["levels"]
    n = len(problems)
    r = problems[0][0].shape[0]
    lhs, ks, rhs, decays, qgs, kds = [], [], [], [], [], []
    for qr, kr, v, beta_b, gc, rev, gc_row in problems:
        q = qr * lax.rsqrt(jnp.sum(qr * qr, -1, keepdims=True) + 1e-6) * (DK_A ** -0.5)
        k = kr * lax.rsqrt(jnp.sum(kr * kr, -1, keepdims=True) + 1e-6)
        gc_col = gc[:, :r] if r <= DK_A else jnp.concatenate([gc] * (r // DK_A), axis=1)
        decays.append(jnp.exp(jnp.where(tri, gc_col - gc_row, -jnp.inf)))
        eg = jnp.exp(gc)
        kb = k * beta_b
        lhs.append(jnp.concatenate([kb, q], axis=0).astype(BF16))
        ks.append(k.astype(BF16))
        rhs.append(jnp.concatenate([v * beta_b, kb * eg], axis=1).astype(BF16))
        qgs.append((q * eg).astype(BF16))
        kds.append((k * jnp.exp(rev)).astype(BF16))
    kk = [_dot_nt(lhs[i], ks[i]) for i in range(n)]
    mm = [jnp.where(strict, kk[i][:r] * decays[i], 0.0) for i in range(n)]
    qk = [(kk[i][r:] * decays[i]).astype(BF16) for i in range(n)]
    t = [eye - jnp.where(levels[0], mm[i], 0.0) for i in range(n)]
    for s, lvl in enumerate(levels[1:], start=1):
        blk = 2 ** s
        tb = [t[i].astype(BF16) for i in range(n)]
        mo = [jnp.where(lvl, mm[i], 0.0).astype(BF16) for i in range(n)]
        if blk % 8:
            x = [_dot(tb[i], mo[i]).astype(BF16) for i in range(n)]
            y = [_dot(x[i], tb[i]) for i in range(n)]
            t = [t[i] - y[i] for i in range(n)]
        else:
            split = [t[i].reshape(r // (2 * blk), 2, blk, r) for i in range(n)]
            lo = [split[i][:, 1].reshape(r // 2, r) for i in range(n)]
            x = [_dot(lo[i].astype(BF16), mo[i]).astype(BF16) for i in range(n)]
            y = [_dot(x[i], tb[i]) for i in range(n)]
            t = [jnp.stack([split[i][:, 0], (lo[i] - y[i]).reshape(r // (2 * blk), blk, r)], axis=1).reshape(r, r)
                 for i in range(n)]
    uw = [_dot(t[i].astype(BF16), rhs[i]) for i in range(n)]
    return [(uw[i][:, :DK_A], uw[i][:, DK_A:].astype(BF16), qgs[i], kds[i], qk[i]) for i in range(n)]


def _gated_norm(o, z, w_onorm):
    o = o * lax.rsqrt(jnp.mean(o * o, -1, keepdims=True) + RMS_EPS) * w_onorm
    return o * _silu(z)


def _delta_prompt_body(cs_ref, z_ref, gb_ref, wn_ref, o_ref, s_out_ref,
                       s_ref, u_s, o_s, wq_s, qkd_s, egl_s, *, tt, side_work=None):
    n_c = tt // CHUNK
    masks = _delta_masks(tt, CHUNK)

    @pl.when(pl.program_id(1) == 0)
    def _():
        s_ref[...] = jnp.zeros(s_ref.shape, F32)

    finish_side_work = side_work() if side_work is not None else None

    gbv = gb_ref[...]
    gc_all, rev_all = _decay_sums(gbv, masks)
    gc_t = gc_all.T
    heads = range(N_HEADS_A)
    hcols = [slice(hd * DK_A, (hd + 1) * DK_A) for hd in heads]
    problems = []
    for hd in heads:
        lg = N_HEADS_A + hd
        problems.append((cs_ref[:, hcols[hd]].astype(F32),
                         cs_ref[:, D_MODEL + hd * DK_A:D_MODEL + (hd + 1) * DK_A].astype(F32),
                         cs_ref[:, 2 * D_MODEL + hd * DK_A:2 * D_MODEL + (hd + 1) * DK_A].astype(F32),
                         jnp.broadcast_to(gbv[:, hd:hd + 1], (tt, DK_A)),
                         jnp.broadcast_to(gc_all[:, lg:lg + 1], (tt, DK_A)),
                         jnp.broadcast_to(rev_all[:, lg:lg + 1], (tt, DK_A)),
                         jnp.broadcast_to(gc_t[lg:lg + 1, :], (tt, tt))))
    for hd, (u, w, qg, kd, qk) in enumerate(_delta_pre(problems, masks)):
        u_s[:, hcols[hd]] = u
        kdt = kd.T
        for c in range(n_c):
            blk = slice(c * CHUNK, (c + 1) * CHUNK)
            idx = hd * n_c + c
            wq_s[idx, 0:CHUNK, :] = w[blk]
            wq_s[idx, CHUNK:2 * CHUNK, :] = qg[blk]
            qkd_s[idx, 0:CHUNK, :] = qk[blk, blk]
            qkd_s[idx, CHUNK:CHUNK + DK_A, :] = kdt[:, blk]
            last = (c + 1) * CHUNK - 1
            egl_s[idx] = jnp.broadcast_to(jnp.exp(problems[hd][4][last:last + 1, :]), (8, DK_A))

    if finish_side_work is not None:
        finish_side_work()

    states = [s_ref[hd] for hd in heads]
    for c in range(n_c):
        blk = slice(c * CHUNK, (c + 1) * CHUNK)
        idx = [hd * n_c + c for hd in heads]
        ws = [_dot(wq_s[idx[hd]], states[hd].astype(BF16)) for hd in heads]
        v_new = [(u_s[blk, hcols[hd]] - ws[hd][:CHUNK]).astype(BF16) for hd in heads]
        upd = [_dot(qkd_s[idx[hd]], v_new[hd]) for hd in heads]
        for hd in heads:
            o_s[blk, hcols[hd]] = ws[hd][CHUNK:] + upd[hd][:CHUNK]
            states[hd] = states[hd] * egl_s[idx[hd]][0:1, :] + upd[hd][CHUNK:]
    for hd in heads:
        s_ref[hd] = states[hd]
        o_ref[:, hcols[hd]] = _gated_norm(o_s[:, hcols[hd]], z_ref[:, hcols[hd]].astype(F32), wn_ref[...]).astype(BF16)

    @pl.when(pl.program_id(1) == pl.num_programs(1) - 1)
    def _():
        s_out_ref[...] = s_ref[...]


def _softmax_keys(s):
    p = jnp.exp(s - jnp.max(s, axis=0, keepdims=True))
    return (p / jnp.sum(p, axis=0, keepdims=True)).astype(BF16)


def _zero_key0(x):
    return jnp.where(lax.broadcasted_iota(jnp.int32, x.shape, 0) == 0, 0.0, x)


def _kv_cols(kvh):
    return slice(kvh * HD_B, (kvh + 1) * HD_B)


def _group_queries(q, kvh):
    return jnp.concatenate([q[:, hh * HD_B:(hh + 1) * HD_B] for hh in range(kvh * GROUP_B, (kvh + 1) * GROUP_B)],
                           axis=0)


def _swa_prompt_cache(kc_ref, vc_ref, ko_ref, vo_ref, nq):
    @pl.when(pl.program_id(1) == pl.num_programs(1) - 1)
    def _():
        ko_ref[...] = kc_ref[(nq - 1) * WINDOW:, :]
        vo_ref[...] = vc_ref[(nq - 1) * WINDOW:, :]


def _swa_prompt_start(q_ref, kp_ref, kc_ref, vp_ref, vc_ref, bias_ref, o_ref, nq):
    n_q = GROUP_B * WINDOW
    key = lax.broadcasted_iota(jnp.int32, (2 * WINDOW, n_q), 0)
    no_prev = (pl.program_id(1) == 0) & (key >= 1) & (key < WINDOW)
    kall = jnp.concatenate([kp_ref[...], kc_ref[...]], axis=0)
    vall = jnp.concatenate([vp_ref[...], vc_ref[...]], axis=0)
    probs = [(j, kvh) for j in range(nq) for kvh in range(N_KV_B)]
    k2 = [_zero_key0(kall[j * WINDOW:(j + 2) * WINDOW]).astype(BF16) for j in range(nq)]
    v2t = [_zero_key0(vall[j * WINDOW:(j + 2) * WINDOW]).T.astype(BF16) for j in range(nq)]
    q = [q_ref[j * WINDOW:(j + 1) * WINDOW, :] for j in range(nq)]
    s = [_dot_nt(k2[j][:, _kv_cols(kvh)], _group_queries(q[j], kvh)) for j, kvh in probs]
    pn = []
    for (j, kvh), sc in zip(probs, s):
        sc = sc + bias_ref[kvh]
        pn.append(_softmax_keys(jnp.where(no_prev, -jnp.inf, sc) if j == 0 else sc))

    def finish():
        ot = [_dot(v2t[j][_kv_cols(kvh), :], p) for (j, kvh), p in zip(probs, pn)]
        for (j, kvh), o in zip(probs, ot):
            for g in range(GROUP_B):
                hh = kvh * GROUP_B + g
                o_ref[j * WINDOW:(j + 1) * WINDOW, hh * HD_B:(hh + 1) * HD_B] = \
                    o[:, g * WINDOW:(g + 1) * WINDOW].T.astype(BF16)

    return finish


def _mixer_prompt_body(cs_ref, z_ref, gb_ref, wn_ref, q_ref, kp_ref, kc_ref, vp_ref, vc_ref, bias_ref,
                       oa_ref, s_out_ref, ob_ref, ko_ref, vo_ref, *scratch, tt):
    nq = tt // WINDOW
    _swa_prompt_cache(kc_ref, vc_ref, ko_ref, vo_ref, nq)
    attention = functools.partial(_swa_prompt_start, q_ref, kp_ref, kc_ref, vp_ref, vc_ref, bias_ref, ob_ref, nq)
    _delta_prompt_body(cs_ref, z_ref, gb_ref, wn_ref, oa_ref, s_out_ref, *scratch, tt=tt, side_work=attention)


def _mixer_prompt(cs, z, gb, w_onorm, qb, k, v, bias_t, *, tt):
    b, l, _ = cs.shape
    n_c = tt // CHUNK
    nq = tt // WINDOW
    tok_spec = lambda n: pl.BlockSpec((None, tt, n), lambda i, t: (i, t, 0))
    prev = lambda n: pl.BlockSpec((None, WINDOW, n), lambda i, t: (i, jnp.maximum(nq * t - 1, 0), 0))
    last = lambda n: pl.BlockSpec((None, WINDOW, n), lambda i, t: (i, 0, 0))
    return pl.pallas_call(
        functools.partial(_mixer_prompt_body, tt=tt),
        grid=(b, l // tt),
        in_specs=[tok_spec(CONV_DIM), tok_spec(D_MODEL), tok_spec(GB_LANES), _const_spec((1, DK_A)),
                  tok_spec(D_MODEL), prev(KV_B), tok_spec(KV_B), prev(KV_B), tok_spec(KV_B),
                  _const_spec((N_KV_B, 2 * WINDOW, GROUP_B * WINDOW))],
        out_specs=[tok_spec(D_MODEL), pl.BlockSpec((None, N_HEADS_A, DK_A, DK_A), lambda i, t: (i, 0, 0, 0)),
                   tok_spec(D_MODEL), last(KV_B), last(KV_B)],
        out_shape=[jax.ShapeDtypeStruct((b, l, D_MODEL), BF16),
                   jax.ShapeDtypeStruct((b, N_HEADS_A, DK_A, DK_A), F32),
                   jax.ShapeDtypeStruct((b, l, D_MODEL), BF16),
                   jax.ShapeDtypeStruct((b, WINDOW, KV_B), F32), jax.ShapeDtypeStruct((b, WINDOW, KV_B), F32)],
        scratch_shapes=[pltpu.VMEM((N_HEADS_A, DK_A, DK_A), F32),
                        pltpu.VMEM((tt, D_MODEL), F32), pltpu.VMEM((tt, D_MODEL), F32),
                        pltpu.VMEM((N_HEADS_A * n_c, 2 * CHUNK, DK_A), BF16),
                        pltpu.VMEM((N_HEADS_A * n_c, CHUNK + DK_A, CHUNK), BF16),
                        pltpu.VMEM((N_HEADS_A * n_c, 8, DK_A), F32)],
        compiler_params=_params(2),
        name="mixer_prompt",
    )(cs, z, gb, w_onorm, qb, k, k, v, v, bias_t)


def _mixer_sample_body(cs_ref, z_ref, gb_ref, wn_ref, s0_ref, q_ref, kn_ref, vn_ref, kc_ref, vc_ref, bias_ref, *rest,
                       bt, nb, ls, fill_layers):
    oa_ref, s_out_ref, ob_ref, ko_ref, vo_ref = rest[-5:]

    def put(ref, idx, val):
        idx = idx if isinstance(idx, tuple) else (idx,)
        if fill_layers:
            for layer_slot in range(fill_layers):
                ref[(layer_slot,) + idx] = val
        else:
            ref[idx] = val
    r = N_HEADS_A * ls
    masks = _delta_masks(r, ls)
    heads = range(N_HEADS_A)
    hrows = [slice(hd * ls, (hd + 1) * ls) for hd in heads]
    hcols = [slice(hd * DK_A, (hd + 1) * DK_A) for hd in heads]
    kvs = range(N_KV_B)

    def per_group(gi, carry):
        bis = [gi * nb + d for d in range(nb)]
        ks, vs, scores = [], [], []
        for bi in bis:
            kf = jnp.concatenate([kc_ref[bi], kn_ref[bi]], axis=0)
            vf = jnp.concatenate([vc_ref[bi], vn_ref[bi]], axis=0)
            put(ko_ref, bi, kf[ls:, :])
            put(vo_ref, bi, vf[ls:, :])
            ks.append(_zero_key0(kf).astype(BF16))
            vs.append(_zero_key0(vf).astype(BF16))
        for d, bi in enumerate(bis):
            q = q_ref[bi]
            scores.append([_dot_nt(ks[d][:, _kv_cols(kvh)], _group_queries(q, kvh)) for kvh in kvs])
        pn = [[_softmax_keys(scores[d][kvh] + bias_ref[kvh]) for kvh in kvs] for d in range(nb)]

        problems, egl = [], []
        for bi in bis:
            cs = cs_ref[bi].astype(F32)
            gbv = gb_ref[bi]
            stack = lambda base: jnp.concatenate([cs[:, base + hd * DK_A: base + (hd + 1) * DK_A] for hd in heads],
                                                 axis=0)
            beta_b = jnp.concatenate([jnp.broadcast_to(gbv[:, hd:hd + 1], (ls, DK_A)) for hd in heads], axis=0)
            g_b = jnp.concatenate(
                [jnp.broadcast_to(gbv[:, N_HEADS_A + hd:N_HEADS_A + hd + 1], (ls, DK_A)) for hd in heads], axis=0)
            gc, rev = _decay_sums(g_b, masks)
            egl.append(jnp.exp(gc))
            problems.append((stack(0), stack(D_MODEL), stack(2 * D_MODEL), beta_b, gc, rev, gc.T[:r, :]))
        pre = _delta_pre(problems, masks)
        both = [[_dot(jnp.concatenate([pre[d][1][hrows[hd]], pre[d][2][hrows[hd]]], axis=0),
                      s0_ref[bis[d], hd].astype(BF16)) for hd in heads] for d in range(nb)]
        v_new = [(pre[d][0] - jnp.concatenate([both[d][hd][:ls] for hd in heads], axis=0)).astype(BF16)
                 for d in range(nb)]
        qkv = [_dot(pre[d][4], v_new[d]) for d in range(nb)]
        upd = [[_dot_tn(pre[d][3][hrows[hd]], v_new[d][hrows[hd]]) for hd in heads] for d in range(nb)]
        att = [[_dot_tn(pn[d][kvh], vs[d][:, _kv_cols(kvh)]) for kvh in kvs] for d in range(nb)]
        for d, bi in enumerate(bis):
            zb = z_ref[bi].astype(F32)
            for hd in heads:
                last = (hd + 1) * ls - 1
                put(s_out_ref, (bi, hd), s0_ref[bi, hd] * egl[d][last:last + 1, :] + upd[d][hd])
                o = both[d][hd][ls:] + qkv[d][hrows[hd]]
                oa_ref[bi, :, hcols[hd]] = _gated_norm(o, zb[:, hcols[hd]], wn_ref[...]).astype(BF16)
            for kvh in kvs:
                for g in range(GROUP_B):
                    hh = kvh * GROUP_B + g
                    ob_ref[bi, :, hh * HD_B:(hh + 1) * HD_B] = att[d][kvh][g * ls:(g + 1) * ls].astype(BF16)
        return carry

    lax.fori_loop(0, bt // nb, per_group, 0)


def _mixer_sample(cs, z, gb, w_onorm, s0_all, qb, k_new, v_new, k_cache_all, v_cache_all, bias_s, prev, *, layer, bt):
    b, ls, _ = cs.shape
    tok_spec = lambda n: pl.BlockSpec((bt, ls, n), lambda i: (i, 0, 0))
    st_spec = pl.BlockSpec((None, bt, N_HEADS_A, DK_A, DK_A), lambda i: (layer, i, 0, 0, 0))
    cache = pl.BlockSpec((None, bt, WINDOW, KV_B), lambda i: (layer, i, 0, 0))
    extra, extra_specs = ([], []) if prev is None else (list(prev), [pl.BlockSpec(memory_space=pl.ANY)] * 3)
    depth = s0_all.shape[0]
    if extra:
        st_out, cache_out = st_spec, cache
    else:
        st_out = pl.BlockSpec((depth, bt, N_HEADS_A, DK_A, DK_A), lambda i: (0, i, 0, 0, 0))
        cache_out = pl.BlockSpec((depth, bt, WINDOW, KV_B), lambda i: (0, i, 0, 0))
    n_in = 11
    return pl.pallas_call(
        functools.partial(_mixer_sample_body, bt=bt, nb=min(bt, MIXER_SAMPLE_ROWS_PER_TRIP), ls=ls,
                          fill_layers=0 if extra else depth),
        grid=(b // bt,),
        in_specs=[tok_spec(CONV_DIM), tok_spec(D_MODEL), tok_spec(GB_LANES), _const_spec((1, DK_A)), st_spec,
                  tok_spec(D_MODEL), tok_spec(KV_B), tok_spec(KV_B), cache, cache,
                  _const_spec((N_KV_B, WINDOW + ls, GROUP_B * ls))] + extra_specs,
        out_specs=[tok_spec(D_MODEL), st_out, tok_spec(D_MODEL), cache_out, cache_out],
        out_shape=[jax.ShapeDtypeStruct((b, ls, D_MODEL), BF16), jax.ShapeDtypeStruct(s0_all.shape, F32),
                   jax.ShapeDtypeStruct((b, ls, D_MODEL), BF16),
                   jax.ShapeDtypeStruct(k_cache_all.shape, F32), jax.ShapeDtypeStruct(v_cache_all.shape, F32)],
        input_output_aliases={n_in: 1, n_in + 1: 3, n_in + 2: 4} if extra else {},
        compiler_params=_params(1),
        name="mixer_sample",
    )(cs, z, gb, w_onorm, s0_all, qb, k_new, v_new, k_cache_all, v_cache_all, bias_s, *extra)


def _layer_norm(y, g, b):
    mu = jnp.mean(y, -1, keepdims=True)
    d = y - mu
    var = jnp.mean(d * d, -1, keepdims=True)
    return d * lax.rsqrt(var + LN_EPS) * g + b


def _tail_body(x_ref, oa_ref, ob_ref, gates_ref, mod_ref, wpa_ref, wpb_ref, wout_ref, ln1g_ref, ln1b_ref,
               wup_ref, wdn_ref, ln2g_ref, ln2b_ref, o_ref, *, bt, tt, alpha):
    mod = lambda i: mod_ref[:, :, i * D_MODEL:(i + 1) * D_MODEL]
    parts = [slice(0, tt // 2), slice(tt // 2, tt)] if bt == 1 and tt % (2 * BF16_SUBLANES) == 0 else [slice(0, tt)]
    ms = [bt * (p.stop - p.start) for p in parts]
    two = range(len(parts))
    flat = lambda a, i: a.reshape(ms[i], a.shape[-1])
    dot = _dot if len(parts) > 1 else _dot_rows
    pa = [dot(flat(oa_ref[:, p, :], i), wpa_ref[...]) for i, p in enumerate(parts)]
    pb = [dot(flat(ob_ref[:, p, :], i), wpb_ref[...]) for i, p in enumerate(parts)]
    mixed = [flat(gates_ref[:, p, 0:D_MODEL].astype(F32), i) * pa[i]
             + flat(gates_ref[:, p, D_MODEL:2 * D_MODEL].astype(F32), i) * pb[i] for i, p in enumerate(parts)]
    attn = [dot(mixed[i].astype(BF16), wout_ref[...]) for i in two]
    x1 = [_layer_norm(alpha * x_ref[:, p, :] + mod(2) * attn[i].reshape(bt, -1, D_MODEL), ln1g_ref[...], ln1b_ref[...])
          for i, p in enumerate(parts)]
    h2 = [flat(x1[i] * (1.0 + mod(4)) + mod(3), i).astype(BF16) for i in two]
    ff = [jnp.zeros((ms[i], D_MODEL), F32) for i in two]
    for c in range(D_FF // D_MODEL):
        cols = slice(c * D_MODEL, (c + 1) * D_MODEL)
        a = [jnp.maximum(dot(h2[i], wup_ref[:, cols]), 0.0) for i in two]
        ff = [ff[i] + dot((a[i] * a[i]).astype(BF16), wdn_ref[cols, :]) for i in two]
    for i, p in enumerate(parts):
        o_ref[:, p, :] = _layer_norm(alpha * x1[i] + mod(5) * ff[i].reshape(bt, -1, D_MODEL),
                                     ln2g_ref[...], ln2b_ref[...])


def _tail(x, oa, ob, gates, mod, w, p, *, layer, bt, tt, alpha):
    b, l, _ = x.shape
    tok_spec = lambda n: pl.BlockSpec((bt, tt, n), lambda i, t: (i, t, 0))
    row = _const_spec((1, D_MODEL))
    sq = _layer_spec((D_MODEL, D_MODEL), layer)
    return pl.pallas_call(
        functools.partial(_tail_body, bt=bt, tt=tt, alpha=alpha),
        grid=(b // bt, l // tt),
        in_specs=[tok_spec(D_MODEL), tok_spec(D_MODEL), tok_spec(D_MODEL), tok_spec(2 * D_MODEL),
                  pl.BlockSpec((bt, 1, 6 * D_MODEL), lambda i, t: (i, 0, 0)),
                  sq, sq, sq, row, row,
                  _layer_spec((D_MODEL, D_FF), layer), _layer_spec((D_FF, D_MODEL), layer), row, row],
        out_specs=tok_spec(D_MODEL),
        out_shape=jax.ShapeDtypeStruct((b, l, D_MODEL), F32),
        compiler_params=_params(2),
        name="tail",
    )(x, oa, ob, gates, mod, w["wpa"], w["wpb"], w["wout"], p["ln1_g"], p["ln1_b"],
      w["wup"], w["wdn"], p["ln2_g"], p["ln2_b"])


def _tiles(b, l, rows):
    tt = min(l, rows)
    bt = max(1, min(b, rows // tt))
    return bt, tt


FRONT_ROWS_LONG, FRONT_ROWS_SHORT = 512, 256
PROJ_ROWS = 128
MIXER_PROMPT_ROWS = 2 * WINDOW
MIXER_SAMPLE_BATCH_ROWS = 8
MIXER_SAMPLE_ROWS_PER_TRIP = 4
TAIL_ROWS = 512


def _layer_params(l, w_conv, a_log, dt_bias, w_onorm, ln1_g, ln1_b, ln2_g, ln2_b):
    gpar = jnp.zeros((2, GB_LANES), F32)
    gpar = gpar.at[0, N_HEADS_A:2 * N_HEADS_A].set(-jnp.exp(a_log[l].astype(F32)))
    gpar = gpar.at[1, N_HEADS_A:2 * N_HEADS_A].set(dt_bias[l].astype(F32))
    row = lambda a: a[l].reshape(1, -1).astype(F32)
    return dict(wconv=w_conv[l].astype(F32), gpar=gpar, w_onorm=row(w_onorm), ln1_g=row(ln1_g), ln1_b=row(ln1_b),
                ln2_g=row(ln2_g), ln2_b=row(ln2_b))


def _trunk_layer(x, mod, layer, w, p, alpha, conv_state, sample_state):
    b, l, _ = x.shape
    bt, tt = _tiles(b, l, FRONT_ROWS_LONG if l % FRONT_ROWS_LONG == 0 else FRONT_ROWS_SHORT)
    cs, z, gb, qb, k, v, gates, conv_out = _front(x, mod, conv_state, w["wm"], w["wba"], p["wconv"], p["gpar"],
                                                  layer=layer, bt=bt, tt=tt)
    if sample_state is None:
        oa, s_new, ob, k_new, v_new = _mixer_prompt(cs, z, gb, p["w_onorm"], qb, k, v, p["bias_t"],
                                                    tt=min(l, MIXER_PROMPT_ROWS))
    else:
        s0_all, kc_all, vc_all, s_prev, k_prev, v_prev = sample_state
        bias_s = p["bias_t"][:, :WINDOW + l, :].reshape(N_KV_B, WINDOW + l, GROUP_B, WINDOW)[..., :l]
        bias_s = bias_s.reshape(N_KV_B, WINDOW + l, GROUP_B * l)
        prev = None if s_prev is None else (s_prev, k_prev, v_prev)
        oa, s_new, ob, k_new, v_new = _mixer_sample(cs, z, gb, p["w_onorm"], s0_all, qb, k, v, kc_all, vc_all, bias_s,
                                                    prev, layer=layer, bt=min(b, MIXER_SAMPLE_BATCH_ROWS))
    bt, tt = _tiles(b, l, TAIL_ROWS)
    x2 = _tail(x, oa, ob, gates, mod, w, p, layer=layer, bt=bt, tt=tt, alpha=alpha)
    return x2, (s_new, conv_out, k_new, v_new)


def kernel(x_prompt, x_sample, state_delta, state_conv, cache_k, cache_v, c_prompt, c_sample, rel_bias, w_ada, b_ada, w_in, w_conv, a_log, dt_bias, w_onorm, sinks, w_pa, w_pb, w_out, ln1_g, ln1_b, w_up, w_down, ln2_g, ln2_b):
    depth = w_in.shape[0]
    alpha = (2 * depth) ** 0.25
    bp = x_prompt.shape[0]
    mod_all = _ada(jnp.concatenate([c_prompt, c_sample], axis=0), w_ada, b_ada)
    bias_t = _bias_table_t(rel_bias, sinks)
    wm, wba = _prep_w_in(w_in)
    w = dict(wm=wm, wba=wba, wpa=_cast_bf16(w_pa), wpb=_cast_bf16(w_pb), wout=_cast_bf16(w_out),
             wup=_cast_bf16(w_up), wdn=_cast_bf16(w_down))
    bs = x_sample.shape[0]
    kc_all = cache_k.reshape(depth, bs, WINDOW, KV_B)
    vc_all = cache_v.reshape(depth, bs, WINDOW, KV_B)
    yp, ys = x_prompt, x_sample
    prompt_outs = [[] for _ in range(4)]
    sample_conv = []
    s_all = k_all = v_all = None
    for l in range(depth):
        p = _layer_params(l, w_conv, a_log, dt_bias, w_onorm, ln1_g, ln1_b, ln2_g, ln2_b)
        p["bias_t"] = bias_t[l]
        mod_p = mod_all[l, :bp][:, None, :]
        mod_s = mod_all[l, bp:][:, None, :]
        zero_conv = jnp.zeros((bp, CONV_W - 1, CONV_DIM), x_prompt.dtype)
        yp, rest_p = _trunk_layer(yp, mod_p, l, w, p, alpha, zero_conv, None)
        ys, (s_all, conv_s, k_all, v_all) = _trunk_layer(ys, mod_s, l, w, p, alpha, state_conv[l],
                                                         (state_delta, kc_all, vc_all, s_all, k_all, v_all))
        for acc, val in zip(prompt_outs, rest_p):
            acc.append(val)
        sample_conv.append(conv_s)
    heads = lambda a: a.reshape(a.shape[:-1] + (N_KV_B, HD_B))
    pd, pc, pk, pv = (jnp.stack(a) for a in prompt_outs)
    return (yp, ys, pd, pc, heads(pk), heads(pv), s_all, jnp.stack(sample_conv), heads(k_all), heads(v_all))
```

```python
import functools
import math

import numpy as np
import jax
import jax.numpy as jnp
from jax import lax
from jax.experimental import pallas as pl
from jax.experimental.pallas import tpu as pltpu

F32 = jnp.float32
BF16 = jnp.bfloat16

D_MODEL = 1024
N_HEADS_A = 8
DK_A = 128
CONV_W = 4
CONV_DIM = 3 * D_MODEL
CHUNK = 64
HD_B = 64
N_HEADS_B = 16
N_KV_B = 4
GROUP_B = N_HEADS_B // N_KV_B
KV_B = N_KV_B * HD_B
WINDOW = 128
N_BUCKETS = 32
MAX_DISTANCE = 128
D_FF = 4 * D_MODEL
LN_EPS = 1e-5
RMS_EPS = 1e-6

OFF_Z = CONV_DIM
OFF_QB = OFF_Z + D_MODEL
OFF_KVB = OFF_QB + D_MODEL
OFF_GATES = OFF_KVB + 2 * KV_B
N_MAIN = OFF_GATES + 2 * D_MODEL
GB_LANES = 128

F32_SUBLANES = 8
BF16_SUBLANES = 16
V7X_VMEM_BYTES = 64 * 1024 * 1024
VMEM_LIMIT = V7X_VMEM_BYTES - 8 * 1024 * 1024


def _params(n_grid):
    return pltpu.CompilerParams(dimension_semantics=("arbitrary",) * n_grid, vmem_limit_bytes=VMEM_LIMIT)


def _dot(a, b):
    return jnp.dot(a, b, preferred_element_type=F32)


def _dot_rows(a, b):
    m = a.shape[0]
    if m <= PROJ_ROWS or m % PROJ_ROWS:
        return _dot(a, b)
    return jnp.concatenate([_dot(a[r:r + PROJ_ROWS], b) for r in range(0, m, PROJ_ROWS)], axis=0)


def _dot_nt(a, b):
    return lax.dot_general(a, b, (((1,), (1,)), ((), ())), preferred_element_type=F32)


def _dot_tn(a, b):
    return lax.dot_general(a, b, (((0,), (0,)), ((), ())), preferred_element_type=F32)


def _sigmoid(x):
    return 1.0 / (1.0 + jnp.exp(-x))


def _silu(x):
    return x * _sigmoid(x)


def _const_spec(shape):
    nd = len(shape)
    return pl.BlockSpec(shape, lambda *_: (0,) * nd, pipeline_mode=pl.Buffered(1))


def _layer_spec(shape, layer):
    nd = len(shape)
    return pl.BlockSpec((None,) + tuple(shape), lambda *_: (layer,) + (0,) * nd, pipeline_mode=pl.Buffered(1))


def _cast_body(w_ref, o_ref):
    o_ref[...] = w_ref[...].astype(BF16)


def _cast_bf16(w):
    depth, k, n = w.shape
    bk = max(8, min(k, (1024 * 1024) // n))
    spec = pl.BlockSpec((None, bk, n), lambda l, i: (l, i, 0))
    return pl.pallas_call(
        _cast_body, grid=(depth, k // bk), in_specs=[spec], out_specs=spec,
        out_shape=jax.ShapeDtypeStruct(w.shape, BF16), compiler_params=_params(2), name="cast_bf16",
    )(w)


def _prep_w_in(w_in):
    o_ba = CONV_DIM + D_MODEL
    o_qb = o_ba + 2 * N_HEADS_A
    wm = jnp.concatenate([w_in[:, :, :o_ba], w_in[:, :, o_qb:]], axis=2).astype(BF16)
    wba = jnp.pad(w_in[:, :, o_ba:o_qb], ((0, 0), (0, 0), (0, GB_LANES - 2 * N_HEADS_A))).astype(BF16)
    return wm, wba


def _ada_body(c_ref, w_ref, b_ref, o_ref):
    s = _silu(c_ref[...]).astype(BF16)
    o_ref[...] = _dot(s, w_ref[...].astype(BF16)) + b_ref[...]


def _ada(c_all, w_ada, b_ada):
    depth = w_ada.shape[0]
    n_rows = c_all.shape[0]
    n_col = w_ada.shape[2] // D_MODEL
    return pl.pallas_call(
        _ada_body,
        grid=(depth, n_col),
        in_specs=[pl.BlockSpec((n_rows, D_MODEL), lambda l, n: (0, 0)),
                  pl.BlockSpec((None, D_MODEL, D_MODEL), lambda l, n: (l, 0, n)),
                  pl.BlockSpec((None, 1, D_MODEL), lambda l, n: (l, 0, n))],
        out_specs=pl.BlockSpec((None, n_rows, D_MODEL), lambda l, n: (l, 0, n)),
        out_shape=jax.ShapeDtypeStruct((depth, n_rows, w_ada.shape[2]), F32),
        compiler_params=_params(2),
        name="ada",
    )(c_all, w_ada, b_ada.reshape(depth, 1, -1))


def _bucket_table():
    r = np.arange(WINDOW)[:, None]
    c = np.arange(2 * WINDOW)[None, :]
    dist = WINDOW + r - c
    n = np.maximum(dist, 0)
    max_exact = N_BUCKETS // 2
    ratio = np.maximum(n, max_exact).astype(np.float32) / np.float32(max_exact)
    large = max_exact + (np.log(ratio) / np.float32(math.log(MAX_DISTANCE / max_exact))
                         * np.float32(N_BUCKETS - max_exact)).astype(np.int32)
    large = np.minimum(large, N_BUCKETS - 1)
    bucket = np.where(n < max_exact, n, large).astype(np.int32)
    valid = ((dist >= 0) & (dist < WINDOW)).astype(np.int32)
    return bucket, valid


def _bias_t_body(rb_ref, sink_ref, bucket_ref, valid_ref, o_ref):
    l = pl.program_id(0)
    h = pl.program_id(1)
    bucket = bucket_ref[...]
    acc = jnp.zeros(bucket.shape, F32)
    for j in range(N_BUCKETS):
        acc = jnp.where(bucket == j, rb_ref[j, h], acc)
    acc = jnp.where(valid_ref[...] > 0, acc, -jnp.inf)
    key = lax.broadcasted_iota(jnp.int32, bucket.shape, 0)
    o_ref[...] = jnp.where(key == 0, sink_ref[l, h], acc)


def _bias_table_t(rel_bias, sinks):
    depth = sinks.shape[0]
    bucket, valid = _bucket_table()
    return pl.pallas_call(
        _bias_t_body,
        grid=(depth, N_HEADS_B),
        in_specs=[pl.BlockSpec(memory_space=pltpu.SMEM), pl.BlockSpec(memory_space=pltpu.SMEM),
                  pl.BlockSpec((2 * WINDOW, WINDOW), lambda l, h: (0, 0)),
                  pl.BlockSpec((2 * WINDOW, WINDOW), lambda l, h: (0, 0))],
        out_specs=pl.BlockSpec((None, None, 2 * WINDOW, WINDOW), lambda l, h: (l, h // GROUP_B, 0, h % GROUP_B)),
        out_shape=jax.ShapeDtypeStruct((depth, N_KV_B, 2 * WINDOW, GROUP_B * WINDOW), F32),
        compiler_params=_params(2),
        name="bias_table_t",
    )(rel_bias.astype(F32), sinks.astype(F32), jnp.asarray(bucket.T.copy()), jnp.asarray(valid.T.copy()))


def _front_body(x_ref, mod_ref, st_ref, wm_ref, wba_ref, wconv_ref, gpar_ref,
                cs_ref, z_ref, gb_ref, qb_ref, k_ref, v_ref, gates_ref, cst_ref, cbuf, *, bt, tt):
    m = bt * tt
    pad, hist = F32_SUBLANES, CONV_W - 1
    sh1 = mod_ref[:, :, 0:D_MODEL]
    sc1 = mod_ref[:, :, D_MODEL:2 * D_MODEL]
    h = (x_ref[...] * (1.0 + sc1) + sh1).reshape(m, D_MODEL).astype(BF16)

    @pl.when(pl.program_id(1) == 0)
    def _():
        cbuf[:, 0:pad, :] = jnp.zeros((bt, pad, CONV_DIM), F32)
        cbuf[:, pad - hist:pad, :] = st_ref[...]

    def proj(off, n):
        return _dot_rows(h, wm_ref[:, off:off + n])

    half = D_MODEL // 2

    def conv_piece(i):
        cols = slice(i * half, (i + 1) * half)
        cbuf[:, pad:pad + tt, cols] = proj(i * half, half).reshape(bt, tt, half)
        u = cbuf[:, :, cols]
        u1 = pltpu.roll(u, 1, 1)
        w0, w1, w2, w3 = (wconv_ref[j:j + 1, cols] for j in range(CONV_W))
        y = (u * w3 + u1 * w2) + pltpu.roll(u * w1 + u1 * w0, 2, 1)
        cs_ref[:, :, cols] = _silu(y[:, pad:, :]).astype(cs_ref.dtype)

    def z_piece(i):
        z_ref[:, :, i * half:(i + 1) * half] = proj(OFF_Z + i * half, half).astype(z_ref.dtype).reshape(bt, tt, half)

    def qb_piece(i):
        q = proj(OFF_QB + i * half, half) * (HD_B ** -0.5)
        qb_ref[:, :, i * half:(i + 1) * half] = q.astype(BF16).reshape(bt, tt, half)

    def kv_piece(_):
        kv = proj(OFF_KVB, 2 * KV_B)
        k_ref[...] = kv[:, :KV_B].reshape(bt, tt, KV_B)
        v_ref[...] = kv[:, KV_B:].reshape(bt, tt, KV_B)

    def gates_piece(i):
        g = _sigmoid(proj(OFF_GATES + i * half, half))
        gates_ref[:, :, i * half:(i + 1) * half] = g.astype(gates_ref.dtype).reshape(bt, tt, half)

    for piece, i in [(conv_piece, 0), (z_piece, 0), (conv_piece, 1), (z_piece, 1), (conv_piece, 2), (qb_piece, 0),
                     (conv_piece, 3), (qb_piece, 1), (conv_piece, 4), (kv_piece, 0), (gates_piece, 0),
                     (conv_piece, 5), (gates_piece, 1), (gates_piece, 2), (gates_piece, 3)]:
        piece(i)
    tail = cbuf[:, tt + pad - hist:tt + pad, :]
    cst_ref[...] = tail
    cbuf[:, pad - hist:pad, :] = tail

    ba = _dot(h, wba_ref[...])
    xg = ba + gpar_ref[1:2, :]
    softplus = jnp.maximum(xg, 0.0) + jnp.log(1.0 + jnp.exp(-jnp.abs(xg)))
    lane = lax.broadcasted_iota(jnp.int32, ba.shape, 1)
    gb = jnp.where(lane < N_HEADS_A, _sigmoid(ba), gpar_ref[0:1, :] * softplus)
    gb_ref[...] = gb.reshape(bt, tt, GB_LANES)


def _front(x, mod, conv_state, wm, wba, wconv, gpar, *, layer, bt, tt):
    b, l, _ = x.shape
    grid = (b // bt, l // tt)
    tok = lambda n, dt=F32: jax.ShapeDtypeStruct((b, l, n), dt)
    act = BF16 if tt % BF16_SUBLANES == 0 else F32
    tok_spec = lambda n: pl.BlockSpec((bt, tt, n), lambda i, t: (i, t, 0))
    return pl.pallas_call(
        functools.partial(_front_body, bt=bt, tt=tt),
        grid=grid,
        in_specs=[tok_spec(D_MODEL),
                  pl.BlockSpec((bt, 1, 6 * D_MODEL), lambda i, t: (i, 0, 0)),
                  pl.BlockSpec((bt, CONV_W - 1, CONV_DIM), lambda i, t: (i, 0, 0)),
                  _layer_spec((D_MODEL, N_MAIN), layer),
                  _layer_spec((D_MODEL, GB_LANES), layer),
                  _const_spec((CONV_W, CONV_DIM)),
                  _const_spec((2, GB_LANES))],
        out_specs=[tok_spec(CONV_DIM), tok_spec(D_MODEL), tok_spec(GB_LANES), tok_spec(D_MODEL),
                   tok_spec(KV_B), tok_spec(KV_B), tok_spec(2 * D_MODEL),
                   pl.BlockSpec((bt, CONV_W - 1, CONV_DIM), lambda i, t: (i, 0, 0))],
        out_shape=[tok(CONV_DIM, act), tok(D_MODEL, act), tok(GB_LANES), tok(D_MODEL, BF16),
                   tok(KV_B), tok(KV_B), tok(2 * D_MODEL, act),
                   jax.ShapeDtypeStruct((b, CONV_W - 1, CONV_DIM), F32)],
        scratch_shapes=[pltpu.VMEM((bt, tt + F32_SUBLANES, CONV_DIM), F32)],
        compiler_params=_params(2),
        name="front",
    )(x, mod, conv_state, wm, wba, wconv, gpar)


def _delta_masks(r, block):
    i = lax.broadcasted_iota(jnp.int32, (r, r), 0)
    j = lax.broadcasted_iota(jnp.int32, (r, r), 1)
    shift = int(math.log2(block))
    same = (i >> shift) == (j >> shift)
    levels = [((i >> (s + 1)) == (j >> (s + 1))) & ((i >> s) != (j >> s)) for s in range(shift)]
    return dict(tri=same & (i >= j), strict=same & (i > j), upper=same & (i < j), eye=(i == j).astype(F32),
                levels=levels)


def _split_dot(a01, x):
    a01 = a01.astype(BF16)
    hi = x.astype(BF16)
    rest = x - hi.astype(F32)
    mid = rest.astype(BF16)
    lo = (rest - mid.astype(F32)).astype(BF16)
    return _dot(a01, hi) + _dot(a01, mid) + _dot(a01, lo)


def _decay_sums(g, masks):
    return _split_dot(masks["tri"], g), _split_dot(masks["upper"], g)


def _delta_pre(problems, masks):
    tri, strict, eye, levels = masks["tri"], masks["strict"], masks["eye"], masks["levels"]
    n = len(problems)
    r = problems[0][0].shape[0]
    lhs, ks, rhs, decays, qgs, kds = [], [], [], [], [], []
    for qr, kr, v, beta_b, gc, rev, gc_row in problems:
        q = qr * lax.rsqrt(jnp.sum(qr * qr, -1, keepdims=True) + 1e-6) * (DK_A ** -0.5)
        k = kr * lax.rsqrt(jnp.sum(kr * kr, -1, keepdims=True) + 1e-6)
        gc_col = gc[:, :r] if r <= DK_A else jnp.concatenate([gc] * (r // DK_A), axis=1)
        decays.append(jnp.exp(jnp.where(tri, gc_col - gc_row, -jnp.inf)))
        eg = jnp.exp(gc)
        kb = k * beta_b
        lhs.append(jnp.concatenate([kb, q], axis=0).astype(BF16))
        ks.append(k.astype(BF16))
        rhs.append(jnp.concatenate([v * beta_b, kb * eg], axis=1).astype(BF16))
        qgs.append((q * eg).astype(BF16))
        kds.append((k * jnp.exp(rev)).astype(BF16))
    kk = [_dot_nt(lhs[i], ks[i]) for i in range(n)]
    mm = [jnp.where(strict, kk[i][:r] * decays[i], 0.0) for i in range(n)]
    qk = [(kk[i][r:] * decays[i]).astype(BF16) for i in range(n)]
    t = [eye - jnp.where(levels[0], mm[i], 0.0) for i in range(n)]
    for s, lvl in enumerate(levels[1:], start=1):
        blk = 2 ** s
        tb = [t[i].astype(BF16) for i in range(n)]
        mo = [jnp.where(lvl, mm[i], 0.0).astype(BF16) for i in range(n)]
        if blk % 8:
            x = [_dot(tb[i], mo[i]).astype(BF16) for i in range(n)]
            y = [_dot(x[i], tb[i]) for i in range(n)]
            t = [t[i] - y[i] for i in range(n)]
        else:
            split = [t[i].reshape(r // (2 * blk), 2, blk, r) for i in range(n)]
            lo = [split[i][:, 1].reshape(r // 2, r) for i in range(n)]
            x = [_dot(lo[i].astype(BF16), mo[i]).astype(BF16) for i in range(n)]
            y = [_dot(x[i], tb[i]) for i in range(n)]
            t = [jnp.stack([split[i][:, 0], (lo[i] - y[i]).reshape(r // (2 * blk), blk, r)], axis=1).reshape(r, r)
                 for i in range(n)]
    uw = [_dot(t[i].astype(BF16), rhs[i]) for i in range(n)]
    return [(uw[i][:, :DK_A], uw[i][:, DK_A:].astype(BF16), qgs[i], kds[i], qk[i]) for i in range(n)]


def _gated_norm(o, z, w_onorm):
    o = o * lax.rsqrt(jnp.mean(o * o, -1, keepdims=True) + RMS_EPS) * w_onorm
    return o * _silu(z)


def _delta_prompt_body(cs_ref, z_ref, gb_ref, wn_ref, o_ref, s_out_ref,
                       s_ref, u_s, o_s, wq_s, qkd_s, egl_s, *, tt, side_work=None):
    n_c = tt // CHUNK
    masks = _delta_masks(tt, CHUNK)

    @pl.when(pl.program_id(1) == 0)
    def _():
        s_ref[...] = jnp.zeros(s_ref.shape, F32)

    finish_side_work = side_work() if side_work is not None else None

    gbv = gb_ref[...]
    gc_all, rev_all = _decay_sums(gbv, masks)
    gc_t = gc_all.T
    heads = range(N_HEADS_A)
    hcols = [slice(hd * DK_A, (hd + 1) * DK_A) for hd in heads]
    problems = []
    for hd in heads:
        lg = N_HEADS_A + hd
        problems.append((cs_ref[:, hcols[hd]].astype(F32),
                         cs_ref[:, D_MODEL + hd * DK_A:D_MODEL + (hd + 1) * DK_A].astype(F32),
                         cs_ref[:, 2 * D_MODEL + hd * DK_A:2 * D_MODEL + (hd + 1) * DK_A].astype(F32),
                         jnp.broadcast_to(gbv[:, hd:hd + 1], (tt, DK_A)),
                         jnp.broadcast_to(gc_all[:, lg:lg + 1], (tt, DK_A)),
                         jnp.broadcast_to(rev_all[:, lg:lg + 1], (tt, DK_A)),
                         jnp.broadcast_to(gc_t[lg:lg + 1, :], (tt, tt))))
    for hd, (u, w, qg, kd, qk) in enumerate(_delta_pre(problems, masks)):
        u_s[:, hcols[hd]] = u
        kdt = kd.T
        for c in range(n_c):
            blk = slice(c * CHUNK, (c + 1) * CHUNK)
            idx = hd * n_c + c
            wq_s[idx, 0:CHUNK, :] = w[blk]
            wq_s[idx, CHUNK:2 * CHUNK, :] = qg[blk]
            qkd_s[idx, 0:CHUNK, :] = qk[blk, blk]
            qkd_s[idx, CHUNK:CHUNK + DK_A, :] = kdt[:, blk]
            last = (c + 1) * CHUNK - 1
            egl_s[idx] = jnp.broadcast_to(jnp.exp(problems[hd][4][last:last + 1, :]), (8, DK_A))

    if finish_side_work is not None:
        finish_side_work()

    states = [s_ref[hd] for hd in heads]
    for c in range(n_c):
        blk = slice(c * CHUNK, (c + 1) * CHUNK)
        idx = [hd * n_c + c for hd in heads]
        ws = [_dot(wq_s[idx[hd]], states[hd].astype(BF16)) for hd in heads]
        v_new = [(u_s[blk, hcols[hd]] - ws[hd][:CHUNK]).astype(BF16) for hd in heads]
        upd = [_dot(qkd_s[idx[hd]], v_new[hd]) for hd in heads]
        for hd in heads:
            o_s[blk, hcols[hd]] = ws[hd][CHUNK:] + upd[hd][:CHUNK]
            states[hd] = states[hd] * egl_s[idx[hd]][0:1, :] + upd[hd][CHUNK:]
    for hd in heads:
        s_ref[hd] = states[hd]
        o_ref[:, hcols[hd]] = _gated_norm(o_s[:, hcols[hd]], z_ref[:, hcols[hd]].astype(F32), wn_ref[...]).astype(BF16)

    @pl.when(pl.program_id(1) == pl.num_programs(1) - 1)
    def _():
        s_out_ref[...] = s_ref[...]


def _softmax_keys(s):
    p = jnp.exp(s - jnp.max(s, axis=0, keepdims=True))
    return (p / jnp.sum(p, axis=0, keepdims=True)).astype(BF16)


def _zero_key0(x):
    return jnp.where(lax.broadcasted_iota(jnp.int32, x.shape, 0) == 0, 0.0, x)


def _kv_cols(kvh):
    return slice(kvh * HD_B, (kvh + 1) * HD_B)


def _group_queries(q, kvh):
    return jnp.concatenate([q[:, hh * HD_B:(hh + 1) * HD_B] for hh in range(kvh * GROUP_B, (kvh + 1) * GROUP_B)],
                           axis=0)


def _swa_prompt_cache(kc_ref, vc_ref, ko_ref, vo_ref, nq):
    @pl.when(pl.program_id(1) == pl.num_programs(1) - 1)
    def _():
        ko_ref[...] = kc_ref[(nq - 1) * WINDOW:, :]
        vo_ref[...] = vc_ref[(nq - 1) * WINDOW:, :]


def _swa_prompt_start(q_ref, kp_ref, kc_ref, vp_ref, vc_ref, bias_ref, o_ref, nq):
    n_q = GROUP_B * WINDOW
    key = lax.broadcasted_iota(jnp.int32, (2 * WINDOW, n_q), 0)
    no_prev = (pl.program_id(1) == 0) & (key >= 1) & (key < WINDOW)
    kall = jnp.concatenate([kp_ref[...], kc_ref[...]], axis=0)
    vall = jnp.concatenate([vp_ref[...], vc_ref[...]], axis=0)
    probs = [(j, kvh) for j in range(nq) for kvh in range(N_KV_B)]
    k2 = [_zero_key0(kall[j * WINDOW:(j + 2) * WINDOW]).astype(BF16) for j in range(nq)]
    v2t = [_zero_key0(vall[j * WINDOW:(j + 2) * WINDOW]).T.astype(BF16) for j in range(nq)]
    q = [q_ref[j * WINDOW:(j + 1) * WINDOW, :] for j in range(nq)]
    s = [_dot_nt(k2[j][:, _kv_cols(kvh)], _group_queries(q[j], kvh)) for j, kvh in probs]
    pn = []
    for (j, kvh), sc in zip(probs, s):
        sc = sc + bias_ref[kvh]
        pn.append(_softmax_keys(jnp.where(no_prev, -jnp.inf, sc) if j == 0 else sc))

    def finish():
        ot = [_dot(v2t[j][_kv_cols(kvh), :], p) for (j, kvh), p in zip(probs, pn)]
        for (j, kvh), o in zip(probs, ot):
            for g in range(GROUP_B):
                hh = kvh * GROUP_B + g
                o_ref[j * WINDOW:(j + 1) * WINDOW, hh * HD_B:(hh + 1) * HD_B] = \
                    o[:, g * WINDOW:(g + 1) * WINDOW].T.astype(BF16)

    return finish


def _mixer_prompt_body(cs_ref, z_ref, gb_ref, wn_ref, q_ref, kp_ref, kc_ref, vp_ref, vc_ref, bias_ref,
                       oa_ref, s_out_ref, ob_ref, ko_ref, vo_ref, *scratch, tt):
    nq = tt // WINDOW
    _swa_prompt_cache(kc_ref, vc_ref, ko_ref, vo_ref, nq)
    attention = functools.partial(_swa_prompt_start, q_ref, kp_ref, kc_ref, vp_ref, vc_ref, bias_ref, ob_ref, nq)
    _delta_prompt_body(cs_ref, z_ref, gb_ref, wn_ref, oa_ref, s_out_ref, *scratch, tt=tt, side_work=attention)


def _mixer_prompt(cs, z, gb, w_onorm, qb, k, v, bias_t, *, tt):
    b, l, _ = cs.shape
    n_c = tt // CHUNK
    nq = tt // WINDOW
    tok_spec = lambda n: pl.BlockSpec((None, tt, n), lambda i, t: (i, t, 0))
    prev = lambda n: pl.BlockSpec((None, WINDOW, n), lambda i, t: (i, jnp.maximum(nq * t - 1, 0), 0))
    last = lambda n: pl.BlockSpec((None, WINDOW, n), lambda i, t: (i, 0, 0))
    return pl.pallas_call(
        functools.partial(_mixer_prompt_body, tt=tt),
        grid=(b, l // tt),
        in_specs=[tok_spec(CONV_DIM), tok_spec(D_MODEL), tok_spec(GB_LANES), _const_spec((1, DK_A)),
                  tok_spec(D_MODEL), prev(KV_B), tok_spec(KV_B), prev(KV_B), tok_spec(KV_B),
                  _const_spec((N_KV_B, 2 * WINDOW, GROUP_B * WINDOW))],
        out_specs=[tok_spec(D_MODEL), pl.BlockSpec((None, N_HEADS_A, DK_A, DK_A), lambda i, t: (i, 0, 0, 0)),
                   tok_spec(D_MODEL), last(KV_B), last(KV_B)],
        out_shape=[jax.ShapeDtypeStruct((b, l, D_MODEL), BF16),
                   jax.ShapeDtypeStruct((b, N_HEADS_A, DK_A, DK_A), F32),
                   jax.ShapeDtypeStruct((b, l, D_MODEL), BF16),
                   jax.ShapeDtypeStruct((b, WINDOW, KV_B), F32), jax.ShapeDtypeStruct((b, WINDOW, KV_B), F32)],
        scratch_shapes=[pltpu.VMEM((N_HEADS_A, DK_A, DK_A), F32),
                        pltpu.VMEM((tt, D_MODEL), F32), pltpu.VMEM((tt, D_MODEL), F32),
                        pltpu.VMEM((N_HEADS_A * n_c, 2 * CHUNK, DK_A), BF16),
                        pltpu.VMEM((N_HEADS_A * n_c, CHUNK + DK_A, CHUNK), BF16),
                        pltpu.VMEM((N_HEADS_A * n_c, 8, DK_A), F32)],
        compiler_params=_params(2),
        name="mixer_prompt",
    )(cs, z, gb, w_onorm, qb, k, k, v, v, bias_t)


def _mixer_sample_body(cs_ref, z_ref, gb_ref, wn_ref, s0_ref, q_ref, kn_ref, vn_ref, kc_ref, vc_ref, bias_ref, *rest,
                       bt, nb, ls, fill_layers):
    oa_ref, s_out_ref, ob_ref, ko_ref, vo_ref = rest[-5:]

    def put(ref, idx, val):
        idx = idx if isinstance(idx, tuple) else (idx,)
        if fill_layers:
            for layer_slot in range(fill_layers):
                ref[(layer_slot,) + idx] = val
        else:
            ref[idx] = val
    r = N_HEADS_A * ls
    masks = _delta_masks(r, ls)
    heads = range(N_HEADS_A)
    hrows = [slice(hd * ls, (hd + 1) * ls) for hd in heads]
    hcols = [slice(hd * DK_A, (hd + 1) * DK_A) for hd in heads]
    kvs = range(N_KV_B)

    def per_group(gi, carry):
        bis = [gi * nb + d for d in range(nb)]
        ks, vs, scores = [], [], []
        for bi in bis:
            kf = jnp.concatenate([kc_ref[bi], kn_ref[bi]], axis=0)
            vf = jnp.concatenate([vc_ref[bi], vn_ref[bi]], axis=0)
            put(ko_ref, bi, kf[ls:, :])
            put(vo_ref, bi, vf[ls:, :])
            ks.append(_zero_key0(kf).astype(BF16))
            vs.append(_zero_key0(vf).astype(BF16))
        for d, bi in enumerate(bis):
            q = q_ref[bi]
            scores.append([_dot_nt(ks[d][:, _kv_cols(kvh)], _group_queries(q, kvh)) for kvh in kvs])
        pn = [[_softmax_keys(scores[d][kvh] + bias_ref[kvh]) for kvh in kvs] for d in range(nb)]

        problems, egl = [], []
        for bi in bis:
            cs = cs_ref[bi].astype(F32)
            gbv = gb_ref[bi]
            stack = lambda base: jnp.concatenate([cs[:, base + hd * DK_A: base + (hd + 1) * DK_A] for hd in heads],
                                                 axis=0)
            beta_b = jnp.concatenate([jnp.broadcast_to(gbv[:, hd:hd + 1], (ls, DK_A)) for hd in heads], axis=0)
            g_b = jnp.concatenate(
                [jnp.broadcast_to(gbv[:, N_HEADS_A + hd:N_HEADS_A + hd + 1], (ls, DK_A)) for hd in heads], axis=0)
            gc, rev = _decay_sums(g_b, masks)
            egl.append(jnp.exp(gc))
            problems.append((stack(0), stack(D_MODEL), stack(2 * D_MODEL), beta_b, gc, rev, gc.T[:r, :]))
        pre = _delta_pre(problems, masks)
        both = [[_dot(jnp.concatenate([pre[d][1][hrows[hd]], pre[d][2][hrows[hd]]], axis=0),
                      s0_ref[bis[d], hd].astype(BF16)) for hd in heads] for d in range(nb)]
        v_new = [(pre[d][0] - jnp.concatenate([both[d][hd][:ls] for hd in heads], axis=0)).astype(BF16)
                 for d in range(nb)]
        qkv = [_dot(pre[d][4], v_new[d]) for d in range(nb)]
        upd = [[_dot_tn(pre[d][3][hrows[hd]], v_new[d][hrows[hd]]) for hd in heads] for d in range(nb)]
        att = [[_dot_tn(pn[d][kvh], vs[d][:, _kv_cols(kvh)]) for kvh in kvs] for d in range(nb)]
        for d, bi in enumerate(bis):
            zb = z_ref[bi].astype(F32)
            for hd in heads:
                last = (hd + 1) * ls - 1
                put(s_out_ref, (bi, hd), s0_ref[bi, hd] * egl[d][last:last + 1, :] + upd[d][hd])
                o = both[d][hd][ls:] + qkv[d][hrows[hd]]
                oa_ref[bi, :, hcols[hd]] = _gated_norm(o, zb[:, hcols[hd]], wn_ref[...]).astype(BF16)
            for kvh in kvs:
                for g in range(GROUP_B):
                    hh = kvh * GROUP_B + g
                    ob_ref[bi, :, hh * HD_B:(hh + 1) * HD_B] = att[d][kvh][g * ls:(g + 1) * ls].astype(BF16)
        return carry

    lax.fori_loop(0, bt // nb, per_group, 0)


def _mixer_sample(cs, z, gb, w_onorm, s0_all, qb, k_new, v_new, k_cache_all, v_cache_all, bias_s, prev, *, layer, bt):
    b, ls, _ = cs.shape
    tok_spec = lambda n: pl.BlockSpec((bt, ls, n), lambda i: (i, 0, 0))
    st_spec = pl.BlockSpec((None, bt, N_HEADS_A, DK_A, DK_A), lambda i: (layer, i, 0, 0, 0))
    cache = pl.BlockSpec((None, bt, WINDOW, KV_B), lambda i: (layer, i, 0, 0))
    extra, extra_specs = ([], []) if prev is None else (list(prev), [pl.BlockSpec(memory_space=pl.ANY)] * 3)
    depth = s0_all.shape[0]
    if extra:
        st_out, cache_out = st_spec, cache
    else:
        st_out = pl.BlockSpec((depth, bt, N_HEADS_A, DK_A, DK_A), lambda i: (0, i, 0, 0, 0))
        cache_out = pl.BlockSpec((depth, bt, WINDOW, KV_B), lambda i: (0, i, 0, 0))
    n_in = 11
    return pl.pallas_call(
        functools.partial(_mixer_sample_body, bt=bt, nb=min(bt, MIXER_SAMPLE_ROWS_PER_TRIP), ls=ls,
                          fill_layers=0 if extra else depth),
        grid=(b // bt,),
        in_specs=[tok_spec(CONV_DIM), tok_spec(D_MODEL), tok_spec(GB_LANES), _const_spec((1, DK_A)), st_spec,
                  tok_spec(D_MODEL), tok_spec(KV_B), tok_spec(KV_B), cache, cache,
                  _const_spec((N_KV_B, WINDOW + ls, GROUP_B * ls))] + extra_specs,
        out_specs=[tok_spec(D_MODEL), st_out, tok_spec(D_MODEL), cache_out, cache_out],
        out_shape=[jax.ShapeDtypeStruct((b, ls, D_MODEL), BF16), jax.ShapeDtypeStruct(s0_all.shape, F32),
                   jax.ShapeDtypeStruct((b, ls, D_MODEL), BF16),
                   jax.ShapeDtypeStruct(k_cache_all.shape, F32), jax.ShapeDtypeStruct(v_cache_all.shape, F32)],
        input_output_aliases={n_in: 1, n_in + 1: 3, n_in + 2: 4} if extra else {},
        compiler_params=_params(1),
        name="mixer_sample",
    )(cs, z, gb, w_onorm, s0_all, qb, k_new, v_new, k_cache_all, v_cache_all, bias_s, *extra)


def _layer_norm(y, g, b):
    mu = jnp.mean(y, -1, keepdims=True)
    d = y - mu
    var = jnp.mean(d * d, -1, keepdims=True)
    return d * lax.rsqrt(var + LN_EPS) * g + b


def _tail_body(x_ref, oa_ref, ob_ref, gates_ref, mod_ref, wpa_ref, wpb_ref, wout_ref, ln1g_ref, ln1b_ref,
               wup_ref, wdn_ref, ln2g_ref, ln2b_ref, o_ref, *, bt, tt, alpha):
    mod = lambda i: mod_ref[:, :, i * D_MODEL:(i + 1) * D_MODEL]
    parts = [slice(0, tt // 2), slice(tt // 2, tt)] if bt == 1 and tt % (2 * BF16_SUBLANES) == 0 else [slice(0, tt)]
    ms = [bt * (p.stop - p.start) for p in parts]
    two = range(len(parts))
    flat = lambda a, i: a.reshape(ms[i], a.shape[-1])
    dot = _dot if len(parts) > 1 else _dot_rows
    pa = [dot(flat(oa_ref[:, p, :], i), wpa_ref[...]) for i, p in enumerate(parts)]
    pb = [dot(flat(ob_ref[:, p, :], i), wpb_ref[...]) for i, p in enumerate(parts)]
    mixed = [flat(gates_ref[:, p, 0:D_MODEL].astype(F32), i) * pa[i]
             + flat(gates_ref[:, p, D_MODEL:2 * D_MODEL].astype(F32), i) * pb[i] for i, p in enumerate(parts)]
    attn = [dot(mixed[i].astype(BF16), wout_ref[...]) for i in two]
    x1 = [_layer_norm(alpha * x_ref[:, p, :] + mod(2) * attn[i].reshape(bt, -1, D_MODEL), ln1g_ref[...], ln1b_ref[...])
          for i, p in enumerate(parts)]
    h2 = [flat(x1[i] * (1.0 + mod(4)) + mod(3), i).astype(BF16) for i in two]
    ff = [jnp.zeros((ms[i], D_MODEL), F32) for i in two]
    for c in range(D_FF // D_MODEL):
        cols = slice(c * D_MODEL, (c + 1) * D_MODEL)
        a = [jnp.maximum(dot(h2[i], wup_ref[:, cols]), 0.0) for i in two]
        ff = [ff[i] + dot((a[i] * a[i]).astype(BF16), wdn_ref[cols, :]) for i in two]
    for i, p in enumerate(parts):
        o_ref[:, p, :] = _layer_norm(alpha * x1[i] + mod(5) * ff[i].reshape(bt, -1, D_MODEL),
                                     ln2g_ref[...], ln2b_ref[...])


def _tail(x, oa, ob, gates, mod, w, p, *, layer, bt, tt, alpha):
    b, l, _ = x.shape
    tok_spec = lambda n: pl.BlockSpec((bt, tt, n), lambda i, t: (i, t, 0))
    row = _const_spec((1, D_MODEL))
    sq = _layer_spec((D_MODEL, D_MODEL), layer)
    return pl.pallas_call(
        functools.partial(_tail_body, bt=bt, tt=tt, alpha=alpha),
        grid=(b // bt, l // tt),
        in_specs=[tok_spec(D_MODEL), tok_spec(D_MODEL), tok_spec(D_MODEL), tok_spec(2 * D_MODEL),
                  pl.BlockSpec((bt, 1, 6 * D_MODEL), lambda i, t: (i, 0, 0)),
                  sq, sq, sq, row, row,
                  _layer_spec((D_MODEL, D_FF), layer), _layer_spec((D_FF, D_MODEL), layer), row, row],
        out_specs=tok_spec(D_MODEL),
        out_shape=jax.ShapeDtypeStruct((b, l, D_MODEL), F32),
        compiler_params=_params(2),
        name="tail",
    )(x, oa, ob, gates, mod, w["wpa"], w["wpb"], w["wout"], p["ln1_g"], p["ln1_b"],
      w["wup"], w["wdn"], p["ln2_g"], p["ln2_b"])


def _tiles(b, l, rows):
    tt = min(l, rows)
    bt = max(1, min(b, rows // tt))
    return bt, tt


FRONT_ROWS_LONG, FRONT_ROWS_SHORT = 512, 256
PROJ_ROWS = 128
MIXER_PROMPT_ROWS = 2 * WINDOW
MIXER_SAMPLE_BATCH_ROWS = 8
MIXER_SAMPLE_ROWS_PER_TRIP = 8
TAIL_ROWS = 512


def _layer_params(l, w_conv, a_log, dt_bias, w_onorm, ln1_g, ln1_b, ln2_g, ln2_b):
    gpar = jnp.zeros((2, GB_LANES), F32)
    gpar = gpar.at[0, N_HEADS_A:2 * N_HEADS_A].set(-jnp.exp(a_log[l].astype(F32)))
    gpar = gpar.at[1, N_HEADS_A:2 * N_HEADS_A].set(dt_bias[l].astype(F32))
    row = lambda a: a[l].reshape(1, -1).astype(F32)
    return dict(wconv=w_conv[l].astype(F32), gpar=gpar, w_onorm=row(w_onorm), ln1_g=row(ln1_g), ln1_b=row(ln1_b),
                ln2_g=row(ln2_g), ln2_b=row(ln2_b))


def _trunk_layer(x, mod, layer, w, p, alpha, conv_state, sample_state):
    b, l, _ = x.shape
    bt, tt = _tiles(b, l, FRONT_ROWS_LONG if l % FRONT_ROWS_LONG == 0 else FRONT_ROWS_SHORT)
    cs, z, gb, qb, k, v, gates, conv_out = _front(x, mod, conv_state, w["wm"], w["wba"], p["wconv"], p["gpar"],
                                                  layer=layer, bt=bt, tt=tt)
    if sample_state is None:
        oa, s_new, ob, k_new, v_new = _mixer_prompt(cs, z, gb, p["w_onorm"], qb, k, v, p["bias_t"],
                                                    tt=min(l, MIXER_PROMPT_ROWS))
    else:
        s0_all, kc_all, vc_all, s_prev, k_prev, v_prev = sample_state
        bias_s = p["bias_t"][:, :WINDOW + l, :].reshape(N_KV_B, WINDOW + l, GROUP_B, WINDOW)[..., :l]
        bias_s = bias_s.reshape(N_KV_B, WINDOW + l, GROUP_B * l)
        prev = None if s_prev is None else (s_prev, k_prev, v_prev)
        oa, s_new, ob, k_new, v_new = _mixer_sample(cs, z, gb, p["w_onorm"], s0_all, qb, k, v, kc_all, vc_all, bias_s,
                                                    prev, layer=layer, bt=min(b, MIXER_SAMPLE_BATCH_ROWS))
    bt, tt = _tiles(b, l, TAIL_ROWS)
    x2 = _tail(x, oa, ob, gates, mod, w, p, layer=layer, bt=bt, tt=tt, alpha=alpha)
    return x2, (s_new, conv_out, k_new, v_new)


def kernel(x_prompt, x_sample, state_delta, state_conv, cache_k, cache_v, c_prompt, c_sample, rel_bias, w_ada, b_ada, w_in, w_conv, a_log, dt_bias, w_onorm, sinks, w_pa, w_pb, w_out, ln1_g, ln1_b, w_up, w_down, ln2_g, ln2_b):
    depth = w_in.shape[0]
    alpha = (2 * depth) ** 0.25
    bp = x_prompt.shape[0]
    mod_all = _ada(jnp.concatenate([c_prompt, c_sample], axis=0), w_ada, b_ada)
    bias_t = _bias_table_t(rel_bias, sinks)
    wm, wba = _prep_w_in(w_in)
    w = dict(wm=wm, wba=wba, wpa=_cast_bf16(w_pa), wpb=_cast_bf16(w_pb), wout=_cast_bf16(w_out),
             wup=_cast_bf16(w_up), wdn=_cast_bf16(w_down))
    bs = x_sample.shape[0]
    kc_all = cache_k.reshape(depth, bs, WINDOW, KV_B)
    vc_all = cache_v.reshape(depth, bs, WINDOW, KV_B)
    yp, ys = x_prompt, x_sample
    prompt_outs = [[] for _ in range(4)]
    sample_conv = []
    s_all = k_all = v_all = None
    for l in range(depth):
        p = _layer_params(l, w_conv, a_log, dt_bias, w_onorm, ln1_g, ln1_b, ln2_g, ln2_b)
        p["bias_t"] = bias_t[l]
        mod_p = mod_all[l, :bp][:, None, :]
        mod_s = mod_all[l, bp:][:, None, :]
        zero_conv = jnp.zeros((bp, CONV_W - 1, CONV_DIM), x_prompt.dtype)
        yp, rest_p = _trunk_layer(yp, mod_p, l, w, p, alpha, zero_conv, None)
        ys, (s_all, conv_s, k_all, v_all) = _trunk_layer(ys, mod_s, l, w, p, alpha, state_conv[l],
                                                         (state_delta, kc_all, vc_all, s_all, k_all, v_all))
        for acc, val in zip(prompt_outs, rest_p):
            acc.append(val)
        sample_conv.append(conv_s)
    heads = lambda a: a.reshape(a.shape[:-1] + (N_KV_B, HD_B))
    pd, pc, pk, pv = (jnp.stack(a) for a in prompt_outs)
    return (yp, ys, pd, pc, heads(pk), heads(pv), s_all, jnp.stack(sample_conv), heads(k_all), heads(v_all))
```

```python
import collections
import functools
import math

import numpy as np
import jax
import jax.numpy as jnp
from jax import lax
from jax.experimental import pallas as pl
from jax.experimental.pallas import tpu as pltpu

F32 = jnp.float32
BF16 = jnp.bfloat16

D_MODEL = 1024
N_HEADS_A = 8
DK_A = 128
CONV_W = 4
CONV_DIM = 3 * D_MODEL
CHUNK = 64
HD_B = 64
N_HEADS_B = 16
N_KV_B = 4
GROUP_B = N_HEADS_B // N_KV_B
KV_B = N_KV_B * HD_B
WINDOW = 128
N_BUCKETS = 32
MAX_DISTANCE = 128
D_FF = 4 * D_MODEL
LN_EPS = 1e-5
RMS_EPS = 1e-6

OFF_Z = CONV_DIM
OFF_QB = OFF_Z + D_MODEL
OFF_KVB = OFF_QB + D_MODEL
OFF_GATES = OFF_KVB + 2 * KV_B
N_MAIN = OFF_GATES + 2 * D_MODEL
GB_LANES = 128

F32_SUBLANES = 8
BF16_SUBLANES = 16
V7X_VMEM_BYTES = 64 * 1024 * 1024
VMEM_LIMIT = V7X_VMEM_BYTES - 8 * 1024 * 1024


def _params(n_grid):
    return pltpu.CompilerParams(dimension_semantics=("arbitrary",) * n_grid, vmem_limit_bytes=VMEM_LIMIT)


def _dot(a, b):
    return jnp.dot(a, b, preferred_element_type=F32)


def _dot_rows(a, b):
    m = a.shape[0]
    if m <= PROJ_ROWS or m % PROJ_ROWS:
        return _dot(a, b)
    return jnp.concatenate([_dot(a[r:r + PROJ_ROWS], b) for r in range(0, m, PROJ_ROWS)], axis=0)


def _dot_nt(a, b):
    return lax.dot_general(a, b, (((1,), (1,)), ((), ())), preferred_element_type=F32)


def _dot_tn(a, b):
    return lax.dot_general(a, b, (((0,), (0,)), ((), ())), preferred_element_type=F32)


def _sigmoid(x):
    return 1.0 / (1.0 + jnp.exp(-x))


def _silu(x):
    return x * _sigmoid(x)


def _const_spec(shape):
    nd = len(shape)
    return pl.BlockSpec(shape, lambda *_: (0,) * nd, pipeline_mode=pl.Buffered(1))


def _layer_spec(shape, layer):
    nd = len(shape)
    return pl.BlockSpec((None,) + tuple(shape), lambda *_: (layer,) + (0,) * nd, pipeline_mode=pl.Buffered(1))


def _cast_body(w_ref, o_ref):
    o_ref[...] = w_ref[...].astype(BF16)


def _cast_bf16(w):
    depth, k, n = w.shape
    bk = max(8, min(k, (1024 * 1024) // n))
    spec = pl.BlockSpec((None, bk, n), lambda l, i: (l, i, 0))
    return pl.pallas_call(
        _cast_body, grid=(depth, k // bk), in_specs=[spec], out_specs=spec,
        out_shape=jax.ShapeDtypeStruct(w.shape, BF16), compiler_params=_params(2), name="cast_bf16",
    )(w)


def _prep_w_in(w_in):
    o_ba = CONV_DIM + D_MODEL
    o_qb = o_ba + 2 * N_HEADS_A
    wm = jnp.concatenate([w_in[:, :, :o_ba], w_in[:, :, o_qb:]], axis=2).astype(BF16)
    wba = jnp.pad(w_in[:, :, o_ba:o_qb], ((0, 0), (0, 0), (0, GB_LANES - 2 * N_HEADS_A))).astype(BF16)
    return wm, wba


def _ada_body(c_ref, w_ref, b_ref, o_ref):
    s = _silu(c_ref[...]).astype(BF16)
    o_ref[...] = _dot(s, w_ref[...].astype(BF16)) + b_ref[...]


def _ada(c_all, w_ada, b_ada):
    depth = w_ada.shape[0]
    n_rows = c_all.shape[0]
    n_col = w_ada.shape[2] // D_MODEL
    return pl.pallas_call(
        _ada_body,
        grid=(depth, n_col),
        in_specs=[pl.BlockSpec((n_rows, D_MODEL), lambda l, n: (0, 0)),
                  pl.BlockSpec((None, D_MODEL, D_MODEL), lambda l, n: (l, 0, n)),
                  pl.BlockSpec((None, 1, D_MODEL), lambda l, n: (l, 0, n))],
        out_specs=pl.BlockSpec((None, n_rows, D_MODEL), lambda l, n: (l, 0, n)),
        out_shape=jax.ShapeDtypeStruct((depth, n_rows, w_ada.shape[2]), F32),
        compiler_params=_params(2),
        name="ada",
    )(c_all, w_ada, b_ada.reshape(depth, 1, -1))


def _bucket_table():
    r = np.arange(WINDOW)[:, None]
    c = np.arange(2 * WINDOW)[None, :]
    dist = WINDOW + r - c
    n = np.maximum(dist, 0)
    max_exact = N_BUCKETS // 2
    ratio = np.maximum(n, max_exact).astype(np.float32) / np.float32(max_exact)
    large = max_exact + (np.log(ratio) / np.float32(math.log(MAX_DISTANCE / max_exact))
                         * np.float32(N_BUCKETS - max_exact)).astype(np.int32)
    large = np.minimum(large, N_BUCKETS - 1)
    bucket = np.where(n < max_exact, n, large).astype(np.int32)
    valid = ((dist >= 0) & (dist < WINDOW)).astype(np.int32)
    return bucket, valid


def _bias_t_body(rb_ref, sink_ref, bucket_ref, valid_ref, o_ref):
    l = pl.program_id(0)
    h = pl.program_id(1)
    bucket = bucket_ref[...]
    acc = jnp.zeros(bucket.shape, F32)
    for j in range(N_BUCKETS):
        acc = jnp.where(bucket == j, rb_ref[j, h], acc)
    acc = jnp.where(valid_ref[...] > 0, acc, -jnp.inf)
    key = lax.broadcasted_iota(jnp.int32, bucket.shape, 0)
    o_ref[...] = jnp.where(key == 0, sink_ref[l, h], acc)


def _bias_table_t(rel_bias, sinks):
    depth = sinks.shape[0]
    bucket, valid = _bucket_table()
    return pl.pallas_call(
        _bias_t_body,
        grid=(depth, N_HEADS_B),
        in_specs=[pl.BlockSpec(memory_space=pltpu.SMEM), pl.BlockSpec(memory_space=pltpu.SMEM),
                  pl.BlockSpec((2 * WINDOW, WINDOW), lambda l, h: (0, 0)),
                  pl.BlockSpec((2 * WINDOW, WINDOW), lambda l, h: (0, 0))],
        out_specs=pl.BlockSpec((None, None, 2 * WINDOW, WINDOW), lambda l, h: (l, h // GROUP_B, 0, h % GROUP_B)),
        out_shape=jax.ShapeDtypeStruct((depth, N_KV_B, 2 * WINDOW, GROUP_B * WINDOW), F32),
        compiler_params=_params(2),
        name="bias_table_t",
    )(rel_bias.astype(F32), sinks.astype(F32), jnp.asarray(bucket.T.copy()), jnp.asarray(valid.T.copy()))


def _front_body(x_ref, mod_ref, st_ref, wm_ref, wba_ref, wconv_ref, gpar_ref,
                cs_ref, z_ref, gb_ref, qb_ref, k_ref, v_ref, gates_ref, cst_ref, cbuf, *, bt, tt):
    m = bt * tt
    pad, hist = F32_SUBLANES, CONV_W - 1
    sh1 = mod_ref[:, :, 0:D_MODEL]
    sc1 = mod_ref[:, :, D_MODEL:2 * D_MODEL]
    h = (x_ref[...] * (1.0 + sc1) + sh1).reshape(m, D_MODEL).astype(BF16)

    @pl.when(pl.program_id(1) == 0)
    def _():
        cbuf[:, 0:pad, :] = jnp.zeros((bt, pad, CONV_DIM), F32)
        cbuf[:, pad - hist:pad, :] = st_ref[...]

    def proj(off, n):
        return _dot_rows(h, wm_ref[:, off:off + n])

    half = D_MODEL // 2

    def conv_piece(i):
        cols = slice(i * half, (i + 1) * half)
        cbuf[:, pad:pad + tt, cols] = proj(i * half, half).reshape(bt, tt, half)
        u = cbuf[:, :, cols]
        u1 = pltpu.roll(u, 1, 1)
        w0, w1, w2, w3 = (wconv_ref[j:j + 1, cols] for j in range(CONV_W))
        y = (u * w3 + u1 * w2) + pltpu.roll(u * w1 + u1 * w0, 2, 1)
        cs_ref[:, :, cols] = _silu(y[:, pad:, :]).astype(cs_ref.dtype)

    def z_piece(i):
        z_ref[:, :, i * half:(i + 1) * half] = proj(OFF_Z + i * half, half).astype(z_ref.dtype).reshape(bt, tt, half)

    def qb_piece(i):
        q = proj(OFF_QB + i * half, half) * (HD_B ** -0.5)
        qb_ref[:, :, i * half:(i + 1) * half] = q.astype(BF16).reshape(bt, tt, half)

    def kv_piece(_):
        kv = proj(OFF_KVB, 2 * KV_B)
        k_ref[...] = kv[:, :KV_B].reshape(bt, tt, KV_B)
        v_ref[...] = kv[:, KV_B:].reshape(bt, tt, KV_B)

    def gates_piece(i):
        g = _sigmoid(proj(OFF_GATES + i * half, half))
        gates_ref[:, :, i * half:(i + 1) * half] = g.astype(gates_ref.dtype).reshape(bt, tt, half)

    for piece, i in [(conv_piece, 0), (z_piece, 0), (conv_piece, 1), (z_piece, 1), (conv_piece, 2), (qb_piece, 0),
                     (conv_piece, 3), (qb_piece, 1), (conv_piece, 4), (kv_piece, 0), (gates_piece, 0),
                     (conv_piece, 5), (gates_piece, 1), (gates_piece, 2), (gates_piece, 3)]:
        piece(i)
    tail = cbuf[:, tt + pad - hist:tt + pad, :]
    cst_ref[...] = tail
    cbuf[:, pad - hist:pad, :] = tail

    ba = _dot(h, wba_ref[...])
    xg = ba + gpar_ref[1:2, :]
    softplus = jnp.maximum(xg, 0.0) + jnp.log(1.0 + jnp.exp(-jnp.abs(xg)))
    lane = lax.broadcasted_iota(jnp.int32, ba.shape, 1)
    gb = jnp.where(lane < N_HEADS_A, _sigmoid(ba), gpar_ref[0:1, :] * softplus)
    gb_ref[...] = gb.reshape(bt, tt, GB_LANES)


def _front(x, mod, conv_state, wm, wba, wconv, gpar, *, layer, bt, tt):
    b, l, _ = x.shape
    grid = (b // bt, l // tt)
    tok = lambda n, dt=F32: jax.ShapeDtypeStruct((b, l, n), dt)
    act = BF16 if tt % BF16_SUBLANES == 0 else F32
    tok_spec = lambda n: pl.BlockSpec((bt, tt, n), lambda i, t: (i, t, 0))
    return pl.pallas_call(
        functools.partial(_front_body, bt=bt, tt=tt),
        grid=grid,
        in_specs=[tok_spec(D_MODEL),
                  pl.BlockSpec((bt, 1, 6 * D_MODEL), lambda i, t: (i, 0, 0)),
                  pl.BlockSpec((bt, CONV_W - 1, CONV_DIM), lambda i, t: (i, 0, 0)),
                  _layer_spec((D_MODEL, N_MAIN), layer),
                  _layer_spec((D_MODEL, GB_LANES), layer),
                  _const_spec((CONV_W, CONV_DIM)),
                  _const_spec((2, GB_LANES))],
        out_specs=[tok_spec(CONV_DIM), tok_spec(D_MODEL), tok_spec(GB_LANES), tok_spec(D_MODEL),
                   tok_spec(KV_B), tok_spec(KV_B), tok_spec(2 * D_MODEL),
                   pl.BlockSpec((bt, CONV_W - 1, CONV_DIM), lambda i, t: (i, 0, 0))],
        out_shape=[tok(CONV_DIM, act), tok(D_MODEL, act), tok(GB_LANES), tok(D_MODEL, BF16),
                   tok(KV_B), tok(KV_B), tok(2 * D_MODEL, act),
                   jax.ShapeDtypeStruct((b, CONV_W - 1, CONV_DIM), F32)],
        scratch_shapes=[pltpu.VMEM((bt, tt + F32_SUBLANES, CONV_DIM), F32)],
        compiler_params=_params(2),
        name="front",
    )(x, mod, conv_state, wm, wba, wconv, gpar)


def _delta_masks(r, block):
    i = lax.broadcasted_iota(jnp.int32, (r, r), 0)
    j = lax.broadcasted_iota(jnp.int32, (r, r), 1)
    shift = int(math.log2(block))
    same = (i >> shift) == (j >> shift)
    levels = [((i >> (s + 1)) == (j >> (s + 1))) & ((i >> s) != (j >> s)) for s in range(shift)]
    return dict(tri=same & (i >= j), strict=same & (i > j), upper=same & (i < j), eye=(i == j).astype(F32),
                levels=levels)


def _split_dot(a01, x):
    a01 = a01.astype(BF16)
    hi = x.astype(BF16)
    rest = x - hi.astype(F32)
    mid = rest.astype(BF16)
    lo = (rest - mid.astype(F32)).astype(BF16)
    return _dot(a01, hi) + _dot(a01, mid) + _dot(a01, lo)


def _decay_sums(g, masks):
    return _split_dot(masks["tri"], g), _split_dot(masks["upper"], g)


def _delta_pre(problems, masks):
    tri, strict, eye, levels = masks["tri"], masks["strict"], masks["eye"], masks["levels"]
    n = len(problems)
    r = problems[0][0].shape[0]
    lhs, ks, rhs, decays, qgs, kds = [], [], [], [], [], []
    for qr, kr, v, beta_b, gc, rev, gc_row in problems:
        q = qr * lax.rsqrt(jnp.sum(qr * qr, -1, keepdims=True) + 1e-6) * (DK_A ** -0.5)
        k = kr * lax.rsqrt(jnp.sum(kr * kr, -1, keepdims=True) + 1e-6)
        gc_col = gc[:, :r] if r <= DK_A else jnp.concatenate([gc] * (r // DK_A), axis=1)
        decays.append(jnp.exp(jnp.where(tri, gc_col - gc_row, -jnp.inf)))
        eg = jnp.exp(gc)
        kb = k * beta_b
        lhs.append(jnp.concatenate([kb, q], axis=0).astype(BF16))
        ks.append(k.astype(BF16))
        rhs.append(jnp.concatenate([v * beta_b, kb * eg], axis=1).astype(BF16))
        qgs.append((q * eg).astype(BF16))
        kds.append((k * jnp.exp(rev)).astype(BF16))
    kk = [_dot_nt(lhs[i], ks[i]) for i in range(n)]
    mm = [jnp.where(strict, kk[i][:r] * decays[i], 0.0) for i in range(n)]
    qk = [(kk[i][r:] * decays[i]).astype(BF16) for i in range(n)]
    t = [eye - jnp.where(levels[0], mm[i], 0.0) for i in range(n)]
    for s, lvl in enumerate(levels[1:], start=1):
        blk = 2 ** s
        tb = [t[i].astype(BF16) for i in range(n)]
        mo = [jnp.where(lvl, mm[i], 0.0).astype(BF16) for i in range(n)]
        if blk % 8:
            x = [_dot(tb[i], mo[i]).astype(BF16) for i in range(n)]
            y = [_dot(x[i], tb[i]) for i in range(n)]
            t = [t[i] - y[i] for i in range(n)]
        else:
            split = [t[i].reshape(r // (2 * blk), 2, blk, r) for i in range(n)]
            lo = [split[i][:, 1].reshape(r // 2, r) for i in range(n)]
            x = [_dot(lo[i].astype(BF16), mo[i]).astype(BF16) for i in range(n)]
            y = [_dot(x[i], tb[i]) for i in range(n)]
            t = [jnp.stack([split[i][:, 0], (lo[i] - y[i]).reshape(r // (2 * blk), blk, r)], axis=1).reshape(r, r)
                 for i in range(n)]
    uw = [_dot(t[i].astype(BF16), rhs[i]) for i in range(n)]
    return [(uw[i][:, :DK_A], uw[i][:, DK_A:].astype(BF16), qgs[i], kds[i], qk[i]) for i in range(n)]


def _gated_norm(o, z, w_onorm):
    o = o * lax.rsqrt(jnp.mean(o * o, -1, keepdims=True) + RMS_EPS) * w_onorm
    return o * _silu(z)


DeltaRow = collections.namedtuple(
    "DeltaRow", "cs_ref z_ref gb_ref o_ref s_out_ref s_ref u_s o_s wq_s qkd_s egl_s")


def _delta_prompt_body(rows, wn_ref, *, tt, side_work=None):
    n_c = tt // CHUNK
    masks = _delta_masks(tt, CHUNK)

    @pl.when(pl.program_id(1) == 0)
    def _():
        for row in rows:
            row.s_ref[...] = jnp.zeros(row.s_ref.shape, F32)

    finish_side_work = side_work() if side_work is not None else None

    heads = range(N_HEADS_A)
    hcols = [slice(hd * DK_A, (hd + 1) * DK_A) for hd in heads]
    units = [(row, hd) for row in rows for hd in heads]
    problems = []
    for row in rows:
        gbv = row.gb_ref[...]
        gc_all, rev_all = _decay_sums(gbv, masks)
        gc_t = gc_all.T
        for hd in heads:
            lg = N_HEADS_A + hd
            problems.append((row.cs_ref[:, hcols[hd]].astype(F32),
                             row.cs_ref[:, D_MODEL + hd * DK_A:D_MODEL + (hd + 1) * DK_A].astype(F32),
                             row.cs_ref[:, 2 * D_MODEL + hd * DK_A:2 * D_MODEL + (hd + 1) * DK_A].astype(F32),
                             jnp.broadcast_to(gbv[:, hd:hd + 1], (tt, DK_A)),
                             jnp.broadcast_to(gc_all[:, lg:lg + 1], (tt, DK_A)),
                             jnp.broadcast_to(rev_all[:, lg:lg + 1], (tt, DK_A)),
                             jnp.broadcast_to(gc_t[lg:lg + 1, :], (tt, tt))))
    for i, ((row, hd), problem) in enumerate(zip(units, problems)):
        if hd == 0:
            pre = _delta_pre(problems[i:i + N_HEADS_A], masks)
        u, w, qg, kd, qk = pre[hd]
        row.u_s[:, hcols[hd]] = u
        kdt = kd.T
        for c in range(n_c):
            blk = slice(c * CHUNK, (c + 1) * CHUNK)
            idx = hd * n_c + c
            row.wq_s[idx, 0:CHUNK, :] = w[blk]
            row.wq_s[idx, CHUNK:2 * CHUNK, :] = qg[blk]
            row.qkd_s[idx, 0:CHUNK, :] = qk[blk, blk]
            row.qkd_s[idx, CHUNK:CHUNK + DK_A, :] = kdt[:, blk]
            last = (c + 1) * CHUNK - 1
            row.egl_s[idx] = jnp.broadcast_to(jnp.exp(problem[4][last:last + 1, :]), (8, DK_A))

    if finish_side_work is not None:
        finish_side_work()

    states = [row.s_ref[hd] for row, hd in units]
    n_u = range(len(units))
    for c in range(n_c):
        blk = slice(c * CHUNK, (c + 1) * CHUNK)
        idx = [hd * n_c + c for _, hd in units]
        ws = [_dot(units[i][0].wq_s[idx[i]], states[i].astype(BF16)) for i in n_u]
        v_new = [(units[i][0].u_s[blk, hcols[units[i][1]]] - ws[i][:CHUNK]).astype(BF16) for i in n_u]
        upd = [_dot(units[i][0].qkd_s[idx[i]], v_new[i]) for i in n_u]
        for i, (row, hd) in enumerate(units):
            row.o_s[blk, hcols[hd]] = ws[i][CHUNK:] + upd[i][:CHUNK]
            states[i] = states[i] * row.egl_s[idx[i]][0:1, :] + upd[i][CHUNK:]
    for i, (row, hd) in enumerate(units):
        row.s_ref[hd] = states[i]
        row.o_ref[:, hcols[hd]] = _gated_norm(row.o_s[:, hcols[hd]], row.z_ref[:, hcols[hd]].astype(F32),
                                              wn_ref[...]).astype(BF16)

    @pl.when(pl.program_id(1) == pl.num_programs(1) - 1)
    def _():
        for row in rows:
            row.s_out_ref[...] = row.s_ref[...]


def _softmax_keys(s):
    p = jnp.exp(s - jnp.max(s, axis=0, keepdims=True))
    return (p / jnp.sum(p, axis=0, keepdims=True)).astype(BF16)


def _zero_key0(x):
    return jnp.where(lax.broadcasted_iota(jnp.int32, x.shape, 0) == 0, 0.0, x)


def _kv_cols(kvh):
    return slice(kvh * HD_B, (kvh + 1) * HD_B)


def _group_queries(q, kvh):
    return jnp.concatenate([q[:, hh * HD_B:(hh + 1) * HD_B] for hh in range(kvh * GROUP_B, (kvh + 1) * GROUP_B)],
                           axis=0)


def _swa_prompt_cache(kc_ref, vc_ref, ko_ref, vo_ref, nq):
    @pl.when(pl.program_id(1) == pl.num_programs(1) - 1)
    def _():
        ko_ref[...] = kc_ref[(nq - 1) * WINDOW:, :]
        vo_ref[...] = vc_ref[(nq - 1) * WINDOW:, :]


def _swa_prompt_start(q_ref, kp_ref, kc_ref, vp_ref, vc_ref, bias_ref, o_ref, nq):
    n_q = GROUP_B * WINDOW
    key = lax.broadcasted_iota(jnp.int32, (2 * WINDOW, n_q), 0)
    no_prev = (pl.program_id(1) == 0) & (key >= 1) & (key < WINDOW)
    kall = jnp.concatenate([kp_ref[...], kc_ref[...]], axis=0)
    vall = jnp.concatenate([vp_ref[...], vc_ref[...]], axis=0)
    probs = [(j, kvh) for j in range(nq) for kvh in range(N_KV_B)]
    k2 = [_zero_key0(kall[j * WINDOW:(j + 2) * WINDOW]).astype(BF16) for j in range(nq)]
    v2t = [_zero_key0(vall[j * WINDOW:(j + 2) * WINDOW]).T.astype(BF16) for j in range(nq)]
    q = [q_ref[j * WINDOW:(j + 1) * WINDOW, :] for j in range(nq)]
    s = [_dot_nt(k2[j][:, _kv_cols(kvh)], _group_queries(q[j], kvh)) for j, kvh in probs]
    pn = []
    for (j, kvh), sc in zip(probs, s):
        sc = sc + bias_ref[kvh]
        pn.append(_softmax_keys(jnp.where(no_prev, -jnp.inf, sc) if j == 0 else sc))

    def finish():
        ot = [_dot(v2t[j][_kv_cols(kvh), :], p) for (j, kvh), p in zip(probs, pn)]
        for (j, kvh), o in zip(probs, ot):
            for g in range(GROUP_B):
                hh = kvh * GROUP_B + g
                o_ref[j * WINDOW:(j + 1) * WINDOW, hh * HD_B:(hh + 1) * HD_B] = \
                    o[:, g * WINDOW:(g + 1) * WINDOW].T.astype(BF16)

    return finish


def _mixer_prompt_body(cs_ref, z_ref, gb_ref, wn_ref, q_ref, kp_ref, kc_ref, vp_ref, vc_ref, bias_ref,
                       oa_ref, s_out_ref, ob_ref, ko_ref, vo_ref, *scratch, tt):
    nq = tt // WINDOW
    n_rows = cs_ref.shape[0]
    for g in range(n_rows):
        _swa_prompt_cache(kc_ref.at[g], vc_ref.at[g], ko_ref.at[g], vo_ref.at[g], nq)

    def attention():
        finishes = [_swa_prompt_start(q_ref.at[g], kp_ref.at[g], kc_ref.at[g], vp_ref.at[g], vc_ref.at[g], bias_ref,
                                      ob_ref.at[g], nq) for g in range(n_rows)]
        return lambda: [finish() for finish in finishes]

    rows = [DeltaRow(cs_ref.at[g], z_ref.at[g], gb_ref.at[g], oa_ref.at[g], s_out_ref.at[g],
                     *(buf.at[g] for buf in scratch)) for g in range(n_rows)]
    _delta_prompt_body(rows, wn_ref, tt=tt, side_work=attention)


def _mixer_prompt(cs, z, gb, w_onorm, qb, k, v, bias_t, *, tt):
    b, l, _ = cs.shape
    n_c = tt // CHUNK
    nq = tt // WINDOW
    gr = MIXER_PROMPT_BATCH_ROWS if b % MIXER_PROMPT_BATCH_ROWS == 0 else 1
    tok_spec = lambda n: pl.BlockSpec((gr, tt, n), lambda i, t: (i, t, 0))
    prev = lambda n: pl.BlockSpec((gr, WINDOW, n), lambda i, t: (i, jnp.maximum(nq * t - 1, 0), 0))
    last = lambda n: pl.BlockSpec((gr, WINDOW, n), lambda i, t: (i, 0, 0))
    return pl.pallas_call(
        functools.partial(_mixer_prompt_body, tt=tt),
        grid=(b // gr, l // tt),
        in_specs=[tok_spec(CONV_DIM), tok_spec(D_MODEL), tok_spec(GB_LANES), _const_spec((1, DK_A)),
                  tok_spec(D_MODEL), prev(KV_B), tok_spec(KV_B), prev(KV_B), tok_spec(KV_B),
                  _const_spec((N_KV_B, 2 * WINDOW, GROUP_B * WINDOW))],
        out_specs=[tok_spec(D_MODEL), pl.BlockSpec((gr, N_HEADS_A, DK_A, DK_A), lambda i, t: (i, 0, 0, 0)),
                   tok_spec(D_MODEL), last(KV_B), last(KV_B)],
        out_shape=[jax.ShapeDtypeStruct((b, l, D_MODEL), BF16),
                   jax.ShapeDtypeStruct((b, N_HEADS_A, DK_A, DK_A), F32),
                   jax.ShapeDtypeStruct((b, l, D_MODEL), BF16),
                   jax.ShapeDtypeStruct((b, WINDOW, KV_B), F32), jax.ShapeDtypeStruct((b, WINDOW, KV_B), F32)],
        scratch_shapes=[pltpu.VMEM((gr, N_HEADS_A, DK_A, DK_A), F32),
                        pltpu.VMEM((gr, tt, D_MODEL), F32), pltpu.VMEM((gr, tt, D_MODEL), F32),
                        pltpu.VMEM((gr, N_HEADS_A * n_c, 2 * CHUNK, DK_A), BF16),
                        pltpu.VMEM((gr, N_HEADS_A * n_c, CHUNK + DK_A, CHUNK), BF16),
                        pltpu.VMEM((gr, N_HEADS_A * n_c, 8, DK_A), F32)],
        compiler_params=_params(2),
        name="mixer_prompt",
    )(cs, z, gb, w_onorm, qb, k, k, v, v, bias_t)


def _mixer_sample_body(cs_ref, z_ref, gb_ref, wn_ref, s0_ref, q_ref, kn_ref, vn_ref, kc_ref, vc_ref, bias_ref, *rest,
                       bt, nb, ls, fill_layers):
    oa_ref, s_out_ref, ob_ref, ko_ref, vo_ref = rest[-5:]

    def put(ref, idx, val):
        idx = idx if isinstance(idx, tuple) else (idx,)
        if fill_layers:
            for layer_slot in range(fill_layers):
                ref[(layer_slot,) + idx] = val
        else:
            ref[idx] = val
    r = N_HEADS_A * ls
    masks = _delta_masks(r, ls)
    heads = range(N_HEADS_A)
    hrows = [slice(hd * ls, (hd + 1) * ls) for hd in heads]
    hcols = [slice(hd * DK_A, (hd + 1) * DK_A) for hd in heads]
    kvs = range(N_KV_B)

    def per_group(gi, carry):
        bis = [gi * nb + d for d in range(nb)]
        ks, vs, scores = [], [], []
        for bi in bis:
            kf = jnp.concatenate([kc_ref[bi], kn_ref[bi]], axis=0)
            vf = jnp.concatenate([vc_ref[bi], vn_ref[bi]], axis=0)
            put(ko_ref, bi, kf[ls:, :])
            put(vo_ref, bi, vf[ls:, :])
            ks.append(_zero_key0(kf).astype(BF16))
            vs.append(_zero_key0(vf).astype(BF16))
        for d, bi in enumerate(bis):
            q = q_ref[bi]
            scores.append([_dot_nt(ks[d][:, _kv_cols(kvh)], _group_queries(q, kvh)) for kvh in kvs])
        pn = [[_softmax_keys(scores[d][kvh] + bias_ref[kvh]) for kvh in kvs] for d in range(nb)]

        problems, egl = [], []
        for bi in bis:
            cs = cs_ref[bi].astype(F32)
            gbv = gb_ref[bi]
            stack = lambda base: jnp.concatenate([cs[:, base + hd * DK_A: base + (hd + 1) * DK_A] for hd in heads],
                                                 axis=0)
            beta_b = jnp.concatenate([jnp.broadcast_to(gbv[:, hd:hd + 1], (ls, DK_A)) for hd in heads], axis=0)
            g_b = jnp.concatenate(
                [jnp.broadcast_to(gbv[:, N_HEADS_A + hd:N_HEADS_A + hd + 1], (ls, DK_A)) for hd in heads], axis=0)
            gc, rev = _decay_sums(g_b, masks)
            egl.append(jnp.exp(gc))
            problems.append((stack(0), stack(D_MODEL), stack(2 * D_MODEL), beta_b, gc, rev, gc.T[:r, :]))
        pre = _delta_pre(problems, masks)
        both = [[_dot(jnp.concatenate([pre[d][1][hrows[hd]], pre[d][2][hrows[hd]]], axis=0),
                      s0_ref[bis[d], hd].astype(BF16)) for hd in heads] for d in range(nb)]
        v_new = [(pre[d][0] - jnp.concatenate([both[d][hd][:ls] for hd in heads], axis=0)).astype(BF16)
                 for d in range(nb)]
        qkv = [_dot(pre[d][4], v_new[d]) for d in range(nb)]
        upd = [[_dot_tn(pre[d][3][hrows[hd]], v_new[d][hrows[hd]]) for hd in heads] for d in range(nb)]
        att = [[_dot_tn(pn[d][kvh], vs[d][:, _kv_cols(kvh)]) for kvh in kvs] for d in range(nb)]
        for d, bi in enumerate(bis):
            zb = z_ref[bi].astype(F32)
            for hd in heads:
                last = (hd + 1) * ls - 1
                put(s_out_ref, (bi, hd), s0_ref[bi, hd] * egl[d][last:last + 1, :] + upd[d][hd])
                o = both[d][hd][ls:] + qkv[d][hrows[hd]]
                oa_ref[bi, :, hcols[hd]] = _gated_norm(o, zb[:, hcols[hd]], wn_ref[...]).astype(BF16)
            for kvh in kvs:
                for g in range(GROUP_B):
                    hh = kvh * GROUP_B + g
                    ob_ref[bi, :, hh * HD_B:(hh + 1) * HD_B] = att[d][kvh][g * ls:(g + 1) * ls].astype(BF16)
        return carry

    lax.fori_loop(0, bt // nb, per_group, 0)


def _mixer_sample(cs, z, gb, w_onorm, s0_all, qb, k_new, v_new, k_cache_all, v_cache_all, bias_s, prev, *, layer, bt):
    b, ls, _ = cs.shape
    tok_spec = lambda n: pl.BlockSpec((bt, ls, n), lambda i: (i, 0, 0))
    st_spec = pl.BlockSpec((None, bt, N_HEADS_A, DK_A, DK_A), lambda i: (layer, i, 0, 0, 0))
    cache = pl.BlockSpec((None, bt, WINDOW, KV_B), lambda i: (layer, i, 0, 0))
    extra, extra_specs = ([], []) if prev is None else (list(prev), [pl.BlockSpec(memory_space=pl.ANY)] * 3)
    depth = s0_all.shape[0]
    if extra:
        st_out, cache_out = st_spec, cache
    else:
        st_out = pl.BlockSpec((depth, bt, N_HEADS_A, DK_A, DK_A), lambda i: (0, i, 0, 0, 0))
        cache_out = pl.BlockSpec((depth, bt, WINDOW, KV_B), lambda i: (0, i, 0, 0))
    n_in = 11
    return pl.pallas_call(
        functools.partial(_mixer_sample_body, bt=bt, nb=min(bt, MIXER_SAMPLE_ROWS_PER_TRIP), ls=ls,
                          fill_layers=0 if extra else depth),
        grid=(b // bt,),
        in_specs=[tok_spec(CONV_DIM), tok_spec(D_MODEL), tok_spec(GB_LANES), _const_spec((1, DK_A)), st_spec,
                  tok_spec(D_MODEL), tok_spec(KV_B), tok_spec(KV_B), cache, cache,
                  _const_spec((N_KV_B, WINDOW + ls, GROUP_B * ls))] + extra_specs,
        out_specs=[tok_spec(D_MODEL), st_out, tok_spec(D_MODEL), cache_out, cache_out],
        out_shape=[jax.ShapeDtypeStruct((b, ls, D_MODEL), BF16), jax.ShapeDtypeStruct(s0_all.shape, F32),
                   jax.ShapeDtypeStruct((b, ls, D_MODEL), BF16),
                   jax.ShapeDtypeStruct(k_cache_all.shape, F32), jax.ShapeDtypeStruct(v_cache_all.shape, F32)],
        input_output_aliases={n_in: 1, n_in + 1: 3, n_in + 2: 4} if extra else {},
        compiler_params=_params(1),
        name="mixer_sample",
    )(cs, z, gb, w_onorm, s0_all, qb, k_new, v_new, k_cache_all, v_cache_all, bias_s, *extra)


def _layer_norm(y, g, b):
    mu = jnp.mean(y, -1, keepdims=True)
    d = y - mu
    var = jnp.mean(d * d, -1, keepdims=True)
    return d * lax.rsqrt(var + LN_EPS) * g + b


def _tail_body(x_ref, oa_ref, ob_ref, gates_ref, mod_ref, wpa_ref, wpb_ref, wout_ref, ln1g_ref, ln1b_ref,
               wup_ref, wdn_ref, ln2g_ref, ln2b_ref, o_ref, *, bt, tt, alpha):
    mod = lambda i: mod_ref[:, :, i * D_MODEL:(i + 1) * D_MODEL]
    parts = [slice(0, tt // 2), slice(tt // 2, tt)] if bt == 1 and tt % (2 * BF16_SUBLANES) == 0 else [slice(0, tt)]
    ms = [bt * (p.stop - p.start) for p in parts]
    two = range(len(parts))
    flat = lambda a, i: a.reshape(ms[i], a.shape[-1])
    dot = _dot if len(parts) > 1 else _dot_rows
    pa = [dot(flat(oa_ref[:, p, :], i), wpa_ref[...]) for i, p in enumerate(parts)]
    pb = [dot(flat(ob_ref[:, p, :], i), wpb_ref[...]) for i, p in enumerate(parts)]
    mixed = [flat(gates_ref[:, p, 0:D_MODEL].astype(F32), i) * pa[i]
             + flat(gates_ref[:, p, D_MODEL:2 * D_MODEL].astype(F32), i) * pb[i] for i, p in enumerate(parts)]
    attn = [dot(mixed[i].astype(BF16), wout_ref[...]) for i in two]
    x1 = [_layer_norm(alpha * x_ref[:, p, :] + mod(2) * attn[i].reshape(bt, -1, D_MODEL), ln1g_ref[...], ln1b_ref[...])
          for i, p in enumerate(parts)]
    h2 = [flat(x1[i] * (1.0 + mod(4)) + mod(3), i).astype(BF16) for i in two]
    ff = [jnp.zeros((ms[i], D_MODEL), F32) for i in two]
    for c in range(D_FF // D_MODEL):
        cols = slice(c * D_MODEL, (c + 1) * D_MODEL)
        a = [jnp.maximum(dot(h2[i], wup_ref[:, cols]), 0.0) for i in two]
        ff = [ff[i] + dot((a[i] * a[i]).astype(BF16), wdn_ref[cols, :]) for i in two]
    for i, p in enumerate(parts):
        o_ref[:, p, :] = _layer_norm(alpha * x1[i] + mod(5) * ff[i].reshape(bt, -1, D_MODEL),
                                     ln2g_ref[...], ln2b_ref[...])


def _tail(x, oa, ob, gates, mod, w, p, *, layer, bt, tt, alpha):
    b, l, _ = x.shape
    tok_spec = lambda n: pl.BlockSpec((bt, tt, n), lambda i, t: (i, t, 0))
    row = _const_spec((1, D_MODEL))
    sq = _layer_spec((D_MODEL, D_MODEL), layer)
    return pl.pallas_call(
        functools.partial(_tail_body, bt=bt, tt=tt, alpha=alpha),
        grid=(b // bt, l // tt),
        in_specs=[tok_spec(D_MODEL), tok_spec(D_MODEL), tok_spec(D_MODEL), tok_spec(2 * D_MODEL),
                  pl.BlockSpec((bt, 1, 6 * D_MODEL), lambda i, t: (i, 0, 0)),
                  sq, sq, sq, row, row,
                  _layer_spec((D_MODEL, D_FF), layer), _layer_spec((D_FF, D_MODEL), layer), row, row],
        out_specs=tok_spec(D_MODEL),
        out_shape=jax.ShapeDtypeStruct((b, l, D_MODEL), F32),
        compiler_params=_params(2),
        name="tail",
    )(x, oa, ob, gates, mod, w["wpa"], w["wpb"], w["wout"], p["ln1_g"], p["ln1_b"],
      w["wup"], w["wdn"], p["ln2_g"], p["ln2_b"])


def _tiles(b, l, rows):
    tt = min(l, rows)
    bt = max(1, min(b, rows // tt))
    return bt, tt


FRONT_ROWS_LONG, FRONT_ROWS_SHORT = 512, 256
PROJ_ROWS = 128
MIXER_PROMPT_ROWS = 2 * WINDOW
MIXER_PROMPT_BATCH_ROWS = 2
MIXER_SAMPLE_BATCH_ROWS = 8
MIXER_SAMPLE_ROWS_PER_TRIP = 4
TAIL_ROWS = 512


def _layer_params(l, w_conv, a_log, dt_bias, w_onorm, ln1_g, ln1_b, ln2_g, ln2_b):
    gpar = jnp.zeros((2, GB_LANES), F32)
    gpar = gpar.at[0, N_HEADS_A:2 * N_HEADS_A].set(-jnp.exp(a_log[l].astype(F32)))
    gpar = gpar.at[1, N_HEADS_A:2 * N_HEADS_A].set(dt_bias[l].astype(F32))
    row = lambda a: a[l].reshape(1, -1).astype(F32)
    return dict(wconv=w_conv[l].astype(F32), gpar=gpar, w_onorm=row(w_onorm), ln1_g=row(ln1_g), ln1_b=row(ln1_b),
                ln2_g=row(ln2_g), ln2_b=row(ln2_b))


def _trunk_layer(x, mod, layer, w, p, alpha, conv_state, sample_state):
    b, l, _ = x.shape
    bt, tt = _tiles(b, l, FRONT_ROWS_LONG if l % FRONT_ROWS_LONG == 0 else FRONT_ROWS_SHORT)
    cs, z, gb, qb, k, v, gates, conv_out = _front(x, mod, conv_state, w["wm"], w["wba"], p["wconv"], p["gpar"],
                                                  layer=layer, bt=bt, tt=tt)
    if sample_state is None:
        oa, s_new, ob, k_new, v_new = _mixer_prompt(cs, z, gb, p["w_onorm"], qb, k, v, p["bias_t"],
                                                    tt=min(l, MIXER_PROMPT_ROWS))
    else:
        s0_all, kc_all, vc_all, s_prev, k_prev, v_prev = sample_state
        bias_s = p["bias_t"][:, :WINDOW + l, :].reshape(N_KV_B, WINDOW + l, GROUP_B, WINDOW)[..., :l]
        bias_s = bias_s.reshape(N_KV_B, WINDOW + l, GROUP_B * l)
        prev = None if s_prev is None else (s_prev, k_prev, v_prev)
        oa, s_new, ob, k_new, v_new = _mixer_sample(cs, z, gb, p["w_onorm"], s0_all, qb, k, v, kc_all, vc_all, bias_s,
                                                    prev, layer=layer, bt=min(b, MIXER_SAMPLE_BATCH_ROWS))
    bt, tt = _tiles(b, l, TAIL_ROWS)
    x2 = _tail(x, oa, ob, gates, mod, w, p, layer=layer, bt=bt, tt=tt, alpha=alpha)
    return x2, (s_new, conv_out, k_new, v_new)


def kernel(x_prompt, x_sample, state_delta, state_conv, cache_k, cache_v, c_prompt, c_sample, rel_bias, w_ada, b_ada, w_in, w_conv, a_log, dt_bias, w_onorm, sinks, w_pa, w_pb, w_out, ln1_g, ln1_b, w_up, w_down, ln2_g, ln2_b):
    depth = w_in.shape[0]
    alpha = (2 * depth) ** 0.25
    bp = x_prompt.shape[0]
    mod_all = _ada(jnp.concatenate([c_prompt, c_sample], axis=0), w_ada, b_ada)
    bias_t = _bias_table_t(rel_bias, sinks)
    wm, wba = _prep_w_in(w_in)
    w = dict(wm=wm, wba=wba, wpa=_cast_bf16(w_pa), wpb=_cast_bf16(w_pb), wout=_cast_bf16(w_out),
             wup=_cast_bf16(w_up), wdn=_cast_bf16(w_down))
    bs = x_sample.shape[0]
    kc_all = cache_k.reshape(depth, bs, WINDOW, KV_B)
    vc_all = cache_v.reshape(depth, bs, WINDOW, KV_B)
    yp, ys = x_prompt, x_sample
    prompt_outs = [[] for _ in range(4)]
    sample_conv = []
    s_all = k_all = v_all = None
    for l in range(depth):
        p = _layer_params(l, w_conv, a_log, dt_bias, w_onorm, ln1_g, ln1_b, ln2_g, ln2_b)
        p["bias_t"] = bias_t[l]
        mod_p = mod_all[l, :bp][:, None, :]
        mod_s = mod_all[l, bp:][:, None, :]
        zero_conv = jnp.zeros((bp, CONV_W - 1, CONV_DIM), x_prompt.dtype)
        yp, rest_p = _trunk_layer(yp, mod_p, l, w, p, alpha, zero_conv, None)
        ys, (s_all, conv_s, k_all, v_all) = _trunk_layer(ys, mod_s, l, w, p, alpha, state_conv[l],
                                                         (state_delta, kc_all, vc_all, s_all, k_all, v_all))
        for acc, val in zip(prompt_outs, rest_p):
            acc.append(val)
        sample_conv.append(conv_s)
    heads = lambda a: a.reshape(a.shape[:-1] + (N_KV_B, HD_B))
    pd, pc, pk, pv = (jnp.stack(a) for a in prompt_outs)
    return (yp, ys, pd, pc, heads(pk), heads(pv), s_all, jnp.stack(sample_conv), heads(k_all), heads(v_all))
```
